```python
import jax, jax.numpy as jnp
from jax import lax
import numpy as np

D_MODEL = 1024
BATCH = 2
SEQ = 8192
DEPTH = 1
DEC_BATCH = 128
DEC_SEQ = 4
PAST_LEN = 8192
PAGE_SIZE = 128

MIX_W = D_MODEL
M_WIDTH = MIX_W // 2
M_HEADS = 4
M_DV = M_WIDTH // M_HEADS
M_DK = M_DV // 2
QK_W = 2 * M_HEADS * M_DK
CONV_W = 4
CHUNK = 128
A_WIDTH = MIX_W - M_WIDTH
A_HEADS = 8
A_HD = A_WIDTH // A_HEADS
A_KV = 2
A_GROUP = A_HEADS // A_KV
KV_W = A_KV * A_HD
WINDOW = 128
D_FF = 4 * D_MODEL
P_DIM = 256
PROJ_W = QK_W + 2 * M_WIDTH + 2 * M_HEADS + A_WIDTH + 2 * KV_W
EPS = 1e-6

kernel_name = "hymba_mlstm_swa_sink_decode_step"

F32 = jnp.float32


def rms_norm(x, g):
    xf = x.astype(F32)
    y = xf * lax.rsqrt(jnp.mean(xf * xf, axis=-1, keepdims=True) + EPS)
    return (y * g.astype(F32)).astype(x.dtype)


def split_projection(z):
    sizes = (QK_W, M_WIDTH, M_WIDTH, 2 * M_HEADS, A_WIDTH, KV_W, KV_W)
    cuts = [int(c) for c in np.cumsum(sizes)[:-1]]
    return jnp.split(z, cuts, axis=-1)


def short_conv(u, buf, w):
    T = u.shape[1]
    up = jnp.concatenate([buf.astype(u.dtype), u], axis=1)
    out = sum(w[j] * up[:, j:j + T] for j in range(CONV_W))
    return jax.nn.silu(out), up[:, T:]


def mlstm_heads(qk, v, gates, b_gates):
    B, T, _ = qk.shape
    q, k = jnp.split(qk, 2, axis=-1)
    q = q.reshape(B, T, M_HEADS, M_DK).transpose(0, 2, 1, 3).astype(F32) * (M_DK ** -0.5)
    k = k.reshape(B, T, M_HEADS, M_DK).transpose(0, 2, 1, 3).astype(F32)
    v = v.reshape(B, T, M_HEADS, M_DV).transpose(0, 2, 1, 3).astype(F32)
    g = gates.astype(F32) + b_gates.astype(F32)
    ig = g[..., :M_HEADS].transpose(0, 2, 1)
    lf = jax.nn.log_sigmoid(g[..., M_HEADS:]).transpose(0, 2, 1)
    return q, k, v, ig, lf


def mlstm_chunk(state, blk):
    c_prev, n_prev, m_prev = state
    q, k, v, ig, lf = blk
    L = q.shape[2]
    b = jnp.cumsum(lf, axis=-1)
    causal = jnp.tril(jnp.ones((L, L), dtype=bool))
    log_d = jnp.where(causal, b[..., :, None] - b[..., None, :] + ig[..., None, :], -jnp.inf)
    log_inter = b + m_prev[..., None]
    m_t = jnp.maximum(log_inter, jnp.max(log_d, axis=-1))
    w_intra = jnp.exp(log_d - m_t[..., None])
    w_inter = jnp.exp(log_inter - m_t)
    s = jnp.einsum('bhtd,bhsd->bhts', q, k) * w_intra
    num = jnp.einsum('bhts,bhsv->bhtv', s, v) + w_inter[..., None] * jnp.einsum('bhtd,bhdv->bhtv', q, c_prev)
    den = jnp.sum(s, axis=-1) + w_inter * jnp.einsum('bhtd,bhd->bht', q, n_prev)
    h = num / jnp.maximum(jnp.abs(den), jnp.exp(-m_t))[..., None]
    b_last = b[..., -1]
    log_w = b_last[..., None] - b + ig
    m_new = jnp.maximum(b_last + m_prev, jnp.max(log_w, axis=-1))
    w_k = jnp.exp(log_w - m_new[..., None])
    decay = jnp.exp(b_last + m_prev - m_new)
    c_new = decay[..., None, None] * c_prev + jnp.einsum('bhs,bhsd,bhsv->bhdv', w_k, k, v)
    n_new = decay[..., None] * n_prev + jnp.einsum('bhs,bhsd->bhd', w_k, k)
    return (c_new, n_new, m_new), h


def mlstm_prompt(q, k, v, ig, lf):
    B, H, T, _ = q.shape
    nc = T // CHUNK

    def to_chunks(a):
        return jnp.moveaxis(a.reshape((B, H, nc, CHUNK) + a.shape[3:]), 2, 0)

    init = (jnp.zeros((B, H, M_DK, M_DV), F32), jnp.zeros((B, H, M_DK), F32), jnp.zeros((B, H), F32))
    final, h = lax.scan(mlstm_chunk, init, tuple(to_chunks(a) for a in (q, k, v, ig, lf)))
    h = jnp.moveaxis(h, 0, 2).reshape(B, H, T, M_DV)
    return final, h


def mlstm_output(h, o, g):
    B, H, T, _ = h.shape
    hn = h * lax.rsqrt(jnp.mean(h * h, axis=-1, keepdims=True) + EPS)
    hn = hn * g.astype(F32).reshape(M_HEADS, 1, M_DV)
    hn = hn.transpose(0, 2, 1, 3).reshape(B, T, M_WIDTH)
    return (jax.nn.sigmoid(o.astype(F32)) * hn).astype(o.dtype)


def alibi_slopes():
    return jnp.exp2(-8.0 * jnp.arange(1, A_HEADS + 1, dtype=F32) / A_HEADS).reshape(A_KV, A_GROUP)


def sink_softmax(scores, sinks):
    s = sinks.astype(F32).reshape(A_KV, A_GROUP)[:, :, None, None]
    m = jnp.maximum(jnp.max(scores, axis=-1, keepdims=True), s)
    e = jnp.exp(scores - m)
    return e / (jnp.sum(e, axis=-1, keepdims=True) + jnp.exp(s - m))


def swa_prompt(q, k, v, sinks):
    B, T = q.shape[:2]
    nb = T // WINDOW
    qb = q.reshape(B, nb, WINDOW, A_KV, A_GROUP, A_HD).astype(F32)
    kb = k.reshape(B, nb, WINDOW, A_KV, A_HD).astype(F32)
    vb = v.reshape(B, nb, WINDOW, A_KV, A_HD).astype(F32)
    pad = ((0, 0), (1, 0), (0, 0), (0, 0), (0, 0))
    kk = jnp.concatenate([jnp.pad(kb, pad)[:, :-1], kb], axis=2)
    vv = jnp.concatenate([jnp.pad(vb, pad)[:, :-1], vb], axis=2)
    qi = jnp.arange(WINDOW)[:, None] + WINDOW
    si = jnp.arange(2 * WINDOW)[None, :]
    d = qi - si
    band = (d >= 0) & (d <= WINDOW)
    not_first = jnp.arange(nb)[:, None, None] > 0
    valid = band[None] & (not_first | (si >= WINDOW)[None])
    scores = jnp.einsum('bnqkgd,bnskd->bnkgqs', qb, kk) * (A_HD ** -0.5)
    scores = scores - alibi_slopes()[:, :, None, None] * d.astype(F32)
    scores = jnp.where(valid[None, :, None, None], scores, -jnp.inf)
    p = sink_softmax(scores, sinks)
    out = jnp.einsum('bnkgqs,bnskd->bnqkgd', p, vv)
    return out.reshape(B, T, A_WIDTH)


def swa_sample(q, k, v, k_buf, v_buf, sinks):
    B, T = q.shape[:2]
    k_all = jnp.concatenate([k_buf.astype(k.dtype), k], axis=1)
    v_all = jnp.concatenate([v_buf.astype(v.dtype), v], axis=1)
    qi = jnp.arange(T)[:, None] + WINDOW
    si = jnp.arange(WINDOW + T)[None, :]
    d = qi - si
    valid = (d >= 0) & (d <= WINDOW)
    qh = q.reshape(B, T, A_KV, A_GROUP, A_HD).astype(F32)
    scores = jnp.einsum('bqkgd,bskd->bkgqs', qh, k_all.astype(F32)) * (A_HD ** -0.5)
    scores = scores - alibi_slopes()[:, :, None, None] * d.astype(F32)
    scores = jnp.where(valid, scores, -jnp.inf)
    p = sink_softmax(scores, sinks)
    out = jnp.einsum('bkgqs,bskd->bqkgd', p, v_all.astype(F32)).reshape(B, T, A_WIDTH)
    return out, k_all[:, T:], v_all[:, T:]


def decoder_layer(x, p, lp, state):
    B, T, _ = x.shape
    xn = rms_norm(x, lp['norm_mix_pre'])
    qk_raw, v_m, o_m, gates, q_a, k_a, v_a = split_projection(xn @ lp['w_in'])
    q_a = q_a.reshape(B, T, A_HEADS, A_HD)
    k_a = k_a.reshape(B, T, A_KV, A_HD)
    v_a = v_a.reshape(B, T, A_KV, A_HD)
    if state is None:
        conv_buf = jnp.zeros((B, CONV_W - 1, QK_W), x.dtype)
    else:
        c0, n0, m0, conv_buf, k_buf, v_buf = state
    qk, conv_new = short_conv(qk_raw, conv_buf, lp['conv_w'])
    q, k, v, ig, lf = mlstm_heads(qk, v_m, gates, lp['b_gates'])
    if state is None:
        (c_new, n_new, m_new), h = mlstm_prompt(q, k, v, ig, lf)
        att = swa_prompt(q_a, k_a, v_a, lp['attn_sinks'])
        k_new, v_new = k_a[:, -WINDOW:], v_a[:, -WINDOW:]
    else:
        init = (c0.astype(F32), n0.astype(F32), m0.astype(F32))
        (c_new, n_new, m_new), h = mlstm_chunk(init, (q, k, v, ig, lf))
        att, k_new, v_new = swa_sample(q_a, k_a, v_a, k_buf, v_buf, lp['attn_sinks'])
    y_m = mlstm_output(h, o_m, lp['mlstm_norm'])
    y_a = rms_norm(att.astype(x.dtype), lp['attn_norm'])
    mix = jnp.concatenate([y_m, y_a], axis=-1) @ lp['w_out']
    x = x + rms_norm(mix, lp['norm_mix_post'])
    u = rms_norm(x, lp['norm_ffn_pre'])
    f = jnp.square(jax.nn.relu(u @ lp['w_up'])) @ lp['w_down']
    x = x + rms_norm(f, lp['norm_ffn_post'])
    x = x + jax.nn.sigmoid(x @ lp['w_pgate']) * (p @ lp['w_pproj'])
    return x, (c_new, n_new, m_new, conv_new, k_new, v_new)


def setup_inputs(seed: int = 0) -> dict:
    key = jax.random.key(seed)
    ks = jax.random.split(key, 26)

    def nrm(k, shape, s):
        return jax.random.normal(k, shape, F32) * s

    def gain(k, width):
        return 1.0 + 0.1 * jax.random.normal(k, (DEPTH, width), F32)

    b_i = nrm(ks[0], (DEPTH, M_HEADS), 0.1)
    b_f = 3.0 + nrm(ks[1], (DEPTH, M_HEADS), 0.5)
    return {
        'x_prompt': nrm(ks[2], (BATCH, SEQ, D_MODEL), 1.0),
        'x_sample': nrm(ks[3], (DEC_BATCH, DEC_SEQ, D_MODEL), 1.0),
        'p_prompt': nrm(ks[4], (DEPTH, BATCH, SEQ, P_DIM), 1.0),
        'p_sample': nrm(ks[5], (DEPTH, DEC_BATCH, DEC_SEQ, P_DIM), 1.0),
        'state_mlstm_c': nrm(ks[6], (DEPTH, DEC_BATCH, M_HEADS, M_DK, M_DV), 0.5),
        'state_mlstm_n': nrm(ks[7], (DEPTH, DEC_BATCH, M_HEADS, M_DK), 1.0),
        'state_mlstm_m': nrm(ks[8], (DEPTH, DEC_BATCH, M_HEADS), 1.0),
        'state_mlstm_conv': nrm(ks[9], (DEPTH, DEC_BATCH, CONV_W - 1, QK_W), 1.0),
        'cache_swa_k': nrm(ks[10], (DEPTH, DEC_BATCH, WINDOW, A_KV, A_HD), 1.0),
        'cache_swa_v': nrm(ks[11], (DEPTH, DEC_BATCH, WINDOW, A_KV, A_HD), 1.0),
        'norm_mix_pre': gain(ks[12], D_MODEL),
        'w_in': nrm(ks[13], (DEPTH, D_MODEL, PROJ_W), D_MODEL ** -0.5),
        'b_gates': jnp.concatenate([b_i, b_f], axis=-1),
        'conv_w': nrm(ks[14], (DEPTH, CONV_W, QK_W), CONV_W ** -0.5),
        'mlstm_norm': gain(ks[15], M_WIDTH),
        'attn_sinks': nrm(ks[16], (DEPTH, A_HEADS), 1.0),
        'attn_norm': gain(ks[17], A_WIDTH),
        'w_out': nrm(ks[18], (DEPTH, MIX_W, D_MODEL), MIX_W ** -0.5),
        'norm_mix_post': gain(ks[19], D_MODEL),
        'norm_ffn_pre': gain(ks[20], D_MODEL),
        'w_up': nrm(ks[21], (DEPTH, D_MODEL, D_FF), D_MODEL ** -0.5),
        'w_down': nrm(ks[22], (DEPTH, D_FF, D_MODEL), D_FF ** -0.5),
        'norm_ffn_post': gain(ks[23], D_MODEL),
        'w_pgate': nrm(ks[24], (DEPTH, D_MODEL, D_MODEL), D_MODEL ** -0.5),
        'w_pproj': nrm(ks[25], (DEPTH, P_DIM, D_MODEL), P_DIM ** -0.5),
    }


def reference(x_prompt, x_sample, p_prompt, p_sample, state_mlstm_c, state_mlstm_n, state_mlstm_m,
              state_mlstm_conv, cache_swa_k, cache_swa_v, norm_mix_pre, w_in, b_gates, conv_w,
              mlstm_norm, attn_sinks, attn_norm, w_out, norm_mix_post, norm_ffn_pre, w_up, w_down,
              norm_ffn_post, w_pgate, w_pproj):
    hp, hs = x_prompt, x_sample
    new_p = [[] for _ in range(6)]
    new_s = [[] for _ in range(6)]
    for i in range(DEPTH):
        lp = {
            'norm_mix_pre': norm_mix_pre[i], 'w_in': w_in[i], 'b_gates': b_gates[i],
            'conv_w': conv_w[i], 'mlstm_norm': mlstm_norm[i], 'attn_sinks': attn_sinks[i],
            'attn_norm': attn_norm[i], 'w_out': w_out[i], 'norm_mix_post': norm_mix_post[i],
            'norm_ffn_pre': norm_ffn_pre[i], 'w_up': w_up[i], 'w_down': w_down[i],
            'norm_ffn_post': norm_ffn_post[i], 'w_pgate': w_pgate[i], 'w_pproj': w_pproj[i],
        }
        hp, sp = decoder_layer(hp, p_prompt[i], lp, None)
        st = (state_mlstm_c[i], state_mlstm_n[i], state_mlstm_m[i], state_mlstm_conv[i],
              cache_swa_k[i], cache_swa_v[i])
        hs, ss = decoder_layer(hs, p_sample[i], lp, st)
        for lst, a in zip(new_p, sp):
            lst.append(a)
        for lst, a in zip(new_s, ss):
            lst.append(a)
    c_p, n_p, m_p, conv_p, k_p, v_p = [jnp.stack(l) for l in new_p]
    c_s, n_s, m_s, conv_s, k_s, v_s = [jnp.stack(l) for l in new_s]
    return (hp, hs, c_p, n_p, m_p, conv_p, k_p, v_p, c_s, n_s, m_s, conv_s, k_s, v_s)
```

```python
import functools

import numpy as np
import jax
import jax.numpy as jnp
from jax import lax
from jax.experimental import pallas as pl
from jax.experimental.pallas import tpu as pltpu

F32 = jnp.float32
BF16 = jnp.bfloat16

D_MODEL = 1024
M_WIDTH = 512
M_HEADS = 4
M_DV = 128
M_DK = 64
QK_W = 512
CONV_W = 4
CHUNK = 128
A_WIDTH = 512
A_HEADS = 8
A_HD = 64
A_KV = 2
A_GROUP = 4
KV_W = 128
WINDOW = 128
D_FF = 4096
P_DIM = 256
EPS = 1e-6
N_GATES = 2 * M_HEADS

VMEM_LIMIT_BYTES = 56 * 1024 * 1024
ROW_TILE = 512
FF_TILE = 1024

NEG_INF = float("-inf")
SLOPES = [2.0 ** (-8.0 * (h + 1) / A_HEADS) for h in range(A_HEADS)]


def _dot(a, b):
    return jnp.dot(a, b, preferred_element_type=F32)


def _dot_nt(a, b):
    return lax.dot_general(a, b, (((1,), (1,)), ((), ())), preferred_element_type=F32)


def _dot_tn(a, b):
    return lax.dot_general(a, b, (((0,), (0,)), ((), ())), preferred_element_type=F32)


def _dot_exact(a, b):
    return lax.dot_general(a, b, (((1,), (0,)), ((), ())), precision=lax.Precision.HIGHEST,
                           preferred_element_type=F32)


def _rms(x, g):
    return x * lax.rsqrt(jnp.mean(x * x, axis=-1, keepdims=True) + EPS) * g


def _in_proj_kernel(x_ref, g_ref, w_ref, wg_ref, wgt_ref,
                    qk_ref, vm_ref, om_ref, qa_ref, ka_ref, va_ref, gc_ref, gr_ref):
    xn = _rms(x_ref[...], g_ref[...]).astype(BF16)
    z = _dot(xn, w_ref[...])
    qk_ref[...] = z[:, 0:512]
    vm_ref[...] = z[:, 512:1024]
    om_ref[...] = z[:, 1024:1536]
    qa_ref[...] = z[:, 1536:2048]
    ka_ref[...] = z[:, 2048:2176]
    va_ref[...] = z[:, 2176:2304]
    gc_ref[...] = _dot(xn, wg_ref[...])
    gr_ref[...] = _dot_nt(wgt_ref[...], xn)


def _in_proj(x2d, g_pre, w_main, w_g, w_gt):
    n = x2d.shape[0]
    tm = min(ROW_TILE, n)
    row = lambda w: pl.BlockSpec((tm, w), lambda i: (i, 0))
    full = lambda a: pl.BlockSpec(a.shape, lambda i: (0,) * a.ndim)
    out_widths = (QK_W, M_WIDTH, M_WIDTH, A_WIDTH, KV_W, KV_W, N_GATES)
    return pl.pallas_call(
        _in_proj_kernel,
        grid=(n // tm,),
        in_specs=[row(D_MODEL), full(g_pre), full(w_main), full(w_g), full(w_gt)],
        out_specs=[row(w) for w in out_widths] + [pl.BlockSpec((N_GATES, tm), lambda i: (0, i))],
        out_shape=[jax.ShapeDtypeStruct((n, w), F32) for w in out_widths]
        + [jax.ShapeDtypeStruct((N_GATES, n), F32)],
        compiler_params=pltpu.CompilerParams(dimension_semantics=("arbitrary",),
                                             vmem_limit_bytes=VMEM_LIMIT_BYTES),
        name="in_proj",
    )(x2d, g_pre, w_main, w_g, w_gt)


def _gate_logs(gc, gr, bg_row, bg_col):
    L = gc.shape[0]
    gcb = gc + bg_row
    grb = gr + bg_col
    ig_c, lf_c = gcb[:, 0:M_HEADS], jax.nn.log_sigmoid(gcb[:, M_HEADS:N_GATES])
    ig_r, lf_r = grb[0:M_HEADS, :], jax.nn.log_sigmoid(grb[M_HEADS:N_GATES, :])
    r = lax.broadcasted_iota(jnp.int32, (L, L), 0)
    c = lax.broadcasted_iota(jnp.int32, (L, L), 1)
    lower = (c <= r).astype(F32)
    upper = (r <= c).astype(F32)
    b_c = _dot_exact(lower, lf_c)
    b_r = _dot_exact(lf_r, upper)
    return ig_c, ig_r, b_c, b_r, (c <= r)


def _mlstm_head(q, k, v, b_c, b_r, ig_c, ig_r, causal, c_prev, n_prev, m_prev):
    L = q.shape[0]
    log_d = jnp.where(causal, b_c - b_r + ig_r, NEG_INF)
    log_inter = b_c + m_prev
    m_t = jnp.maximum(log_inter, jnp.max(log_d, axis=-1, keepdims=True))
    w_intra = jnp.exp(log_d - m_t)
    w_inter = jnp.exp(log_inter - m_t)
    qb, kb, vb = q.astype(BF16), k.astype(BF16), v.astype(BF16)
    s = _dot_nt(qb, kb) * w_intra
    num = _dot(s.astype(BF16), vb) + w_inter * _dot(qb, c_prev.astype(BF16))
    den = jnp.sum(s, axis=-1, keepdims=True) + w_inter * jnp.sum(q * n_prev, axis=-1, keepdims=True)
    h = num / jnp.maximum(jnp.abs(den), jnp.exp(-m_t))
    b_last = b_c[L - 1:L, :]
    log_w = b_last - b_c + ig_c
    m_new = jnp.maximum(b_last + m_prev, jnp.max(log_w, axis=0, keepdims=True))
    w_k = jnp.exp(log_w - m_new)
    decay = jnp.exp(b_last + m_prev - m_new)
    kw = k * w_k
    c_new = decay * c_prev + _dot_tn(kw.astype(BF16), vb)
    n_new = decay * n_prev + jnp.sum(kw, axis=0, keepdims=True)
    return h, c_new, n_new, m_new


def _mlstm_out(h, o, g):
    hn = h * lax.rsqrt(jnp.mean(h * h, axis=-1, keepdims=True) + EPS) * g
    return jax.nn.sigmoid(o) * hn


def _conv_silu(up_ref, base, w, L):
    out = w[0:1, :] * up_ref[base:base + L, :]
    for j in range(1, CONV_W):
        out = out + w[j:j + 1, :] * up_ref[base + j:base + j + L, :]
    return jax.nn.silu(out)


def _sink_softmax(scores, sink):
    m = sink
    for sc in scores:
        m = jnp.maximum(m, jnp.max(sc, axis=-1, keepdims=True))
    es = [jnp.exp(sc - m) for sc in scores]
    den = jnp.exp(sink - m)
    for e in es:
        den = den + jnp.sum(e, axis=-1, keepdims=True)
    return [e / den for e in es]


def _mixer_prompt_kernel(qk_ref, vm_ref, om_ref, qa_ref, ka_ref, va_ref, gc_ref, gr_ref,
                         bgr_ref, bgc_ref, cw_ref, mn_ref, sink_ref,
                         ym_ref, att_ref, c_out, n_out, m_out,
                         up_ref, c_st, n_st, m_st, kp_ref, vp_ref):
    ci = pl.program_id(1)
    L = CHUNK

    @pl.when(ci == 0)
    def _():
        up_ref[...] = jnp.zeros_like(up_ref)
        c_st[...] = jnp.zeros_like(c_st)
        n_st[...] = jnp.zeros_like(n_st)
        m_st[...] = jnp.zeros_like(m_st)
        kp_ref[...] = jnp.zeros_like(kp_ref)
        vp_ref[...] = jnp.zeros_like(vp_ref)

    qk_raw = qk_ref[...]
    up_ref[8:8 + L, :] = qk_raw
    qk = _conv_silu(up_ref, 5, cw_ref[...], L)
    up_ref[5:8, :] = qk_ref[L - 3:L, :]

    ig_c, ig_r, b_c, b_r, causal = _gate_logs(gc_ref[...], gr_ref[...], bgr_ref[...], bgc_ref[...])
    v_m = vm_ref[...]
    o_m = om_ref[...]
    for h in range(M_HEADS):
        q = qk[:, h * M_DK:(h + 1) * M_DK] * (M_DK ** -0.5)
        k = qk[:, QK_W // 2 + h * M_DK:QK_W // 2 + (h + 1) * M_DK]
        v = v_m[:, h * M_DV:(h + 1) * M_DV]
        hh, c_new, n_new, m_new = _mlstm_head(
            q, k, v, b_c[:, h:h + 1], b_r[h:h + 1, :], ig_c[:, h:h + 1], ig_r[h:h + 1, :], causal,
            c_st[h], n_st[h:h + 1, 0:M_DK], m_st[0:1, h:h + 1])
        c_st[h] = c_new
        n_st[h:h + 1, 0:M_DK] = n_new
        m_st[0:1, h:h + 1] = m_new
        ym_ref[:, h * M_DV:(h + 1) * M_DV] = _mlstm_out(
            hh, o_m[:, h * M_DV:(h + 1) * M_DV], mn_ref[:, h * M_DV:(h + 1) * M_DV])

    t = lax.broadcasted_iota(jnp.int32, (L, L), 0)
    s = lax.broadcasted_iota(jnp.int32, (L, L), 1)
    dist = (t - s).astype(F32)
    valid_prev = (s >= t) & (ci > 0)
    valid_cur = s <= t
    q_a = qa_ref[...]
    k_c, v_c = ka_ref[...], va_ref[...]
    k_p, v_p = kp_ref[...], vp_ref[...]
    for kv in range(A_KV):
        lanes = slice(kv * A_HD, (kv + 1) * A_HD)
        kpb, kcb = k_p[:, lanes].astype(BF16), k_c[:, lanes].astype(BF16)
        vpb, vcb = v_p[:, lanes].astype(BF16), v_c[:, lanes].astype(BF16)
        for g in range(A_GROUP):
            hd = kv * A_GROUP + g
            qh = q_a[:, hd * A_HD:(hd + 1) * A_HD].astype(BF16)
            sc_p = _dot_nt(qh, kpb) * (A_HD ** -0.5) - SLOPES[hd] * (dist + float(WINDOW))
            sc_c = _dot_nt(qh, kcb) * (A_HD ** -0.5) - SLOPES[hd] * dist
            sc_p = jnp.where(valid_prev, sc_p, NEG_INF)
            sc_c = jnp.where(valid_cur, sc_c, NEG_INF)
            p_p, p_c = _sink_softmax([sc_p, sc_c], sink_ref[0:1, hd:hd + 1])
            att_ref[:, hd * A_HD:(hd + 1) * A_HD] = _dot(p_p.astype(BF16), vpb) + _dot(p_c.astype(BF16), vcb)
    kp_ref[...] = k_c
    vp_ref[...] = v_c

    @pl.when(ci == pl.num_programs(1) - 1)
    def _():
        c_out[0] = c_st[...]
        n_out[0] = n_st[0:M_HEADS, 0:M_DK]
        m_out[0] = m_st[0:1, 0:M_HEADS]


def _mixer_prompt(pieces, bg_row, bg_col, conv_w, mlstm_norm, sinks, batch, seq):
    qk, vm, om, qa, ka, va, gc, gr = pieces
    nc = seq // CHUNK
    row = lambda w: pl.BlockSpec((CHUNK, w), lambda b, c: (b * nc + c, 0))
    full = lambda a: pl.BlockSpec(a.shape, lambda b, c: (0,) * a.ndim)
    n = batch * seq
    return pl.pallas_call(
        _mixer_prompt_kernel,
        grid=(batch, nc),
        in_specs=[row(QK_W), row(M_WIDTH), row(M_WIDTH), row(A_WIDTH), row(KV_W), row(KV_W), row(N_GATES),
                  pl.BlockSpec((N_GATES, CHUNK), lambda b, c: (0, b * nc + c)),
                  full(bg_row), full(bg_col), full(conv_w), full(mlstm_norm), full(sinks)],
        out_specs=[row(M_WIDTH), row(A_WIDTH),
                   pl.BlockSpec((1, M_HEADS, M_DK, M_DV), lambda b, c: (b, 0, 0, 0)),
                   pl.BlockSpec((1, M_HEADS, M_DK), lambda b, c: (b, 0, 0)),
                   pl.BlockSpec((1, 1, M_HEADS), lambda b, c: (b, 0, 0))],
        out_shape=[jax.ShapeDtypeStruct((n, M_WIDTH), F32), jax.ShapeDtypeStruct((n, A_WIDTH), F32),
                   jax.ShapeDtypeStruct((batch, M_HEADS, M_DK, M_DV), F32),
                   jax.ShapeDtypeStruct((batch, M_HEADS, M_DK), F32),
                   jax.ShapeDtypeStruct((batch, 1, M_HEADS), F32)],
        scratch_shapes=[pltpu.VMEM((8 + CHUNK, QK_W), F32),
                        pltpu.VMEM((M_HEADS, M_DK, M_DV), F32),
                        pltpu.VMEM((8, 128), F32), pltpu.VMEM((8, 128), F32),
                        pltpu.VMEM((CHUNK, KV_W), F32), pltpu.VMEM((CHUNK, KV_W), F32)],
        compiler_params=pltpu.CompilerParams(dimension_semantics=("arbitrary", "arbitrary"),
                                             vmem_limit_bytes=VMEM_LIMIT_BYTES),
        name="mixer_prompt",
    )(qk, vm, om, qa, ka, va, gc, gr, bg_row, bg_col, conv_w, mlstm_norm, sinks)


def _mixer_sample_kernel(qk_ref, vm_ref, om_ref, qs_ref, kn_ref, vn_ref, gc_ref, gr_ref,
                         c0_ref, n0_ref, m0_ref, cb_ref, kc_ref, vc_ref,
                         bgr_ref, bgc_ref, cw_ref, mn_ref, sinkr_ref, sloper_ref,
                         ym_ref, att_ref, c_out, n_out, m_out, cb_out, k_out, v_out,
                         up_ref):
    L = qk_ref.shape[1]
    up_ref[0:CONV_W - 1, :] = cb_ref[0]
    up_ref[CONV_W - 1:CONV_W - 1 + L, :] = qk_ref[0]
    qk = _conv_silu(up_ref, 0, cw_ref[...], L)
    cb_out[0] = up_ref[L:L + CONV_W - 1, :]

    ig_c, ig_r, b_c, b_r, causal = _gate_logs(gc_ref[0], gr_ref[0], bgr_ref[...], bgc_ref[...])
    v_m = vm_ref[0]
    o_m = om_ref[0]
    for h in range(M_HEADS):
        q = qk[:, h * M_DK:(h + 1) * M_DK] * (M_DK ** -0.5)
        k = qk[:, QK_W // 2 + h * M_DK:QK_W // 2 + (h + 1) * M_DK]
        v = v_m[:, h * M_DV:(h + 1) * M_DV]
        hh, c_new, n_new, m_new = _mlstm_head(
            q, k, v, b_c[:, h:h + 1], b_r[h:h + 1, :], ig_c[:, h:h + 1], ig_r[h:h + 1, :], causal,
            c0_ref[0, h], n0_ref[0, h:h + 1, :], m0_ref[0, 0:1, h:h + 1])
        c_out[0, h] = c_new
        n_out[0, h:h + 1, :] = n_new
        m_out[0, 0:1, h:h + 1] = m_new
        ym_ref[0, :, h * M_DV:(h + 1) * M_DV] = _mlstm_out(
            hh, o_m[:, h * M_DV:(h + 1) * M_DV], mn_ref[:, h * M_DV:(h + 1) * M_DV])

    rows = A_GROUP * L
    k_c, v_c = kc_ref[0], vc_ref[0]
    k_n, v_n = kn_ref[0], vn_ref[0]
    def token_of_row(shape):
        r = lax.broadcasted_iota(jnp.int32, shape, 0).astype(F32)
        return r - L * jnp.floor((r + 0.5) / L)

    ti = token_of_row((rows, WINDOW))
    ji = lax.broadcasted_iota(jnp.int32, (rows, WINDOW), 1).astype(F32)
    dist_c = ti + WINDOW - ji
    valid_c = ji >= ti
    tn = token_of_row((rows, L))
    sn = lax.broadcasted_iota(jnp.int32, (rows, L), 1).astype(F32)
    dist_n = tn - sn
    valid_n = sn <= tn
    for kv in range(A_KV):
        lanes = slice(kv * A_HD, (kv + 1) * A_HD)
        qh = qs_ref[0, kv].astype(BF16)
        slope = sloper_ref[kv]
        sc_c = _dot_nt(qh, k_c[:, lanes].astype(BF16)) * (A_HD ** -0.5) - slope * dist_c
        sc_n = _dot_nt(qh, k_n[:, lanes].astype(BF16)) * (A_HD ** -0.5) - slope * dist_n
        sc_c = jnp.where(valid_c, sc_c, NEG_INF)
        sc_n = jnp.where(valid_n, sc_n, NEG_INF)
        p_c, p_n = _sink_softmax([sc_c, sc_n], sinkr_ref[kv])
        att_ref[0, kv] = (_dot(p_c.astype(BF16), v_c[:, lanes].astype(BF16))
                          + _dot(p_n.astype(BF16), v_n[:, lanes].astype(BF16)))

    k_out[0, 0:WINDOW - L, :] = k_c[L:WINDOW, :]
    k_out[0, WINDOW - L:WINDOW, :] = k_n
    v_out[0, 0:WINDOW - L, :] = v_c[L:WINDOW, :]
    v_out[0, WINDOW - L:WINDOW, :] = v_n


def _mixer_sample(qk, vm, om, qs, kn, vn, gc, gr, c0, n0, m0, cb, kc, vc,
                  bg_row, bg_col, conv_w, mlstm_norm, sink_rows, slope_rows):
    nb, L = qk.shape[0], qk.shape[1]
    per_b = lambda a: pl.BlockSpec((1,) + a.shape[1:], lambda b: (b,) + (0,) * (a.ndim - 1))
    full = lambda a: pl.BlockSpec(a.shape, lambda b: (0,) * a.ndim)
    ins = [qk, vm, om, qs, kn, vn, gc, gr, c0, n0, m0, cb, kc, vc]
    consts = [bg_row, bg_col, conv_w, mlstm_norm, sink_rows, slope_rows]
    outs = [jax.ShapeDtypeStruct((nb, L, M_WIDTH), F32),
            jax.ShapeDtypeStruct((nb, A_KV, A_GROUP * L, A_HD), F32),
            jax.ShapeDtypeStruct(c0.shape, F32), jax.ShapeDtypeStruct(n0.shape, F32),
            jax.ShapeDtypeStruct(m0.shape, F32), jax.ShapeDtypeStruct(cb.shape, F32),
            jax.ShapeDtypeStruct(kc.shape, F32), jax.ShapeDtypeStruct(vc.shape, F32)]
    return pl.pallas_call(
        _mixer_sample_kernel,
        grid=(nb,),
        in_specs=[per_b(a) for a in ins] + [full(a) for a in consts],
        out_specs=[per_b(o) for o in outs],
        out_shape=outs,
        scratch_shapes=[pltpu.VMEM((8, QK_W), F32)],
        compiler_params=pltpu.CompilerParams(dimension_semantics=("arbitrary",),
                                             vmem_limit_bytes=VMEM_LIMIT_BYTES),
        name="mixer_sample",
    )(*ins, *consts)


def _post_kernel(x_ref, ym_ref, att_ref, p_ref, an_ref, g_post_ref, g_fpre_ref, g_fpost_ref,
                 wout_ref, wup_ref, wdown_ref, wpg_ref, wpp_ref, o_ref):
    y_a = _rms(att_ref[...], an_ref[...])
    y = jnp.concatenate([ym_ref[...], y_a], axis=-1).astype(BF16)
    x1 = x_ref[...] + _rms(_dot(y, wout_ref[...]), g_post_ref[...])
    u = _rms(x1, g_fpre_ref[...]).astype(BF16)
    f = jnp.zeros_like(x1)
    for j in range(D_FF // FF_TILE):
        cols = slice(j * FF_TILE, (j + 1) * FF_TILE)
        hid = jnp.square(jnp.maximum(_dot(u, wup_ref[:, cols]), 0.0)).astype(BF16)
        f = f + _dot(hid, wdown_ref[cols, :])
    x2 = x1 + _rms(f, g_fpost_ref[...])
    gate = jax.nn.sigmoid(_dot(x2.astype(BF16), wpg_ref[...]))
    o_ref[...] = x2 + gate * _dot(p_ref[...].astype(BF16), wpp_ref[...])


def _post(x2d, ym, att, p2d, gains, weights):
    n = x2d.shape[0]
    tm = min(ROW_TILE, n)
    row = lambda w: pl.BlockSpec((tm, w), lambda i: (i, 0))
    full = lambda a: pl.BlockSpec(a.shape, lambda i: (0,) * a.ndim)
    resident = lambda a: pl.BlockSpec(a.shape, lambda i: (0,) * a.ndim, pipeline_mode=pl.Buffered(1))
    return pl.pallas_call(
        _post_kernel,
        grid=(n // tm,),
        in_specs=[row(D_MODEL), row(M_WIDTH), row(A_WIDTH), row(P_DIM)]
        + [full(g) for g in gains] + [resident(w) for w in weights],
        out_specs=row(D_MODEL),
        out_shape=jax.ShapeDtypeStruct((n, D_MODEL), F32),
        compiler_params=pltpu.CompilerParams(dimension_semantics=("arbitrary",),
                                             vmem_limit_bytes=VMEM_LIMIT_BYTES),
        name="post",
    )(x2d, ym, att, p2d, *gains, *weights)


def kernel(x_prompt, x_sample, p_prompt, p_sample, state_mlstm_c, state_mlstm_n, state_mlstm_m,
           state_mlstm_conv, cache_swa_k, cache_swa_v, norm_mix_pre, w_in, b_gates, conv_w,
           mlstm_norm, attn_sinks, attn_norm, w_out, norm_mix_post, norm_ffn_pre, w_up, w_down,
           norm_ffn_post, w_pgate, w_pproj):
    depth = w_in.shape[0]
    assert depth == 1, "single-layer decoder"
    B, T, _ = x_prompt.shape
    SB, ST, _ = x_sample.shape
    i = 0

    wi = w_in[i]
    o_qk, o_vm, o_om, o_g, o_qa, o_ka, o_va = np.cumsum(
        [0, QK_W, M_WIDTH, M_WIDTH, N_GATES, A_WIDTH, KV_W])
    w_main = jnp.concatenate([wi[:, o_qk:o_g], wi[:, o_qa:]], axis=1).astype(BF16)
    w_g = wi[:, o_g:o_qa].astype(BF16)
    w_gt = w_g.T
    g_pre = norm_mix_pre[i].reshape(1, D_MODEL)
    bg_row = b_gates[i].reshape(1, N_GATES)
    bg_col = b_gates[i].reshape(N_GATES, 1)
    cw = conv_w[i]
    mn = mlstm_norm[i].reshape(1, M_WIDTH)
    sinks = attn_sinks[i].reshape(1, A_HEADS)
    gains = [attn_norm[i].reshape(1, A_WIDTH), norm_mix_post[i].reshape(1, D_MODEL),
             norm_ffn_pre[i].reshape(1, D_MODEL), norm_ffn_post[i].reshape(1, D_MODEL)]
    weights = [w_out[i].astype(BF16), w_up[i].astype(BF16), w_down[i].astype(BF16),
               w_pgate[i].astype(BF16), w_pproj[i].astype(BF16)]

    xp = x_prompt.reshape(B * T, D_MODEL)
    pieces = _in_proj(xp, g_pre, w_main, w_g, w_gt)
    ym, att, c_p, n_p, m_p = _mixer_prompt(pieces, bg_row, bg_col, cw, mn, sinks, B, T)
    y_prompt = _post(xp, ym, att, p_prompt[i].reshape(B * T, P_DIM), gains, weights).reshape(B, T, D_MODEL)
    qk_p, ka_p, va_p = pieces[0], pieces[4], pieces[5]
    conv_p = qk_p.reshape(B, T, QK_W)[:, T - (CONV_W - 1):]
    k_p = ka_p.reshape(B, T, A_KV, A_HD)[:, T - WINDOW:]
    v_p = va_p.reshape(B, T, A_KV, A_HD)[:, T - WINDOW:]

    xs = x_sample.reshape(SB * ST, D_MODEL)
    qk_s, vm_s, om_s, qa_s, ka_s, va_s, gc_s, gr_s = _in_proj(xs, g_pre, w_main, w_g, w_gt)
    per_seq = lambda a: a.reshape(SB, ST, a.shape[-1])
    qs = qa_s.reshape(SB, ST, A_KV, A_GROUP, A_HD).transpose(0, 2, 3, 1, 4).reshape(SB, A_KV, A_GROUP * ST, A_HD)
    gr_seq = gr_s.reshape(N_GATES, SB, ST).transpose(1, 0, 2)
    sink_rows = jnp.repeat(attn_sinks[i].reshape(A_KV, A_GROUP), ST, axis=1).reshape(A_KV, A_GROUP * ST, 1)
    slope_rows = jnp.asarray(np.repeat(np.asarray(SLOPES, np.float32).reshape(A_KV, A_GROUP), ST, axis=1)
                             .reshape(A_KV, A_GROUP * ST, 1))
    ym_s, att_s, c_s, n_s, m_s, conv_s, k_s, v_s = _mixer_sample(
        per_seq(qk_s), per_seq(vm_s), per_seq(om_s), qs, per_seq(ka_s), per_seq(va_s), per_seq(gc_s), gr_seq,
        state_mlstm_c[i], state_mlstm_n[i], state_mlstm_m[i].reshape(SB, 1, M_HEADS), state_mlstm_conv[i],
        cache_swa_k[i].reshape(SB, WINDOW, KV_W), cache_swa_v[i].reshape(SB, WINDOW, KV_W),
        bg_row, bg_col, cw, mn, sink_rows, slope_rows)
    att_s2 = att_s.reshape(SB, A_KV, A_GROUP, ST, A_HD).transpose(0, 3, 1, 2, 4).reshape(SB * ST, A_WIDTH)
    y_sample = _post(xs, ym_s.reshape(SB * ST, M_WIDTH), att_s2, p_sample[i].reshape(SB * ST, P_DIM),
                     gains, weights).reshape(SB, ST, D_MODEL)

    stack = lambda a: a[None]
    return (y_prompt, y_sample,
            stack(c_p), stack(n_p), stack(m_p.reshape(B, M_HEADS)), stack(conv_p), stack(k_p), stack(v_p),
            stack(c_s), stack(n_s), stack(m_s.reshape(SB, M_HEADS)), stack(conv_s),
            stack(k_s.reshape(SB, WINDOW, A_KV, A_HD)), stack(v_s.reshape(SB, WINDOW, A_KV, A_HD)))
```

```python
import numpy as np
import jax
import jax.numpy as jnp
from jax import lax
from jax.experimental import pallas as pl
from jax.experimental.pallas import tpu as pltpu

F32 = jnp.float32
BF16 = jnp.bfloat16

D_MODEL = 1024
M_WIDTH = 512
M_HEADS = 4
M_DV = 128
M_DK = 64
QK_W = 512
CONV_W = 4
CHUNK = 128
A_WIDTH = 512
A_HEADS = 8
A_HD = 64
A_KV = 2
A_GROUP = 4
KV_W = 128
WINDOW = 128
D_FF = 4096
P_DIM = 256
EPS = 1e-6
N_GATES = 2 * M_HEADS

VMEM_LIMIT_BYTES = 56 * 1024 * 1024
ROW_TILE = 512
FF_TILE = 1024
SAMPLE_SEQ_TILE = 8

NEG_INF = float("-inf")
SLOPES = [2.0 ** (-8.0 * (h + 1) / A_HEADS) for h in range(A_HEADS)]
HEAD_ORDER = [kv * A_GROUP + g for g in range(A_GROUP) for kv in range(A_KV)]


def _dot(a, b):
    return jnp.dot(a, b, preferred_element_type=F32)


def _dot_nt(a, b):
    return lax.dot_general(a, b, (((1,), (1,)), ((), ())), preferred_element_type=F32)


def _dot_tn(a, b):
    return lax.dot_general(a, b, (((0,), (0,)), ((), ())), preferred_element_type=F32)


def _dot_exact(a, b):
    return lax.dot_general(a, b, (((1,), (0,)), ((), ())), precision=lax.Precision.HIGHEST,
                           preferred_element_type=F32)


def _rms(x, g):
    return x * lax.rsqrt(jnp.mean(x * x, axis=-1, keepdims=True) + EPS) * g


def _in_proj_kernel(x_ref, g_ref, w_ref, wg_ref, wgt_ref,
                    qk_ref, vm_ref, om_ref, qa_ref, ka_ref, va_ref, gc_ref, gr_ref):
    xn = _rms(x_ref[...], g_ref[...]).astype(BF16)
    z = _dot(xn, w_ref[...])
    qk_ref[...] = z[:, 0:512]
    vm_ref[...] = z[:, 512:1024]
    om_ref[...] = z[:, 1024:1536]
    qa_ref[...] = z[:, 1536:2048]
    ka_ref[...] = z[:, 2048:2176]
    va_ref[...] = z[:, 2176:2304]
    gc_ref[...] = _dot(xn, wg_ref[...])
    gr_ref[...] = _dot_nt(wgt_ref[...], xn)


def _in_proj(x2d, g_pre, w_main, w_g, w_gt):
    n = x2d.shape[0]
    tm = min(ROW_TILE, n)
    row = lambda w: pl.BlockSpec((tm, w), lambda i: (i, 0))
    full = lambda a: pl.BlockSpec(a.shape, lambda i: (0,) * a.ndim)
    out_widths = (QK_W, M_WIDTH, M_WIDTH, A_WIDTH, KV_W, KV_W, N_GATES)
    return pl.pallas_call(
        _in_proj_kernel,
        grid=(n // tm,),
        in_specs=[row(D_MODEL), full(g_pre), full(w_main), full(w_g), full(w_gt)],
        out_specs=[row(w) for w in out_widths] + [pl.BlockSpec((N_GATES, tm), lambda i: (0, i))],
        out_shape=[jax.ShapeDtypeStruct((n, w), F32) for w in out_widths]
        + [jax.ShapeDtypeStruct((N_GATES, n), F32)],
        compiler_params=pltpu.CompilerParams(dimension_semantics=("arbitrary",),
                                             vmem_limit_bytes=VMEM_LIMIT_BYTES),
        name="in_proj",
    )(x2d, g_pre, w_main, w_g, w_gt)


def _mlstm_out(h, o, g):
    hn = h * lax.rsqrt(jnp.mean(h * h, axis=-1, keepdims=True) + EPS) * g
    return jax.nn.sigmoid(o) * hn


def _conv_silu(up_ref, base, w, L):
    out = w[0:1, :] * up_ref[base:base + L, :]
    for j in range(1, CONV_W):
        out = out + w[j:j + 1, :] * up_ref[base + j:base + j + L, :]
    return jax.nn.silu(out)


def _mixer_prompt_kernel(qk_ref, vm_ref, om_ref, qa_ref, ka_ref, va_ref, gc_ref, gr_ref,
                         bgr_ref, bgc_ref, cw_ref, mn_ref, sink_ref,
                         ym_ref, att_ref, c_out, m_out,
                         up_ref, c_st, m_st, kp_ref, vp_ref, bias_ref, sinkrep_ref):
    ci = pl.program_id(1)
    L = CHUNK
    t = lax.broadcasted_iota(jnp.int32, (L, L), 0)
    s = lax.broadcasted_iota(jnp.int32, (L, L), 1)

    @pl.when(ci == 0)
    def _():
        up_ref[...] = jnp.zeros_like(up_ref)
        c_st[...] = jnp.zeros_like(c_st)
        m_st[...] = jnp.zeros_like(m_st)
        kp_ref[...] = jnp.zeros_like(kp_ref)
        vp_ref[...] = jnp.zeros_like(vp_ref)
        dist = (t - s).astype(F32)
        for kv in range(A_KV):
            for g in range(A_GROUP):
                hd = kv * A_GROUP + g
                rows = slice(g * L, (g + 1) * L)
                bias_ref[kv, rows, 0:L] = jnp.full((L, L), NEG_INF, F32)
                bias_ref[kv, rows, L:2 * L] = jnp.where(s <= t, -SLOPES[hd] * dist, NEG_INF)
                sinkrep_ref[kv, rows, :] = jnp.broadcast_to(sink_ref[0:1, hd:hd + 1], (L, KV_W))

    @pl.when(ci == 1)
    def _():
        dist = (t - s).astype(F32) + float(WINDOW)
        for kv in range(A_KV):
            for g in range(A_GROUP):
                hd = kv * A_GROUP + g
                bias_ref[kv, g * L:(g + 1) * L, 0:L] = jnp.where(s >= t, -SLOPES[hd] * dist, NEG_INF)

    up_ref[8:8 + L, :] = qk_ref[...]
    qk = _conv_silu(up_ref, 5, cw_ref[...], L)
    up_ref[5:8, :] = qk_ref[L - 3:L, :]

    gcb = gc_ref[...] + bgr_ref[...]
    grb = gr_ref[...] + bgc_ref[...]
    ig_c, lf_c = gcb[:, 0:M_HEADS], jax.nn.log_sigmoid(gcb[:, M_HEADS:N_GATES])
    ig_r, lf_r = grb[0:M_HEADS, :], jax.nn.log_sigmoid(grb[M_HEADS:N_GATES, :])
    b_c = _dot_exact((s <= t).astype(F32), lf_c)
    b_r = _dot_exact(lf_r, (t <= s).astype(F32))
    a_r = ig_r - b_r
    causal_bias = jnp.where(s <= t, 0.0, NEG_INF)
    m_prev = m_st[0:1, 0:M_HEADS]
    hcol = lax.broadcasted_iota(jnp.int32, (L, M_HEADS), 1)
    ones_v = jnp.ones((L, M_DV), BF16)
    H = range(M_HEADS)
    qb = [(qk[:, h * M_DK:(h + 1) * M_DK] * (M_DK ** -0.5)).astype(BF16) for h in H]
    kf = [qk[:, QK_W // 2 + h * M_DK:QK_W // 2 + (h + 1) * M_DK] for h in H]
    kb = [k.astype(BF16) for k in kf]
    vext = [jnp.concatenate([vm_ref[:, h * M_DV:(h + 1) * M_DV].astype(BF16), ones_v], axis=1) for h in H]
    c_prev = [c_st[h] for h in H]
    qk_h = [_dot_nt(qb[h], kb[h]) for h in H]
    qc_h = [_dot(qb[h], c_prev[h].astype(BF16)) for h in H]
    log_d = [b_c[:, h:h + 1] + a_r[h:h + 1, :] + causal_bias for h in H]
    m_intra = jnp.full((L, M_HEADS), NEG_INF, F32)
    for h in H:
        m_intra = jnp.where(hcol == h, jnp.max(log_d[h], axis=-1, keepdims=True), m_intra)
    log_inter = b_c + m_prev
    m_t = jnp.maximum(log_inter, m_intra)
    w_inter = jnp.exp(log_inter - m_t)
    emt = jnp.exp(-m_t)
    s_h = [(qk_h[h] * jnp.exp(log_d[h] - m_t[:, h:h + 1])).astype(BF16) for h in H]
    o_h = [_dot(s_h[h], vext[h]) for h in H]
    for h in H:
        wi = w_inter[:, h:h + 1]
        num = o_h[h][:, 0:M_DV] + wi * qc_h[h][:, 0:M_DV]
        den = o_h[h][:, M_DV:2 * M_DV] + wi * qc_h[h][:, M_DV:2 * M_DV]
        hh = num / jnp.maximum(jnp.abs(den), emt[:, h:h + 1])
        cols = slice(h * M_DV, (h + 1) * M_DV)
        ym_ref[:, cols] = _mlstm_out(hh, om_ref[:, cols], mn_ref[:, cols])
    b_last = b_c[L - 1:L, :]
    log_w = b_last - b_c + ig_c
    m_new = jnp.maximum(b_last + m_prev, jnp.max(log_w, axis=0, keepdims=True))
    w_k = jnp.exp(log_w - m_new)
    decay = jnp.exp(b_last + m_prev - m_new)
    kw = [(kf[h] * w_k[:, h:h + 1]).astype(BF16) for h in H]
    dc = [_dot_tn(kw[h], vext[h]) for h in H]
    for h in H:
        c_st[h] = decay[:, h:h + 1] * c_prev[h] + dc[h]
    m_st[0:1, 0:M_HEADS] = m_new

    lane = lax.broadcasted_iota(jnp.int32, (2 * L, KV_W), 1)
    low2 = lane < A_HD
    low = lax.broadcasted_iota(jnp.int32, (L, KV_W), 1) < A_HD
    k_both = jnp.concatenate([kp_ref[...], ka_ref[...]], axis=0)
    v_both = jnp.concatenate([vp_ref[...], va_ref[...]], axis=0)
    ones_k = jnp.ones((2 * L, KV_W), BF16)
    kmask = [jnp.where(low2, k_both, 0.0).astype(BF16), jnp.where(low2, 0.0, k_both).astype(BF16)]
    vext_a = [jnp.concatenate([jnp.where(low2, v_both, 1.0).astype(BF16), ones_k], axis=1),
              jnp.concatenate([jnp.where(low2, 1.0, v_both).astype(BF16), ones_k], axis=1)]
    qst = jnp.concatenate([qa_ref[:, g * KV_W:(g + 1) * KV_W] for g in range(A_GROUP)], axis=0)
    qst = (qst * (A_HD ** -0.5)).astype(BF16)
    KV = range(A_KV)
    sc = [_dot_nt(qst, kmask[kv]) + bias_ref[kv] for kv in KV]
    m_a = [jnp.maximum(jnp.max(jnp.maximum(sc[kv][:, 0:L], sc[kv][:, L:2 * L]), axis=-1, keepdims=True),
                       sinkrep_ref[kv]) for kv in KV]
    e = [jnp.concatenate([jnp.exp(sc[kv][:, 0:L] - m_a[kv]), jnp.exp(sc[kv][:, L:2 * L] - m_a[kv])],
                         axis=1).astype(BF16) for kv in KV]
    o = [_dot(e[kv], vext_a[kv]) for kv in KV]
    exs = [jnp.exp(sinkrep_ref[kv] - m_a[kv]) for kv in KV]
    for g in range(A_GROUP):
        rows = slice(g * L, (g + 1) * L)
        pv = jnp.where(low, o[0][rows, 0:KV_W], o[1][rows, 0:KV_W])
        den = jnp.where(low, o[0][rows, KV_W:2 * KV_W] + exs[0][rows, :],
                        o[1][rows, KV_W:2 * KV_W] + exs[1][rows, :])
        att_ref[:, g * KV_W:(g + 1) * KV_W] = pv / den
    kp_ref[...] = ka_ref[...]
    vp_ref[...] = va_ref[...]

    @pl.when(ci == pl.num_programs(1) - 1)
    def _():
        c_out[0] = c_st[...]
        m_out[0] = m_st[0:1, 0:M_HEADS]


def _mixer_prompt(pieces, bg_row, bg_col, conv_w, mlstm_norm, sinks, batch, seq):
    qk, vm, om, qa, ka, va, gc, gr = pieces
    nc = seq // CHUNK
    row = lambda w: pl.BlockSpec((CHUNK, w), lambda b, c: (b * nc + c, 0))
    full = lambda a: pl.BlockSpec(a.shape, lambda b, c: (0,) * a.ndim)
    n = batch * seq
    return pl.pallas_call(
        _mixer_prompt_kernel,
        grid=(batch, nc),
        in_specs=[row(QK_W), row(M_WIDTH), row(M_WIDTH), row(A_WIDTH), row(KV_W), row(KV_W), row(N_GATES),
                  pl.BlockSpec((N_GATES, CHUNK), lambda b, c: (0, b * nc + c)),
                  full(bg_row), full(bg_col), full(conv_w), full(mlstm_norm), full(sinks)],
        out_specs=[row(M_WIDTH), row(A_WIDTH),
                   pl.BlockSpec((1, M_HEADS, M_DK, 2 * M_DV), lambda b, c: (b, 0, 0, 0)),
                   pl.BlockSpec((1, 1, M_HEADS), lambda b, c: (b, 0, 0))],
        out_shape=[jax.ShapeDtypeStruct((n, M_WIDTH), F32), jax.ShapeDtypeStruct((n, A_WIDTH), F32),
                   jax.ShapeDtypeStruct((batch, M_HEADS, M_DK, 2 * M_DV), F32),
                   jax.ShapeDtypeStruct((batch, 1, M_HEADS), F32)],
        scratch_shapes=[pltpu.VMEM((8 + CHUNK, QK_W), F32),
                        pltpu.VMEM((M_HEADS, M_DK, 2 * M_DV), F32),
                        pltpu.VMEM((8, 128), F32),
                        pltpu.VMEM((CHUNK, KV_W), F32), pltpu.VMEM((CHUNK, KV_W), F32),
                        pltpu.VMEM((A_KV, A_GROUP * CHUNK, 2 * CHUNK), F32),
                        pltpu.VMEM((A_KV, A_GROUP * CHUNK, KV_W), F32)],
        compiler_params=pltpu.CompilerParams(dimension_semantics=("arbitrary", "arbitrary"),
                                             vmem_limit_bytes=VMEM_LIMIT_BYTES),
        name="mixer_prompt",
    )(qk, vm, om, qa, ka, va, gc, gr, bg_row, bg_col, conv_w, mlstm_norm, sinks)


def _mixer_sample(qk, vm, om, qs, kn, vn, gc, gr, c0, n0, m0, cb, kc, vc,
                  bg_row, bg_col, conv_w, mlstm_norm, sink_rows, slope_rows):
    nb, L = qk.shape[0], qk.shape[1]
    ts = min(SAMPLE_SEQ_TILE, nb)
    per_b = lambda a: pl.BlockSpec((ts,) + a.shape[1:], lambda b: (b,) + (0,) * (a.ndim - 1))
    full = lambda a: pl.BlockSpec(a.shape, lambda b: (0,) * a.ndim)
    ins = [qk, vm, om, qs, kn, vn, gc, gr, c0, n0, m0, cb, kc, vc]
    consts = [bg_row, bg_col, conv_w, mlstm_norm, sink_rows, slope_rows]
    outs = [jax.ShapeDtypeStruct((nb, L, M_WIDTH), F32),
            jax.ShapeDtypeStruct((nb, A_KV, A_GROUP * L, A_HD), F32),
            jax.ShapeDtypeStruct(c0.shape, F32), jax.ShapeDtypeStruct(n0.shape, F32),
            jax.ShapeDtypeStruct(m0.shape, F32), jax.ShapeDtypeStruct(cb.shape, F32),
            jax.ShapeDtypeStruct(kc.shape, F32), jax.ShapeDtypeStruct(vc.shape, F32)]
    return pl.pallas_call(
        _mixer_sample_tile_kernel,
        grid=(nb // ts,),
        in_specs=[per_b(a) for a in ins] + [full(a) for a in consts],
        out_specs=[per_b(o) for o in outs],
        out_shape=outs,
        scratch_shapes=[pltpu.VMEM((ts, 8, QK_W), F32)],
        compiler_params=pltpu.CompilerParams(dimension_semantics=("arbitrary",),
                                             vmem_limit_bytes=VMEM_LIMIT_BYTES),
        name="mixer_sample",
    )(*ins, *consts)


def _small_cumsum(x, axis):
    idx = lax.broadcasted_iota(jnp.int32, x.shape, axis)
    out = jnp.zeros_like(x)
    for r in range(x.shape[axis]):
        out = out + jnp.where(idx >= r, lax.slice_in_dim(x, r, r + 1, axis=axis), 0.0)
    return out


def _mixer_sample_tile_kernel(qk_ref, vm_ref, om_ref, qs_ref, kn_ref, vn_ref, gc_ref, gr_ref,
                              c0_ref, n0_ref, m0_ref, cb_ref, kc_ref, vc_ref,
                              bgr_ref, bgc_ref, cw_ref, mn_ref, sinkr_ref, sloper_ref,
                              ym_ref, att_ref, c_out, n_out, m_out, cb_out, k_out, v_out,
                              up_ref):
    NB, L = qk_ref.shape[0], qk_ref.shape[1]
    SEQ = range(NB)
    H = range(M_HEADS)
    cw = cw_ref[...]

    for b in SEQ:
        up_ref[b, 0:CONV_W - 1, :] = cb_ref[b]
        up_ref[b, CONV_W - 1:CONV_W - 1 + L, :] = qk_ref[b]
    qk = []
    for b in SEQ:
        acc = cw[0:1, :] * up_ref[b, 0:L, :]
        for j in range(1, CONV_W):
            acc = acc + cw[j:j + 1, :] * up_ref[b, j:j + L, :]
        qk.append(jax.nn.silu(acc))
        cb_out[b] = up_ref[b, L:L + CONV_W - 1, :]

    r = lax.broadcasted_iota(jnp.int32, (L, L), 0)
    c = lax.broadcasted_iota(jnp.int32, (L, L), 1)
    causal = c <= r
    gcb = [gc_ref[b] + bgr_ref[...] for b in SEQ]
    grb = [gr_ref[b] + bgc_ref[...] for b in SEQ]
    ig_c = [g[:, 0:M_HEADS] for g in gcb]
    ig_r = [g[0:M_HEADS, :] for g in grb]
    b_c = [_small_cumsum(jax.nn.log_sigmoid(g[:, M_HEADS:N_GATES]), 0) for g in gcb]
    b_r = [_small_cumsum(jax.nn.log_sigmoid(g[M_HEADS:N_GATES, :]), 1) for g in grb]

    P = [(b, h) for b in SEQ for h in H]
    q = [qk[b][:, h * M_DK:(h + 1) * M_DK] * (M_DK ** -0.5) for b, h in P]
    k = [qk[b][:, QK_W // 2 + h * M_DK:QK_W // 2 + (h + 1) * M_DK] for b, h in P]
    qb = [x.astype(BF16) for x in q]
    kb = [x.astype(BF16) for x in k]
    vb = [vm_ref[b, :, h * M_DV:(h + 1) * M_DV].astype(BF16) for b, h in P]
    c_prev = [c0_ref[b, h] for b, h in P]
    n_prev = [n0_ref[b, h:h + 1, :] for b, h in P]
    m_prev = [m0_ref[b, 0:1, h:h + 1] for b, h in P]
    bc = [b_c[b][:, h:h + 1] for b, h in P]
    igc = [ig_c[b][:, h:h + 1] for b, h in P]
    log_d = [jnp.where(causal, b_c[b][:, h:h + 1] - b_r[b][h:h + 1, :] + ig_r[b][h:h + 1, :], NEG_INF)
             for b, h in P]
    N = range(len(P))
    qkt = [_dot_nt(qb[i], kb[i]) for i in N]
    qc = [_dot(qb[i], c_prev[i].astype(BF16)) for i in N]
    log_inter = [bc[i] + m_prev[i] for i in N]
    m_t = [jnp.maximum(log_inter[i], jnp.max(log_d[i], axis=-1, keepdims=True)) for i in N]
    w_inter = [jnp.exp(log_inter[i] - m_t[i]) for i in N]
    s = [qkt[i] * jnp.exp(log_d[i] - m_t[i]) for i in N]
    sv = [_dot(s[i].astype(BF16), vb[i]) for i in N]
    qn = [jnp.sum(q[i] * n_prev[i], axis=-1, keepdims=True) for i in N]
    for i, (b, h) in enumerate(P):
        num = sv[i] + w_inter[i] * qc[i]
        den = jnp.sum(s[i], axis=-1, keepdims=True) + w_inter[i] * qn[i]
        hh = num / jnp.maximum(jnp.abs(den), jnp.exp(-m_t[i]))
        cols = slice(h * M_DV, (h + 1) * M_DV)
        ym_ref[b, :, cols] = _mlstm_out(hh, om_ref[b, :, cols], mn_ref[:, cols])
    b_last = [bc[i][L - 1:L, :] for i in N]
    log_w = [b_last[i] - bc[i] + igc[i] for i in N]
    m_new = [jnp.maximum(b_last[i] + m_prev[i], jnp.max(log_w[i], axis=0, keepdims=True)) for i in N]
    kw = [k[i] * jnp.exp(log_w[i] - m_new[i]) for i in N]
    decay = [jnp.exp(b_last[i] + m_prev[i] - m_new[i]) for i in N]
    dc = [_dot_tn(kw[i].astype(BF16), vb[i]) for i in N]
    for i, (b, h) in enumerate(P):
        c_out[b, h] = decay[i] * c_prev[i] + dc[i]
        n_out[b, h:h + 1, :] = decay[i] * n_prev[i] + jnp.sum(kw[i], axis=0, keepdims=True)
        m_out[b, 0:1, h:h + 1] = m_new[i]

    rows = A_GROUP * L

    def token_of_row(shape):
        ri = lax.broadcasted_iota(jnp.int32, shape, 0).astype(F32)
        return ri - L * jnp.floor((ri + 0.5) / L)

    ti = token_of_row((rows, WINDOW))
    ji = lax.broadcasted_iota(jnp.int32, (rows, WINDOW), 1).astype(F32)
    tn = token_of_row((rows, L))
    sn = lax.broadcasted_iota(jnp.int32, (rows, L), 1).astype(F32)
    bias_c = [jnp.where(ji >= ti, -sloper_ref[kv] * (ti + WINDOW - ji), NEG_INF) for kv in range(A_KV)]
    bias_n = [jnp.where(sn <= tn, -sloper_ref[kv] * (tn - sn), NEG_INF) for kv in range(A_KV)]
    Q = [(b, kv) for b in SEQ for kv in range(A_KV)]
    M = range(len(Q))
    qh = [(qs_ref[b, kv] * (A_HD ** -0.5)).astype(BF16) for b, kv in Q]
    kcb = [kc_ref[b, :, kv * A_HD:(kv + 1) * A_HD].astype(BF16) for b, kv in Q]
    knb = [kn_ref[b, :, kv * A_HD:(kv + 1) * A_HD].astype(BF16) for b, kv in Q]
    vcb = [vc_ref[b, :, kv * A_HD:(kv + 1) * A_HD].astype(BF16) for b, kv in Q]
    vnb = [vn_ref[b, :, kv * A_HD:(kv + 1) * A_HD].astype(BF16) for b, kv in Q]
    sc_c = [_dot_nt(qh[i], kcb[i]) + bias_c[Q[i][1]] for i in M]
    sc_n = [_dot_nt(qh[i], knb[i]) + bias_n[Q[i][1]] for i in M]
    sink = [sinkr_ref[kv] for b, kv in Q]
    m_a = [jnp.maximum(sink[i], jnp.maximum(jnp.max(sc_c[i], axis=-1, keepdims=True),
                                            jnp.max(sc_n[i], axis=-1, keepdims=True))) for i in M]
    e_c = [jnp.exp(sc_c[i] - m_a[i]) for i in M]
    e_n = [jnp.exp(sc_n[i] - m_a[i]) for i in M]
    den_a = [jnp.exp(sink[i] - m_a[i]) + jnp.sum(e_c[i], axis=-1, keepdims=True)
             + jnp.sum(e_n[i], axis=-1, keepdims=True) for i in M]
    pv = [_dot((e_c[i] / den_a[i]).astype(BF16), vcb[i]) + _dot((e_n[i] / den_a[i]).astype(BF16), vnb[i])
          for i in M]
    for i, (b, kv) in enumerate(Q):
        att_ref[b, kv] = pv[i]

    for b in SEQ:
        k_out[b, 0:WINDOW - L, :] = kc_ref[b, L:WINDOW, :]
        k_out[b, WINDOW - L:WINDOW, :] = kn_ref[b]
        v_out[b, 0:WINDOW - L, :] = vc_ref[b, L:WINDOW, :]
        v_out[b, WINDOW - L:WINDOW, :] = vn_ref[b]


def _post_kernel(x_ref, ym_ref, att_ref, p_ref, an_ref, g_post_ref, g_fpre_ref, g_fpost_ref,
                 wout_ref, wup_ref, wdown_ref, wpg_ref, wpp_ref, o_ref):
    y_a = _rms(att_ref[...], an_ref[...])
    y = jnp.concatenate([ym_ref[...], y_a], axis=-1).astype(BF16)
    x1 = x_ref[...] + _rms(_dot(y, wout_ref[...]), g_post_ref[...])
    u = _rms(x1, g_fpre_ref[...]).astype(BF16)
    f = jnp.zeros_like(x1)
    for j in range(D_FF // FF_TILE):
        cols = slice(j * FF_TILE, (j + 1) * FF_TILE)
        hid = jnp.square(jnp.maximum(_dot(u, wup_ref[:, cols]), 0.0)).astype(BF16)
        f = f + _dot(hid, wdown_ref[cols, :])
    x2 = x1 + _rms(f, g_fpost_ref[...])
    gate = jax.nn.sigmoid(_dot(x2.astype(BF16), wpg_ref[...]))
    o_ref[...] = x2 + gate * _dot(p_ref[...].astype(BF16), wpp_ref[...])


def _post(x2d, ym, att, p2d, gains, weights):
    n = x2d.shape[0]
    tm = min(ROW_TILE, n)
    row = lambda w: pl.BlockSpec((tm, w), lambda i: (i, 0))
    full = lambda a: pl.BlockSpec(a.shape, lambda i: (0,) * a.ndim)
    resident = lambda a: pl.BlockSpec(a.shape, lambda i: (0,) * a.ndim, pipeline_mode=pl.Buffered(1))
    return pl.pallas_call(
        _post_kernel,
        grid=(n // tm,),
        in_specs=[row(D_MODEL), row(M_WIDTH), row(A_WIDTH), row(P_DIM)]
        + [full(g) for g in gains] + [resident(w) for w in weights],
        out_specs=row(D_MODEL),
        out_shape=jax.ShapeDtypeStruct((n, D_MODEL), F32),
        compiler_params=pltpu.CompilerParams(dimension_semantics=("arbitrary",),
                                             vmem_limit_bytes=VMEM_LIMIT_BYTES),
        name="post",
    )(x2d, ym, att, p2d, *gains, *weights)


def _permute_heads(a, axis):
    shape = a.shape
    a = a.reshape(shape[:axis] + (A_HEADS, A_HD) + shape[axis + 1:])
    a = jnp.take(a, np.asarray(HEAD_ORDER), axis=axis)
    return a.reshape(shape)


def kernel(x_prompt, x_sample, p_prompt, p_sample, state_mlstm_c, state_mlstm_n, state_mlstm_m,
           state_mlstm_conv, cache_swa_k, cache_swa_v, norm_mix_pre, w_in, b_gates, conv_w,
           mlstm_norm, attn_sinks, attn_norm, w_out, norm_mix_post, norm_ffn_pre, w_up, w_down,
           norm_ffn_post, w_pgate, w_pproj):
    depth = w_in.shape[0]
    assert depth == 1, "single-layer decoder"
    B, T, _ = x_prompt.shape
    SB, ST, _ = x_sample.shape
    i = 0

    wi = w_in[i]
    o_qk, o_vm, o_om, o_g, o_qa, o_ka, o_va = np.cumsum(
        [0, QK_W, M_WIDTH, M_WIDTH, N_GATES, A_WIDTH, KV_W])
    w_main = jnp.concatenate([wi[:, o_qk:o_g], _permute_heads(wi[:, o_qa:o_ka], 1), wi[:, o_ka:]],
                             axis=1).astype(BF16)
    w_g = wi[:, o_g:o_qa].astype(BF16)
    w_gt = w_g.T
    g_pre = norm_mix_pre[i].reshape(1, D_MODEL)
    bg_row = b_gates[i].reshape(1, N_GATES)
    bg_col = b_gates[i].reshape(N_GATES, 1)
    cw = conv_w[i]
    mn = mlstm_norm[i].reshape(1, M_WIDTH)
    sinks = attn_sinks[i].reshape(1, A_HEADS)
    gains = [_permute_heads(attn_norm[i], 0).reshape(1, A_WIDTH), norm_mix_post[i].reshape(1, D_MODEL),
             norm_ffn_pre[i].reshape(1, D_MODEL), norm_ffn_post[i].reshape(1, D_MODEL)]
    wo = jnp.concatenate([w_out[i][:M_WIDTH], _permute_heads(w_out[i][M_WIDTH:], 0)], axis=0)
    weights = [wo.astype(BF16), w_up[i].astype(BF16), w_down[i].astype(BF16),
               w_pgate[i].astype(BF16), w_pproj[i].astype(BF16)]

    xp = x_prompt.reshape(B * T, D_MODEL)
    pieces = _in_proj(xp, g_pre, w_main, w_g, w_gt)
    ym, att, cn_p, m_p = _mixer_prompt(pieces, bg_row, bg_col, cw, mn, sinks, B, T)
    y_prompt = _post(xp, ym, att, p_prompt[i].reshape(B * T, P_DIM), gains, weights).reshape(B, T, D_MODEL)
    c_p, n_p = cn_p[..., :M_DV], cn_p[..., M_DV]
    qk_p, ka_p, va_p = pieces[0], pieces[4], pieces[5]
    conv_p = qk_p.reshape(B, T, QK_W)[:, T - (CONV_W - 1):]
    k_p = ka_p.reshape(B, T, KV_W)[:, T - WINDOW:].reshape(B, WINDOW, A_KV, A_HD)
    v_p = va_p.reshape(B, T, KV_W)[:, T - WINDOW:].reshape(B, WINDOW, A_KV, A_HD)

    xs = x_sample.reshape(SB * ST, D_MODEL)
    qk_s, vm_s, om_s, qa_s, ka_s, va_s, gc_s, gr_s = _in_proj(xs, g_pre, w_main, w_g, w_gt)
    per_seq = lambda a: a.reshape(SB, ST, a.shape[-1])
    qs = qa_s.reshape(SB, ST, A_GROUP, A_KV, A_HD).transpose(0, 3, 2, 1, 4).reshape(SB, A_KV, A_GROUP * ST, A_HD)
    gr_seq = gr_s.reshape(N_GATES, SB, ST).transpose(1, 0, 2)
    sink_rows = jnp.repeat(attn_sinks[i].reshape(A_KV, A_GROUP), ST, axis=1).reshape(A_KV, A_GROUP * ST, 1)
    slope_rows = jnp.asarray(np.repeat(np.asarray(SLOPES, np.float32).reshape(A_KV, A_GROUP), ST, axis=1)
                             .reshape(A_KV, A_GROUP * ST, 1))
    ym_s, att_s, c_s, n_s, m_s, conv_s, k_s, v_s = _mixer_sample(
        per_seq(qk_s), per_seq(vm_s), per_seq(om_s), qs, per_seq(ka_s), per_seq(va_s), per_seq(gc_s), gr_seq,
        state_mlstm_c[i], state_mlstm_n[i], state_mlstm_m[i].reshape(SB, 1, M_HEADS), state_mlstm_conv[i],
        cache_swa_k[i].reshape(SB, WINDOW, KV_W), cache_swa_v[i].reshape(SB, WINDOW, KV_W),
        bg_row, bg_col, cw, mn, sink_rows, slope_rows)
    att_s2 = att_s.reshape(SB, A_KV, A_GROUP, ST, A_HD).transpose(0, 3, 2, 1, 4).reshape(SB * ST, A_WIDTH)
    y_sample = _post(xs, ym_s.reshape(SB * ST, M_WIDTH), att_s2, p_sample[i].reshape(SB * ST, P_DIM),
                     gains, weights).reshape(SB, ST, D_MODEL)

    stack = lambda a: a[None]
    return (y_prompt, y_sample,
            stack(c_p), stack(n_p), stack(m_p.reshape(B, M_HEADS)), stack(conv_p), stack(k_p), stack(v_p),
            stack(c_s), stack(n_s), stack(m_s.reshape(SB, M_HEADS)), stack(conv_s),
            stack(k_s.reshape(SB, WINDOW, A_KV, A_HD)), stack(v_s.reshape(SB, WINDOW, A_KV, A_HD)))
```

```python
import numpy as np
import jax
import jax.numpy as jnp
from jax import lax
from jax.experimental import pallas as pl
from jax.experimental.pallas import tpu as pltpu

F32 = jnp.float32
BF16 = jnp.bfloat16

D_MODEL = 1024
M_WIDTH = 512
M_HEADS = 4
M_DV = 128
M_DK = 64
QK_W = 512
CONV_W = 4
CHUNK = 128
A_WIDTH = 512
A_HEADS = 8
A_HD = 64
A_KV = 2
A_GROUP = 4
KV_W = 128
WINDOW = 128
D_FF = 4096
P_DIM = 256
EPS = 1e-6
N_GATES = 2 * M_HEADS

VMEM_LIMIT_BYTES = 56 * 1024 * 1024
ROW_TILE = 512
FF_TILE = 1024
SAMPLE_SEQ_TILE = 8

NEG_INF = float("-inf")
SLOPES = [2.0 ** (-8.0 * (h + 1) / A_HEADS) for h in range(A_HEADS)]
HEAD_ORDER = [kv * A_GROUP + g for g in range(A_GROUP) for kv in range(A_KV)]


def _dot(a, b):
    return jnp.dot(a, b, preferred_element_type=F32)


def _dot_nt(a, b):
    return lax.dot_general(a, b, (((1,), (1,)), ((), ())), preferred_element_type=F32)


def _dot_tn(a, b):
    return lax.dot_general(a, b, (((0,), (0,)), ((), ())), preferred_element_type=F32)


def _dot_exact(a, b):
    return lax.dot_general(a, b, (((1,), (0,)), ((), ())), precision=lax.Precision.HIGHEST,
                           preferred_element_type=F32)


def _rms(x, g):
    return x * lax.rsqrt(jnp.mean(x * x, axis=-1, keepdims=True) + EPS) * g


def _in_proj_kernel(x_ref, g_ref, w_ref, wg_ref, wgt_ref,
                    qk_ref, vm_ref, om_ref, qa_ref, ka_ref, va_ref, gc_ref, gr_ref):
    xn = _rms(x_ref[...], g_ref[...]).astype(BF16)
    z = _dot(xn, w_ref[...])
    qk_ref[...] = z[:, 0:512]
    vm_ref[...] = z[:, 512:1024]
    om_ref[...] = z[:, 1024:1536]
    qa_ref[...] = z[:, 1536:2048]
    ka_ref[...] = z[:, 2048:2176]
    va_ref[...] = z[:, 2176:2304]
    gc_ref[...] = _dot(xn, wg_ref[...])
    gr_ref[...] = _dot_nt(wgt_ref[...], xn)


def _in_proj(x2d, g_pre, w_main, w_g, w_gt):
    n = x2d.shape[0]
    tm = min(ROW_TILE, n)
    row = lambda w: pl.BlockSpec((tm, w), lambda i: (i, 0))
    full = lambda a: pl.BlockSpec(a.shape, lambda i: (0,) * a.ndim)
    out_widths = (QK_W, M_WIDTH, M_WIDTH, A_WIDTH, KV_W, KV_W, N_GATES)
    return pl.pallas_call(
        _in_proj_kernel,
        grid=(n // tm,),
        in_specs=[row(D_MODEL), full(g_pre), full(w_main), full(w_g), full(w_gt)],
        out_specs=[row(w) for w in out_widths] + [pl.BlockSpec((N_GATES, tm), lambda i: (0, i))],
        out_shape=[jax.ShapeDtypeStruct((n, w), F32) for w in out_widths]
        + [jax.ShapeDtypeStruct((N_GATES, n), F32)],
        compiler_params=pltpu.CompilerParams(dimension_semantics=("arbitrary",),
                                             vmem_limit_bytes=VMEM_LIMIT_BYTES),
        name="in_proj",
    )(x2d, g_pre, w_main, w_g, w_gt)


def _mlstm_out(h, o, g):
    hn = h * lax.rsqrt(jnp.mean(h * h, axis=-1, keepdims=True) + EPS) * g
    return jax.nn.sigmoid(o) * hn


def _conv_silu(up_ref, base, w, L):
    out = w[0:1, :] * up_ref[base:base + L, :]
    for j in range(1, CONV_W):
        out = out + w[j:j + 1, :] * up_ref[base + j:base + j + L, :]
    return jax.nn.silu(out)


def _mixer_prompt_kernel(qk_ref, vm_ref, om_ref, qa_ref, ka_ref, va_ref, gc_ref, gr_ref,
                         bgr_ref, bgc_ref, cw_ref, mn_ref, sink_ref,
                         ym_ref, att_ref, c_out, m_out,
                         up_ref, c_st, m_st, kp_ref, vp_ref, bias_ref, sinkrep_ref):
    ci = pl.program_id(1)
    L = CHUNK
    t = lax.broadcasted_iota(jnp.int32, (L, L), 0)
    s = lax.broadcasted_iota(jnp.int32, (L, L), 1)

    @pl.when(ci == 0)
    def _():
        up_ref[...] = jnp.zeros_like(up_ref)
        c_st[...] = jnp.zeros_like(c_st)
        m_st[...] = jnp.zeros_like(m_st)
        kp_ref[...] = jnp.zeros_like(kp_ref)
        vp_ref[...] = jnp.zeros_like(vp_ref)
        dist = (t - s).astype(F32)
        for kv in range(A_KV):
            for g in range(A_GROUP):
                hd = kv * A_GROUP + g
                rows = slice(g * L, (g + 1) * L)
                bias_ref[kv, rows, 0:L] = jnp.full((L, L), NEG_INF, F32)
                bias_ref[kv, rows, L:2 * L] = jnp.where(s <= t, -SLOPES[hd] * dist, NEG_INF)
                sinkrep_ref[kv, rows, :] = jnp.broadcast_to(sink_ref[0:1, hd:hd + 1], (L, KV_W))

    @pl.when(ci == 1)
    def _():
        dist = (t - s).astype(F32) + float(WINDOW)
        for kv in range(A_KV):
            for g in range(A_GROUP):
                hd = kv * A_GROUP + g
                bias_ref[kv, g * L:(g + 1) * L, 0:L] = jnp.where(s >= t, -SLOPES[hd] * dist, NEG_INF)

    up_ref[8:8 + L, :] = qk_ref[...]
    qk = _conv_silu(up_ref, 5, cw_ref[...], L)
    up_ref[5:8, :] = qk_ref[L - 3:L, :]

    gcb = gc_ref[...] + bgr_ref[...]
    grb = gr_ref[...] + bgc_ref[...]
    ig_c, lf_c = gcb[:, 0:M_HEADS], jax.nn.log_sigmoid(gcb[:, M_HEADS:N_GATES])
    ig_r, lf_r = grb[0:M_HEADS, :], jax.nn.log_sigmoid(grb[M_HEADS:N_GATES, :])
    b_c = _dot_exact((s <= t).astype(F32), lf_c)
    b_r = _dot_exact(lf_r, (t <= s).astype(F32))
    a_r = ig_r - b_r
    causal_bias = jnp.where(s <= t, 0.0, NEG_INF)
    m_prev = m_st[0:1, 0:M_HEADS]
    hcol = lax.broadcasted_iota(jnp.int32, (L, M_HEADS), 1)
    ones_v = jnp.ones((L, M_DV), BF16)
    H = range(M_HEADS)
    qb = [(qk[:, h * M_DK:(h + 1) * M_DK] * (M_DK ** -0.5)).astype(BF16) for h in H]
    kf = [qk[:, QK_W // 2 + h * M_DK:QK_W // 2 + (h + 1) * M_DK] for h in H]
    kb = [k.astype(BF16) for k in kf]
    vext = [jnp.concatenate([vm_ref[:, h * M_DV:(h + 1) * M_DV].astype(BF16), ones_v], axis=1) for h in H]
    c_prev = [c_st[h] for h in H]
    qk_h = [_dot_nt(qb[h], kb[h]) for h in H]
    qc_h = [_dot(qb[h], c_prev[h].astype(BF16)) for h in H]
    log_d = [b_c[:, h:h + 1] + a_r[h:h + 1, :] + causal_bias for h in H]
    m_intra = jnp.full((L, M_HEADS), NEG_INF, F32)
    for h in H:
        m_intra = jnp.where(hcol == h, jnp.max(log_d[h], axis=-1, keepdims=True), m_intra)
    log_inter = b_c + m_prev
    m_t = jnp.maximum(log_inter, m_intra)
    w_inter = jnp.exp(log_inter - m_t)
    emt = jnp.exp(-m_t)
    s_h = [(qk_h[h] * jnp.exp(log_d[h] - m_t[:, h:h + 1])).astype(BF16) for h in H]
    o_h = [_dot(s_h[h], vext[h]) for h in H]
    for h in H:
        wi = w_inter[:, h:h + 1]
        num = o_h[h][:, 0:M_DV] + wi * qc_h[h][:, 0:M_DV]
        den = o_h[h][:, M_DV:2 * M_DV] + wi * qc_h[h][:, M_DV:2 * M_DV]
        hh = num / jnp.maximum(jnp.abs(den), emt[:, h:h + 1])
        cols = slice(h * M_DV, (h + 1) * M_DV)
        ym_ref[:, cols] = _mlstm_out(hh, om_ref[:, cols], mn_ref[:, cols])
    b_last = b_c[L - 1:L, :]
    log_w = b_last - b_c + ig_c
    m_new = jnp.maximum(b_last + m_prev, jnp.max(log_w, axis=0, keepdims=True))
    w_k = jnp.exp(log_w - m_new)
    decay = jnp.exp(b_last + m_prev - m_new)
    kw = [(kf[h] * w_k[:, h:h + 1]).astype(BF16) for h in H]
    dc = [_dot_tn(kw[h], vext[h]) for h in H]
    for h in H:
        c_st[h] = decay[:, h:h + 1] * c_prev[h] + dc[h]
    m_st[0:1, 0:M_HEADS] = m_new

    lane = lax.broadcasted_iota(jnp.int32, (2 * L, KV_W), 1)
    low2 = lane < A_HD
    low = lax.broadcasted_iota(jnp.int32, (L, KV_W), 1) < A_HD
    k_both = jnp.concatenate([kp_ref[...], ka_ref[...]], axis=0)
    v_both = jnp.concatenate([vp_ref[...], va_ref[...]], axis=0)
    ones_k = jnp.ones((2 * L, KV_W), BF16)
    kmask = [jnp.where(low2, k_both, 0.0).astype(BF16), jnp.where(low2, 0.0, k_both).astype(BF16)]
    vext_a = [jnp.concatenate([jnp.where(low2, v_both, 1.0).astype(BF16), ones_k], axis=1),
              jnp.concatenate([jnp.where(low2, 1.0, v_both).astype(BF16), ones_k], axis=1)]
    qst = jnp.concatenate([qa_ref[:, g * KV_W:(g + 1) * KV_W] for g in range(A_GROUP)], axis=0)
    qst = (qst * (A_HD ** -0.5)).astype(BF16)
    KV = range(A_KV)
    sc = [_dot_nt(qst, kmask[kv]) + bias_ref[kv] for kv in KV]
    m_a = [jnp.maximum(jnp.max(jnp.maximum(sc[kv][:, 0:L], sc[kv][:, L:2 * L]), axis=-1, keepdims=True),
                       sinkrep_ref[kv]) for kv in KV]
    e = [jnp.concatenate([jnp.exp(sc[kv][:, 0:L] - m_a[kv]), jnp.exp(sc[kv][:, L:2 * L] - m_a[kv])],
                         axis=1).astype(BF16) for kv in KV]
    o = [_dot(e[kv], vext_a[kv]) for kv in KV]
    exs = [jnp.exp(sinkrep_ref[kv] - m_a[kv]) for kv in KV]
    for g in range(A_GROUP):
        rows = slice(g * L, (g + 1) * L)
        pv = jnp.where(low, o[0][rows, 0:KV_W], o[1][rows, 0:KV_W])
        den = jnp.where(low, o[0][rows, KV_W:2 * KV_W] + exs[0][rows, :],
                        o[1][rows, KV_W:2 * KV_W] + exs[1][rows, :])
        att_ref[:, g * KV_W:(g + 1) * KV_W] = pv / den
    kp_ref[...] = ka_ref[...]
    vp_ref[...] = va_ref[...]

    @pl.when(ci == pl.num_programs(1) - 1)
    def _():
        c_out[0] = c_st[...]
        m_out[0] = m_st[0:1, 0:M_HEADS]


def _mixer_prompt(pieces, bg_row, bg_col, conv_w, mlstm_norm, sinks, batch, seq):
    qk, vm, om, qa, ka, va, gc, gr = pieces
    nc = seq // CHUNK
    row = lambda w: pl.BlockSpec((CHUNK, w), lambda b, c: (b * nc + c, 0))
    full = lambda a: pl.BlockSpec(a.shape, lambda b, c: (0,) * a.ndim)
    n = batch * seq
    return pl.pallas_call(
        _mixer_prompt_kernel,
        grid=(batch, nc),
        in_specs=[row(QK_W), row(M_WIDTH), row(M_WIDTH), row(A_WIDTH), row(KV_W), row(KV_W), row(N_GATES),
                  pl.BlockSpec((N_GATES, CHUNK), lambda b, c: (0, b * nc + c)),
                  full(bg_row), full(bg_col), full(conv_w), full(mlstm_norm), full(sinks)],
        out_specs=[row(M_WIDTH), row(A_WIDTH),
                   pl.BlockSpec((1, M_HEADS, M_DK, 2 * M_DV), lambda b, c: (b, 0, 0, 0)),
                   pl.BlockSpec((1, 1, M_HEADS), lambda b, c: (b, 0, 0))],
        out_shape=[jax.ShapeDtypeStruct((n, M_WIDTH), F32), jax.ShapeDtypeStruct((n, A_WIDTH), F32),
                   jax.ShapeDtypeStruct((batch, M_HEADS, M_DK, 2 * M_DV), F32),
                   jax.ShapeDtypeStruct((batch, 1, M_HEADS), F32)],
        scratch_shapes=[pltpu.VMEM((8 + CHUNK, QK_W), F32),
                        pltpu.VMEM((M_HEADS, M_DK, 2 * M_DV), F32),
                        pltpu.VMEM((8, 128), F32),
                        pltpu.VMEM((CHUNK, KV_W), F32), pltpu.VMEM((CHUNK, KV_W), F32),
                        pltpu.VMEM((A_KV, A_GROUP * CHUNK, 2 * CHUNK), F32),
                        pltpu.VMEM((A_KV, A_GROUP * CHUNK, KV_W), F32)],
        compiler_params=pltpu.CompilerParams(dimension_semantics=("arbitrary", "arbitrary"),
                                             vmem_limit_bytes=VMEM_LIMIT_BYTES),
        name="mixer_prompt",
    )(qk, vm, om, qa, ka, va, gc, gr, bg_row, bg_col, conv_w, mlstm_norm, sinks)


def _mixer_sample(qk, vm, om, qs, kn, vn, gc, gr, c0, n0, m0, cb, kc, vc,
                  bg_row, bg_col, conv_w, mlstm_norm, sink_rows, slope_rows):
    nb, L = qk.shape[0], qk.shape[1]
    ts = min(SAMPLE_SEQ_TILE, nb)
    per_b = lambda a: pl.BlockSpec((ts,) + a.shape[1:], lambda b: (b,) + (0,) * (a.ndim - 1))
    full = lambda a: pl.BlockSpec(a.shape, lambda b: (0,) * a.ndim)
    ins = [qk, vm, om, qs, kn, vn, gc, gr, c0, n0, m0, cb, kc, vc]
    consts = [bg_row, bg_col, conv_w, mlstm_norm, sink_rows, slope_rows]
    outs = [jax.ShapeDtypeStruct((nb, L, M_WIDTH), F32),
            jax.ShapeDtypeStruct((nb, A_KV, A_GROUP * L, A_HD), F32),
            jax.ShapeDtypeStruct(c0.shape, F32), jax.ShapeDtypeStruct(n0.shape, F32),
            jax.ShapeDtypeStruct(m0.shape, F32), jax.ShapeDtypeStruct(cb.shape, F32),
            jax.ShapeDtypeStruct(kc.shape, F32), jax.ShapeDtypeStruct(vc.shape, F32)]
    return pl.pallas_call(
        _mixer_sample_tile_kernel,
        grid=(nb // ts,),
        in_specs=[per_b(a) for a in ins] + [full(a) for a in consts],
        out_specs=[per_b(o) for o in outs],
        out_shape=outs,
        scratch_shapes=[pltpu.VMEM((ts, 8, QK_W), F32)],
        compiler_params=pltpu.CompilerParams(dimension_semantics=("arbitrary",),
                                             vmem_limit_bytes=VMEM_LIMIT_BYTES),
        name="mixer_sample",
    )(*ins, *consts)


def _small_cumsum(x, axis):
    idx = lax.broadcasted_iota(jnp.int32, x.shape, axis)
    out = jnp.zeros_like(x)
    for r in range(x.shape[axis]):
        out = out + jnp.where(idx >= r, lax.slice_in_dim(x, r, r + 1, axis=axis), 0.0)
    return out


def _mixer_sample_tile_kernel(qk_ref, vm_ref, om_ref, qs_ref, kn_ref, vn_ref, gc_ref, gr_ref,
                              c0_ref, n0_ref, m0_ref, cb_ref, kc_ref, vc_ref,
                              bgr_ref, bgc_ref, cw_ref, mn_ref, sinkr_ref, sloper_ref,
                              ym_ref, att_ref, c_out, n_out, m_out, cb_out, k_out, v_out,
                              up_ref):
    NB, L = qk_ref.shape[0], qk_ref.shape[1]
    SEQ = range(NB)
    H = range(M_HEADS)
    cw = cw_ref[...]

    for b in SEQ:
        up_ref[b, 0:CONV_W - 1, :] = cb_ref[b]
        up_ref[b, CONV_W - 1:CONV_W - 1 + L, :] = qk_ref[b]
    qk = []
    for b in SEQ:
        acc = cw[0:1, :] * up_ref[b, 0:L, :]
        for j in range(1, CONV_W):
            acc = acc + cw[j:j + 1, :] * up_ref[b, j:j + L, :]
        qk.append(jax.nn.silu(acc))
        cb_out[b] = up_ref[b, L:L + CONV_W - 1, :]

    r = lax.broadcasted_iota(jnp.int32, (L, L), 0)
    c = lax.broadcasted_iota(jnp.int32, (L, L), 1)
    causal = c <= r
    gcb = [gc_ref[b] + bgr_ref[...] for b in SEQ]
    grb = [gr_ref[b] + bgc_ref[...] for b in SEQ]
    ig_c = [g[:, 0:M_HEADS] for g in gcb]
    ig_r = [g[0:M_HEADS, :] for g in grb]
    b_c = [_small_cumsum(jax.nn.log_sigmoid(g[:, M_HEADS:N_GATES]), 0) for g in gcb]
    b_r = [_small_cumsum(jax.nn.log_sigmoid(g[M_HEADS:N_GATES, :]), 1) for g in grb]

    P = [(b, h) for b in SEQ for h in H]
    q = [qk[b][:, h * M_DK:(h + 1) * M_DK] * (M_DK ** -0.5) for b, h in P]
    k = [qk[b][:, QK_W // 2 + h * M_DK:QK_W // 2 + (h + 1) * M_DK] for b, h in P]
    qb = [x.astype(BF16) for x in q]
    kb = [x.astype(BF16) for x in k]
    vb = [vm_ref[b, :, h * M_DV:(h + 1) * M_DV].astype(BF16) for b, h in P]
    c_prev = [c0_ref[b, h] for b, h in P]
    n_prev = [n0_ref[b, h:h + 1, :] for b, h in P]
    m_prev = [m0_ref[b, 0:1, h:h + 1] for b, h in P]
    bc = [b_c[b][:, h:h + 1] for b, h in P]
    igc = [ig_c[b][:, h:h + 1] for b, h in P]
    log_d = [jnp.where(causal, b_c[b][:, h:h + 1] - b_r[b][h:h + 1, :] + ig_r[b][h:h + 1, :], NEG_INF)
             for b, h in P]
    N = range(len(P))
    qkt = [_dot_nt(qb[i], kb[i]) for i in N]
    qc = [_dot(qb[i], c_prev[i].astype(BF16)) for i in N]
    log_inter = [bc[i] + m_prev[i] for i in N]
    m_t = [jnp.maximum(log_inter[i], jnp.max(log_d[i], axis=-1, keepdims=True)) for i in N]
    w_inter = [jnp.exp(log_inter[i] - m_t[i]) for i in N]
    s = [qkt[i] * jnp.exp(log_d[i] - m_t[i]) for i in N]
    sv = [_dot(s[i].astype(BF16), vb[i]) for i in N]
    qn = [jnp.sum(q[i] * n_prev[i], axis=-1, keepdims=True) for i in N]
    for i, (b, h) in enumerate(P):
        num = sv[i] + w_inter[i] * qc[i]
        den = jnp.sum(s[i], axis=-1, keepdims=True) + w_inter[i] * qn[i]
        hh = num / jnp.maximum(jnp.abs(den), jnp.exp(-m_t[i]))
        cols = slice(h * M_DV, (h + 1) * M_DV)
        ym_ref[b, :, cols] = _mlstm_out(hh, om_ref[b, :, cols], mn_ref[:, cols])
    b_last = [bc[i][L - 1:L, :] for i in N]
    log_w = [b_last[i] - bc[i] + igc[i] for i in N]
    m_new = [jnp.maximum(b_last[i] + m_prev[i], jnp.max(log_w[i], axis=0, keepdims=True)) for i in N]
    kw = [k[i] * jnp.exp(log_w[i] - m_new[i]) for i in N]
    decay = [jnp.exp(b_last[i] + m_prev[i] - m_new[i]) for i in N]
    dc = [_dot_tn(kw[i].astype(BF16), vb[i]) for i in N]
    for i, (b, h) in enumerate(P):
        c_out[b, h] = decay[i] * c_prev[i] + dc[i]
        n_out[b, h:h + 1, :] = decay[i] * n_prev[i] + jnp.sum(kw[i], axis=0, keepdims=True)
        m_out[b, 0:1, h:h + 1] = m_new[i]

    rows = A_GROUP * L

    def token_of_row(shape):
        ri = lax.broadcasted_iota(jnp.int32, shape, 0).astype(F32)
        return ri - L * jnp.floor((ri + 0.5) / L)

    ti = token_of_row((rows, WINDOW))
    ji = lax.broadcasted_iota(jnp.int32, (rows, WINDOW), 1).astype(F32)
    tn = token_of_row((rows, L))
    sn = lax.broadcasted_iota(jnp.int32, (rows, L), 1).astype(F32)
    bias_c = [jnp.where(ji >= ti, -sloper_ref[kv] * (ti + WINDOW - ji), NEG_INF) for kv in range(A_KV)]
    bias_n = [jnp.where(sn <= tn, -sloper_ref[kv] * (tn - sn), NEG_INF) for kv in range(A_KV)]
    Q = [(b, kv) for b in SEQ for kv in range(A_KV)]
    M = range(len(Q))
    qh = [(qs_ref[b, kv] * (A_HD ** -0.5)).astype(BF16) for b, kv in Q]
    kcb = [kc_ref[b, :, kv * A_HD:(kv + 1) * A_HD].astype(BF16) for b, kv in Q]
    knb = [kn_ref[b, :, kv * A_HD:(kv + 1) * A_HD].astype(BF16) for b, kv in Q]
    vcb = [vc_ref[b, :, kv * A_HD:(kv + 1) * A_HD].astype(BF16) for b, kv in Q]
    vnb = [vn_ref[b, :, kv * A_HD:(kv + 1) * A_HD].astype(BF16) for b, kv in Q]
    sc_c = [_dot_nt(qh[i], kcb[i]) + bias_c[Q[i][1]] for i in M]
    sc_n = [_dot_nt(qh[i], knb[i]) + bias_n[Q[i][1]] for i in M]
    sink = [sinkr_ref[kv] for b, kv in Q]
    m_a = [jnp.maximum(sink[i], jnp.maximum(jnp.max(sc_c[i], axis=-1, keepdims=True),
                                            jnp.max(sc_n[i], axis=-1, keepdims=True))) for i in M]
    e_c = [jnp.exp(sc_c[i] - m_a[i]) for i in M]
    e_n = [jnp.exp(sc_n[i] - m_a[i]) for i in M]
    den_a = [jnp.exp(sink[i] - m_a[i]) + jnp.sum(e_c[i], axis=-1, keepdims=True)
             + jnp.sum(e_n[i], axis=-1, keepdims=True) for i in M]
    pv = [_dot((e_c[i] / den_a[i]).astype(BF16), vcb[i]) + _dot((e_n[i] / den_a[i]).astype(BF16), vnb[i])
          for i in M]
    for i, (b, kv) in enumerate(Q):
        att_ref[b, kv] = pv[i]

    for b in SEQ:
        k_out[b, 0:WINDOW - L, :] = kc_ref[b, L:WINDOW, :]
        k_out[b, WINDOW - L:WINDOW, :] = kn_ref[b]
        v_out[b, 0:WINDOW - L, :] = vc_ref[b, L:WINDOW, :]
        v_out[b, WINDOW - L:WINDOW, :] = vn_ref[b]


def _in_proj_sample_kernel(x_ref, g_ref, wn_ref, wt_ref, wgt_ref,
                           qkn_ref, qa_ref, ka_ref, va_ref, qkt_ref, vt_ref, ot_ref, gt_ref):
    xn = _rms(x_ref[...], g_ref[...]).astype(BF16)
    zn = _dot(xn, wn_ref[...])
    qkn_ref[...] = zn[:, 0:QK_W]
    qa_ref[...] = zn[:, QK_W:QK_W + A_WIDTH]
    ka_ref[...] = zn[:, QK_W + A_WIDTH:QK_W + A_WIDTH + KV_W]
    va_ref[...] = zn[:, QK_W + A_WIDTH + KV_W:QK_W + A_WIDTH + 2 * KV_W]
    zt = _dot_nt(wt_ref[...], xn)
    qkt_ref[...] = zt[0:QK_W, :]
    vt_ref[...] = zt[QK_W:QK_W + M_WIDTH, :]
    ot_ref[...] = zt[QK_W + M_WIDTH:QK_W + 2 * M_WIDTH, :]
    gt_ref[...] = _dot_nt(wgt_ref[...], xn)


def _in_proj_sample(x2d, g_pre, w_nat, w_t, w_gt):
    n = x2d.shape[0]
    shapes = [(n, QK_W), (n, A_WIDTH), (n, KV_W), (n, KV_W), (QK_W, n), (M_WIDTH, n), (M_WIDTH, n), (N_GATES, n)]
    return pl.pallas_call(
        _in_proj_sample_kernel,
        out_shape=[jax.ShapeDtypeStruct(s, F32) for s in shapes],
        compiler_params=pltpu.CompilerParams(vmem_limit_bytes=VMEM_LIMIT_BYTES),
        name="in_proj_sample",
    )(x2d, g_pre, w_nat, w_t, w_gt)


SAMPLE_V_TILE = 32


def _mlstm_sample_kernel(qt_ref, kt_ref, vt_ref, ot_ref, gt_ref, bg_ref, cbq_ref, cbk_ref, cwq_ref, cwk_ref,
                         gain_ref, c_ref, n_ref, m_ref,
                         ym_ref, c_out, n_out, m_out,
                         ct_ref, q_s, ik_s, num_s):
    h = pl.program_id(0)
    NB = c_ref.shape[0]
    L = qt_ref.shape[1] // NB
    T = range(L)

    def conv(raw_ref, cb_ref, cw_ref):
        ups = [cb_ref[j] for j in range(CONV_W - 1)] + [raw_ref[:, t * NB:(t + 1) * NB] for t in T]
        w = [cw_ref[:, j:j + 1] for j in range(CONV_W)]
        outs = []
        for t in T:
            acc = w[0] * ups[t]
            for j in range(1, CONV_W):
                acc = acc + w[j] * ups[t + j]
            outs.append(jax.nn.silu(acc))
        return outs

    q = [x * (M_DK ** -0.5) for x in conv(qt_ref, cbq_ref, cwq_ref)]
    k = conv(kt_ref, cbk_ref, cwk_ref)

    ig_all = gt_ref[pl.ds(h, 1), :] + bg_ref[pl.ds(h, 1), :]
    lf_all = jax.nn.log_sigmoid(gt_ref[pl.ds(h + M_HEADS, 1), :] + bg_ref[pl.ds(h + M_HEADS, 1), :])
    m = m_ref[pl.ds(h, 1), :]
    f, ms = [], []
    n = n_ref[0]
    den = []
    for t in T:
        ig, lf = ig_all[:, t * NB:(t + 1) * NB], lf_all[:, t * NB:(t + 1) * NB]
        m_new = jnp.maximum(lf + m, ig)
        f_t = jnp.exp(lf + m - m_new)
        ik = jnp.exp(ig - m_new) * k[t]
        m = m_new
        n = f_t * n + ik
        f.append(f_t)
        ms.append(m_new)
        den.append(jnp.sum(q[t] * n, axis=0, keepdims=True))
        ik_s[t] = ik
        q_s[t] = q[t]
    m_out[0] = m
    n_out[0] = n

    for d in range(M_DK):
        ct_ref[d * M_DV:(d + 1) * M_DV, :] = c_ref[:, d * M_DV:(d + 1) * M_DV].T

    VT = SAMPLE_V_TILE
    for vq in range(M_DV // VT):
        vts = [vt_ref[vq * VT:(vq + 1) * VT, t * NB:(t + 1) * NB] for t in T]

        def step(d, accs, vq=vq, vts=vts):
            r0 = pl.multiple_of(d * M_DV + vq * VT, VT)
            c = ct_ref[pl.ds(r0, VT), :]
            new = []
            for t in T:
                c = f[t] * c + ik_s[t, pl.ds(d, 1), :] * vts[t]
                new.append(accs[t] + q_s[t, pl.ds(d, 1), :] * c)
            ct_ref[pl.ds(r0, VT), :] = c
            return tuple(new)

        accs = lax.fori_loop(0, M_DK, step, tuple(jnp.zeros((VT, NB), F32) for _ in T), unroll=8)
        for t in T:
            num_s[t, vq * VT:(vq + 1) * VT, :] = accs[t]

    for t in T:
        hh = num_s[t] / jnp.maximum(jnp.abs(den[t]), jnp.exp(-ms[t]))
        hn = hh * lax.rsqrt(jnp.mean(hh * hh, axis=0, keepdims=True) + EPS) * gain_ref[...]
        y = jax.nn.sigmoid(ot_ref[:, t * NB:(t + 1) * NB]) * hn
        ym_ref[t * NB:(t + 1) * NB, :] = y.T

    for d in range(M_DK):
        c_out[:, d * M_DV:(d + 1) * M_DV] = ct_ref[d * M_DV:(d + 1) * M_DV, :].T


def _mlstm_sample(qkt, vt, ot, gt, bg_col, cbt, cwt, gain_col, c2d, nt, mt):
    nb = c2d.shape[0]
    n = qkt.shape[1]
    L = n // nb
    hblk = lambda rows, off: pl.BlockSpec((rows, n), lambda h, off=off: (h + off, 0))
    full = lambda a: pl.BlockSpec(a.shape, lambda h: (0,) * a.ndim)
    kq = QK_W // 2 // M_DK
    return pl.pallas_call(
        _mlstm_sample_kernel,
        grid=(M_HEADS,),
        in_specs=[hblk(M_DK, 0), hblk(M_DK, kq), hblk(M_DV, 0), hblk(M_DV, 0), full(gt), full(bg_col),
                  pl.BlockSpec((CONV_W - 1, M_DK, nb), lambda h: (0, h, 0)),
                  pl.BlockSpec((CONV_W - 1, M_DK, nb), lambda h: (0, h + kq, 0)),
                  pl.BlockSpec((M_DK, CONV_W), lambda h: (h, 0)),
                  pl.BlockSpec((M_DK, CONV_W), lambda h: (h + kq, 0)),
                  pl.BlockSpec((M_DV, 1), lambda h: (h, 0)),
                  pl.BlockSpec((nb, M_DK * M_DV), lambda h: (0, h)),
                  pl.BlockSpec((1, M_DK, nb), lambda h: (h, 0, 0)),
                  full(mt)],
        out_specs=[pl.BlockSpec((n, M_DV), lambda h: (0, h)),
                   pl.BlockSpec((nb, M_DK * M_DV), lambda h: (0, h)),
                   pl.BlockSpec((1, M_DK, nb), lambda h: (h, 0, 0)),
                   pl.BlockSpec((1, 1, nb), lambda h: (h, 0, 0))],
        out_shape=[jax.ShapeDtypeStruct((n, M_WIDTH), F32), jax.ShapeDtypeStruct(c2d.shape, F32),
                   jax.ShapeDtypeStruct(nt.shape, F32), jax.ShapeDtypeStruct((M_HEADS, 1, nb), F32)],
        scratch_shapes=[pltpu.VMEM((M_DK * M_DV, nb), F32), pltpu.VMEM((L, M_DK, nb), F32),
                        pltpu.VMEM((L, M_DK, nb), F32), pltpu.VMEM((L, M_DV, nb), F32)],
        compiler_params=pltpu.CompilerParams(dimension_semantics=("arbitrary",),
                                             vmem_limit_bytes=VMEM_LIMIT_BYTES),
        name="mlstm_sample",
    )(qkt, qkt, vt, ot, gt, bg_col, cbt, cbt, cwt, cwt, gain_col, c2d, nt, mt)


def _attn_sample_kernel(qs_ref, kn_ref, vn_ref, kc_ref, vc_ref, sinkr_ref, sloper_ref, att_ref):
    NB = qs_ref.shape[0]
    L = kn_ref.shape[1] // A_KV
    R = A_KV * A_GROUP * L
    KC, KN = A_KV * WINDOW, A_KV * L

    def split(idx, inner):
        x = idx.astype(F32)
        hi = jnp.floor((x + 0.5) / inner)
        return hi, x - inner * hi

    def bias(nkeys, offset):
        r = lax.broadcasted_iota(jnp.int32, (R, nkeys), 0)
        c = lax.broadcasted_iota(jnp.int32, (R, nkeys), 1)
        kv_q, gt = split(r, A_GROUP * L)
        _, tq = split(r, L)
        pos, kv_k = split(c, A_KV)
        dist = tq + offset - pos
        ok = (kv_q == kv_k) & (dist >= 0.0) & (dist <= float(WINDOW))
        return jnp.where(ok, -sloper_ref[...] * dist, NEG_INF)

    bias_c = bias(KC, float(WINDOW))
    bias_n = bias(KN, 0.0)
    sink = sinkr_ref[...]
    SEQ = range(NB)
    qh = [(qs_ref[b] * (A_HD ** -0.5)).astype(BF16) for b in SEQ]
    kcb = [kc_ref[b].astype(BF16) for b in SEQ]
    knb = [kn_ref[b].astype(BF16) for b in SEQ]
    vcb = [vc_ref[b].astype(BF16) for b in SEQ]
    vnb = [vn_ref[b].astype(BF16) for b in SEQ]
    sc_c = [_dot_nt(qh[b], kcb[b]) + bias_c for b in SEQ]
    sc_n = [_dot_nt(qh[b], knb[b]) + bias_n for b in SEQ]
    m_a = [jnp.maximum(sink, jnp.maximum(jnp.max(sc_c[b], axis=-1, keepdims=True),
                                         jnp.max(sc_n[b], axis=-1, keepdims=True))) for b in SEQ]
    e_c = [jnp.exp(sc_c[b] - m_a[b]) for b in SEQ]
    e_n = [jnp.exp(sc_n[b] - m_a[b]) for b in SEQ]
    den_a = [jnp.exp(sink - m_a[b]) + jnp.sum(e_c[b], axis=-1, keepdims=True)
             + jnp.sum(e_n[b], axis=-1, keepdims=True) for b in SEQ]
    pv = [_dot((e_c[b] / den_a[b]).astype(BF16), vcb[b]) + _dot((e_n[b] / den_a[b]).astype(BF16), vnb[b])
          for b in SEQ]
    for b in SEQ:
        att_ref[b] = pv[b]


def _attn_sample(qs, kn, vn, kc, vc, sink_rows, slope_rows):
    nb = qs.shape[0]
    ts = min(SAMPLE_SEQ_TILE, nb)
    per_b = lambda a: pl.BlockSpec((ts,) + a.shape[1:], lambda b: (b,) + (0,) * (a.ndim - 1))
    full = lambda a: pl.BlockSpec(a.shape, lambda b: (0,) * a.ndim)
    return pl.pallas_call(
        _attn_sample_kernel,
        grid=(nb // ts,),
        in_specs=[per_b(a) for a in (qs, kn, vn, kc, vc)] + [full(sink_rows), full(slope_rows)],
        out_specs=per_b(qs),
        out_shape=jax.ShapeDtypeStruct(qs.shape, F32),
        compiler_params=pltpu.CompilerParams(dimension_semantics=("arbitrary",),
                                             vmem_limit_bytes=VMEM_LIMIT_BYTES),
        name="attn_sample",
    )(qs, kn, vn, kc, vc, sink_rows, slope_rows)


def _post_kernel(x_ref, ym_ref, att_ref, p_ref, an_ref, g_post_ref, g_fpre_ref, g_fpost_ref,
                 wout_ref, wup_ref, wdown_ref, wpg_ref, wpp_ref, o_ref):
    y_a = _rms(att_ref[...], an_ref[...])
    y = jnp.concatenate([ym_ref[...], y_a], axis=-1).astype(BF16)
    x1 = x_ref[...] + _rms(_dot(y, wout_ref[...]), g_post_ref[...])
    u = _rms(x1, g_fpre_ref[...]).astype(BF16)
    f = jnp.zeros_like(x1)
    for j in range(D_FF // FF_TILE):
        cols = slice(j * FF_TILE, (j + 1) * FF_TILE)
        hid = jnp.square(jnp.maximum(_dot(u, wup_ref[:, cols]), 0.0)).astype(BF16)
        f = f + _dot(hid, wdown_ref[cols, :])
    x2 = x1 + _rms(f, g_fpost_ref[...])
    gate = jax.nn.sigmoid(_dot(x2.astype(BF16), wpg_ref[...]))
    o_ref[...] = x2 + gate * _dot(p_ref[...].astype(BF16), wpp_ref[...])


def _post(x2d, ym, att, p2d, gains, weights):
    n = x2d.shape[0]
    tm = min(ROW_TILE, n)
    row = lambda w: pl.BlockSpec((tm, w), lambda i: (i, 0))
    full = lambda a: pl.BlockSpec(a.shape, lambda i: (0,) * a.ndim)
    resident = lambda a: pl.BlockSpec(a.shape, lambda i: (0,) * a.ndim, pipeline_mode=pl.Buffered(1))
    return pl.pallas_call(
        _post_kernel,
        grid=(n // tm,),
        in_specs=[row(D_MODEL), row(M_WIDTH), row(A_WIDTH), row(P_DIM)]
        + [full(g) for g in gains] + [resident(w) for w in weights],
        out_specs=row(D_MODEL),
        out_shape=jax.ShapeDtypeStruct((n, D_MODEL), F32),
        compiler_params=pltpu.CompilerParams(dimension_semantics=("arbitrary",),
                                             vmem_limit_bytes=VMEM_LIMIT_BYTES),
        name="post",
    )(x2d, ym, att, p2d, *gains, *weights)


def _permute_heads(a, axis):
    shape = a.shape
    a = a.reshape(shape[:axis] + (A_HEADS, A_HD) + shape[axis + 1:])
    a = jnp.take(a, np.asarray(HEAD_ORDER), axis=axis)
    return a.reshape(shape)


def kernel(x_prompt, x_sample, p_prompt, p_sample, state_mlstm_c, state_mlstm_n, state_mlstm_m,
           state_mlstm_conv, cache_swa_k, cache_swa_v, norm_mix_pre, w_in, b_gates, conv_w,
           mlstm_norm, attn_sinks, attn_norm, w_out, norm_mix_post, norm_ffn_pre, w_up, w_down,
           norm_ffn_post, w_pgate, w_pproj):
    depth = w_in.shape[0]
    assert depth == 1, "single-layer decoder"
    B, T, _ = x_prompt.shape
    SB, ST, _ = x_sample.shape
    i = 0

    wi = w_in[i]
    o_qk, o_vm, o_om, o_g, o_qa, o_ka, o_va = np.cumsum(
        [0, QK_W, M_WIDTH, M_WIDTH, N_GATES, A_WIDTH, KV_W])
    w_main = jnp.concatenate([wi[:, o_qk:o_g], _permute_heads(wi[:, o_qa:o_ka], 1), wi[:, o_ka:]],
                             axis=1).astype(BF16)
    w_g = wi[:, o_g:o_qa].astype(BF16)
    w_gt = w_g.T
    g_pre = norm_mix_pre[i].reshape(1, D_MODEL)
    bg_row = b_gates[i].reshape(1, N_GATES)
    bg_col = b_gates[i].reshape(N_GATES, 1)
    cw = conv_w[i]
    mn = mlstm_norm[i].reshape(1, M_WIDTH)
    sinks = attn_sinks[i].reshape(1, A_HEADS)
    gains = [_permute_heads(attn_norm[i], 0).reshape(1, A_WIDTH), norm_mix_post[i].reshape(1, D_MODEL),
             norm_ffn_pre[i].reshape(1, D_MODEL), norm_ffn_post[i].reshape(1, D_MODEL)]
    wo = jnp.concatenate([w_out[i][:M_WIDTH], _permute_heads(w_out[i][M_WIDTH:], 0)], axis=0)
    weights = [wo.astype(BF16), w_up[i].astype(BF16), w_down[i].astype(BF16),
               w_pgate[i].astype(BF16), w_pproj[i].astype(BF16)]

    xp = x_prompt.reshape(B * T, D_MODEL)
    pieces = _in_proj(xp, g_pre, w_main, w_g, w_gt)
    ym, att, cn_p, m_p = _mixer_prompt(pieces, bg_row, bg_col, cw, mn, sinks, B, T)
    y_prompt = _post(xp, ym, att, p_prompt[i].reshape(B * T, P_DIM), gains, weights).reshape(B, T, D_MODEL)
    c_p, n_p = cn_p[..., :M_DV], cn_p[..., M_DV]
    qk_p, ka_p, va_p = pieces[0], pieces[4], pieces[5]
    conv_p = qk_p.reshape(B, T, QK_W)[:, T - (CONV_W - 1):]
    k_p = ka_p.reshape(B, T, KV_W)[:, T - WINDOW:].reshape(B, WINDOW, A_KV, A_HD)
    v_p = va_p.reshape(B, T, KV_W)[:, T - WINDOW:].reshape(B, WINDOW, A_KV, A_HD)

    xs = x_sample.transpose(1, 0, 2).reshape(ST * SB, D_MODEL)
    ps = p_sample[i].transpose(1, 0, 2).reshape(ST * SB, P_DIM)
    w_nat = jnp.concatenate([w_main[:, 0:QK_W], w_main[:, QK_W + 2 * M_WIDTH:]], axis=1)
    w_t = w_main[:, 0:QK_W + 2 * M_WIDTH].T
    qkn_s, qa_s, ka_s, va_s, qkt_s, vt_s, ot_s, gt_s = _in_proj_sample(xs, g_pre, w_nat, w_t, w_gt)
    cbt = state_mlstm_conv[i].transpose(1, 2, 0)
    ym_s, c_s2, nt_s, mt_s = _mlstm_sample(
        qkt_s, vt_s, ot_s, gt_s, bg_col, cbt, cw.T, mn.reshape(M_WIDTH, 1),
        state_mlstm_c[i].reshape(SB, M_HEADS * M_DK * M_DV), state_mlstm_n[i].transpose(1, 2, 0),
        state_mlstm_m[i].T)
    c_s = c_s2.reshape(SB, M_HEADS, M_DK, M_DV)
    n_s = nt_s.transpose(2, 0, 1)
    m_s = mt_s.reshape(M_HEADS, SB).T
    conv_s = qkn_s.reshape(ST, SB, QK_W)[ST - (CONV_W - 1):].transpose(1, 0, 2)

    qs = qa_s.reshape(ST, SB, A_GROUP, A_KV, A_HD).transpose(1, 3, 2, 0, 4).reshape(SB, A_HEADS * ST, A_HD)
    kn = ka_s.reshape(ST, SB, A_KV, A_HD).transpose(1, 0, 2, 3)
    vn = va_s.reshape(ST, SB, A_KV, A_HD).transpose(1, 0, 2, 3)
    sink_rows = jnp.repeat(attn_sinks[i], ST).reshape(A_HEADS * ST, 1)
    slope_rows = jnp.asarray(np.repeat(np.asarray(SLOPES, np.float32), ST).reshape(A_HEADS * ST, 1))
    att_s = _attn_sample(qs, kn.reshape(SB, ST * A_KV, A_HD), vn.reshape(SB, ST * A_KV, A_HD),
                         cache_swa_k[i].reshape(SB, WINDOW * A_KV, A_HD),
                         cache_swa_v[i].reshape(SB, WINDOW * A_KV, A_HD), sink_rows, slope_rows)
    att_s2 = att_s.reshape(SB, A_KV, A_GROUP, ST, A_HD).transpose(3, 0, 2, 1, 4).reshape(ST * SB, A_WIDTH)
    y_s = _post(xs, ym_s, att_s2, ps, gains, weights)
    y_sample = y_s.reshape(ST, SB, D_MODEL).transpose(1, 0, 2)
    k_s = jnp.concatenate([cache_swa_k[i][:, ST:], kn], axis=1)
    v_s = jnp.concatenate([cache_swa_v[i][:, ST:], vn], axis=1)

    stack = lambda a: a[None]
    return (y_prompt, y_sample,
            stack(c_p), stack(n_p), stack(m_p.reshape(B, M_HEADS)), stack(conv_p), stack(k_p), stack(v_p),
            stack(c_s), stack(n_s), stack(m_s), stack(conv_s), stack(k_s), stack(v_s))
```

```python
import numpy as np
import jax
import jax.numpy as jnp
from jax import lax
from jax.experimental import pallas as pl
from jax.experimental.pallas import tpu as pltpu

F32 = jnp.float32
BF16 = jnp.bfloat16

D_MODEL = 1024
M_WIDTH = 512
M_HEADS = 4
M_DV = 128
M_DK = 64
QK_W = 512
CONV_W = 4
CHUNK = 128
A_WIDTH = 512
A_HEADS = 8
A_HD = 64
A_KV = 2
A_GROUP = 4
KV_W = 128
WINDOW = 128
D_FF = 4096
P_DIM = 256
EPS = 1e-6
N_GATES = 2 * M_HEADS

VMEM_LIMIT_BYTES = 56 * 1024 * 1024
ROW_TILE = 512
FF_TILE = 1024
SAMPLE_SEQ_TILE = 8

NEG_INF = float("-inf")
SLOPES = [2.0 ** (-8.0 * (h + 1) / A_HEADS) for h in range(A_HEADS)]
HEAD_ORDER = [kv * A_GROUP + g for g in range(A_GROUP) for kv in range(A_KV)]


def _dot(a, b):
    return jnp.dot(a, b, preferred_element_type=F32)


def _dot_nt(a, b):
    return lax.dot_general(a, b, (((1,), (1,)), ((), ())), preferred_element_type=F32)


def _dot_tn(a, b):
    return lax.dot_general(a, b, (((0,), (0,)), ((), ())), preferred_element_type=F32)


def _dot_exact(a, b):
    return lax.dot_general(a, b, (((1,), (0,)), ((), ())), precision=lax.Precision.HIGHEST,
                           preferred_element_type=F32)


def _rms(x, g):
    return x * lax.rsqrt(jnp.mean(x * x, axis=-1, keepdims=True) + EPS) * g


def _in_proj_kernel(x_ref, g_ref, w_ref, wg_ref, wgt_ref,
                    qk_ref, vm_ref, om_ref, qa_ref, ka_ref, va_ref, gc_ref, gr_ref):
    xn = _rms(x_ref[...], g_ref[...]).astype(BF16)
    z = _dot(xn, w_ref[...])
    qk_ref[...] = z[:, 0:512]
    vm_ref[...] = z[:, 512:1024]
    om_ref[...] = z[:, 1024:1536]
    qa_ref[...] = z[:, 1536:2048]
    ka_ref[...] = z[:, 2048:2176]
    va_ref[...] = z[:, 2176:2304]
    gc_ref[...] = _dot(xn, wg_ref[...])
    gr_ref[...] = _dot_nt(wgt_ref[...], xn)


def _in_proj(x2d, g_pre, w_main, w_g, w_gt):
    n = x2d.shape[0]
    tm = min(ROW_TILE, n)
    row = lambda w: pl.BlockSpec((tm, w), lambda i: (i, 0))
    full = lambda a: pl.BlockSpec(a.shape, lambda i: (0,) * a.ndim)
    out_widths = (QK_W, M_WIDTH, M_WIDTH, A_WIDTH, KV_W, KV_W, N_GATES)
    return pl.pallas_call(
        _in_proj_kernel,
        grid=(n // tm,),
        in_specs=[row(D_MODEL), full(g_pre), full(w_main), full(w_g), full(w_gt)],
        out_specs=[row(w) for w in out_widths] + [pl.BlockSpec((N_GATES, tm), lambda i: (0, i))],
        out_shape=[jax.ShapeDtypeStruct((n, w), F32) for w in out_widths]
        + [jax.ShapeDtypeStruct((N_GATES, n), F32)],
        compiler_params=pltpu.CompilerParams(dimension_semantics=("arbitrary",),
                                             vmem_limit_bytes=VMEM_LIMIT_BYTES),
        name="in_proj",
    )(x2d, g_pre, w_main, w_g, w_gt)


def _mlstm_out(h, o, g):
    hn = h * lax.rsqrt(jnp.mean(h * h, axis=-1, keepdims=True) + EPS) * g
    return jax.nn.sigmoid(o) * hn


def _conv_silu(up_ref, base, w, L):
    out = w[0:1, :] * up_ref[base:base + L, :]
    for j in range(1, CONV_W):
        out = out + w[j:j + 1, :] * up_ref[base + j:base + j + L, :]
    return jax.nn.silu(out)


def _mixer_prompt_kernel(qk_ref, vm_ref, om_ref, qa_ref, ka_ref, va_ref, gc_ref, gr_ref,
                         bgr_ref, bgc_ref, cw_ref, mn_ref, sink_ref,
                         ym_ref, att_ref, c_out, m_out,
                         up_ref, c_st, m_st, kp_ref, vp_ref, bias_ref, sinkrep_ref):
    ci = pl.program_id(1)
    L = CHUNK
    t = lax.broadcasted_iota(jnp.int32, (L, L), 0)
    s = lax.broadcasted_iota(jnp.int32, (L, L), 1)

    @pl.when(ci == 0)
    def _():
        up_ref[...] = jnp.zeros_like(up_ref)
        c_st[...] = jnp.zeros_like(c_st)
        m_st[...] = jnp.zeros_like(m_st)
        kp_ref[...] = jnp.zeros_like(kp_ref)
        vp_ref[...] = jnp.zeros_like(vp_ref)
        dist = (t - s).astype(F32)
        for kv in range(A_KV):
            for g in range(A_GROUP):
                hd = kv * A_GROUP + g
                rows = slice(g * L, (g + 1) * L)
                bias_ref[kv, rows, 0:L] = jnp.full((L, L), NEG_INF, F32)
                bias_ref[kv, rows, L:2 * L] = jnp.where(s <= t, -SLOPES[hd] * dist, NEG_INF)
                sinkrep_ref[kv, rows, :] = jnp.broadcast_to(sink_ref[0:1, hd:hd + 1], (L, KV_W))

    @pl.when(ci == 1)
    def _():
        dist = (t - s).astype(F32) + float(WINDOW)
        for kv in range(A_KV):
            for g in range(A_GROUP):
                hd = kv * A_GROUP + g
                bias_ref[kv, g * L:(g + 1) * L, 0:L] = jnp.where(s >= t, -SLOPES[hd] * dist, NEG_INF)

    up_ref[8:8 + L, :] = qk_ref[...]
    qk = _conv_silu(up_ref, 5, cw_ref[...], L)
    up_ref[5:8, :] = qk_ref[L - 3:L, :]

    gcb = gc_ref[...] + bgr_ref[...]
    grb = gr_ref[...] + bgc_ref[...]
    ig_c, lf_c = gcb[:, 0:M_HEADS], jax.nn.log_sigmoid(gcb[:, M_HEADS:N_GATES])
    ig_r, lf_r = grb[0:M_HEADS, :], jax.nn.log_sigmoid(grb[M_HEADS:N_GATES, :])
    b_c = _dot_exact((s <= t).astype(F32), lf_c)
    b_r = _dot_exact(lf_r, (t <= s).astype(F32))
    a_r = ig_r - b_r
    causal_bias = jnp.where(s <= t, 0.0, NEG_INF)
    m_prev = m_st[0:1, 0:M_HEADS]
    hcol = lax.broadcasted_iota(jnp.int32, (L, M_HEADS), 1)
    ones_v = jnp.ones((L, M_DV), BF16)
    H = range(M_HEADS)
    qb = [(qk[:, h * M_DK:(h + 1) * M_DK] * (M_DK ** -0.5)).astype(BF16) for h in H]
    kf = [qk[:, QK_W // 2 + h * M_DK:QK_W // 2 + (h + 1) * M_DK] for h in H]
    kb = [k.astype(BF16) for k in kf]
    vext = [jnp.concatenate([vm_ref[:, h * M_DV:(h + 1) * M_DV].astype(BF16), ones_v], axis=1) for h in H]
    c_prev = [c_st[h] for h in H]
    qk_h = [_dot_nt(qb[h], kb[h]) for h in H]
    qc_h = [_dot(qb[h], c_prev[h].astype(BF16)) for h in H]
    log_d = [b_c[:, h:h + 1] + a_r[h:h + 1, :] + causal_bias for h in H]
    m_intra = jnp.full((L, M_HEADS), NEG_INF, F32)
    for h in H:
        m_intra = jnp.where(hcol == h, jnp.max(log_d[h], axis=-1, keepdims=True), m_intra)
    log_inter = b_c + m_prev
    m_t = jnp.maximum(log_inter, m_intra)
    w_inter = jnp.exp(log_inter - m_t)
    emt = jnp.exp(-m_t)
    s_h = [(qk_h[h] * jnp.exp(log_d[h] - m_t[:, h:h + 1])).astype(BF16) for h in H]
    o_h = [_dot(s_h[h], vext[h]) for h in H]
    for h in H:
        wi = w_inter[:, h:h + 1]
        num = o_h[h][:, 0:M_DV] + wi * qc_h[h][:, 0:M_DV]
        den = o_h[h][:, M_DV:2 * M_DV] + wi * qc_h[h][:, M_DV:2 * M_DV]
        hh = num / jnp.maximum(jnp.abs(den), emt[:, h:h + 1])
        cols = slice(h * M_DV, (h + 1) * M_DV)
        ym_ref[:, cols] = _mlstm_out(hh, om_ref[:, cols], mn_ref[:, cols])
    b_last = b_c[L - 1:L, :]
    log_w = b_last - b_c + ig_c
    m_new = jnp.maximum(b_last + m_prev, jnp.max(log_w, axis=0, keepdims=True))
    w_k = jnp.exp(log_w - m_new)
    decay = jnp.exp(b_last + m_prev - m_new)
    kw = [(kf[h] * w_k[:, h:h + 1]).astype(BF16) for h in H]
    dc = [_dot_tn(kw[h], vext[h]) for h in H]
    for h in H:
        c_st[h] = decay[:, h:h + 1] * c_prev[h] + dc[h]
    m_st[0:1, 0:M_HEADS] = m_new

    lane = lax.broadcasted_iota(jnp.int32, (2 * L, KV_W), 1)
    low2 = lane < A_HD
    low = lax.broadcasted_iota(jnp.int32, (L, KV_W), 1) < A_HD
    k_both = jnp.concatenate([kp_ref[...], ka_ref[...]], axis=0)
    v_both = jnp.concatenate([vp_ref[...], va_ref[...]], axis=0)
    ones_k = jnp.ones((2 * L, KV_W), BF16)
    kmask = [jnp.where(low2, k_both, 0.0).astype(BF16), jnp.where(low2, 0.0, k_both).astype(BF16)]
    vext_a = [jnp.concatenate([jnp.where(low2, v_both, 1.0).astype(BF16), ones_k], axis=1),
              jnp.concatenate([jnp.where(low2, 1.0, v_both).astype(BF16), ones_k], axis=1)]
    qst = jnp.concatenate([qa_ref[:, g * KV_W:(g + 1) * KV_W] for g in range(A_GROUP)], axis=0)
    qst = (qst * (A_HD ** -0.5)).astype(BF16)
    KV = range(A_KV)
    sc = [_dot_nt(qst, kmask[kv]) + bias_ref[kv] for kv in KV]
    m_a = [jnp.maximum(jnp.max(jnp.maximum(sc[kv][:, 0:L], sc[kv][:, L:2 * L]), axis=-1, keepdims=True),
                       sinkrep_ref[kv]) for kv in KV]
    e = [jnp.concatenate([jnp.exp(sc[kv][:, 0:L] - m_a[kv]), jnp.exp(sc[kv][:, L:2 * L] - m_a[kv])],
                         axis=1).astype(BF16) for kv in KV]
    o = [_dot(e[kv], vext_a[kv]) for kv in KV]
    exs = [jnp.exp(sinkrep_ref[kv] - m_a[kv]) for kv in KV]
    for g in range(A_GROUP):
        rows = slice(g * L, (g + 1) * L)
        pv = jnp.where(low, o[0][rows, 0:KV_W], o[1][rows, 0:KV_W])
        den = jnp.where(low, o[0][rows, KV_W:2 * KV_W] + exs[0][rows, :],
                        o[1][rows, KV_W:2 * KV_W] + exs[1][rows, :])
        att_ref[:, g * KV_W:(g + 1) * KV_W] = pv / den
    kp_ref[...] = ka_ref[...]
    vp_ref[...] = va_ref[...]

    @pl.when(ci == pl.num_programs(1) - 1)
    def _():
        c_out[0] = c_st[...]
        m_out[0] = m_st[0:1, 0:M_HEADS]


def _mixer_prompt(pieces, bg_row, bg_col, conv_w, mlstm_norm, sinks, batch, seq):
    qk, vm, om, qa, ka, va, gc, gr = pieces
    nc = seq // CHUNK
    row = lambda w: pl.BlockSpec((CHUNK, w), lambda b, c: (b * nc + c, 0))
    full = lambda a: pl.BlockSpec(a.shape, lambda b, c: (0,) * a.ndim)
    n = batch * seq
    return pl.pallas_call(
        _mixer_prompt_kernel,
        grid=(batch, nc),
        in_specs=[row(QK_W), row(M_WIDTH), row(M_WIDTH), row(A_WIDTH), row(KV_W), row(KV_W), row(N_GATES),
                  pl.BlockSpec((N_GATES, CHUNK), lambda b, c: (0, b * nc + c)),
                  full(bg_row), full(bg_col), full(conv_w), full(mlstm_norm), full(sinks)],
        out_specs=[row(M_WIDTH), row(A_WIDTH),
                   pl.BlockSpec((1, M_HEADS, M_DK, 2 * M_DV), lambda b, c: (b, 0, 0, 0)),
                   pl.BlockSpec((1, 1, M_HEADS), lambda b, c: (b, 0, 0))],
        out_shape=[jax.ShapeDtypeStruct((n, M_WIDTH), F32), jax.ShapeDtypeStruct((n, A_WIDTH), F32),
                   jax.ShapeDtypeStruct((batch, M_HEADS, M_DK, 2 * M_DV), F32),
                   jax.ShapeDtypeStruct((batch, 1, M_HEADS), F32)],
        scratch_shapes=[pltpu.VMEM((8 + CHUNK, QK_W), F32),
                        pltpu.VMEM((M_HEADS, M_DK, 2 * M_DV), F32),
                        pltpu.VMEM((8, 128), F32),
                        pltpu.VMEM((CHUNK, KV_W), F32), pltpu.VMEM((CHUNK, KV_W), F32),
                        pltpu.VMEM((A_KV, A_GROUP * CHUNK, 2 * CHUNK), F32),
                        pltpu.VMEM((A_KV, A_GROUP * CHUNK, KV_W), F32)],
        compiler_params=pltpu.CompilerParams(dimension_semantics=("arbitrary", "arbitrary"),
                                             vmem_limit_bytes=VMEM_LIMIT_BYTES),
        name="mixer_prompt",
    )(qk, vm, om, qa, ka, va, gc, gr, bg_row, bg_col, conv_w, mlstm_norm, sinks)


def _mixer_sample(qk, vm, om, qs, kn, vn, gc, gr, c0, n0, m0, cb, kc, vc,
                  bg_row, bg_col, conv_w, mlstm_norm, sink_rows, slope_rows):
    nb, L = qk.shape[0], qk.shape[1]
    ts = min(SAMPLE_SEQ_TILE, nb)
    per_b = lambda a: pl.BlockSpec((ts,) + a.shape[1:], lambda b: (b,) + (0,) * (a.ndim - 1))
    full = lambda a: pl.BlockSpec(a.shape, lambda b: (0,) * a.ndim)
    ins = [qk, vm, om, qs, kn, vn, gc, gr, c0, n0, m0, cb, kc, vc]
    consts = [bg_row, bg_col, conv_w, mlstm_norm, sink_rows, slope_rows]
    outs = [jax.ShapeDtypeStruct((nb, L, M_WIDTH), F32),
            jax.ShapeDtypeStruct((nb, A_KV, A_GROUP * L, A_HD), F32),
            jax.ShapeDtypeStruct(c0.shape, F32), jax.ShapeDtypeStruct(n0.shape, F32),
            jax.ShapeDtypeStruct(m0.shape, F32), jax.ShapeDtypeStruct(cb.shape, F32),
            jax.ShapeDtypeStruct(kc.shape, F32), jax.ShapeDtypeStruct(vc.shape, F32)]
    return pl.pallas_call(
        _mixer_sample_tile_kernel,
        grid=(nb // ts,),
        in_specs=[per_b(a) for a in ins] + [full(a) for a in consts],
        out_specs=[per_b(o) for o in outs],
        out_shape=outs,
        scratch_shapes=[pltpu.VMEM((ts, 8, QK_W), F32)],
        compiler_params=pltpu.CompilerParams(dimension_semantics=("arbitrary",),
                                             vmem_limit_bytes=VMEM_LIMIT_BYTES),
        name="mixer_sample",
    )(*ins, *consts)


def _small_cumsum(x, axis):
    idx = lax.broadcasted_iota(jnp.int32, x.shape, axis)
    out = jnp.zeros_like(x)
    for r in range(x.shape[axis]):
        out = out + jnp.where(idx >= r, lax.slice_in_dim(x, r, r + 1, axis=axis), 0.0)
    return out


def _mixer_sample_tile_kernel(qk_ref, vm_ref, om_ref, qs_ref, kn_ref, vn_ref, gc_ref, gr_ref,
                              c0_ref, n0_ref, m0_ref, cb_ref, kc_ref, vc_ref,
                              bgr_ref, bgc_ref, cw_ref, mn_ref, sinkr_ref, sloper_ref,
                              ym_ref, att_ref, c_out, n_out, m_out, cb_out, k_out, v_out,
                              up_ref):
    NB, L = qk_ref.shape[0], qk_ref.shape[1]
    SEQ = range(NB)
    H = range(M_HEADS)
    cw = cw_ref[...]

    for b in SEQ:
        up_ref[b, 0:CONV_W - 1, :] = cb_ref[b]
        up_ref[b, CONV_W - 1:CONV_W - 1 + L, :] = qk_ref[b]
    qk = []
    for b in SEQ:
        acc = cw[0:1, :] * up_ref[b, 0:L, :]
        for j in range(1, CONV_W):
            acc = acc + cw[j:j + 1, :] * up_ref[b, j:j + L, :]
        qk.append(jax.nn.silu(acc))
        cb_out[b] = up_ref[b, L:L + CONV_W - 1, :]

    r = lax.broadcasted_iota(jnp.int32, (L, L), 0)
    c = lax.broadcasted_iota(jnp.int32, (L, L), 1)
    causal = c <= r
    gcb = [gc_ref[b] + bgr_ref[...] for b in SEQ]
    grb = [gr_ref[b] + bgc_ref[...] for b in SEQ]
    ig_c = [g[:, 0:M_HEADS] for g in gcb]
    ig_r = [g[0:M_HEADS, :] for g in grb]
    b_c = [_small_cumsum(jax.nn.log_sigmoid(g[:, M_HEADS:N_GATES]), 0) for g in gcb]
    b_r = [_small_cumsum(jax.nn.log_sigmoid(g[M_HEADS:N_GATES, :]), 1) for g in grb]

    P = [(b, h) for b in SEQ for h in H]
    q = [qk[b][:, h * M_DK:(h + 1) * M_DK] * (M_DK ** -0.5) for b, h in P]
    k = [qk[b][:, QK_W // 2 + h * M_DK:QK_W // 2 + (h + 1) * M_DK] for b, h in P]
    qb = [x.astype(BF16) for x in q]
    kb = [x.astype(BF16) for x in k]
    vb = [vm_ref[b, :, h * M_DV:(h + 1) * M_DV].astype(BF16) for b, h in P]
    c_prev = [c0_ref[b, h] for b, h in P]
    n_prev = [n0_ref[b, h:h + 1, :] for b, h in P]
    m_prev = [m0_ref[b, 0:1, h:h + 1] for b, h in P]
    bc = [b_c[b][:, h:h + 1] for b, h in P]
    igc = [ig_c[b][:, h:h + 1] for b, h in P]
    log_d = [jnp.where(causal, b_c[b][:, h:h + 1] - b_r[b][h:h + 1, :] + ig_r[b][h:h + 1, :], NEG_INF)
             for b, h in P]
    N = range(len(P))
    qkt = [_dot_nt(qb[i], kb[i]) for i in N]
    qc = [_dot(qb[i], c_prev[i].astype(BF16)) for i in N]
    log_inter = [bc[i] + m_prev[i] for i in N]
    m_t = [jnp.maximum(log_inter[i], jnp.max(log_d[i], axis=-1, keepdims=True)) for i in N]
    w_inter = [jnp.exp(log_inter[i] - m_t[i]) for i in N]
    s = [qkt[i] * jnp.exp(log_d[i] - m_t[i]) for i in N]
    sv = [_dot(s[i].astype(BF16), vb[i]) for i in N]
    qn = [jnp.sum(q[i] * n_prev[i], axis=-1, keepdims=True) for i in N]
    for i, (b, h) in enumerate(P):
        num = sv[i] + w_inter[i] * qc[i]
        den = jnp.sum(s[i], axis=-1, keepdims=True) + w_inter[i] * qn[i]
        hh = num / jnp.maximum(jnp.abs(den), jnp.exp(-m_t[i]))
        cols = slice(h * M_DV, (h + 1) * M_DV)
        ym_ref[b, :, cols] = _mlstm_out(hh, om_ref[b, :, cols], mn_ref[:, cols])
    b_last = [bc[i][L - 1:L, :] for i in N]
    log_w = [b_last[i] - bc[i] + igc[i] for i in N]
    m_new = [jnp.maximum(b_last[i] + m_prev[i], jnp.max(log_w[i], axis=0, keepdims=True)) for i in N]
    kw = [k[i] * jnp.exp(log_w[i] - m_new[i]) for i in N]
    decay = [jnp.exp(b_last[i] + m_prev[i] - m_new[i]) for i in N]
    dc = [_dot_tn(kw[i].astype(BF16), vb[i]) for i in N]
    for i, (b, h) in enumerate(P):
        c_out[b, h] = decay[i] * c_prev[i] + dc[i]
        n_out[b, h:h + 1, :] = decay[i] * n_prev[i] + jnp.sum(kw[i], axis=0, keepdims=True)
        m_out[b, 0:1, h:h + 1] = m_new[i]

    rows = A_GROUP * L

    def token_of_row(shape):
        ri = lax.broadcasted_iota(jnp.int32, shape, 0).astype(F32)
        return ri - L * jnp.floor((ri + 0.5) / L)

    ti = token_of_row((rows, WINDOW))
    ji = lax.broadcasted_iota(jnp.int32, (rows, WINDOW), 1).astype(F32)
    tn = token_of_row((rows, L))
    sn = lax.broadcasted_iota(jnp.int32, (rows, L), 1).astype(F32)
    bias_c = [jnp.where(ji >= ti, -sloper_ref[kv] * (ti + WINDOW - ji), NEG_INF) for kv in range(A_KV)]
    bias_n = [jnp.where(sn <= tn, -sloper_ref[kv] * (tn - sn), NEG_INF) for kv in range(A_KV)]
    Q = [(b, kv) for b in SEQ for kv in range(A_KV)]
    M = range(len(Q))
    qh = [(qs_ref[b, kv] * (A_HD ** -0.5)).astype(BF16) for b, kv in Q]
    kcb = [kc_ref[b, :, kv * A_HD:(kv + 1) * A_HD].astype(BF16) for b, kv in Q]
    knb = [kn_ref[b, :, kv * A_HD:(kv + 1) * A_HD].astype(BF16) for b, kv in Q]
    vcb = [vc_ref[b, :, kv * A_HD:(kv + 1) * A_HD].astype(BF16) for b, kv in Q]
    vnb = [vn_ref[b, :, kv * A_HD:(kv + 1) * A_HD].astype(BF16) for b, kv in Q]
    sc_c = [_dot_nt(qh[i], kcb[i]) + bias_c[Q[i][1]] for i in M]
    sc_n = [_dot_nt(qh[i], knb[i]) + bias_n[Q[i][1]] for i in M]
    sink = [sinkr_ref[kv] for b, kv in Q]
    m_a = [jnp.maximum(sink[i], jnp.maximum(jnp.max(sc_c[i], axis=-1, keepdims=True),
                                            jnp.max(sc_n[i], axis=-1, keepdims=True))) for i in M]
    e_c = [jnp.exp(sc_c[i] - m_a[i]) for i in M]
    e_n = [jnp.exp(sc_n[i] - m_a[i]) for i in M]
    den_a = [jnp.exp(sink[i] - m_a[i]) + jnp.sum(e_c[i], axis=-1, keepdims=True)
             + jnp.sum(e_n[i], axis=-1, keepdims=True) for i in M]
    pv = [_dot((e_c[i] / den_a[i]).astype(BF16), vcb[i]) + _dot((e_n[i] / den_a[i]).astype(BF16), vnb[i])
          for i in M]
    for i, (b, kv) in enumerate(Q):
        att_ref[b, kv] = pv[i]

    for b in SEQ:
        k_out[b, 0:WINDOW - L, :] = kc_ref[b, L:WINDOW, :]
        k_out[b, WINDOW - L:WINDOW, :] = kn_ref[b]
        v_out[b, 0:WINDOW - L, :] = vc_ref[b, L:WINDOW, :]
        v_out[b, WINDOW - L:WINDOW, :] = vn_ref[b]


def _in_proj_sample_kernel(x_ref, g_ref, wn_ref, wt_ref, wgt_ref,
                           qkn_ref, qa_ref, ka_ref, va_ref, qkt_ref, vt_ref, ot_ref, gt_ref):
    xn = _rms(x_ref[...], g_ref[...]).astype(BF16)
    zn = _dot(xn, wn_ref[...])
    qkn_ref[...] = zn[:, 0:QK_W]
    qa_ref[...] = zn[:, QK_W:QK_W + A_WIDTH]
    ka_ref[...] = zn[:, QK_W + A_WIDTH:QK_W + A_WIDTH + KV_W]
    va_ref[...] = zn[:, QK_W + A_WIDTH + KV_W:QK_W + A_WIDTH + 2 * KV_W]
    zt = _dot_nt(wt_ref[...], xn)
    qkt_ref[...] = zt[0:QK_W, :]
    vt_ref[...] = zt[QK_W:QK_W + M_WIDTH, :]
    ot_ref[...] = zt[QK_W + M_WIDTH:QK_W + 2 * M_WIDTH, :]
    gt_ref[...] = _dot_nt(wgt_ref[...], xn)


def _in_proj_sample(x2d, g_pre, w_nat, w_t, w_gt):
    n = x2d.shape[0]
    shapes = [(n, QK_W), (n, A_WIDTH), (n, KV_W), (n, KV_W), (QK_W, n), (M_WIDTH, n), (M_WIDTH, n), (N_GATES, n)]
    return pl.pallas_call(
        _in_proj_sample_kernel,
        out_shape=[jax.ShapeDtypeStruct(s, F32) for s in shapes],
        compiler_params=pltpu.CompilerParams(vmem_limit_bytes=VMEM_LIMIT_BYTES),
        name="in_proj_sample",
    )(x2d, g_pre, w_nat, w_t, w_gt)


SAMPLE_V_TILE = 32


def _mlstm_sample_kernel(qt_ref, kt_ref, vt_ref, ot_ref, gt_ref, bg_ref, cbq_ref, cbk_ref, cwq_ref, cwk_ref,
                         gain_ref, c_ref, n_ref, m_ref,
                         ym_ref, c_out, n_out, m_out,
                         ct_ref, q_s, ik_s, num_s):
    h = pl.program_id(0)
    NB = c_ref.shape[0]
    L = qt_ref.shape[1] // NB
    T = range(L)

    def conv(raw_ref, cb_ref, cw_ref):
        ups = [cb_ref[j] for j in range(CONV_W - 1)] + [raw_ref[:, t * NB:(t + 1) * NB] for t in T]
        w = [cw_ref[:, j:j + 1] for j in range(CONV_W)]
        outs = []
        for t in T:
            acc = w[0] * ups[t]
            for j in range(1, CONV_W):
                acc = acc + w[j] * ups[t + j]
            outs.append(jax.nn.silu(acc))
        return outs

    q = [x * (M_DK ** -0.5) for x in conv(qt_ref, cbq_ref, cwq_ref)]
    k = conv(kt_ref, cbk_ref, cwk_ref)

    ig_all = gt_ref[pl.ds(h, 1), :] + bg_ref[pl.ds(h, 1), :]
    lf_all = jax.nn.log_sigmoid(gt_ref[pl.ds(h + M_HEADS, 1), :] + bg_ref[pl.ds(h + M_HEADS, 1), :])
    m = m_ref[pl.ds(h, 1), :]
    f, ms = [], []
    n = n_ref[0]
    den = []
    for t in T:
        ig, lf = ig_all[:, t * NB:(t + 1) * NB], lf_all[:, t * NB:(t + 1) * NB]
        m_new = jnp.maximum(lf + m, ig)
        f_t = jnp.exp(lf + m - m_new)
        ik = jnp.exp(ig - m_new) * k[t]
        m = m_new
        n = f_t * n + ik
        f.append(f_t)
        ms.append(m_new)
        den.append(jnp.sum(q[t] * n, axis=0, keepdims=True))
        ik_s[t] = ik
        q_s[t] = q[t]
    m_out[0] = m
    n_out[0] = n

    for d in range(M_DK):
        ct_ref[d * M_DV:(d + 1) * M_DV, :] = c_ref[:, d * M_DV:(d + 1) * M_DV].T

    VT = SAMPLE_V_TILE
    for vq in range(M_DV // VT):
        vts = [vt_ref[vq * VT:(vq + 1) * VT, t * NB:(t + 1) * NB] for t in T]

        def step(d, accs, vq=vq, vts=vts):
            r0 = pl.multiple_of(d * M_DV + vq * VT, VT)
            c = ct_ref[pl.ds(r0, VT), :]
            new = []
            for t in T:
                c = f[t] * c + ik_s[t, pl.ds(d, 1), :] * vts[t]
                new.append(accs[t] + q_s[t, pl.ds(d, 1), :] * c)
            ct_ref[pl.ds(r0, VT), :] = c
            return tuple(new)

        accs = lax.fori_loop(0, M_DK, step, tuple(jnp.zeros((VT, NB), F32) for _ in T), unroll=8)
        for t in T:
            num_s[t, vq * VT:(vq + 1) * VT, :] = accs[t]

    for t in T:
        hh = num_s[t] / jnp.maximum(jnp.abs(den[t]), jnp.exp(-ms[t]))
        hn = hh * lax.rsqrt(jnp.mean(hh * hh, axis=0, keepdims=True) + EPS) * gain_ref[...]
        y = jax.nn.sigmoid(ot_ref[:, t * NB:(t + 1) * NB]) * hn
        ym_ref[t * NB:(t + 1) * NB, :] = y.T

    for d in range(M_DK):
        c_out[:, d * M_DV:(d + 1) * M_DV] = ct_ref[d * M_DV:(d + 1) * M_DV, :].T


def _mlstm_sample(qkt, vt, ot, gt, bg_col, cbt, cwt, gain_col, c2d, nt, mt):
    nb = c2d.shape[0]
    n = qkt.shape[1]
    L = n // nb
    hblk = lambda rows, off: pl.BlockSpec((rows, n), lambda h, off=off: (h + off, 0))
    full = lambda a: pl.BlockSpec(a.shape, lambda h: (0,) * a.ndim)
    kq = QK_W // 2 // M_DK
    return pl.pallas_call(
        _mlstm_sample_kernel,
        grid=(M_HEADS,),
        in_specs=[hblk(M_DK, 0), hblk(M_DK, kq), hblk(M_DV, 0), hblk(M_DV, 0), full(gt), full(bg_col),
                  pl.BlockSpec((CONV_W - 1, M_DK, nb), lambda h: (0, h, 0)),
                  pl.BlockSpec((CONV_W - 1, M_DK, nb), lambda h: (0, h + kq, 0)),
                  pl.BlockSpec((M_DK, CONV_W), lambda h: (h, 0)),
                  pl.BlockSpec((M_DK, CONV_W), lambda h: (h + kq, 0)),
                  pl.BlockSpec((M_DV, 1), lambda h: (h, 0)),
                  pl.BlockSpec((nb, M_DK * M_DV), lambda h: (0, h)),
                  pl.BlockSpec((1, M_DK, nb), lambda h: (h, 0, 0)),
                  full(mt)],
        out_specs=[pl.BlockSpec((n, M_DV), lambda h: (0, h)),
                   pl.BlockSpec((nb, M_DK * M_DV), lambda h: (0, h)),
                   pl.BlockSpec((1, M_DK, nb), lambda h: (h, 0, 0)),
                   pl.BlockSpec((1, 1, nb), lambda h: (h, 0, 0))],
        out_shape=[jax.ShapeDtypeStruct((n, M_WIDTH), F32), jax.ShapeDtypeStruct(c2d.shape, F32),
                   jax.ShapeDtypeStruct(nt.shape, F32), jax.ShapeDtypeStruct((M_HEADS, 1, nb), F32)],
        scratch_shapes=[pltpu.VMEM((M_DK * M_DV, nb), F32), pltpu.VMEM((L, M_DK, nb), F32),
                        pltpu.VMEM((L, M_DK, nb), F32), pltpu.VMEM((L, M_DV, nb), F32)],
        compiler_params=pltpu.CompilerParams(dimension_semantics=("arbitrary",),
                                             vmem_limit_bytes=VMEM_LIMIT_BYTES),
        name="mlstm_sample",
    )(qkt, qkt, vt, ot, gt, bg_col, cbt, cbt, cwt, cwt, gain_col, c2d, nt, mt)


def _attn_sample_kernel(qm_ref, kn_ref, vn_ref, kct_ref, vct_ref, sinkr_ref, sloper_ref,
                        att_ref, kct_out, vct_out, xk_ref, xv_ref):
    NB, L = kn_ref.shape[0], kn_ref.shape[1]
    R = A_KV * A_GROUP * L

    def split(idx, inner):
        x = idx.astype(F32)
        hi = jnp.floor((x + 0.5) / inner)
        return hi, x - inner * hi

    def bias(nkeys, offset):
        r = lax.broadcasted_iota(jnp.int32, (R, nkeys), 0)
        pos = lax.broadcasted_iota(jnp.int32, (R, nkeys), 1).astype(F32)
        _, tq = split(r, L)
        dist = tq + offset - pos
        return jnp.where((dist >= 0.0) & (dist <= float(WINDOW)), -sloper_ref[...] * dist, NEG_INF)

    @pl.when(pl.program_id(0) == 0)
    def _():
        xk_ref[...] = jnp.zeros_like(xk_ref)
        xv_ref[...] = jnp.zeros_like(xv_ref)

    bias_c = bias(WINDOW, float(WINDOW))
    bias_n = bias(L, 0.0)
    sink = sinkr_ref[...]
    rq = lax.broadcasted_iota(jnp.int32, (R, KV_W), 0)
    cq = lax.broadcasted_iota(jnp.int32, (R, KV_W), 1)
    same_kv = (rq < A_GROUP * L) == (cq < A_HD)
    newest = lax.broadcasted_iota(jnp.int32, (KV_W, WINDOW), 1) >= WINDOW - L
    SEQ = range(NB)
    qh = [jnp.where(same_kv, qm_ref[b] * (A_HD ** -0.5), 0.0).astype(BF16) for b in SEQ]
    kt = [kct_ref[b] for b in SEQ]
    vt = [vct_ref[b] for b in SEQ]
    knb = [kn_ref[b].astype(BF16) for b in SEQ]
    vnb = [vn_ref[b].astype(BF16) for b in SEQ]
    sc_c = [_dot(qh[b], kt[b].astype(BF16)) + bias_c for b in SEQ]
    sc_n = [_dot_nt(qh[b], knb[b]) + bias_n for b in SEQ]
    m_a = [jnp.maximum(sink, jnp.maximum(jnp.max(sc_c[b], axis=-1, keepdims=True),
                                         jnp.max(sc_n[b], axis=-1, keepdims=True))) for b in SEQ]
    e_c = [jnp.exp(sc_c[b] - m_a[b]) for b in SEQ]
    e_n = [jnp.exp(sc_n[b] - m_a[b]) for b in SEQ]
    den_a = [jnp.exp(sink - m_a[b]) + jnp.sum(e_c[b], axis=-1, keepdims=True)
             + jnp.sum(e_n[b], axis=-1, keepdims=True) for b in SEQ]
    pv = [_dot_nt((e_c[b] / den_a[b]).astype(BF16), vt[b].astype(BF16))
          + _dot((e_n[b] / den_a[b]).astype(BF16), vnb[b]) for b in SEQ]
    for b in SEQ:
        att_ref[b] = pv[b]

    for b in SEQ:
        xk_ref[b, WINDOW - L:WINDOW, :] = kn_ref[b]
        xv_ref[b, WINDOW - L:WINDOW, :] = vn_ref[b]
    for b in SEQ:
        kct_out[b] = jnp.where(newest, xk_ref[b].T, pltpu.roll(kt[b], WINDOW - L, axis=1))
        vct_out[b] = jnp.where(newest, xv_ref[b].T, pltpu.roll(vt[b], WINDOW - L, axis=1))


def _attn_sample(qm, kn, vn, kct, vct, sink_rows, slope_rows):
    nb = qm.shape[0]
    ts = min(SAMPLE_SEQ_TILE, nb)
    per_b = lambda a: pl.BlockSpec((ts,) + a.shape[1:], lambda b: (b,) + (0,) * (a.ndim - 1))
    full = lambda a: pl.BlockSpec(a.shape, lambda b: (0,) * a.ndim)
    outs = [jax.ShapeDtypeStruct(a.shape, F32) for a in (qm, kct, vct)]
    return pl.pallas_call(
        _attn_sample_kernel,
        grid=(nb // ts,),
        in_specs=[per_b(a) for a in (qm, kn, vn, kct, vct)] + [full(sink_rows), full(slope_rows)],
        out_specs=[per_b(o) for o in outs],
        out_shape=outs,
        scratch_shapes=[pltpu.VMEM((ts, WINDOW, KV_W), F32), pltpu.VMEM((ts, WINDOW, KV_W), F32)],
        compiler_params=pltpu.CompilerParams(dimension_semantics=("arbitrary",),
                                             vmem_limit_bytes=VMEM_LIMIT_BYTES),
        name="attn_sample",
    )(qm, kn, vn, kct, vct, sink_rows, slope_rows)


def _post_kernel(x_ref, ym_ref, att_ref, p_ref, an_ref, g_post_ref, g_fpre_ref, g_fpost_ref,
                 wout_ref, wup_ref, wdown_ref, wpg_ref, wpp_ref, o_ref):
    y_a = _rms(att_ref[...], an_ref[...])
    y = jnp.concatenate([ym_ref[...], y_a], axis=-1).astype(BF16)
    x1 = x_ref[...] + _rms(_dot(y, wout_ref[...]), g_post_ref[...])
    u = _rms(x1, g_fpre_ref[...]).astype(BF16)
    f = jnp.zeros_like(x1)
    for j in range(D_FF // FF_TILE):
        cols = slice(j * FF_TILE, (j + 1) * FF_TILE)
        hid = jnp.square(jnp.maximum(_dot(u, wup_ref[:, cols]), 0.0)).astype(BF16)
        f = f + _dot(hid, wdown_ref[cols, :])
    x2 = x1 + _rms(f, g_fpost_ref[...])
    gate = jax.nn.sigmoid(_dot(x2.astype(BF16), wpg_ref[...]))
    o_ref[...] = x2 + gate * _dot(p_ref[...].astype(BF16), wpp_ref[...])


def _post(x2d, ym, att, p2d, gains, weights):
    n = x2d.shape[0]
    tm = min(ROW_TILE, n)
    row = lambda w: pl.BlockSpec((tm, w), lambda i: (i, 0))
    full = lambda a: pl.BlockSpec(a.shape, lambda i: (0,) * a.ndim)
    resident = lambda a: pl.BlockSpec(a.shape, lambda i: (0,) * a.ndim, pipeline_mode=pl.Buffered(1))
    return pl.pallas_call(
        _post_kernel,
        grid=(n // tm,),
        in_specs=[row(D_MODEL), row(M_WIDTH), row(A_WIDTH), row(P_DIM)]
        + [full(g) for g in gains] + [resident(w) for w in weights],
        out_specs=row(D_MODEL),
        out_shape=jax.ShapeDtypeStruct((n, D_MODEL), F32),
        compiler_params=pltpu.CompilerParams(dimension_semantics=("arbitrary",),
                                             vmem_limit_bytes=VMEM_LIMIT_BYTES),
        name="post",
    )(x2d, ym, att, p2d, *gains, *weights)


def _permute_heads(a, axis):
    shape = a.shape
    a = a.reshape(shape[:axis] + (A_HEADS, A_HD) + shape[axis + 1:])
    a = jnp.take(a, np.asarray(HEAD_ORDER), axis=axis)
    return a.reshape(shape)


def kernel(x_prompt, x_sample, p_prompt, p_sample, state_mlstm_c, state_mlstm_n, state_mlstm_m,
           state_mlstm_conv, cache_swa_k, cache_swa_v, norm_mix_pre, w_in, b_gates, conv_w,
           mlstm_norm, attn_sinks, attn_norm, w_out, norm_mix_post, norm_ffn_pre, w_up, w_down,
           norm_ffn_post, w_pgate, w_pproj):
    depth = w_in.shape[0]
    assert depth == 1, "single-layer decoder"
    B, T, _ = x_prompt.shape
    SB, ST, _ = x_sample.shape
    i = 0

    wi = w_in[i]
    o_qk, o_vm, o_om, o_g, o_qa, o_ka, o_va = np.cumsum(
        [0, QK_W, M_WIDTH, M_WIDTH, N_GATES, A_WIDTH, KV_W])
    w_main = jnp.concatenate([wi[:, o_qk:o_g], _permute_heads(wi[:, o_qa:o_ka], 1), wi[:, o_ka:]],
                             axis=1).astype(BF16)
    w_g = wi[:, o_g:o_qa].astype(BF16)
    w_gt = w_g.T
    g_pre = norm_mix_pre[i].reshape(1, D_MODEL)
    bg_row = b_gates[i].reshape(1, N_GATES)
    bg_col = b_gates[i].reshape(N_GATES, 1)
    cw = conv_w[i]
    mn = mlstm_norm[i].reshape(1, M_WIDTH)
    sinks = attn_sinks[i].reshape(1, A_HEADS)
    gains = [_permute_heads(attn_norm[i], 0).reshape(1, A_WIDTH), norm_mix_post[i].reshape(1, D_MODEL),
             norm_ffn_pre[i].reshape(1, D_MODEL), norm_ffn_post[i].reshape(1, D_MODEL)]
    wo = jnp.concatenate([w_out[i][:M_WIDTH], _permute_heads(w_out[i][M_WIDTH:], 0)], axis=0)
    weights = [wo.astype(BF16), w_up[i].astype(BF16), w_down[i].astype(BF16),
               w_pgate[i].astype(BF16), w_pproj[i].astype(BF16)]

    xp = x_prompt.reshape(B * T, D_MODEL)
    pieces = _in_proj(xp, g_pre, w_main, w_g, w_gt)
    ym, att, cn_p, m_p = _mixer_prompt(pieces, bg_row, bg_col, cw, mn, sinks, B, T)
    y_prompt = _post(xp, ym, att, p_prompt[i].reshape(B * T, P_DIM), gains, weights).reshape(B, T, D_MODEL)
    c_p, n_p = cn_p[..., :M_DV], cn_p[..., M_DV]
    qk_p, ka_p, va_p = pieces[0], pieces[4], pieces[5]
    conv_p = qk_p.reshape(B, T, QK_W)[:, T - (CONV_W - 1):]
    k_p = ka_p.reshape(B, T, KV_W)[:, T - WINDOW:].reshape(B, WINDOW, A_KV, A_HD)
    v_p = va_p.reshape(B, T, KV_W)[:, T - WINDOW:].reshape(B, WINDOW, A_KV, A_HD)

    xs = x_sample.transpose(1, 0, 2).reshape(ST * SB, D_MODEL)
    ps = p_sample[i].transpose(1, 0, 2).reshape(ST * SB, P_DIM)
    w_nat = jnp.concatenate([w_main[:, 0:QK_W], w_main[:, QK_W + 2 * M_WIDTH:]], axis=1)
    w_t = w_main[:, 0:QK_W + 2 * M_WIDTH].T
    qkn_s, qa_s, ka_s, va_s, qkt_s, vt_s, ot_s, gt_s = _in_proj_sample(xs, g_pre, w_nat, w_t, w_gt)
    cbt = state_mlstm_conv[i].transpose(1, 2, 0)
    ym_s, c_s2, nt_s, mt_s = _mlstm_sample(
        qkt_s, vt_s, ot_s, gt_s, bg_col, cbt, cw.T, mn.reshape(M_WIDTH, 1),
        state_mlstm_c[i].reshape(SB, M_HEADS * M_DK * M_DV), state_mlstm_n[i].transpose(1, 2, 0),
        state_mlstm_m[i].T)
    c_s = c_s2.reshape(SB, M_HEADS, M_DK, M_DV)
    n_s = nt_s.transpose(2, 0, 1)
    m_s = mt_s.reshape(M_HEADS, SB).T
    conv_s = qkn_s.reshape(ST, SB, QK_W)[ST - (CONV_W - 1):].transpose(1, 0, 2)

    qs = qa_s.reshape(ST, SB, A_GROUP, A_KV, A_HD).transpose(1, 3, 2, 0, 4).reshape(SB, A_HEADS * ST, A_HD)
    qm = jnp.concatenate([qs, qs], axis=-1)
    kn = ka_s.reshape(ST, SB, KV_W).transpose(1, 0, 2)
    vn = va_s.reshape(ST, SB, KV_W).transpose(1, 0, 2)
    sink_rows = jnp.repeat(attn_sinks[i], ST).reshape(A_HEADS * ST, 1)
    slope_rows = jnp.asarray(np.repeat(np.asarray(SLOPES, np.float32), ST).reshape(A_HEADS * ST, 1))
    kct = cache_swa_k[i].transpose(0, 2, 3, 1).reshape(SB, KV_W, WINDOW)
    vct = cache_swa_v[i].transpose(0, 2, 3, 1).reshape(SB, KV_W, WINDOW)
    att_s, kct_s, vct_s = _attn_sample(qm, kn, vn, kct, vct, sink_rows, slope_rows)
    att_d = att_s.reshape(SB, A_KV, A_GROUP * ST, A_KV, A_HD)
    att_k = jnp.stack([att_d[:, kv, :, kv, :] for kv in range(A_KV)], axis=1)
    att_s2 = att_k.reshape(SB, A_KV, A_GROUP, ST, A_HD).transpose(3, 0, 2, 1, 4).reshape(ST * SB, A_WIDTH)
    y_s = _post(xs, ym_s, att_s2, ps, gains, weights)
    y_sample = y_s.reshape(ST, SB, D_MODEL).transpose(1, 0, 2)
    k_s = kct_s.reshape(SB, A_KV, A_HD, WINDOW).transpose(0, 3, 1, 2)
    v_s = vct_s.reshape(SB, A_KV, A_HD, WINDOW).transpose(0, 3, 1, 2)

    stack = lambda a: a[None]
    return (y_prompt, y_sample,
            stack(c_p), stack(n_p), stack(m_p.reshape(B, M_HEADS)), stack(conv_p), stack(k_p), stack(v_p),
            stack(c_s), stack(n_s), stack(m_s), stack(conv_s), stack(k_s), stack(v_s))
```

```python
import functools

import numpy as np
import jax
import jax.numpy as jnp
from jax import lax
from jax.experimental import pallas as pl
from jax.experimental.pallas import tpu as pltpu

F32 = jnp.float32
BF16 = jnp.bfloat16

D_MODEL = 1024
M_WIDTH = 512
M_HEADS = 4
M_DV = 128
M_DK = 64
QK_W = 512
CONV_W = 4
CHUNK = 128
A_WIDTH = 512
A_HEADS = 8
A_HD = 64
A_KV = 2
A_GROUP = 4
KV_W = 128
WINDOW = 128
D_FF = 4096
P_DIM = 256
EPS = 1e-6
N_GATES = 2 * M_HEADS

VMEM_LIMIT_BYTES = 56 * 1024 * 1024
ROW_TILE = 512
FF_TILE = 1024
CHUNKS_PER_STEP = 2
SAMPLE_SEQ_TILE = 8
SAMPLE_V_TILE = 32

NEG_INF = float("-inf")
SLOPES = [2.0 ** (-8.0 * (h + 1) / A_HEADS) for h in range(A_HEADS)]
HEAD_ORDER = [kv * A_GROUP + g for g in range(A_GROUP) for kv in range(A_KV)]


def _dot(a, b):
    return jnp.dot(a, b, preferred_element_type=F32)


def _dot_nt(a, b):
    return lax.dot_general(a, b, (((1,), (1,)), ((), ())), preferred_element_type=F32)


def _dot_tn(a, b):
    return lax.dot_general(a, b, (((0,), (0,)), ((), ())), preferred_element_type=F32)


def _rms(x, g):
    return x * lax.rsqrt(jnp.mean(x * x, axis=-1, keepdims=True) + EPS) * g


def _conv_silu(up_ref, base, w, L):
    out = w[0:1, :] * up_ref[base:base + L, :]
    for j in range(1, CONV_W):
        out = out + w[j:j + 1, :] * up_ref[base + j:base + j + L, :]
    return jax.nn.silu(out)


def _chunk_cumsum_lanes(x):
    pos = lax.broadcasted_iota(jnp.int32, x.shape, 1)
    shift = 1
    while shift < CHUNK:
        x = x + jnp.where(pos >= shift, pltpu.roll(x, shift, axis=1), 0.0)
        shift *= 2
    return x


def _in_proj_kernel(x_ref, g_ref, w_ref, wgt_ref, cw_ref, bgc_ref,
                    q_ref, k_ref, vm_ref, os_ref, qa_ref, ka_ref, va_ref, gc_ref, gr_ref, tail_ref,
                    up_ref, *, seq_tiles):
    i = pl.program_id(0)
    tm = x_ref.shape[0]
    xn = _rms(x_ref[...], g_ref[...]).astype(BF16)
    z = _dot(xn, w_ref[...])

    @pl.when(i % seq_tiles == 0)
    def _():
        up_ref[0:8, :] = jnp.zeros((8, QK_W), F32)

    up_ref[8:8 + tm, :] = z[:, 0:QK_W]
    qk = _conv_silu(up_ref, 5, cw_ref[...], tm)
    q_ref[...] = (qk[:, 0:QK_W // 2] * (M_DK ** -0.5)).astype(BF16)
    k_ref[...] = qk[:, QK_W // 2:QK_W]
    up_ref[5:8, :] = up_ref[5 + tm:8 + tm, :]

    @pl.when(i % seq_tiles == seq_tiles - 1)
    def _():
        tail_ref[0] = up_ref[tm:8 + tm, :]

    vm_ref[...] = z[:, 512:1024].astype(BF16)
    os_ref[...] = jax.nn.sigmoid(z[:, 1024:1536])
    qa_ref[...] = (z[:, 1536:2048] * (A_HD ** -0.5)).astype(BF16)
    ka_ref[...] = z[:, 2048:2176]
    va_ref[...] = z[:, 2176:2304]

    g = _dot_nt(wgt_ref[...], xn) + bgc_ref[...]
    row = lax.broadcasted_iota(jnp.int32, (N_GATES, CHUNK), 0)
    for c in range(tm // CHUNK):
        gch = g[:, c * CHUNK:(c + 1) * CHUNK]
        logs = jnp.where(row >= M_HEADS, jax.nn.log_sigmoid(gch), gch)
        out = jnp.where(row >= M_HEADS, _chunk_cumsum_lanes(logs), logs)
        gr_ref[:, c * CHUNK:(c + 1) * CHUNK] = out
        gc_ref[c * CHUNK:(c + 1) * CHUNK, :] = out.T


def _in_proj(x2d, g_pre, w_main, w_gt, conv_w, bg_col, batch):
    n = x2d.shape[0]
    tm = ROW_TILE
    seq_tiles = n // batch // tm
    row = lambda w: pl.BlockSpec((tm, w), lambda i: (i, 0))
    full = lambda a: pl.BlockSpec(a.shape, lambda i: (0,) * a.ndim)
    outs = [(QK_W // 2, BF16), (QK_W // 2, F32), (M_WIDTH, BF16), (M_WIDTH, F32), (A_WIDTH, BF16),
            (KV_W, F32), (KV_W, F32), (N_GATES, F32)]
    return pl.pallas_call(
        functools.partial(_in_proj_kernel, seq_tiles=seq_tiles),
        grid=(n // tm,),
        in_specs=[row(D_MODEL), full(g_pre), full(w_main), full(w_gt), full(conv_w), full(bg_col)],
        out_specs=[row(w) for w, _ in outs] + [pl.BlockSpec((N_GATES, tm), lambda i: (0, i)),
                                                pl.BlockSpec((1, 8, QK_W), lambda i: (i // seq_tiles, 0, 0))],
        out_shape=[jax.ShapeDtypeStruct((n, w), dt) for w, dt in outs]
        + [jax.ShapeDtypeStruct((N_GATES, n), F32), jax.ShapeDtypeStruct((batch, 8, QK_W), F32)],
        scratch_shapes=[pltpu.VMEM((8 + tm, QK_W), F32)],
        compiler_params=pltpu.CompilerParams(dimension_semantics=("arbitrary",),
                                             vmem_limit_bytes=VMEM_LIMIT_BYTES),
        name="in_proj",
    )(x2d, g_pre, w_main, w_gt, conv_w, bg_col)


def _mlstm_out(h, o_sig, g):
    return o_sig * (h * lax.rsqrt(jnp.mean(h * h, axis=-1, keepdims=True) + EPS) * g)


def _mixer_prompt_kernel(q_ref, k_ref, vm_ref, os_ref, qa_ref, ka_ref, va_ref, gc_ref, gr_ref, mn_ref, sink_ref,
                         ym_ref, att_ref, c_out, m_out,
                         c_st, m_st, kp_ref, vp_ref, bias0_ref, bias_ref, sinkrep_ref):
    ci = pl.program_id(1)
    L = CHUNK
    J = range(CHUNKS_PER_STEP)
    H = range(M_HEADS)
    KV = range(A_KV)
    t = lax.broadcasted_iota(jnp.int32, (L, L), 0)
    s = lax.broadcasted_iota(jnp.int32, (L, L), 1)

    @pl.when(ci == 0)
    def _():
        c_st[...] = jnp.zeros_like(c_st)
        m_st[...] = jnp.zeros_like(m_st)
        kp_ref[...] = jnp.zeros_like(kp_ref)
        vp_ref[...] = jnp.zeros_like(vp_ref)
        dist = (t - s).astype(F32)
        for kv in KV:
            for g in range(A_GROUP):
                hd = kv * A_GROUP + g
                rows = slice(g * L, (g + 1) * L)
                cur = jnp.where(s <= t, -SLOPES[hd] * dist, NEG_INF)
                bias_ref[kv, rows, 0:L] = jnp.where(s >= t, -SLOPES[hd] * (dist + float(WINDOW)), NEG_INF)
                bias_ref[kv, rows, L:2 * L] = cur
                bias0_ref[kv, rows, 0:L] = jnp.full((L, L), NEG_INF, F32)
                bias0_ref[kv, rows, L:2 * L] = cur
                sinkrep_ref[kv, rows, :] = jnp.broadcast_to(sink_ref[0:1, hd:hd + 1], (L, KV_W))

    @pl.when(ci == 1)
    def _():
        for kv in KV:
            bias0_ref[kv, :, 0:L] = bias_ref[kv, :, 0:L]

    rows = [slice(j * L, (j + 1) * L) for j in J]
    P = [(j, h) for j in J for h in H]

    ig_c = [gc_ref[rows[j], 0:M_HEADS] for j in J]
    b_c = [gc_ref[rows[j], M_HEADS:N_GATES] for j in J]
    a_r = [gr_ref[0:M_HEADS, rows[j]] - gr_ref[M_HEADS:N_GATES, rows[j]] for j in J]
    causal_bias = jnp.where(s <= t, 0.0, NEG_INF)
    hcol = lax.broadcasted_iota(jnp.int32, (L, M_HEADS), 1)
    m_prev = [m_st[0:1, 0:M_HEADS]]
    w_k, decay = [], []
    for j in J:
        b_last = b_c[j][L - 1:L, :]
        log_w = b_last - b_c[j] + ig_c[j]
        m_new = jnp.maximum(b_last + m_prev[j], jnp.max(log_w, axis=0, keepdims=True))
        w_k.append(jnp.exp(log_w - m_new))
        decay.append(jnp.exp(b_last + m_prev[j] - m_new))
        m_prev.append(m_new)
    m_st[0:1, 0:M_HEADS] = m_prev[-1]

    ones_v = jnp.ones((L, M_DV), BF16)
    qb = {p: q_ref[rows[p[0]], p[1] * M_DK:(p[1] + 1) * M_DK] for p in P}
    kf = {p: k_ref[rows[p[0]], p[1] * M_DK:(p[1] + 1) * M_DK] for p in P}
    kb = {p: kf[p].astype(BF16) for p in P}
    vext = {p: jnp.concatenate([vm_ref[rows[p[0]], p[1] * M_DV:(p[1] + 1) * M_DV], ones_v], axis=1) for p in P}
    qk_h = {p: _dot_nt(qb[p], kb[p]) for p in P}
    kw = {p: (kf[p] * w_k[p[0]][:, p[1]:p[1] + 1]).astype(BF16) for p in P}
    dc = {p: _dot_tn(kw[p], vext[p]) for p in P}
    log_d = {p: b_c[p[0]][:, p[1]:p[1] + 1] + a_r[p[0]][p[1]:p[1] + 1, :] + causal_bias for p in P}
    m_t, w_inter, emt = [], [], []
    for j in J:
        m_intra = jnp.full((L, M_HEADS), NEG_INF, F32)
        for h in H:
            m_intra = jnp.where(hcol == h, jnp.max(log_d[(j, h)], axis=-1, keepdims=True), m_intra)
        log_inter = b_c[j] + m_prev[j]
        m_t.append(jnp.maximum(log_inter, m_intra))
        w_inter.append(jnp.exp(log_inter - m_t[j]))
        emt.append(jnp.exp(-m_t[j]))
    s_h = {p: (qk_h[p] * jnp.exp(log_d[p] - m_t[p[0]][:, p[1]:p[1] + 1])).astype(BF16) for p in P}
    o_h = {p: _dot(s_h[p], vext[p]) for p in P}
    c_cur = [c_st[h] for h in H]
    qc = {}
    for j in J:
        for h in H:
            qc[(j, h)] = _dot(qb[(j, h)], c_cur[h].astype(BF16))
        c_cur = [decay[j][:, h:h + 1] * c_cur[h] + dc[(j, h)] for h in H]
    for h in H:
        c_st[h] = c_cur[h]
    for j, h in P:
        wi = w_inter[j][:, h:h + 1]
        num = o_h[(j, h)][:, 0:M_DV] + wi * qc[(j, h)][:, 0:M_DV]
        den = o_h[(j, h)][:, M_DV:2 * M_DV] + wi * qc[(j, h)][:, M_DV:2 * M_DV]
        hh = num / jnp.maximum(jnp.abs(den), emt[j][:, h:h + 1])
        cols = slice(h * M_DV, (h + 1) * M_DV)
        ym_ref[rows[j], cols] = _mlstm_out(hh, os_ref[rows[j], cols], mn_ref[:, cols])

    nk = (CHUNKS_PER_STEP + 1) * L
    low3 = lax.broadcasted_iota(jnp.int32, (nk, KV_W), 1) < A_HD
    low = lax.broadcasted_iota(jnp.int32, (L, KV_W), 1) < A_HD
    k_all = jnp.concatenate([kp_ref[...], ka_ref[...]], axis=0)
    v_all = jnp.concatenate([vp_ref[...], va_ref[...]], axis=0)
    ones_k = jnp.ones((nk, KV_W), BF16)
    kmask = [jnp.where(low3, k_all, 0.0).astype(BF16), jnp.where(low3, 0.0, k_all).astype(BF16)]
    vext_a = [jnp.concatenate([jnp.where(low3, v_all, 1.0).astype(BF16), ones_k], axis=1),
              jnp.concatenate([jnp.where(low3, 1.0, v_all).astype(BF16), ones_k], axis=1)]
    qst = [jnp.concatenate([qa_ref[rows[j], g * KV_W:(g + 1) * KV_W] for g in range(A_GROUP)], axis=0)
           for j in J]
    Q = [(j, kv) for j in J for kv in KV]
    bias_of = lambda j, kv: (bias0_ref if j == 0 else bias_ref)[kv]
    sc = {p: _dot_nt(qst[p[0]], kmask[p[1]][p[0] * L:(p[0] + 2) * L]) + bias_of(*p) for p in Q}
    m_a = {p: jnp.maximum(jnp.max(jnp.maximum(sc[p][:, 0:L], sc[p][:, L:2 * L]), axis=-1, keepdims=True),
                          sinkrep_ref[p[1]]) for p in Q}
    e = {p: jnp.concatenate([jnp.exp(sc[p][:, 0:L] - m_a[p]), jnp.exp(sc[p][:, L:2 * L] - m_a[p])],
                            axis=1).astype(BF16) for p in Q}
    o = {p: _dot(e[p], vext_a[p[1]][p[0] * L:(p[0] + 2) * L]) for p in Q}
    exs = {p: jnp.exp(sinkrep_ref[p[1]] - m_a[p]) for p in Q}
    for j in J:
        for g in range(A_GROUP):
            gr_ = slice(g * L, (g + 1) * L)
            o0, o1 = o[(j, 0)], o[(j, 1)]
            pv = jnp.where(low, o0[gr_, 0:KV_W], o1[gr_, 0:KV_W])
            den = jnp.where(low, o0[gr_, KV_W:2 * KV_W] + exs[(j, 0)][gr_, :],
                            o1[gr_, KV_W:2 * KV_W] + exs[(j, 1)][gr_, :])
            att_ref[rows[j], g * KV_W:(g + 1) * KV_W] = pv / den
    kp_ref[...] = ka_ref[(CHUNKS_PER_STEP - 1) * L:CHUNKS_PER_STEP * L, :]
    vp_ref[...] = va_ref[(CHUNKS_PER_STEP - 1) * L:CHUNKS_PER_STEP * L, :]

    @pl.when(ci == pl.num_programs(1) - 1)
    def _():
        c_out[0] = c_st[...]
        m_out[0] = m_st[0:1, 0:M_HEADS]


def _mixer_prompt(q, k, vm, os_, qa, ka, va, gc, gr, mlstm_norm, sinks, batch, seq):
    tr = CHUNKS_PER_STEP * CHUNK
    ns = seq // tr
    row = lambda w: pl.BlockSpec((tr, w), lambda b, c: (b * ns + c, 0))
    full = lambda a: pl.BlockSpec(a.shape, lambda b, c: (0,) * a.ndim)
    n = batch * seq
    bias_shape = (A_KV, A_GROUP * CHUNK, 2 * CHUNK)
    return pl.pallas_call(
        _mixer_prompt_kernel,
        grid=(batch, ns),
        in_specs=[row(QK_W // 2), row(QK_W // 2), row(M_WIDTH), row(M_WIDTH), row(A_WIDTH), row(KV_W), row(KV_W),
                  row(N_GATES), pl.BlockSpec((N_GATES, tr), lambda b, c: (0, b * ns + c)),
                  full(mlstm_norm), full(sinks)],
        out_specs=[row(M_WIDTH), row(A_WIDTH),
                   pl.BlockSpec((1, M_HEADS, M_DK, 2 * M_DV), lambda b, c: (b, 0, 0, 0)),
                   pl.BlockSpec((1, 1, M_HEADS), lambda b, c: (b, 0, 0))],
        out_shape=[jax.ShapeDtypeStruct((n, M_WIDTH), F32), jax.ShapeDtypeStruct((n, A_WIDTH), F32),
                   jax.ShapeDtypeStruct((batch, M_HEADS, M_DK, 2 * M_DV), F32),
                   jax.ShapeDtypeStruct((batch, 1, M_HEADS), F32)],
        scratch_shapes=[pltpu.VMEM((M_HEADS, M_DK, 2 * M_DV), F32),
                        pltpu.VMEM((8, 128), F32),
                        pltpu.VMEM((CHUNK, KV_W), F32), pltpu.VMEM((CHUNK, KV_W), F32),
                        pltpu.VMEM(bias_shape, F32), pltpu.VMEM(bias_shape, F32),
                        pltpu.VMEM((A_KV, A_GROUP * CHUNK, KV_W), F32)],
        compiler_params=pltpu.CompilerParams(dimension_semantics=("arbitrary", "arbitrary"),
                                             vmem_limit_bytes=VMEM_LIMIT_BYTES),
        name="mixer_prompt",
    )(q, k, vm, os_, qa, ka, va, gc, gr, mlstm_norm, sinks)


Z_W = QK_W + 2 * M_WIDTH + A_WIDTH + 2 * KV_W
Z_GROUPS = [(0, 512), (512, 1024), (1024, 1536), (1536, 2048), (2048, 2304)]


def _front_kernel(x_ref, g_ref, w_ref, wgt_ref, cw_ref, bgc_ref, mn_ref, sink_ref,
                  ym_ref, att_ref, c_out, m_out, ktail_ref, vtail_ref, ctail_ref,
                  z_ref, g_st, up_ref, c_st, m_st, kp_ref, vp_ref, bias0_ref, bias_ref, sinkrep_ref):
    ci = pl.program_id(1)
    last = pl.num_programs(1) - 1
    L = CHUNK
    J = range(CHUNKS_PER_STEP)
    H = range(M_HEADS)
    KV = range(A_KV)
    TR = CHUNKS_PER_STEP * L
    nxt = ci % 2
    cur = 1 - nxt
    t = lax.broadcasted_iota(jnp.int32, (L, L), 0)
    s = lax.broadcasted_iota(jnp.int32, (L, L), 1)

    @pl.when(ci == 0)
    def _():
        z_ref[cur] = jnp.zeros((TR, Z_W), F32)
        g_st[cur] = jnp.zeros((N_GATES, TR), F32)
        c_st[...] = jnp.zeros_like(c_st)
        m_st[...] = jnp.zeros_like(m_st)
        kp_ref[...] = jnp.zeros_like(kp_ref)
        vp_ref[...] = jnp.zeros_like(vp_ref)
        up_ref[0:8, :] = jnp.zeros((8, QK_W), F32)
        dist = (t - s).astype(F32)
        for kv in KV:
            for g in range(A_GROUP):
                hd = kv * A_GROUP + g
                rows = slice(g * L, (g + 1) * L)
                cur_b = jnp.where(s <= t, -SLOPES[hd] * dist, NEG_INF)
                bias_ref[kv, rows, 0:L] = jnp.where(s >= t, -SLOPES[hd] * (dist + float(WINDOW)), NEG_INF)
                bias_ref[kv, rows, L:2 * L] = cur_b
                bias0_ref[kv, rows, 0:L] = jnp.full((L, L), NEG_INF, F32)
                bias0_ref[kv, rows, L:2 * L] = cur_b
                sinkrep_ref[kv, rows, :] = jnp.broadcast_to(sink_ref[0:1, hd:hd + 1], (L, KV_W))

    @pl.when(ci == 1)
    def _():
        c_st[...] = jnp.zeros_like(c_st)
        m_st[...] = jnp.zeros_like(m_st)
        kp_ref[...] = jnp.zeros_like(kp_ref)
        vp_ref[...] = jnp.zeros_like(vp_ref)
        up_ref[0:8, :] = jnp.zeros((8, QK_W), F32)

    @pl.when(ci == 2)
    def _():
        for kv in KV:
            bias0_ref[kv, :, 0:L] = bias_ref[kv, :, 0:L]

    xn = _rms(x_ref[...], g_ref[...]).astype(BF16)

    def project(gi):
        lo, hi = Z_GROUPS[gi]
        z_ref[nxt, :, lo:hi] = _dot(xn, w_ref[:, lo:hi])

    up_ref[8:8 + TR, :] = z_ref[cur, :, 0:QK_W]
    qk = _conv_silu(up_ref, 5, cw_ref[...], TR)
    up_ref[5:8, :] = up_ref[5 + TR:8 + TR, :]
    q_all = (qk[:, 0:QK_W // 2] * (M_DK ** -0.5)).astype(BF16)
    k_all_m = qk[:, QK_W // 2:QK_W]
    gcur = g_st[cur] + bgc_ref[...]
    grow = lax.broadcasted_iota(jnp.int32, (N_GATES, L), 0)
    rows = [slice(j * L, (j + 1) * L) for j in J]
    gr, gc = [], []
    for j in J:
        gch = gcur[:, rows[j]]
        logs = jnp.where(grow >= M_HEADS, jax.nn.log_sigmoid(gch), gch)
        out = jnp.where(grow >= M_HEADS, _chunk_cumsum_lanes(logs), logs)
        gr.append(out)
        gc.append(out.T)
    project(0)

    P = [(j, h) for j in J for h in H]
    ig_c = [gc[j][:, 0:M_HEADS] for j in J]
    b_c = [gc[j][:, M_HEADS:N_GATES] for j in J]
    a_r = [gr[j][0:M_HEADS, :] - gr[j][M_HEADS:N_GATES, :] for j in J]
    causal_bias = jnp.where(s <= t, 0.0, NEG_INF)
    hcol = lax.broadcasted_iota(jnp.int32, (L, M_HEADS), 1)
    m_prev = [m_st[0:1, 0:M_HEADS]]
    w_k, decay = [], []
    for j in J:
        b_last = b_c[j][L - 1:L, :]
        log_w = b_last - b_c[j] + ig_c[j]
        m_new = jnp.maximum(b_last + m_prev[j], jnp.max(log_w, axis=0, keepdims=True))
        w_k.append(jnp.exp(log_w - m_new))
        decay.append(jnp.exp(b_last + m_prev[j] - m_new))
        m_prev.append(m_new)
    m_st[0:1, 0:M_HEADS] = m_prev[-1]

    ones_v = jnp.ones((L, M_DV), BF16)
    qb = {p: q_all[rows[p[0]], p[1] * M_DK:(p[1] + 1) * M_DK] for p in P}
    kf = {p: k_all_m[rows[p[0]], p[1] * M_DK:(p[1] + 1) * M_DK] for p in P}
    kb = {p: kf[p].astype(BF16) for p in P}
    vext = {p: jnp.concatenate([z_ref[cur, rows[p[0]], QK_W + p[1] * M_DV:QK_W + (p[1] + 1) * M_DV].astype(BF16),
                                ones_v], axis=1) for p in P}
    qk_h = {p: _dot_nt(qb[p], kb[p]) for p in P}
    kw = {p: (kf[p] * w_k[p[0]][:, p[1]:p[1] + 1]).astype(BF16) for p in P}
    dc = {p: _dot_tn(kw[p], vext[p]) for p in P}
    project(1)
    log_d = {p: b_c[p[0]][:, p[1]:p[1] + 1] + a_r[p[0]][p[1]:p[1] + 1, :] + causal_bias for p in P}
    m_t, w_inter, emt = [], [], []
    for j in J:
        m_intra = jnp.full((L, M_HEADS), NEG_INF, F32)
        for h in H:
            m_intra = jnp.where(hcol == h, jnp.max(log_d[(j, h)], axis=-1, keepdims=True), m_intra)
        log_inter = b_c[j] + m_prev[j]
        m_t.append(jnp.maximum(log_inter, m_intra))
        w_inter.append(jnp.exp(log_inter - m_t[j]))
        emt.append(jnp.exp(-m_t[j]))
    s_h = {p: (qk_h[p] * jnp.exp(log_d[p] - m_t[p[0]][:, p[1]:p[1] + 1])).astype(BF16) for p in P}
    c_cur = [c_st[h] for h in H]
    qc = {}
    for j in J:
        for h in H:
            qc[(j, h)] = _dot(qb[(j, h)], c_cur[h].astype(BF16))
        c_cur = [decay[j][:, h:h + 1] * c_cur[h] + dc[(j, h)] for h in H]
    for h in H:
        c_st[h] = c_cur[h]
    project(2)
    o_h = {p: _dot(s_h[p], vext[p]) for p in P}
    for j, h in P:
        wi = w_inter[j][:, h:h + 1]
        num = o_h[(j, h)][:, 0:M_DV] + wi * qc[(j, h)][:, 0:M_DV]
        den = o_h[(j, h)][:, M_DV:2 * M_DV] + wi * qc[(j, h)][:, M_DV:2 * M_DV]
        hh = num / jnp.maximum(jnp.abs(den), emt[j][:, h:h + 1])
        cols = slice(h * M_DV, (h + 1) * M_DV)
        o_sig = jax.nn.sigmoid(z_ref[cur, rows[j], QK_W + M_WIDTH + h * M_DV:QK_W + M_WIDTH + (h + 1) * M_DV])
        ym_ref[rows[j], cols] = _mlstm_out(hh, o_sig, mn_ref[:, cols])

    a0 = QK_W + 2 * M_WIDTH
    nk = (CHUNKS_PER_STEP + 1) * L
    low3 = lax.broadcasted_iota(jnp.int32, (nk, KV_W), 1) < A_HD
    low = lax.broadcasted_iota(jnp.int32, (L, KV_W), 1) < A_HD
    ka_cur = z_ref[cur, :, a0 + A_WIDTH:a0 + A_WIDTH + KV_W]
    va_cur = z_ref[cur, :, a0 + A_WIDTH + KV_W:a0 + A_WIDTH + 2 * KV_W]
    k_all = jnp.concatenate([kp_ref[...], ka_cur], axis=0)
    v_all = jnp.concatenate([vp_ref[...], va_cur], axis=0)
    ones_k = jnp.ones((nk, KV_W), BF16)
    kmask = [jnp.where(low3, k_all, 0.0).astype(BF16), jnp.where(low3, 0.0, k_all).astype(BF16)]
    vext_a = [jnp.concatenate([jnp.where(low3, v_all, 1.0).astype(BF16), ones_k], axis=1),
              jnp.concatenate([jnp.where(low3, 1.0, v_all).astype(BF16), ones_k], axis=1)]
    qst = [(jnp.concatenate([z_ref[cur, rows[j], a0 + g * KV_W:a0 + (g + 1) * KV_W] for g in range(A_GROUP)],
                            axis=0) * (A_HD ** -0.5)).astype(BF16) for j in J]
    Q = [(j, kv) for j in J for kv in KV]
    bias_of = lambda j, kv: (bias0_ref if j == 0 else bias_ref)[kv]
    sc = {p: _dot_nt(qst[p[0]], kmask[p[1]][p[0] * L:(p[0] + 2) * L]) + bias_of(*p) for p in Q}
    project(3)
    m_a = {p: jnp.maximum(jnp.max(jnp.maximum(sc[p][:, 0:L], sc[p][:, L:2 * L]), axis=-1, keepdims=True),
                          sinkrep_ref[p[1]]) for p in Q}
    e = {p: jnp.concatenate([jnp.exp(sc[p][:, 0:L] - m_a[p]), jnp.exp(sc[p][:, L:2 * L] - m_a[p])],
                            axis=1).astype(BF16) for p in Q}
    o = {p: _dot(e[p], vext_a[p[1]][p[0] * L:(p[0] + 2) * L]) for p in Q}
    project(4)
    g_st[nxt] = _dot_nt(wgt_ref[...], xn)
    exs = {p: jnp.exp(sinkrep_ref[p[1]] - m_a[p]) for p in Q}
    for j in J:
        for g in range(A_GROUP):
            gr_ = slice(g * L, (g + 1) * L)
            o0, o1 = o[(j, 0)], o[(j, 1)]
            pv = jnp.where(low, o0[gr_, 0:KV_W], o1[gr_, 0:KV_W])
            den = jnp.where(low, o0[gr_, KV_W:2 * KV_W] + exs[(j, 0)][gr_, :],
                            o1[gr_, KV_W:2 * KV_W] + exs[(j, 1)][gr_, :])
            att_ref[rows[j], g * KV_W:(g + 1) * KV_W] = pv / den
    kp_ref[...] = ka_cur[(CHUNKS_PER_STEP - 1) * L:CHUNKS_PER_STEP * L, :]
    vp_ref[...] = va_cur[(CHUNKS_PER_STEP - 1) * L:CHUNKS_PER_STEP * L, :]

    @pl.when(ci == last)
    def _():
        c_out[0] = c_st[...]
        m_out[0] = m_st[0:1, 0:M_HEADS]
        ktail_ref[0] = kp_ref[...]
        vtail_ref[0] = vp_ref[...]
        ctail_ref[0] = up_ref[0:8, :]


def _front(x2d, g_pre, w_main, w_gt, conv_w, bg_col, mlstm_norm, sinks, batch, seq):
    tr = CHUNKS_PER_STEP * CHUNK
    ns = seq // tr
    n = batch * seq
    full = lambda a: pl.BlockSpec(a.shape, lambda b, c: (0,) * a.ndim)
    resident = lambda a: pl.BlockSpec(a.shape, lambda b, c: (0,) * a.ndim, pipeline_mode=pl.Buffered(1))
    out_row = lambda w: pl.BlockSpec((tr, w), lambda b, c: (b * ns + jnp.maximum(c - 1, 0), 0))
    per_seq = lambda shape: pl.BlockSpec((1,) + shape, lambda b, c: (b,) + (0,) * len(shape))
    bias_shape = (A_KV, A_GROUP * CHUNK, 2 * CHUNK)
    return pl.pallas_call(
        _front_kernel,
        grid=(batch, ns + 1),
        in_specs=[pl.BlockSpec((tr, D_MODEL), lambda b, c: (b * ns + jnp.minimum(c, ns - 1), 0)),
                  full(g_pre), resident(w_main), full(w_gt), full(conv_w), full(bg_col),
                  full(mlstm_norm), full(sinks)],
        out_specs=[out_row(M_WIDTH), out_row(A_WIDTH),
                   per_seq((M_HEADS, M_DK, 2 * M_DV)), per_seq((1, M_HEADS)),
                   per_seq((CHUNK, KV_W)), per_seq((CHUNK, KV_W)), per_seq((8, QK_W))],
        out_shape=[jax.ShapeDtypeStruct((n, M_WIDTH), F32), jax.ShapeDtypeStruct((n, A_WIDTH), F32),
                   jax.ShapeDtypeStruct((batch, M_HEADS, M_DK, 2 * M_DV), F32),
                   jax.ShapeDtypeStruct((batch, 1, M_HEADS), F32),
                   jax.ShapeDtypeStruct((batch, CHUNK, KV_W), F32), jax.ShapeDtypeStruct((batch, CHUNK, KV_W), F32),
                   jax.ShapeDtypeStruct((batch, 8, QK_W), F32)],
        scratch_shapes=[pltpu.VMEM((2, tr, Z_W), F32), pltpu.VMEM((2, N_GATES, tr), F32),
                        pltpu.VMEM((8 + tr, QK_W), F32),
                        pltpu.VMEM((M_HEADS, M_DK, 2 * M_DV), F32), pltpu.VMEM((8, 128), F32),
                        pltpu.VMEM((CHUNK, KV_W), F32), pltpu.VMEM((CHUNK, KV_W), F32),
                        pltpu.VMEM(bias_shape, F32), pltpu.VMEM(bias_shape, F32),
                        pltpu.VMEM((A_KV, A_GROUP * CHUNK, KV_W), F32)],
        compiler_params=pltpu.CompilerParams(dimension_semantics=("arbitrary", "arbitrary"),
                                             vmem_limit_bytes=VMEM_LIMIT_BYTES),
        name="front",
    )(x2d, g_pre, w_main, w_gt, conv_w, bg_col, mlstm_norm, sinks)


def _in_proj_sample_kernel(x_ref, g_ref, wn_ref, wt_ref, wgt_ref,
                           qkn_ref, qa_ref, ka_ref, va_ref, qkt_ref, vt_ref, ot_ref, gt_ref):
    xn = _rms(x_ref[...], g_ref[...]).astype(BF16)
    zn = _dot(xn, wn_ref[...])
    qkn_ref[...] = zn[:, 0:QK_W]
    qa_ref[...] = zn[:, QK_W:QK_W + A_WIDTH]
    ka_ref[...] = zn[:, QK_W + A_WIDTH:QK_W + A_WIDTH + KV_W]
    va_ref[...] = zn[:, QK_W + A_WIDTH + KV_W:QK_W + A_WIDTH + 2 * KV_W]
    zt = _dot_nt(wt_ref[...], xn)
    qkt_ref[...] = zt[0:QK_W, :]
    vt_ref[...] = zt[QK_W:QK_W + M_WIDTH, :]
    ot_ref[...] = zt[QK_W + M_WIDTH:QK_W + 2 * M_WIDTH, :]
    gt_ref[...] = _dot_nt(wgt_ref[...], xn)


def _in_proj_sample(x2d, g_pre, w_nat, w_t, w_gt):
    n = x2d.shape[0]
    shapes = [(n, QK_W), (n, A_WIDTH), (n, KV_W), (n, KV_W), (QK_W, n), (M_WIDTH, n), (M_WIDTH, n), (N_GATES, n)]
    return pl.pallas_call(
        _in_proj_sample_kernel,
        out_shape=[jax.ShapeDtypeStruct(s, F32) for s in shapes],
        compiler_params=pltpu.CompilerParams(vmem_limit_bytes=VMEM_LIMIT_BYTES),
        name="in_proj_sample",
    )(x2d, g_pre, w_nat, w_t, w_gt)


def _mlstm_sample_kernel(qt_ref, kt_ref, vt_ref, ot_ref, gt_ref, bg_ref, cbq_ref, cbk_ref, cwq_ref, cwk_ref,
                         gain_ref, c_ref, n_ref, m_ref,
                         ym_ref, c_out, n_out, m_out,
                         ct_ref, q_s, ik_s, num_s):
    h = pl.program_id(0)
    NB = c_ref.shape[0]
    L = qt_ref.shape[1] // NB
    T = range(L)

    def conv(raw_ref, cb_ref, cw_ref):
        ups = [cb_ref[j] for j in range(CONV_W - 1)] + [raw_ref[:, t * NB:(t + 1) * NB] for t in T]
        w = [cw_ref[:, j:j + 1] for j in range(CONV_W)]
        outs = []
        for t in T:
            acc = w[0] * ups[t]
            for j in range(1, CONV_W):
                acc = acc + w[j] * ups[t + j]
            outs.append(jax.nn.silu(acc))
        return outs

    q = [x * (M_DK ** -0.5) for x in conv(qt_ref, cbq_ref, cwq_ref)]
    k = conv(kt_ref, cbk_ref, cwk_ref)

    ig_all = gt_ref[pl.ds(h, 1), :] + bg_ref[pl.ds(h, 1), :]
    lf_all = jax.nn.log_sigmoid(gt_ref[pl.ds(h + M_HEADS, 1), :] + bg_ref[pl.ds(h + M_HEADS, 1), :])
    m = m_ref[pl.ds(h, 1), :]
    f, ms = [], []
    n = n_ref[0]
    den = []
    for t in T:
        ig, lf = ig_all[:, t * NB:(t + 1) * NB], lf_all[:, t * NB:(t + 1) * NB]
        m_new = jnp.maximum(lf + m, ig)
        f_t = jnp.exp(lf + m - m_new)
        ik = jnp.exp(ig - m_new) * k[t]
        m = m_new
        n = f_t * n + ik
        f.append(f_t)
        ms.append(m_new)
        den.append(jnp.sum(q[t] * n, axis=0, keepdims=True))
        ik_s[t] = ik
        q_s[t] = q[t]
    m_out[0] = m
    n_out[0] = n

    for d in range(M_DK):
        ct_ref[d * M_DV:(d + 1) * M_DV, :] = c_ref[:, d * M_DV:(d + 1) * M_DV].T

    VT = SAMPLE_V_TILE
    for vq in range(M_DV // VT):
        vts = [vt_ref[vq * VT:(vq + 1) * VT, t * NB:(t + 1) * NB] for t in T]

        def step(d, accs, vq=vq, vts=vts):
            r0 = pl.multiple_of(d * M_DV + vq * VT, VT)
            c = ct_ref[pl.ds(r0, VT), :]
            new = []
            for t in T:
                c = f[t] * c + ik_s[t, pl.ds(d, 1), :] * vts[t]
                new.append(accs[t] + q_s[t, pl.ds(d, 1), :] * c)
            ct_ref[pl.ds(r0, VT), :] = c
            return tuple(new)

        accs = lax.fori_loop(0, M_DK, step, tuple(jnp.zeros((VT, NB), F32) for _ in T), unroll=8)
        for t in T:
            num_s[t, vq * VT:(vq + 1) * VT, :] = accs[t]

    for t in T:
        hh = num_s[t] / jnp.maximum(jnp.abs(den[t]), jnp.exp(-ms[t]))
        hn = hh * lax.rsqrt(jnp.mean(hh * hh, axis=0, keepdims=True) + EPS) * gain_ref[...]
        y = jax.nn.sigmoid(ot_ref[:, t * NB:(t + 1) * NB]) * hn
        ym_ref[t * NB:(t + 1) * NB, :] = y.T

    for d in range(M_DK):
        c_out[:, d * M_DV:(d + 1) * M_DV] = ct_ref[d * M_DV:(d + 1) * M_DV, :].T


def _mlstm_sample(qkt, vt, ot, gt, bg_col, cbt, cwt, gain_col, c2d, nt, mt):
    nb = c2d.shape[0]
    n = qkt.shape[1]
    L = n // nb
    hblk = lambda rows, off: pl.BlockSpec((rows, n), lambda h, off=off: (h + off, 0))
    full = lambda a: pl.BlockSpec(a.shape, lambda h: (0,) * a.ndim)
    kq = QK_W // 2 // M_DK
    return pl.pallas_call(
        _mlstm_sample_kernel,
        grid=(M_HEADS,),
        in_specs=[hblk(M_DK, 0), hblk(M_DK, kq), hblk(M_DV, 0), hblk(M_DV, 0), full(gt), full(bg_col),
                  pl.BlockSpec((CONV_W - 1, M_DK, nb), lambda h: (0, h, 0)),
                  pl.BlockSpec((CONV_W - 1, M_DK, nb), lambda h: (0, h + kq, 0)),
                  pl.BlockSpec((M_DK, CONV_W), lambda h: (h, 0)),
                  pl.BlockSpec((M_DK, CONV_W), lambda h: (h + kq, 0)),
                  pl.BlockSpec((M_DV, 1), lambda h: (h, 0)),
                  pl.BlockSpec((nb, M_DK * M_DV), lambda h: (0, h)),
                  pl.BlockSpec((1, M_DK, nb), lambda h: (h, 0, 0)),
                  full(mt)],
        out_specs=[pl.BlockSpec((n, M_DV), lambda h: (0, h)),
                   pl.BlockSpec((nb, M_DK * M_DV), lambda h: (0, h)),
                   pl.BlockSpec((1, M_DK, nb), lambda h: (h, 0, 0)),
                   pl.BlockSpec((1, 1, nb), lambda h: (h, 0, 0))],
        out_shape=[jax.ShapeDtypeStruct((n, M_WIDTH), F32), jax.ShapeDtypeStruct(c2d.shape, F32),
                   jax.ShapeDtypeStruct(nt.shape, F32), jax.ShapeDtypeStruct((M_HEADS, 1, nb), F32)],
        scratch_shapes=[pltpu.VMEM((M_DK * M_DV, nb), F32), pltpu.VMEM((L, M_DK, nb), F32),
                        pltpu.VMEM((L, M_DK, nb), F32), pltpu.VMEM((L, M_DV, nb), F32)],
        compiler_params=pltpu.CompilerParams(dimension_semantics=("arbitrary",),
                                             vmem_limit_bytes=VMEM_LIMIT_BYTES),
        name="mlstm_sample",
    )(qkt, qkt, vt, ot, gt, bg_col, cbt, cbt, cwt, cwt, gain_col, c2d, nt, mt)


def _attn_sample_kernel(qm_ref, kn_ref, vn_ref, kct_ref, vct_ref, sinkr_ref, sloper_ref,
                        att_ref, kct_out, vct_out, xk_ref, xv_ref):
    NB, L = kn_ref.shape[0], kn_ref.shape[1]
    R = A_KV * A_GROUP * L

    def bias(nkeys, offset):
        r = lax.broadcasted_iota(jnp.int32, (R, nkeys), 0).astype(F32)
        pos = lax.broadcasted_iota(jnp.int32, (R, nkeys), 1).astype(F32)
        tq = r - L * jnp.floor((r + 0.5) / L)
        dist = tq + offset - pos
        return jnp.where((dist >= 0.0) & (dist <= float(WINDOW)), -sloper_ref[...] * dist, NEG_INF)

    @pl.when(pl.program_id(0) == 0)
    def _():
        xk_ref[...] = jnp.zeros_like(xk_ref)
        xv_ref[...] = jnp.zeros_like(xv_ref)

    bias_c = bias(WINDOW, float(WINDOW))
    bias_n = bias(L, 0.0)
    sink = sinkr_ref[...]
    rq = lax.broadcasted_iota(jnp.int32, (R, KV_W), 0)
    cq = lax.broadcasted_iota(jnp.int32, (R, KV_W), 1)
    same_kv = (rq < A_GROUP * L) == (cq < A_HD)
    newest = lax.broadcasted_iota(jnp.int32, (KV_W, WINDOW), 1) >= WINDOW - L
    SEQ = range(NB)
    qh = [jnp.where(same_kv, qm_ref[b] * (A_HD ** -0.5), 0.0).astype(BF16) for b in SEQ]
    kt = [kct_ref[b] for b in SEQ]
    vt = [vct_ref[b] for b in SEQ]
    knb = [kn_ref[b].astype(BF16) for b in SEQ]
    vnb = [vn_ref[b].astype(BF16) for b in SEQ]
    sc_c = [_dot(qh[b], kt[b].astype(BF16)) + bias_c for b in SEQ]
    sc_n = [_dot_nt(qh[b], knb[b]) + bias_n for b in SEQ]
    m_a = [jnp.maximum(sink, jnp.maximum(jnp.max(sc_c[b], axis=-1, keepdims=True),
                                         jnp.max(sc_n[b], axis=-1, keepdims=True))) for b in SEQ]
    e_c = [jnp.exp(sc_c[b] - m_a[b]) for b in SEQ]
    e_n = [jnp.exp(sc_n[b] - m_a[b]) for b in SEQ]
    den_a = [jnp.exp(sink - m_a[b]) + jnp.sum(e_c[b], axis=-1, keepdims=True)
             + jnp.sum(e_n[b], axis=-1, keepdims=True) for b in SEQ]
    pv = [_dot_nt((e_c[b] / den_a[b]).astype(BF16), vt[b].astype(BF16))
          + _dot((e_n[b] / den_a[b]).astype(BF16), vnb[b]) for b in SEQ]
    for b in SEQ:
        att_ref[b] = pv[b]

    for b in SEQ:
        xk_ref[b, WINDOW - L:WINDOW, :] = kn_ref[b]
        xv_ref[b, WINDOW - L:WINDOW, :] = vn_ref[b]
    for b in SEQ:
        kct_out[b] = jnp.where(newest, xk_ref[b].T, pltpu.roll(kt[b], WINDOW - L, axis=1))
        vct_out[b] = jnp.where(newest, xv_ref[b].T, pltpu.roll(vt[b], WINDOW - L, axis=1))


def _attn_sample(qm, kn, vn, kct, vct, sink_rows, slope_rows):
    nb = qm.shape[0]
    ts = min(SAMPLE_SEQ_TILE, nb)
    per_b = lambda a: pl.BlockSpec((ts,) + a.shape[1:], lambda b: (b,) + (0,) * (a.ndim - 1))
    full = lambda a: pl.BlockSpec(a.shape, lambda b: (0,) * a.ndim)
    outs = [jax.ShapeDtypeStruct(a.shape, F32) for a in (qm, kct, vct)]
    return pl.pallas_call(
        _attn_sample_kernel,
        grid=(nb // ts,),
        in_specs=[per_b(a) for a in (qm, kn, vn, kct, vct)] + [full(sink_rows), full(slope_rows)],
        out_specs=[per_b(o) for o in outs],
        out_shape=outs,
        scratch_shapes=[pltpu.VMEM((ts, WINDOW, KV_W), F32), pltpu.VMEM((ts, WINDOW, KV_W), F32)],
        compiler_params=pltpu.CompilerParams(dimension_semantics=("arbitrary",),
                                             vmem_limit_bytes=VMEM_LIMIT_BYTES),
        name="attn_sample",
    )(qm, kn, vn, kct, vct, sink_rows, slope_rows)


def _post_kernel(x_ref, ym_ref, att_ref, p_ref, an_ref, g_post_ref, g_fpre_ref, g_fpost_ref,
                 wout_ref, wup_ref, wdown_ref, wpg_ref, wpp_ref, o_ref):
    y_a = _rms(att_ref[...], an_ref[...])
    y = jnp.concatenate([ym_ref[...], y_a], axis=-1).astype(BF16)
    x1 = x_ref[...] + _rms(_dot(y, wout_ref[...]), g_post_ref[...])
    u = _rms(x1, g_fpre_ref[...]).astype(BF16)
    f = jnp.zeros_like(x1)
    for j in range(D_FF // FF_TILE):
        cols = slice(j * FF_TILE, (j + 1) * FF_TILE)
        hid = jnp.square(jnp.maximum(_dot(u, wup_ref[:, cols]), 0.0)).astype(BF16)
        f = f + _dot(hid, wdown_ref[cols, :])
    x2 = x1 + _rms(f, g_fpost_ref[...])
    gate = jax.nn.sigmoid(_dot(x2.astype(BF16), wpg_ref[...]))
    o_ref[...] = x2 + gate * _dot(p_ref[...].astype(BF16), wpp_ref[...])


def _post(x2d, ym, att, p2d, gains, weights):
    n = x2d.shape[0]
    tm = min(ROW_TILE, n)
    row = lambda w: pl.BlockSpec((tm, w), lambda i: (i, 0))
    full = lambda a: pl.BlockSpec(a.shape, lambda i: (0,) * a.ndim)
    resident = lambda a: pl.BlockSpec(a.shape, lambda i: (0,) * a.ndim, pipeline_mode=pl.Buffered(1))
    return pl.pallas_call(
        _post_kernel,
        grid=(n // tm,),
        in_specs=[row(D_MODEL), row(M_WIDTH), row(A_WIDTH), row(P_DIM)]
        + [full(g) for g in gains] + [resident(w) for w in weights],
        out_specs=row(D_MODEL),
        out_shape=jax.ShapeDtypeStruct((n, D_MODEL), F32),
        compiler_params=pltpu.CompilerParams(dimension_semantics=("arbitrary",),
                                             vmem_limit_bytes=VMEM_LIMIT_BYTES),
        name="post",
    )(x2d, ym, att, p2d, *gains, *weights)


def _permute_heads(a, axis):
    shape = a.shape
    a = a.reshape(shape[:axis] + (A_HEADS, A_HD) + shape[axis + 1:])
    a = jnp.take(a, np.asarray(HEAD_ORDER), axis=axis)
    return a.reshape(shape)


def kernel(x_prompt, x_sample, p_prompt, p_sample, state_mlstm_c, state_mlstm_n, state_mlstm_m,
           state_mlstm_conv, cache_swa_k, cache_swa_v, norm_mix_pre, w_in, b_gates, conv_w,
           mlstm_norm, attn_sinks, attn_norm, w_out, norm_mix_post, norm_ffn_pre, w_up, w_down,
           norm_ffn_post, w_pgate, w_pproj):
    depth = w_in.shape[0]
    assert depth == 1, "single-layer decoder"
    B, T, _ = x_prompt.shape
    SB, ST, _ = x_sample.shape
    assert T % (CHUNKS_PER_STEP * CHUNK) == 0 and T % ROW_TILE == 0 and SB % SAMPLE_SEQ_TILE == 0
    i = 0

    wi = w_in[i]
    o_qk, o_vm, o_om, o_g, o_qa, o_ka, o_va = np.cumsum(
        [0, QK_W, M_WIDTH, M_WIDTH, N_GATES, A_WIDTH, KV_W])
    w_main = jnp.concatenate([wi[:, o_qk:o_g], _permute_heads(wi[:, o_qa:o_ka], 1), wi[:, o_ka:]],
                             axis=1).astype(BF16)
    w_gt = wi[:, o_g:o_qa].astype(BF16).T
    g_pre = norm_mix_pre[i].reshape(1, D_MODEL)
    bg_col = b_gates[i].reshape(N_GATES, 1)
    cw = conv_w[i]
    mn = mlstm_norm[i].reshape(1, M_WIDTH)
    sinks = attn_sinks[i].reshape(1, A_HEADS)
    gains = [_permute_heads(attn_norm[i], 0).reshape(1, A_WIDTH), norm_mix_post[i].reshape(1, D_MODEL),
             norm_ffn_pre[i].reshape(1, D_MODEL), norm_ffn_post[i].reshape(1, D_MODEL)]
    wo = jnp.concatenate([w_out[i][:M_WIDTH], _permute_heads(w_out[i][M_WIDTH:], 0)], axis=0)
    weights = [wo.astype(BF16), w_up[i].astype(BF16), w_down[i].astype(BF16),
               w_pgate[i].astype(BF16), w_pproj[i].astype(BF16)]

    xp = x_prompt.reshape(B * T, D_MODEL)
    ym, att, cn_p, m_p, kt_p, vt_p, tail_p = _front(xp, g_pre, w_main, w_gt, cw, bg_col, mn, sinks, B, T)
    y_prompt = _post(xp, ym, att, p_prompt[i].reshape(B * T, P_DIM), gains, weights).reshape(B, T, D_MODEL)
    c_p, n_p = cn_p[..., :M_DV], cn_p[..., M_DV]
    conv_p = tail_p[:, 8 - (CONV_W - 1):]
    k_p = kt_p.reshape(B, WINDOW, A_KV, A_HD)
    v_p = vt_p.reshape(B, WINDOW, A_KV, A_HD)

    xs = x_sample.transpose(1, 0, 2).reshape(ST * SB, D_MODEL)
    ps = p_sample[i].transpose(1, 0, 2).reshape(ST * SB, P_DIM)
    w_nat = jnp.concatenate([w_main[:, 0:QK_W], w_main[:, QK_W + 2 * M_WIDTH:]], axis=1)
    w_t = w_main[:, 0:QK_W + 2 * M_WIDTH].T
    qkn_s, qa_s, ka_s, va_s, qkt_s, vt_s, ot_s, gt_s = _in_proj_sample(xs, g_pre, w_nat, w_t, w_gt)
    cbt = state_mlstm_conv[i].transpose(1, 2, 0)
    ym_s, c_s2, nt_s, mt_s = _mlstm_sample(
        qkt_s, vt_s, ot_s, gt_s, bg_col, cbt, cw.T, mn.reshape(M_WIDTH, 1),
        state_mlstm_c[i].reshape(SB, M_HEADS * M_DK * M_DV), state_mlstm_n[i].transpose(1, 2, 0),
        state_mlstm_m[i].T)
    c_s = c_s2.reshape(SB, M_HEADS, M_DK, M_DV)
    n_s = nt_s.transpose(2, 0, 1)
    m_s = mt_s.reshape(M_HEADS, SB).T
    conv_s = qkn_s.reshape(ST, SB, QK_W)[ST - (CONV_W - 1):].transpose(1, 0, 2)

    qs = qa_s.reshape(ST, SB, A_GROUP, A_KV, A_HD).transpose(1, 3, 2, 0, 4).reshape(SB, A_HEADS * ST, A_HD)
    qm = jnp.concatenate([qs, qs], axis=-1)
    kn = ka_s.reshape(ST, SB, KV_W).transpose(1, 0, 2)
    vn = va_s.reshape(ST, SB, KV_W).transpose(1, 0, 2)
    sink_rows = jnp.repeat(attn_sinks[i], ST).reshape(A_HEADS * ST, 1)
    slope_rows = jnp.asarray(np.repeat(np.asarray(SLOPES, np.float32), ST).reshape(A_HEADS * ST, 1))
    kct = cache_swa_k[i].transpose(0, 2, 3, 1).reshape(SB, KV_W, WINDOW)
    vct = cache_swa_v[i].transpose(0, 2, 3, 1).reshape(SB, KV_W, WINDOW)
    att_s, kct_s, vct_s = _attn_sample(qm, kn, vn, kct, vct, sink_rows, slope_rows)
    att_d = att_s.reshape(SB, A_KV, A_GROUP * ST, A_KV, A_HD)
    att_k = jnp.stack([att_d[:, kv, :, kv, :] for kv in range(A_KV)], axis=1)
    att_s2 = att_k.reshape(SB, A_KV, A_GROUP, ST, A_HD).transpose(3, 0, 2, 1, 4).reshape(ST * SB, A_WIDTH)
    y_s = _post(xs, ym_s, att_s2, ps, gains, weights)
    y_sample = y_s.reshape(ST, SB, D_MODEL).transpose(1, 0, 2)
    k_s = kct_s.reshape(SB, A_KV, A_HD, WINDOW).transpose(0, 3, 1, 2)
    v_s = vct_s.reshape(SB, A_KV, A_HD, WINDOW).transpose(0, 3, 1, 2)

    stack = lambda a: a[None]
    return (y_prompt, y_sample,
            stack(c_p), stack(n_p), stack(m_p.reshape(B, M_HEADS)), stack(conv_p), stack(k_p), stack(v_p),
            stack(c_s), stack(n_s), stack(m_s), stack(conv_s), stack(k_s), stack(v_s))
```

```python
import functools

import numpy as np
import jax
import jax.numpy as jnp
from jax import lax
from jax.experimental import pallas as pl
from jax.experimental.pallas import tpu as pltpu

F32 = jnp.float32
BF16 = jnp.bfloat16

D_MODEL = 1024
M_WIDTH = 512
M_HEADS = 4
M_DV = 128
M_DK = 64
QK_W = 512
CONV_W = 4
CHUNK = 128
A_WIDTH = 512
A_HEADS = 8
A_HD = 64
A_KV = 2
A_GROUP = 4
KV_W = 128
WINDOW = 128
D_FF = 4096
P_DIM = 256
EPS = 1e-6
N_GATES = 2 * M_HEADS

VMEM_LIMIT_BYTES = 56 * 1024 * 1024
ROW_TILE = 512
FF_TILE = 1024
CHUNKS_PER_STEP = 2
SAMPLE_SEQ_TILE = 8
SAMPLE_V_TILE = 32

NEG_INF = float("-inf")
SLOPES = [2.0 ** (-8.0 * (h + 1) / A_HEADS) for h in range(A_HEADS)]
HEAD_ORDER = [kv * A_GROUP + g for g in range(A_GROUP) for kv in range(A_KV)]


def _dot(a, b):
    return jnp.dot(a, b, preferred_element_type=F32)


def _dot_nt(a, b):
    return lax.dot_general(a, b, (((1,), (1,)), ((), ())), preferred_element_type=F32)


def _dot_tn(a, b):
    return lax.dot_general(a, b, (((0,), (0,)), ((), ())), preferred_element_type=F32)


def _rms(x, g):
    return x * lax.rsqrt(jnp.mean(x * x, axis=-1, keepdims=True) + EPS) * g


def _conv_silu(up_ref, base, w, L):
    out = w[0:1, :] * up_ref[base:base + L, :]
    for j in range(1, CONV_W):
        out = out + w[j:j + 1, :] * up_ref[base + j:base + j + L, :]
    return jax.nn.silu(out)


def _chunk_cumsum_lanes(x):
    pos = lax.broadcasted_iota(jnp.int32, x.shape, 1)
    shift = 1
    while shift < CHUNK:
        x = x + jnp.where(pos >= shift, pltpu.roll(x, shift, axis=1), 0.0)
        shift *= 2
    return x


def _in_proj_kernel(x_ref, g_ref, w_ref, wgt_ref, cw_ref, bgc_ref,
                    q_ref, k_ref, vm_ref, os_ref, qa_ref, ka_ref, va_ref, gc_ref, gr_ref, tail_ref,
                    up_ref, *, seq_tiles):
    i = pl.program_id(0)
    tm = x_ref.shape[0]
    xn = _rms(x_ref[...], g_ref[...]).astype(BF16)
    z = _dot(xn, w_ref[...])

    @pl.when(i % seq_tiles == 0)
    def _():
        up_ref[0:8, :] = jnp.zeros((8, QK_W), F32)

    up_ref[8:8 + tm, :] = z[:, 0:QK_W]
    qk = _conv_silu(up_ref, 5, cw_ref[...], tm)
    q_ref[...] = (qk[:, 0:QK_W // 2] * (M_DK ** -0.5)).astype(BF16)
    k_ref[...] = qk[:, QK_W // 2:QK_W]
    up_ref[5:8, :] = up_ref[5 + tm:8 + tm, :]

    @pl.when(i % seq_tiles == seq_tiles - 1)
    def _():
        tail_ref[0] = up_ref[tm:8 + tm, :]

    vm_ref[...] = z[:, 512:1024].astype(BF16)
    os_ref[...] = jax.nn.sigmoid(z[:, 1024:1536])
    qa_ref[...] = (z[:, 1536:2048] * (A_HD ** -0.5)).astype(BF16)
    ka_ref[...] = z[:, 2048:2176]
    va_ref[...] = z[:, 2176:2304]

    g = _dot_nt(wgt_ref[...], xn) + bgc_ref[...]
    row = lax.broadcasted_iota(jnp.int32, (N_GATES, CHUNK), 0)
    for c in range(tm // CHUNK):
        gch = g[:, c * CHUNK:(c + 1) * CHUNK]
        logs = jnp.where(row >= M_HEADS, jax.nn.log_sigmoid(gch), gch)
        out = jnp.where(row >= M_HEADS, _chunk_cumsum_lanes(logs), logs)
        gr_ref[:, c * CHUNK:(c + 1) * CHUNK] = out
        gc_ref[c * CHUNK:(c + 1) * CHUNK, :] = out.T


def _in_proj(x2d, g_pre, w_main, w_gt, conv_w, bg_col, batch):
    n = x2d.shape[0]
    tm = ROW_TILE
    seq_tiles = n // batch // tm
    row = lambda w: pl.BlockSpec((tm, w), lambda i: (i, 0))
    full = lambda a: pl.BlockSpec(a.shape, lambda i: (0,) * a.ndim)
    outs = [(QK_W // 2, BF16), (QK_W // 2, F32), (M_WIDTH, BF16), (M_WIDTH, F32), (A_WIDTH, BF16),
            (KV_W, F32), (KV_W, F32), (N_GATES, F32)]
    return pl.pallas_call(
        functools.partial(_in_proj_kernel, seq_tiles=seq_tiles),
        grid=(n // tm,),
        in_specs=[row(D_MODEL), full(g_pre), full(w_main), full(w_gt), full(conv_w), full(bg_col)],
        out_specs=[row(w) for w, _ in outs] + [pl.BlockSpec((N_GATES, tm), lambda i: (0, i)),
                                                pl.BlockSpec((1, 8, QK_W), lambda i: (i // seq_tiles, 0, 0))],
        out_shape=[jax.ShapeDtypeStruct((n, w), dt) for w, dt in outs]
        + [jax.ShapeDtypeStruct((N_GATES, n), F32), jax.ShapeDtypeStruct((batch, 8, QK_W), F32)],
        scratch_shapes=[pltpu.VMEM((8 + tm, QK_W), F32)],
        compiler_params=pltpu.CompilerParams(dimension_semantics=("arbitrary",),
                                             vmem_limit_bytes=VMEM_LIMIT_BYTES),
        name="in_proj",
    )(x2d, g_pre, w_main, w_gt, conv_w, bg_col)


def _mlstm_out(h, o_sig, g):
    return o_sig * (h * lax.rsqrt(jnp.mean(h * h, axis=-1, keepdims=True) + EPS) * g)


def _mixer_prompt_kernel(q_ref, k_ref, vm_ref, os_ref, qa_ref, ka_ref, va_ref, gc_ref, gr_ref, mn_ref, sink_ref,
                         ym_ref, att_ref, c_out, m_out,
                         c_st, m_st, kp_ref, vp_ref, bias0_ref, bias_ref, sinkrep_ref):
    ci = pl.program_id(1)
    L = CHUNK
    J = range(CHUNKS_PER_STEP)
    H = range(M_HEADS)
    KV = range(A_KV)
    t = lax.broadcasted_iota(jnp.int32, (L, L), 0)
    s = lax.broadcasted_iota(jnp.int32, (L, L), 1)

    @pl.when(ci == 0)
    def _():
        c_st[...] = jnp.zeros_like(c_st)
        m_st[...] = jnp.zeros_like(m_st)
        kp_ref[...] = jnp.zeros_like(kp_ref)
        vp_ref[...] = jnp.zeros_like(vp_ref)
        dist = (t - s).astype(F32)
        for kv in KV:
            for g in range(A_GROUP):
                hd = kv * A_GROUP + g
                rows = slice(g * L, (g + 1) * L)
                cur = jnp.where(s <= t, -SLOPES[hd] * dist, NEG_INF)
                bias_ref[kv, rows, 0:L] = jnp.where(s >= t, -SLOPES[hd] * (dist + float(WINDOW)), NEG_INF)
                bias_ref[kv, rows, L:2 * L] = cur
                bias0_ref[kv, rows, 0:L] = jnp.full((L, L), NEG_INF, F32)
                bias0_ref[kv, rows, L:2 * L] = cur
                sinkrep_ref[kv, rows, :] = jnp.broadcast_to(sink_ref[0:1, hd:hd + 1], (L, KV_W))

    @pl.when(ci == 1)
    def _():
        for kv in KV:
            bias0_ref[kv, :, 0:L] = bias_ref[kv, :, 0:L]

    rows = [slice(j * L, (j + 1) * L) for j in J]
    P = [(j, h) for j in J for h in H]

    ig_c = [gc_ref[rows[j], 0:M_HEADS] for j in J]
    b_c = [gc_ref[rows[j], M_HEADS:N_GATES] for j in J]
    a_r = [gr_ref[0:M_HEADS, rows[j]] - gr_ref[M_HEADS:N_GATES, rows[j]] for j in J]
    causal_bias = jnp.where(s <= t, 0.0, NEG_INF)
    hcol = lax.broadcasted_iota(jnp.int32, (L, M_HEADS), 1)
    m_prev = [m_st[0:1, 0:M_HEADS]]
    w_k, decay = [], []
    for j in J:
        b_last = b_c[j][L - 1:L, :]
        log_w = b_last - b_c[j] + ig_c[j]
        m_new = jnp.maximum(b_last + m_prev[j], jnp.max(log_w, axis=0, keepdims=True))
        w_k.append(jnp.exp(log_w - m_new))
        decay.append(jnp.exp(b_last + m_prev[j] - m_new))
        m_prev.append(m_new)
    m_st[0:1, 0:M_HEADS] = m_prev[-1]

    ones_v = jnp.ones((L, M_DV), BF16)
    qb = {p: q_ref[rows[p[0]], p[1] * M_DK:(p[1] + 1) * M_DK] for p in P}
    kf = {p: k_ref[rows[p[0]], p[1] * M_DK:(p[1] + 1) * M_DK] for p in P}
    kb = {p: kf[p].astype(BF16) for p in P}
    vext = {p: jnp.concatenate([vm_ref[rows[p[0]], p[1] * M_DV:(p[1] + 1) * M_DV], ones_v], axis=1) for p in P}
    qk_h = {p: _dot_nt(qb[p], kb[p]) for p in P}
    kw = {p: (kf[p] * w_k[p[0]][:, p[1]:p[1] + 1]).astype(BF16) for p in P}
    dc = {p: _dot_tn(kw[p], vext[p]) for p in P}
    log_d = {p: b_c[p[0]][:, p[1]:p[1] + 1] + a_r[p[0]][p[1]:p[1] + 1, :] + causal_bias for p in P}
    m_t, w_inter, emt = [], [], []
    for j in J:
        m_intra = jnp.full((L, M_HEADS), NEG_INF, F32)
        for h in H:
            m_intra = jnp.where(hcol == h, jnp.max(log_d[(j, h)], axis=-1, keepdims=True), m_intra)
        log_inter = b_c[j] + m_prev[j]
        m_t.append(jnp.maximum(log_inter, m_intra))
        w_inter.append(jnp.exp(log_inter - m_t[j]))
        emt.append(jnp.exp(-m_t[j]))
    s_h = {p: (qk_h[p] * jnp.exp(log_d[p] - m_t[p[0]][:, p[1]:p[1] + 1])).astype(BF16) for p in P}
    o_h = {p: _dot(s_h[p], vext[p]) for p in P}
    c_cur = [c_st[h] for h in H]
    qc = {}
    for j in J:
        for h in H:
            qc[(j, h)] = _dot(qb[(j, h)], c_cur[h].astype(BF16))
        c_cur = [decay[j][:, h:h + 1] * c_cur[h] + dc[(j, h)] for h in H]
    for h in H:
        c_st[h] = c_cur[h]
    for j, h in P:
        wi = w_inter[j][:, h:h + 1]
        num = o_h[(j, h)][:, 0:M_DV] + wi * qc[(j, h)][:, 0:M_DV]
        den = o_h[(j, h)][:, M_DV:2 * M_DV] + wi * qc[(j, h)][:, M_DV:2 * M_DV]
        hh = num / jnp.maximum(jnp.abs(den), emt[j][:, h:h + 1])
        cols = slice(h * M_DV, (h + 1) * M_DV)
        ym_ref[rows[j], cols] = _mlstm_out(hh, os_ref[rows[j], cols], mn_ref[:, cols])

    nk = (CHUNKS_PER_STEP + 1) * L
    low3 = lax.broadcasted_iota(jnp.int32, (nk, KV_W), 1) < A_HD
    low = lax.broadcasted_iota(jnp.int32, (L, KV_W), 1) < A_HD
    k_all = jnp.concatenate([kp_ref[...], ka_ref[...]], axis=0)
    v_all = jnp.concatenate([vp_ref[...], va_ref[...]], axis=0)
    ones_k = jnp.ones((nk, KV_W), BF16)
    kmask = [jnp.where(low3, k_all, 0.0).astype(BF16), jnp.where(low3, 0.0, k_all).astype(BF16)]
    vext_a = [jnp.concatenate([jnp.where(low3, v_all, 1.0).astype(BF16), ones_k], axis=1),
              jnp.concatenate([jnp.where(low3, 1.0, v_all).astype(BF16), ones_k], axis=1)]
    qst = [jnp.concatenate([qa_ref[rows[j], g * KV_W:(g + 1) * KV_W] for g in range(A_GROUP)], axis=0)
           for j in J]
    Q = [(j, kv) for j in J for kv in KV]
    bias_of = lambda j, kv: (bias0_ref if j == 0 else bias_ref)[kv]
    sc = {p: _dot_nt(qst[p[0]], kmask[p[1]][p[0] * L:(p[0] + 2) * L]) + bias_of(*p) for p in Q}
    m_a = {p: jnp.maximum(jnp.max(jnp.maximum(sc[p][:, 0:L], sc[p][:, L:2 * L]), axis=-1, keepdims=True),
                          sinkrep_ref[p[1]]) for p in Q}
    e = {p: jnp.concatenate([jnp.exp(sc[p][:, 0:L] - m_a[p]), jnp.exp(sc[p][:, L:2 * L] - m_a[p])],
                            axis=1).astype(BF16) for p in Q}
    o = {p: _dot(e[p], vext_a[p[1]][p[0] * L:(p[0] + 2) * L]) for p in Q}
    exs = {p: jnp.exp(sinkrep_ref[p[1]] - m_a[p]) for p in Q}
    for j in J:
        for g in range(A_GROUP):
            gr_ = slice(g * L, (g + 1) * L)
            o0, o1 = o[(j, 0)], o[(j, 1)]
            pv = jnp.where(low, o0[gr_, 0:KV_W], o1[gr_, 0:KV_W])
            den = jnp.where(low, o0[gr_, KV_W:2 * KV_W] + exs[(j, 0)][gr_, :],
                            o1[gr_, KV_W:2 * KV_W] + exs[(j, 1)][gr_, :])
            att_ref[rows[j], g * KV_W:(g + 1) * KV_W] = pv / den
    kp_ref[...] = ka_ref[(CHUNKS_PER_STEP - 1) * L:CHUNKS_PER_STEP * L, :]
    vp_ref[...] = va_ref[(CHUNKS_PER_STEP - 1) * L:CHUNKS_PER_STEP * L, :]

    @pl.when(ci == pl.num_programs(1) - 1)
    def _():
        c_out[0] = c_st[...]
        m_out[0] = m_st[0:1, 0:M_HEADS]


def _mixer_prompt(q, k, vm, os_, qa, ka, va, gc, gr, mlstm_norm, sinks, batch, seq):
    tr = CHUNKS_PER_STEP * CHUNK
    ns = seq // tr
    row = lambda w: pl.BlockSpec((tr, w), lambda b, c: (b * ns + c, 0))
    full = lambda a: pl.BlockSpec(a.shape, lambda b, c: (0,) * a.ndim)
    n = batch * seq
    bias_shape = (A_KV, A_GROUP * CHUNK, 2 * CHUNK)
    return pl.pallas_call(
        _mixer_prompt_kernel,
        grid=(batch, ns),
        in_specs=[row(QK_W // 2), row(QK_W // 2), row(M_WIDTH), row(M_WIDTH), row(A_WIDTH), row(KV_W), row(KV_W),
                  row(N_GATES), pl.BlockSpec((N_GATES, tr), lambda b, c: (0, b * ns + c)),
                  full(mlstm_norm), full(sinks)],
        out_specs=[row(M_WIDTH), row(A_WIDTH),
                   pl.BlockSpec((1, M_HEADS, M_DK, 2 * M_DV), lambda b, c: (b, 0, 0, 0)),
                   pl.BlockSpec((1, 1, M_HEADS), lambda b, c: (b, 0, 0))],
        out_shape=[jax.ShapeDtypeStruct((n, M_WIDTH), F32), jax.ShapeDtypeStruct((n, A_WIDTH), F32),
                   jax.ShapeDtypeStruct((batch, M_HEADS, M_DK, 2 * M_DV), F32),
                   jax.ShapeDtypeStruct((batch, 1, M_HEADS), F32)],
        scratch_shapes=[pltpu.VMEM((M_HEADS, M_DK, 2 * M_DV), F32),
                        pltpu.VMEM((8, 128), F32),
                        pltpu.VMEM((CHUNK, KV_W), F32), pltpu.VMEM((CHUNK, KV_W), F32),
                        pltpu.VMEM(bias_shape, F32), pltpu.VMEM(bias_shape, F32),
                        pltpu.VMEM((A_KV, A_GROUP * CHUNK, KV_W), F32)],
        compiler_params=pltpu.CompilerParams(dimension_semantics=("arbitrary", "arbitrary"),
                                             vmem_limit_bytes=VMEM_LIMIT_BYTES),
        name="mixer_prompt",
    )(q, k, vm, os_, qa, ka, va, gc, gr, mlstm_norm, sinks)


Z_W = QK_W + 2 * M_WIDTH + A_WIDTH + 2 * KV_W
Z_GROUPS = [(0, 512), (512, 1024), (1024, 1536), (1536, 2048), (2048, 2304)]


def _front_kernel(x_ref, g_ref, w_ref, wgt_ref, cw_ref, bgc_ref, mn_ref, sink_ref,
                  ym_ref, att_ref, c_out, m_out, ktail_ref, vtail_ref, ctail_ref,
                  z_ref, g_st, up_ref, c_st, m_st, kp_ref, vp_ref, bias0_ref, bias_ref, sinkrep_ref):
    ci = pl.program_id(1)
    last = pl.num_programs(1) - 1
    L = CHUNK
    J = range(CHUNKS_PER_STEP)
    H = range(M_HEADS)
    KV = range(A_KV)
    TR = CHUNKS_PER_STEP * L
    nxt = ci % 2
    cur = 1 - nxt
    t = lax.broadcasted_iota(jnp.int32, (L, L), 0)
    s = lax.broadcasted_iota(jnp.int32, (L, L), 1)

    @pl.when(ci == 0)
    def _():
        z_ref[cur] = jnp.zeros((TR, Z_W), F32)
        g_st[cur] = jnp.zeros((N_GATES, TR), F32)
        c_st[...] = jnp.zeros_like(c_st)
        m_st[...] = jnp.zeros_like(m_st)
        kp_ref[...] = jnp.zeros_like(kp_ref)
        vp_ref[...] = jnp.zeros_like(vp_ref)
        up_ref[0:8, :] = jnp.zeros((8, QK_W), F32)
        dist = (t - s).astype(F32)
        for kv in KV:
            for g in range(A_GROUP):
                hd = kv * A_GROUP + g
                rows = slice(g * L, (g + 1) * L)
                cur_b = jnp.where(s <= t, -SLOPES[hd] * dist, NEG_INF)
                bias_ref[kv, rows, 0:L] = jnp.where(s >= t, -SLOPES[hd] * (dist + float(WINDOW)), NEG_INF)
                bias_ref[kv, rows, L:2 * L] = cur_b
                bias0_ref[kv, rows, 0:L] = jnp.full((L, L), NEG_INF, F32)
                bias0_ref[kv, rows, L:2 * L] = cur_b
                sinkrep_ref[kv, rows, :] = jnp.broadcast_to(sink_ref[0:1, hd:hd + 1], (L, KV_W))

    @pl.when(ci == 1)
    def _():
        c_st[...] = jnp.zeros_like(c_st)
        m_st[...] = jnp.zeros_like(m_st)
        kp_ref[...] = jnp.zeros_like(kp_ref)
        vp_ref[...] = jnp.zeros_like(vp_ref)
        up_ref[0:8, :] = jnp.zeros((8, QK_W), F32)

    @pl.when(ci == 2)
    def _():
        for kv in KV:
            bias0_ref[kv, :, 0:L] = bias_ref[kv, :, 0:L]

    up_ref[8:8 + TR, :] = z_ref[cur, :, 0:QK_W]
    gcur = g_st[cur] + bgc_ref[...]

    xn = _rms(x_ref[...], g_ref[...]).astype(BF16)
    for lo, hi in Z_GROUPS:
        z_ref[nxt, :, lo:hi] = _dot(xn, w_ref[:, lo:hi])
    g_st[nxt] = _dot_nt(wgt_ref[...], xn)

    qk = _conv_silu(up_ref, 5, cw_ref[...], TR)
    up_ref[5:8, :] = up_ref[5 + TR:8 + TR, :]
    q_all = (qk[:, 0:QK_W // 2] * (M_DK ** -0.5)).astype(BF16)
    k_all_m = qk[:, QK_W // 2:QK_W]
    grow = lax.broadcasted_iota(jnp.int32, (N_GATES, L), 0)
    rows = [slice(j * L, (j + 1) * L) for j in J]
    gr, gc = [], []
    for j in J:
        gch = gcur[:, rows[j]]
        logs = jnp.where(grow >= M_HEADS, jax.nn.log_sigmoid(gch), gch)
        out = jnp.where(grow >= M_HEADS, _chunk_cumsum_lanes(logs), logs)
        gr.append(out)
        gc.append(out.T)

    P = [(j, h) for j in J for h in H]
    ig_c = [gc[j][:, 0:M_HEADS] for j in J]
    b_c = [gc[j][:, M_HEADS:N_GATES] for j in J]
    a_r = [gr[j][0:M_HEADS, :] - gr[j][M_HEADS:N_GATES, :] for j in J]
    causal_bias = jnp.where(s <= t, 0.0, NEG_INF)
    hcol = lax.broadcasted_iota(jnp.int32, (L, M_HEADS), 1)
    m_prev = [m_st[0:1, 0:M_HEADS]]
    w_k, decay = [], []
    for j in J:
        b_last = b_c[j][L - 1:L, :]
        log_w = b_last - b_c[j] + ig_c[j]
        m_new = jnp.maximum(b_last + m_prev[j], jnp.max(log_w, axis=0, keepdims=True))
        w_k.append(jnp.exp(log_w - m_new))
        decay.append(jnp.exp(b_last + m_prev[j] - m_new))
        m_prev.append(m_new)
    m_st[0:1, 0:M_HEADS] = m_prev[-1]

    ones_v = jnp.ones((L, M_DV), BF16)
    qb = {p: q_all[rows[p[0]], p[1] * M_DK:(p[1] + 1) * M_DK] for p in P}
    kf = {p: k_all_m[rows[p[0]], p[1] * M_DK:(p[1] + 1) * M_DK] for p in P}
    kb = {p: kf[p].astype(BF16) for p in P}
    vext = {p: jnp.concatenate([z_ref[cur, rows[p[0]], QK_W + p[1] * M_DV:QK_W + (p[1] + 1) * M_DV].astype(BF16),
                                ones_v], axis=1) for p in P}
    qk_h = {p: _dot_nt(qb[p], kb[p]) for p in P}
    kw = {p: (kf[p] * w_k[p[0]][:, p[1]:p[1] + 1]).astype(BF16) for p in P}
    dc = {p: _dot_tn(kw[p], vext[p]) for p in P}
    log_d = {p: b_c[p[0]][:, p[1]:p[1] + 1] + a_r[p[0]][p[1]:p[1] + 1, :] + causal_bias for p in P}
    m_t, w_inter, emt = [], [], []
    for j in J:
        m_intra = jnp.full((L, M_HEADS), NEG_INF, F32)
        for h in H:
            m_intra = jnp.where(hcol == h, jnp.max(log_d[(j, h)], axis=-1, keepdims=True), m_intra)
        log_inter = b_c[j] + m_prev[j]
        m_t.append(jnp.maximum(log_inter, m_intra))
        w_inter.append(jnp.exp(log_inter - m_t[j]))
        emt.append(jnp.exp(-m_t[j]))
    s_h = {p: (qk_h[p] * jnp.exp(log_d[p] - m_t[p[0]][:, p[1]:p[1] + 1])).astype(BF16) for p in P}
    c_cur = [c_st[h] for h in H]
    qc = {}
    for j in J:
        for h in H:
            qc[(j, h)] = _dot(qb[(j, h)], c_cur[h].astype(BF16))
        c_cur = [decay[j][:, h:h + 1] * c_cur[h] + dc[(j, h)] for h in H]
    for h in H:
        c_st[h] = c_cur[h]
    o_h = {p: _dot(s_h[p], vext[p]) for p in P}
    for j, h in P:
        wi = w_inter[j][:, h:h + 1]
        num = o_h[(j, h)][:, 0:M_DV] + wi * qc[(j, h)][:, 0:M_DV]
        den = o_h[(j, h)][:, M_DV:2 * M_DV] + wi * qc[(j, h)][:, M_DV:2 * M_DV]
        hh = num / jnp.maximum(jnp.abs(den), emt[j][:, h:h + 1])
        cols = slice(h * M_DV, (h + 1) * M_DV)
        o_sig = jax.nn.sigmoid(z_ref[cur, rows[j], QK_W + M_WIDTH + h * M_DV:QK_W + M_WIDTH + (h + 1) * M_DV])
        ym_ref[rows[j], cols] = _mlstm_out(hh, o_sig, mn_ref[:, cols])

    a0 = QK_W + 2 * M_WIDTH
    nk = (CHUNKS_PER_STEP + 1) * L
    low3 = lax.broadcasted_iota(jnp.int32, (nk, KV_W), 1) < A_HD
    low = lax.broadcasted_iota(jnp.int32, (L, KV_W), 1) < A_HD
    ka_cur = z_ref[cur, :, a0 + A_WIDTH:a0 + A_WIDTH + KV_W]
    va_cur = z_ref[cur, :, a0 + A_WIDTH + KV_W:a0 + A_WIDTH + 2 * KV_W]
    k_all = jnp.concatenate([kp_ref[...], ka_cur], axis=0)
    v_all = jnp.concatenate([vp_ref[...], va_cur], axis=0)
    ones_k = jnp.ones((nk, KV_W), BF16)
    kmask = [jnp.where(low3, k_all, 0.0).astype(BF16), jnp.where(low3, 0.0, k_all).astype(BF16)]
    vext_a = [jnp.concatenate([jnp.where(low3, v_all, 1.0).astype(BF16), ones_k], axis=1),
              jnp.concatenate([jnp.where(low3, 1.0, v_all).astype(BF16), ones_k], axis=1)]
    qst = [(jnp.concatenate([z_ref[cur, rows[j], a0 + g * KV_W:a0 + (g + 1) * KV_W] for g in range(A_GROUP)],
                            axis=0) * (A_HD ** -0.5)).astype(BF16) for j in J]
    Q = [(j, kv) for j in J for kv in KV]
    bias_of = lambda j, kv: (bias0_ref if j == 0 else bias_ref)[kv]
    sc = {p: _dot_nt(qst[p[0]], kmask[p[1]][p[0] * L:(p[0] + 2) * L]) + bias_of(*p) for p in Q}
    m_a = {p: jnp.maximum(jnp.max(jnp.maximum(sc[p][:, 0:L], sc[p][:, L:2 * L]), axis=-1, keepdims=True),
                          sinkrep_ref[p[1]]) for p in Q}
    e = {p: jnp.concatenate([jnp.exp(sc[p][:, 0:L] - m_a[p]), jnp.exp(sc[p][:, L:2 * L] - m_a[p])],
                            axis=1).astype(BF16) for p in Q}
    o = {p: _dot(e[p], vext_a[p[1]][p[0] * L:(p[0] + 2) * L]) for p in Q}
    exs = {p: jnp.exp(sinkrep_ref[p[1]] - m_a[p]) for p in Q}
    for j in J:
        for g in range(A_GROUP):
            gr_ = slice(g * L, (g + 1) * L)
            o0, o1 = o[(j, 0)], o[(j, 1)]
            pv = jnp.where(low, o0[gr_, 0:KV_W], o1[gr_, 0:KV_W])
            den = jnp.where(low, o0[gr_, KV_W:2 * KV_W] + exs[(j, 0)][gr_, :],
                            o1[gr_, KV_W:2 * KV_W] + exs[(j, 1)][gr_, :])
            att_ref[rows[j], g * KV_W:(g + 1) * KV_W] = pv / den
    kp_ref[...] = ka_cur[(CHUNKS_PER_STEP - 1) * L:CHUNKS_PER_STEP * L, :]
    vp_ref[...] = va_cur[(CHUNKS_PER_STEP - 1) * L:CHUNKS_PER_STEP * L, :]

    @pl.when(ci == last)
    def _():
        c_out[0] = c_st[...]
        m_out[0] = m_st[0:1, 0:M_HEADS]
        ktail_ref[0] = kp_ref[...]
        vtail_ref[0] = vp_ref[...]
        ctail_ref[0] = up_ref[0:8, :]


def _front(x2d, g_pre, w_main, w_gt, conv_w, bg_col, mlstm_norm, sinks, batch, seq):
    tr = CHUNKS_PER_STEP * CHUNK
    ns = seq // tr
    n = batch * seq
    full = lambda a: pl.BlockSpec(a.shape, lambda b, c: (0,) * a.ndim)
    resident = lambda a: pl.BlockSpec(a.shape, lambda b, c: (0,) * a.ndim, pipeline_mode=pl.Buffered(1))
    out_row = lambda w: pl.BlockSpec((tr, w), lambda b, c: (b * ns + jnp.maximum(c - 1, 0), 0))
    per_seq = lambda shape: pl.BlockSpec((1,) + shape, lambda b, c: (b,) + (0,) * len(shape))
    bias_shape = (A_KV, A_GROUP * CHUNK, 2 * CHUNK)
    return pl.pallas_call(
        _front_kernel,
        grid=(batch, ns + 1),
        in_specs=[pl.BlockSpec((tr, D_MODEL), lambda b, c: (b * ns + jnp.minimum(c, ns - 1), 0)),
                  full(g_pre), resident(w_main), full(w_gt), full(conv_w), full(bg_col),
                  full(mlstm_norm), full(sinks)],
        out_specs=[out_row(M_WIDTH), out_row(A_WIDTH),
                   per_seq((M_HEADS, M_DK, 2 * M_DV)), per_seq((1, M_HEADS)),
                   per_seq((CHUNK, KV_W)), per_seq((CHUNK, KV_W)), per_seq((8, QK_W))],
        out_shape=[jax.ShapeDtypeStruct((n, M_WIDTH), F32), jax.ShapeDtypeStruct((n, A_WIDTH), F32),
                   jax.ShapeDtypeStruct((batch, M_HEADS, M_DK, 2 * M_DV), F32),
                   jax.ShapeDtypeStruct((batch, 1, M_HEADS), F32),
                   jax.ShapeDtypeStruct((batch, CHUNK, KV_W), F32), jax.ShapeDtypeStruct((batch, CHUNK, KV_W), F32),
                   jax.ShapeDtypeStruct((batch, 8, QK_W), F32)],
        scratch_shapes=[pltpu.VMEM((2, tr, Z_W), F32), pltpu.VMEM((2, N_GATES, tr), F32),
                        pltpu.VMEM((8 + tr, QK_W), F32),
                        pltpu.VMEM((M_HEADS, M_DK, 2 * M_DV), F32), pltpu.VMEM((8, 128), F32),
                        pltpu.VMEM((CHUNK, KV_W), F32), pltpu.VMEM((CHUNK, KV_W), F32),
                        pltpu.VMEM(bias_shape, F32), pltpu.VMEM(bias_shape, F32),
                        pltpu.VMEM((A_KV, A_GROUP * CHUNK, KV_W), F32)],
        compiler_params=pltpu.CompilerParams(dimension_semantics=("arbitrary", "arbitrary"),
                                             vmem_limit_bytes=VMEM_LIMIT_BYTES),
        name="front",
    )(x2d, g_pre, w_main, w_gt, conv_w, bg_col, mlstm_norm, sinks)


def _in_proj_sample_kernel(x_ref, g_ref, wn_ref, wt_ref, wgt_ref,
                           qkn_ref, qa_ref, ka_ref, va_ref, qkt_ref, vt_ref, ot_ref, gt_ref):
    xn = _rms(x_ref[...], g_ref[...]).astype(BF16)
    zn = _dot(xn, wn_ref[...])
    qkn_ref[...] = zn[:, 0:QK_W]
    qa_ref[...] = zn[:, QK_W:QK_W + A_WIDTH]
    ka_ref[...] = zn[:, QK_W + A_WIDTH:QK_W + A_WIDTH + KV_W]
    va_ref[...] = zn[:, QK_W + A_WIDTH + KV_W:QK_W + A_WIDTH + 2 * KV_W]
    zt = _dot_nt(wt_ref[...], xn)
    qkt_ref[...] = zt[0:QK_W, :]
    vt_ref[...] = zt[QK_W:QK_W + M_WIDTH, :]
    ot_ref[...] = zt[QK_W + M_WIDTH:QK_W + 2 * M_WIDTH, :]
    gt_ref[...] = _dot_nt(wgt_ref[...], xn)


def _in_proj_sample(x2d, g_pre, w_nat, w_t, w_gt):
    n = x2d.shape[0]
    shapes = [(n, QK_W), (n, A_WIDTH), (n, KV_W), (n, KV_W), (QK_W, n), (M_WIDTH, n), (M_WIDTH, n), (N_GATES, n)]
    return pl.pallas_call(
        _in_proj_sample_kernel,
        out_shape=[jax.ShapeDtypeStruct(s, F32) for s in shapes],
        compiler_params=pltpu.CompilerParams(vmem_limit_bytes=VMEM_LIMIT_BYTES),
        name="in_proj_sample",
    )(x2d, g_pre, w_nat, w_t, w_gt)


def _mlstm_sample_kernel(qt_ref, kt_ref, vt_ref, ot_ref, gt_ref, bg_ref, cbq_ref, cbk_ref, cwq_ref, cwk_ref,
                         gain_ref, c_ref, n_ref, m_ref,
                         ym_ref, c_out, n_out, m_out,
                         ct_ref, q_s, ik_s, num_s):
    h = pl.program_id(0)
    NB = c_ref.shape[0]
    L = qt_ref.shape[1] // NB
    T = range(L)

    def conv(raw_ref, cb_ref, cw_ref):
        ups = [cb_ref[j] for j in range(CONV_W - 1)] + [raw_ref[:, t * NB:(t + 1) * NB] for t in T]
        w = [cw_ref[:, j:j + 1] for j in range(CONV_W)]
        outs = []
        for t in T:
            acc = w[0] * ups[t]
            for j in range(1, CONV_W):
                acc = acc + w[j] * ups[t + j]
            outs.append(jax.nn.silu(acc))
        return outs

    q = [x * (M_DK ** -0.5) for x in conv(qt_ref, cbq_ref, cwq_ref)]
    k = conv(kt_ref, cbk_ref, cwk_ref)

    ig_all = gt_ref[pl.ds(h, 1), :] + bg_ref[pl.ds(h, 1), :]
    lf_all = jax.nn.log_sigmoid(gt_ref[pl.ds(h + M_HEADS, 1), :] + bg_ref[pl.ds(h + M_HEADS, 1), :])
    m = m_ref[pl.ds(h, 1), :]
    f, ms = [], []
    n = n_ref[0]
    den = []
    for t in T:
        ig, lf = ig_all[:, t * NB:(t + 1) * NB], lf_all[:, t * NB:(t + 1) * NB]
        m_new = jnp.maximum(lf + m, ig)
        f_t = jnp.exp(lf + m - m_new)
        ik = jnp.exp(ig - m_new) * k[t]
        m = m_new
        n = f_t * n + ik
        f.append(f_t)
        ms.append(m_new)
        den.append(jnp.sum(q[t] * n, axis=0, keepdims=True))
        ik_s[t] = ik
        q_s[t] = q[t]
    m_out[0] = m
    n_out[0] = n

    for d in range(M_DK):
        ct_ref[d * M_DV:(d + 1) * M_DV, :] = c_ref[:, d * M_DV:(d + 1) * M_DV].T

    VT = SAMPLE_V_TILE
    for vq in range(M_DV // VT):
        vts = [vt_ref[vq * VT:(vq + 1) * VT, t * NB:(t + 1) * NB] for t in T]

        def step(d, accs, vq=vq, vts=vts):
            r0 = pl.multiple_of(d * M_DV + vq * VT, VT)
            c = ct_ref[pl.ds(r0, VT), :]
            new = []
            for t in T:
                c = f[t] * c + ik_s[t, pl.ds(d, 1), :] * vts[t]
                new.append(accs[t] + q_s[t, pl.ds(d, 1), :] * c)
            ct_ref[pl.ds(r0, VT), :] = c
            return tuple(new)

        accs = lax.fori_loop(0, M_DK, step, tuple(jnp.zeros((VT, NB), F32) for _ in T), unroll=8)
        for t in T:
            num_s[t, vq * VT:(vq + 1) * VT, :] = accs[t]

    for t in T:
        hh = num_s[t] / jnp.maximum(jnp.abs(den[t]), jnp.exp(-ms[t]))
        hn = hh * lax.rsqrt(jnp.mean(hh * hh, axis=0, keepdims=True) + EPS) * gain_ref[...]
        y = jax.nn.sigmoid(ot_ref[:, t * NB:(t + 1) * NB]) * hn
        ym_ref[t * NB:(t + 1) * NB, :] = y.T

    for d in range(M_DK):
        c_out[:, d * M_DV:(d + 1) * M_DV] = ct_ref[d * M_DV:(d + 1) * M_DV, :].T


def _mlstm_sample(qkt, vt, ot, gt, bg_col, cbt, cwt, gain_col, c2d, nt, mt):
    nb = c2d.shape[0]
    n = qkt.shape[1]
    L = n // nb
    hblk = lambda rows, off: pl.BlockSpec((rows, n), lambda h, off=off: (h + off, 0))
    full = lambda a: pl.BlockSpec(a.shape, lambda h: (0,) * a.ndim)
    kq = QK_W // 2 // M_DK
    return pl.pallas_call(
        _mlstm_sample_kernel,
        grid=(M_HEADS,),
        in_specs=[hblk(M_DK, 0), hblk(M_DK, kq), hblk(M_DV, 0), hblk(M_DV, 0), full(gt), full(bg_col),
                  pl.BlockSpec((CONV_W - 1, M_DK, nb), lambda h: (0, h, 0)),
                  pl.BlockSpec((CONV_W - 1, M_DK, nb), lambda h: (0, h + kq, 0)),
                  pl.BlockSpec((M_DK, CONV_W), lambda h: (h, 0)),
                  pl.BlockSpec((M_DK, CONV_W), lambda h: (h + kq, 0)),
                  pl.BlockSpec((M_DV, 1), lambda h: (h, 0)),
                  pl.BlockSpec((nb, M_DK * M_DV), lambda h: (0, h)),
                  pl.BlockSpec((1, M_DK, nb), lambda h: (h, 0, 0)),
                  full(mt)],
        out_specs=[pl.BlockSpec((n, M_DV), lambda h: (0, h)),
                   pl.BlockSpec((nb, M_DK * M_DV), lambda h: (0, h)),
                   pl.BlockSpec((1, M_DK, nb), lambda h: (h, 0, 0)),
                   pl.BlockSpec((1, 1, nb), lambda h: (h, 0, 0))],
        out_shape=[jax.ShapeDtypeStruct((n, M_WIDTH), F32), jax.ShapeDtypeStruct(c2d.shape, F32),
                   jax.ShapeDtypeStruct(nt.shape, F32), jax.ShapeDtypeStruct((M_HEADS, 1, nb), F32)],
        scratch_shapes=[pltpu.VMEM((M_DK * M_DV, nb), F32), pltpu.VMEM((L, M_DK, nb), F32),
                        pltpu.VMEM((L, M_DK, nb), F32), pltpu.VMEM((L, M_DV, nb), F32)],
        compiler_params=pltpu.CompilerParams(dimension_semantics=("arbitrary",),
                                             vmem_limit_bytes=VMEM_LIMIT_BYTES),
        name="mlstm_sample",
    )(qkt, qkt, vt, ot, gt, bg_col, cbt, cbt, cwt, cwt, gain_col, c2d, nt, mt)


def _attn_sample_kernel(qm_ref, kn_ref, vn_ref, kct_ref, vct_ref, sinkr_ref, sloper_ref,
                        att_ref, kct_out, vct_out, xk_ref, xv_ref):
    NB, L = kn_ref.shape[0], kn_ref.shape[1]
    R = A_KV * A_GROUP * L

    def bias(nkeys, offset):
        r = lax.broadcasted_iota(jnp.int32, (R, nkeys), 0).astype(F32)
        pos = lax.broadcasted_iota(jnp.int32, (R, nkeys), 1).astype(F32)
        tq = r - L * jnp.floor((r + 0.5) / L)
        dist = tq + offset - pos
        return jnp.where((dist >= 0.0) & (dist <= float(WINDOW)), -sloper_ref[...] * dist, NEG_INF)

    @pl.when(pl.program_id(0) == 0)
    def _():
        xk_ref[...] = jnp.zeros_like(xk_ref)
        xv_ref[...] = jnp.zeros_like(xv_ref)

    bias_c = bias(WINDOW, float(WINDOW))
    bias_n = bias(L, 0.0)
    sink = sinkr_ref[...]
    rq = lax.broadcasted_iota(jnp.int32, (R, KV_W), 0)
    cq = lax.broadcasted_iota(jnp.int32, (R, KV_W), 1)
    same_kv = (rq < A_GROUP * L) == (cq < A_HD)
    newest = lax.broadcasted_iota(jnp.int32, (KV_W, WINDOW), 1) >= WINDOW - L
    SEQ = range(NB)
    qh = [jnp.where(same_kv, qm_ref[b] * (A_HD ** -0.5), 0.0).astype(BF16) for b in SEQ]
    kt = [kct_ref[b] for b in SEQ]
    vt = [vct_ref[b] for b in SEQ]
    knb = [kn_ref[b].astype(BF16) for b in SEQ]
    vnb = [vn_ref[b].astype(BF16) for b in SEQ]
    sc_c = [_dot(qh[b], kt[b].astype(BF16)) + bias_c for b in SEQ]
    sc_n = [_dot_nt(qh[b], knb[b]) + bias_n for b in SEQ]
    m_a = [jnp.maximum(sink, jnp.maximum(jnp.max(sc_c[b], axis=-1, keepdims=True),
                                         jnp.max(sc_n[b], axis=-1, keepdims=True))) for b in SEQ]
    e_c = [jnp.exp(sc_c[b] - m_a[b]) for b in SEQ]
    e_n = [jnp.exp(sc_n[b] - m_a[b]) for b in SEQ]
    den_a = [jnp.exp(sink - m_a[b]) + jnp.sum(e_c[b], axis=-1, keepdims=True)
             + jnp.sum(e_n[b], axis=-1, keepdims=True) for b in SEQ]
    pv = [_dot_nt((e_c[b] / den_a[b]).astype(BF16), vt[b].astype(BF16))
          + _dot((e_n[b] / den_a[b]).astype(BF16), vnb[b]) for b in SEQ]
    for b in SEQ:
        att_ref[b] = pv[b]

    for b in SEQ:
        xk_ref[b, WINDOW - L:WINDOW, :] = kn_ref[b]
        xv_ref[b, WINDOW - L:WINDOW, :] = vn_ref[b]
    for b in SEQ:
        kct_out[b] = jnp.where(newest, xk_ref[b].T, pltpu.roll(kt[b], WINDOW - L, axis=1))
        vct_out[b] = jnp.where(newest, xv_ref[b].T, pltpu.roll(vt[b], WINDOW - L, axis=1))


def _attn_sample(qm, kn, vn, kct, vct, sink_rows, slope_rows):
    nb = qm.shape[0]
    ts = min(SAMPLE_SEQ_TILE, nb)
    per_b = lambda a: pl.BlockSpec((ts,) + a.shape[1:], lambda b: (b,) + (0,) * (a.ndim - 1))
    full = lambda a: pl.BlockSpec(a.shape, lambda b: (0,) * a.ndim)
    outs = [jax.ShapeDtypeStruct(a.shape, F32) for a in (qm, kct, vct)]
    return pl.pallas_call(
        _attn_sample_kernel,
        grid=(nb // ts,),
        in_specs=[per_b(a) for a in (qm, kn, vn, kct, vct)] + [full(sink_rows), full(slope_rows)],
        out_specs=[per_b(o) for o in outs],
        out_shape=outs,
        scratch_shapes=[pltpu.VMEM((ts, WINDOW, KV_W), F32), pltpu.VMEM((ts, WINDOW, KV_W), F32)],
        compiler_params=pltpu.CompilerParams(dimension_semantics=("arbitrary",),
                                             vmem_limit_bytes=VMEM_LIMIT_BYTES),
        name="attn_sample",
    )(qm, kn, vn, kct, vct, sink_rows, slope_rows)


def _post_kernel(x_ref, ym_ref, att_ref, p_ref, an_ref, g_post_ref, g_fpre_ref, g_fpost_ref,
                 wout_ref, wup_ref, wdown_ref, wpg_ref, wpp_ref, o_ref):
    y_a = _rms(att_ref[...], an_ref[...])
    y = jnp.concatenate([ym_ref[...], y_a], axis=-1).astype(BF16)
    x1 = x_ref[...] + _rms(_dot(y, wout_ref[...]), g_post_ref[...])
    u = _rms(x1, g_fpre_ref[...]).astype(BF16)
    f = jnp.zeros_like(x1)
    for j in range(D_FF // FF_TILE):
        cols = slice(j * FF_TILE, (j + 1) * FF_TILE)
        hid = jnp.square(jnp.maximum(_dot(u, wup_ref[:, cols]), 0.0)).astype(BF16)
        f = f + _dot(hid, wdown_ref[cols, :])
    x2 = x1 + _rms(f, g_fpost_ref[...])
    gate = jax.nn.sigmoid(_dot(x2.astype(BF16), wpg_ref[...]))
    o_ref[...] = x2 + gate * _dot(p_ref[...].astype(BF16), wpp_ref[...])


def _post(x2d, ym, att, p2d, gains, weights):
    n = x2d.shape[0]
    tm = min(ROW_TILE, n)
    row = lambda w: pl.BlockSpec((tm, w), lambda i: (i, 0))
    full = lambda a: pl.BlockSpec(a.shape, lambda i: (0,) * a.ndim)
    resident = lambda a: pl.BlockSpec(a.shape, lambda i: (0,) * a.ndim, pipeline_mode=pl.Buffered(1))
    return pl.pallas_call(
        _post_kernel,
        grid=(n // tm,),
        in_specs=[row(D_MODEL), row(M_WIDTH), row(A_WIDTH), row(P_DIM)]
        + [full(g) for g in gains] + [resident(w) for w in weights],
        out_specs=row(D_MODEL),
        out_shape=jax.ShapeDtypeStruct((n, D_MODEL), F32),
        compiler_params=pltpu.CompilerParams(dimension_semantics=("arbitrary",),
                                             vmem_limit_bytes=VMEM_LIMIT_BYTES),
        name="post",
    )(x2d, ym, att, p2d, *gains, *weights)


def _permute_heads(a, axis):
    shape = a.shape
    a = a.reshape(shape[:axis] + (A_HEADS, A_HD) + shape[axis + 1:])
    a = jnp.take(a, np.asarray(HEAD_ORDER), axis=axis)
    return a.reshape(shape)


def kernel(x_prompt, x_sample, p_prompt, p_sample, state_mlstm_c, state_mlstm_n, state_mlstm_m,
           state_mlstm_conv, cache_swa_k, cache_swa_v, norm_mix_pre, w_in, b_gates, conv_w,
           mlstm_norm, attn_sinks, attn_norm, w_out, norm_mix_post, norm_ffn_pre, w_up, w_down,
           norm_ffn_post, w_pgate, w_pproj):
    depth = w_in.shape[0]
    assert depth == 1, "single-layer decoder"
    B, T, _ = x_prompt.shape
    SB, ST, _ = x_sample.shape
    assert T % (CHUNKS_PER_STEP * CHUNK) == 0 and T % ROW_TILE == 0 and SB % SAMPLE_SEQ_TILE == 0
    i = 0

    wi = w_in[i]
    o_qk, o_vm, o_om, o_g, o_qa, o_ka, o_va = np.cumsum(
        [0, QK_W, M_WIDTH, M_WIDTH, N_GATES, A_WIDTH, KV_W])
    w_main = jnp.concatenate([wi[:, o_qk:o_g], _permute_heads(wi[:, o_qa:o_ka], 1), wi[:, o_ka:]],
                             axis=1).astype(BF16)
    w_gt = wi[:, o_g:o_qa].astype(BF16).T
    g_pre = norm_mix_pre[i].reshape(1, D_MODEL)
    bg_col = b_gates[i].reshape(N_GATES, 1)
    cw = conv_w[i]
    mn = mlstm_norm[i].reshape(1, M_WIDTH)
    sinks = attn_sinks[i].reshape(1, A_HEADS)
    gains = [_permute_heads(attn_norm[i], 0).reshape(1, A_WIDTH), norm_mix_post[i].reshape(1, D_MODEL),
             norm_ffn_pre[i].reshape(1, D_MODEL), norm_ffn_post[i].reshape(1, D_MODEL)]
    wo = jnp.concatenate([w_out[i][:M_WIDTH], _permute_heads(w_out[i][M_WIDTH:], 0)], axis=0)
    weights = [wo.astype(BF16), w_up[i].astype(BF16), w_down[i].astype(BF16),
               w_pgate[i].astype(BF16), w_pproj[i].astype(BF16)]

    xp = x_prompt.reshape(B * T, D_MODEL)
    ym, att, cn_p, m_p, kt_p, vt_p, tail_p = _front(xp, g_pre, w_main, w_gt, cw, bg_col, mn, sinks, B, T)
    y_prompt = _post(xp, ym, att, p_prompt[i].reshape(B * T, P_DIM), gains, weights).reshape(B, T, D_MODEL)
    c_p, n_p = cn_p[..., :M_DV], cn_p[..., M_DV]
    conv_p = tail_p[:, 8 - (CONV_W - 1):]
    k_p = kt_p.reshape(B, WINDOW, A_KV, A_HD)
    v_p = vt_p.reshape(B, WINDOW, A_KV, A_HD)

    xs = x_sample.transpose(1, 0, 2).reshape(ST * SB, D_MODEL)
    ps = p_sample[i].transpose(1, 0, 2).reshape(ST * SB, P_DIM)
    w_nat = jnp.concatenate([w_main[:, 0:QK_W], w_main[:, QK_W + 2 * M_WIDTH:]], axis=1)
    w_t = w_main[:, 0:QK_W + 2 * M_WIDTH].T
    qkn_s, qa_s, ka_s, va_s, qkt_s, vt_s, ot_s, gt_s = _in_proj_sample(xs, g_pre, w_nat, w_t, w_gt)
    cbt = state_mlstm_conv[i].transpose(1, 2, 0)
    ym_s, c_s2, nt_s, mt_s = _mlstm_sample(
        qkt_s, vt_s, ot_s, gt_s, bg_col, cbt, cw.T, mn.reshape(M_WIDTH, 1),
        state_mlstm_c[i].reshape(SB, M_HEADS * M_DK * M_DV), state_mlstm_n[i].transpose(1, 2, 0),
        state_mlstm_m[i].T)
    c_s = c_s2.reshape(SB, M_HEADS, M_DK, M_DV)
    n_s = nt_s.transpose(2, 0, 1)
    m_s = mt_s.reshape(M_HEADS, SB).T
    conv_s = qkn_s.reshape(ST, SB, QK_W)[ST - (CONV_W - 1):].transpose(1, 0, 2)

    qs = qa_s.reshape(ST, SB, A_GROUP, A_KV, A_HD).transpose(1, 3, 2, 0, 4).reshape(SB, A_HEADS * ST, A_HD)
    qm = jnp.concatenate([qs, qs], axis=-1)
    kn = ka_s.reshape(ST, SB, KV_W).transpose(1, 0, 2)
    vn = va_s.reshape(ST, SB, KV_W).transpose(1, 0, 2)
    sink_rows = jnp.repeat(attn_sinks[i], ST).reshape(A_HEADS * ST, 1)
    slope_rows = jnp.asarray(np.repeat(np.asarray(SLOPES, np.float32), ST).reshape(A_HEADS * ST, 1))
    kct = cache_swa_k[i].transpose(0, 2, 3, 1).reshape(SB, KV_W, WINDOW)
    vct = cache_swa_v[i].transpose(0, 2, 3, 1).reshape(SB, KV_W, WINDOW)
    att_s, kct_s, vct_s = _attn_sample(qm, kn, vn, kct, vct, sink_rows, slope_rows)
    att_d = att_s.reshape(SB, A_KV, A_GROUP * ST, A_KV, A_HD)
    att_k = jnp.stack([att_d[:, kv, :, kv, :] for kv in range(A_KV)], axis=1)
    att_s2 = att_k.reshape(SB, A_KV, A_GROUP, ST, A_HD).transpose(3, 0, 2, 1, 4).reshape(ST * SB, A_WIDTH)
    y_s = _post(xs, ym_s, att_s2, ps, gains, weights)
    y_sample = y_s.reshape(ST, SB, D_MODEL).transpose(1, 0, 2)
    k_s = kct_s.reshape(SB, A_KV, A_HD, WINDOW).transpose(0, 3, 1, 2)
    v_s = vct_s.reshape(SB, A_KV, A_HD, WINDOW).transpose(0, 3, 1, 2)

    stack = lambda a: a[None]
    return (y_prompt, y_sample,
            stack(c_p), stack(n_p), stack(m_p.reshape(B, M_HEADS)), stack(conv_p), stack(k_p), stack(v_p),
            stack(c_s), stack(n_s), stack(m_s), stack(conv_s), stack(k_s), stack(v_s))
```

```python
import functools

import numpy as np
import jax
import jax.numpy as jnp
from jax import lax
from jax.experimental import pallas as pl
from jax.experimental.pallas import tpu as pltpu

F32 = jnp.float32
BF16 = jnp.bfloat16

D_MODEL = 1024
M_WIDTH = 512
M_HEADS = 4
M_DV = 128
M_DK = 64
QK_W = 512
CONV_W = 4
CHUNK = 128
A_WIDTH = 512
A_HEADS = 8
A_HD = 64
A_KV = 2
A_GROUP = 4
KV_W = 128
WINDOW = 128
D_FF = 4096
P_DIM = 256
EPS = 1e-6
N_GATES = 2 * M_HEADS

VMEM_LIMIT_BYTES = 56 * 1024 * 1024
ROW_TILE = 512
FF_TILE = 1024
CHUNKS_PER_STEP = 2
SAMPLE_SEQ_TILE = 8
SAMPLE_V_TILE = 32

NEG_INF = float("-inf")
SLOPES = [2.0 ** (-8.0 * (h + 1) / A_HEADS) for h in range(A_HEADS)]
HEAD_ORDER = [kv * A_GROUP + g for g in range(A_GROUP) for kv in range(A_KV)]


def _dot(a, b):
    return jnp.dot(a, b, preferred_element_type=F32)


def _dot_nt(a, b):
    return lax.dot_general(a, b, (((1,), (1,)), ((), ())), preferred_element_type=F32)


def _dot_tn(a, b):
    return lax.dot_general(a, b, (((0,), (0,)), ((), ())), preferred_element_type=F32)


def _rms(x, g):
    return x * lax.rsqrt(jnp.mean(x * x, axis=-1, keepdims=True) + EPS) * g


def _conv_silu(up_ref, base, w, L):
    out = w[0:1, :] * up_ref[base:base + L, :]
    for j in range(1, CONV_W):
        out = out + w[j:j + 1, :] * up_ref[base + j:base + j + L, :]
    return jax.nn.silu(out)


def _chunk_cumsum_lanes(x):
    pos = lax.broadcasted_iota(jnp.int32, x.shape, 1)
    shift = 1
    while shift < CHUNK:
        x = x + jnp.where(pos >= shift, pltpu.roll(x, shift, axis=1), 0.0)
        shift *= 2
    return x


def _in_proj_kernel(x_ref, g_ref, w_ref, wgt_ref, cw_ref, bgc_ref,
                    q_ref, k_ref, vm_ref, os_ref, qa_ref, ka_ref, va_ref, gc_ref, gr_ref, tail_ref,
                    up_ref, *, seq_tiles):
    i = pl.program_id(0)
    tm = x_ref.shape[0]
    xn = _rms(x_ref[...], g_ref[...]).astype(BF16)
    z = _dot(xn, w_ref[...])

    @pl.when(i % seq_tiles == 0)
    def _():
        up_ref[0:8, :] = jnp.zeros((8, QK_W), F32)

    up_ref[8:8 + tm, :] = z[:, 0:QK_W]
    qk = _conv_silu(up_ref, 5, cw_ref[...], tm)
    q_ref[...] = (qk[:, 0:QK_W // 2] * (M_DK ** -0.5)).astype(BF16)
    k_ref[...] = qk[:, QK_W // 2:QK_W]
    up_ref[5:8, :] = up_ref[5 + tm:8 + tm, :]

    @pl.when(i % seq_tiles == seq_tiles - 1)
    def _():
        tail_ref[0] = up_ref[tm:8 + tm, :]

    vm_ref[...] = z[:, 512:1024].astype(BF16)
    os_ref[...] = jax.nn.sigmoid(z[:, 1024:1536])
    qa_ref[...] = (z[:, 1536:2048] * (A_HD ** -0.5)).astype(BF16)
    ka_ref[...] = z[:, 2048:2176]
    va_ref[...] = z[:, 2176:2304]

    g = _dot_nt(wgt_ref[...], xn) + bgc_ref[...]
    row = lax.broadcasted_iota(jnp.int32, (N_GATES, CHUNK), 0)
    for c in range(tm // CHUNK):
        gch = g[:, c * CHUNK:(c + 1) * CHUNK]
        logs = jnp.where(row >= M_HEADS, jax.nn.log_sigmoid(gch), gch)
        out = jnp.where(row >= M_HEADS, _chunk_cumsum_lanes(logs), logs)
        gr_ref[:, c * CHUNK:(c + 1) * CHUNK] = out
        gc_ref[c * CHUNK:(c + 1) * CHUNK, :] = out.T


def _in_proj(x2d, g_pre, w_main, w_gt, conv_w, bg_col, batch):
    n = x2d.shape[0]
    tm = ROW_TILE
    seq_tiles = n // batch // tm
    row = lambda w: pl.BlockSpec((tm, w), lambda i: (i, 0))
    full = lambda a: pl.BlockSpec(a.shape, lambda i: (0,) * a.ndim)
    outs = [(QK_W // 2, BF16), (QK_W // 2, F32), (M_WIDTH, BF16), (M_WIDTH, F32), (A_WIDTH, BF16),
            (KV_W, F32), (KV_W, F32), (N_GATES, F32)]
    return pl.pallas_call(
        functools.partial(_in_proj_kernel, seq_tiles=seq_tiles),
        grid=(n // tm,),
        in_specs=[row(D_MODEL), full(g_pre), full(w_main), full(w_gt), full(conv_w), full(bg_col)],
        out_specs=[row(w) for w, _ in outs] + [pl.BlockSpec((N_GATES, tm), lambda i: (0, i)),
                                                pl.BlockSpec((1, 8, QK_W), lambda i: (i // seq_tiles, 0, 0))],
        out_shape=[jax.ShapeDtypeStruct((n, w), dt) for w, dt in outs]
        + [jax.ShapeDtypeStruct((N_GATES, n), F32), jax.ShapeDtypeStruct((batch, 8, QK_W), F32)],
        scratch_shapes=[pltpu.VMEM((8 + tm, QK_W), F32)],
        compiler_params=pltpu.CompilerParams(dimension_semantics=("arbitrary",),
                                             vmem_limit_bytes=VMEM_LIMIT_BYTES),
        name="in_proj",
    )(x2d, g_pre, w_main, w_gt, conv_w, bg_col)


def _mlstm_out(h, o_sig, g):
    return o_sig * (h * lax.rsqrt(jnp.mean(h * h, axis=-1, keepdims=True) + EPS) * g)


def _mixer_prompt_kernel(q_ref, k_ref, vm_ref, os_ref, qa_ref, ka_ref, va_ref, gc_ref, gr_ref, mn_ref, sink_ref,
                         ym_ref, att_ref, c_out, m_out,
                         c_st, m_st, kp_ref, vp_ref, bias0_ref, bias_ref, sinkrep_ref):
    ci = pl.program_id(1)
    L = CHUNK
    J = range(CHUNKS_PER_STEP)
    H = range(M_HEADS)
    KV = range(A_KV)
    t = lax.broadcasted_iota(jnp.int32, (L, L), 0)
    s = lax.broadcasted_iota(jnp.int32, (L, L), 1)

    @pl.when(ci == 0)
    def _():
        c_st[...] = jnp.zeros_like(c_st)
        m_st[...] = jnp.zeros_like(m_st)
        kp_ref[...] = jnp.zeros_like(kp_ref)
        vp_ref[...] = jnp.zeros_like(vp_ref)
        dist = (t - s).astype(F32)
        for kv in KV:
            for g in range(A_GROUP):
                hd = kv * A_GROUP + g
                rows = slice(g * L, (g + 1) * L)
                cur = jnp.where(s <= t, -SLOPES[hd] * dist, NEG_INF)
                bias_ref[kv, rows, 0:L] = jnp.where(s >= t, -SLOPES[hd] * (dist + float(WINDOW)), NEG_INF)
                bias_ref[kv, rows, L:2 * L] = cur
                bias0_ref[kv, rows, 0:L] = jnp.full((L, L), NEG_INF, F32)
                bias0_ref[kv, rows, L:2 * L] = cur
                sinkrep_ref[kv, rows, :] = jnp.broadcast_to(sink_ref[0:1, hd:hd + 1], (L, KV_W))

    @pl.when(ci == 1)
    def _():
        for kv in KV:
            bias0_ref[kv, :, 0:L] = bias_ref[kv, :, 0:L]

    rows = [slice(j * L, (j + 1) * L) for j in J]
    P = [(j, h) for j in J for h in H]

    ig_c = [gc_ref[rows[j], 0:M_HEADS] for j in J]
    b_c = [gc_ref[rows[j], M_HEADS:N_GATES] for j in J]
    a_r = [gr_ref[0:M_HEADS, rows[j]] - gr_ref[M_HEADS:N_GATES, rows[j]] for j in J]
    causal_bias = jnp.where(s <= t, 0.0, NEG_INF)
    hcol = lax.broadcasted_iota(jnp.int32, (L, M_HEADS), 1)
    m_prev = [m_st[0:1, 0:M_HEADS]]
    w_k, decay = [], []
    for j in J:
        b_last = b_c[j][L - 1:L, :]
        log_w = b_last - b_c[j] + ig_c[j]
        m_new = jnp.maximum(b_last + m_prev[j], jnp.max(log_w, axis=0, keepdims=True))
        w_k.append(jnp.exp(log_w - m_new))
        decay.append(jnp.exp(b_last + m_prev[j] - m_new))
        m_prev.append(m_new)
    m_st[0:1, 0:M_HEADS] = m_prev[-1]

    ones_v = jnp.ones((L, M_DV), BF16)
    qb = {p: q_ref[rows[p[0]], p[1] * M_DK:(p[1] + 1) * M_DK] for p in P}
    kf = {p: k_ref[rows[p[0]], p[1] * M_DK:(p[1] + 1) * M_DK] for p in P}
    kb = {p: kf[p].astype(BF16) for p in P}
    vext = {p: jnp.concatenate([vm_ref[rows[p[0]], p[1] * M_DV:(p[1] + 1) * M_DV], ones_v], axis=1) for p in P}
    qk_h = {p: _dot_nt(qb[p], kb[p]) for p in P}
    kw = {p: (kf[p] * w_k[p[0]][:, p[1]:p[1] + 1]).astype(BF16) for p in P}
    dc = {p: _dot_tn(kw[p], vext[p]) for p in P}
    log_d = {p: b_c[p[0]][:, p[1]:p[1] + 1] + a_r[p[0]][p[1]:p[1] + 1, :] + causal_bias for p in P}
    m_t, w_inter, emt = [], [], []
    for j in J:
        m_intra = jnp.full((L, M_HEADS), NEG_INF, F32)
        for h in H:
            m_intra = jnp.where(hcol == h, jnp.max(log_d[(j, h)], axis=-1, keepdims=True), m_intra)
        log_inter = b_c[j] + m_prev[j]
        m_t.append(jnp.maximum(log_inter, m_intra))
        w_inter.append(jnp.exp(log_inter - m_t[j]))
        emt.append(jnp.exp(-m_t[j]))
    s_h = {p: (qk_h[p] * jnp.exp(log_d[p] - m_t[p[0]][:, p[1]:p[1] + 1])).astype(BF16) for p in P}
    o_h = {p: _dot(s_h[p], vext[p]) for p in P}
    c_cur = [c_st[h] for h in H]
    qc = {}
    for j in J:
        for h in H:
            qc[(j, h)] = _dot(qb[(j, h)], c_cur[h].astype(BF16))
        c_cur = [decay[j][:, h:h + 1] * c_cur[h] + dc[(j, h)] for h in H]
    for h in H:
        c_st[h] = c_cur[h]
    for j, h in P:
        wi = w_inter[j][:, h:h + 1]
        num = o_h[(j, h)][:, 0:M_DV] + wi * qc[(j, h)][:, 0:M_DV]
        den = o_h[(j, h)][:, M_DV:2 * M_DV] + wi * qc[(j, h)][:, M_DV:2 * M_DV]
        hh = num / jnp.maximum(jnp.abs(den), emt[j][:, h:h + 1])
        cols = slice(h * M_DV, (h + 1) * M_DV)
        ym_ref[rows[j], cols] = _mlstm_out(hh, os_ref[rows[j], cols], mn_ref[:, cols])

    nk = (CHUNKS_PER_STEP + 1) * L
    low3 = lax.broadcasted_iota(jnp.int32, (nk, KV_W), 1) < A_HD
    low = lax.broadcasted_iota(jnp.int32, (L, KV_W), 1) < A_HD
    k_all = jnp.concatenate([kp_ref[...], ka_ref[...]], axis=0)
    v_all = jnp.concatenate([vp_ref[...], va_ref[...]], axis=0)
    ones_k = jnp.ones((nk, KV_W), BF16)
    kmask = [jnp.where(low3, k_all, 0.0).astype(BF16), jnp.where(low3, 0.0, k_all).astype(BF16)]
    vext_a = [jnp.concatenate([jnp.where(low3, v_all, 1.0).astype(BF16), ones_k], axis=1),
              jnp.concatenate([jnp.where(low3, 1.0, v_all).astype(BF16), ones_k], axis=1)]
    qst = [jnp.concatenate([qa_ref[rows[j], g * KV_W:(g + 1) * KV_W] for g in range(A_GROUP)], axis=0)
           for j in J]
    Q = [(j, kv) for j in J for kv in KV]
    bias_of = lambda j, kv: (bias0_ref if j == 0 else bias_ref)[kv]
    sc = {p: _dot_nt(qst[p[0]], kmask[p[1]][p[0] * L:(p[0] + 2) * L]) + bias_of(*p) for p in Q}
    m_a = {p: jnp.maximum(jnp.max(jnp.maximum(sc[p][:, 0:L], sc[p][:, L:2 * L]), axis=-1, keepdims=True),
                          sinkrep_ref[p[1]]) for p in Q}
    e = {p: jnp.concatenate([jnp.exp(sc[p][:, 0:L] - m_a[p]), jnp.exp(sc[p][:, L:2 * L] - m_a[p])],
                            axis=1).astype(BF16) for p in Q}
    o = {p: _dot(e[p], vext_a[p[1]][p[0] * L:(p[0] + 2) * L]) for p in Q}
    exs = {p: jnp.exp(sinkrep_ref[p[1]] - m_a[p]) for p in Q}
    for j in J:
        for g in range(A_GROUP):
            gr_ = slice(g * L, (g + 1) * L)
            o0, o1 = o[(j, 0)], o[(j, 1)]
            pv = jnp.where(low, o0[gr_, 0:KV_W], o1[gr_, 0:KV_W])
            den = jnp.where(low, o0[gr_, KV_W:2 * KV_W] + exs[(j, 0)][gr_, :],
                            o1[gr_, KV_W:2 * KV_W] + exs[(j, 1)][gr_, :])
            att_ref[rows[j], g * KV_W:(g + 1) * KV_W] = pv / den
    kp_ref[...] = ka_ref[(CHUNKS_PER_STEP - 1) * L:CHUNKS_PER_STEP * L, :]
    vp_ref[...] = va_ref[(CHUNKS_PER_STEP - 1) * L:CHUNKS_PER_STEP * L, :]

    @pl.when(ci == pl.num_programs(1) - 1)
    def _():
        c_out[0] = c_st[...]
        m_out[0] = m_st[0:1, 0:M_HEADS]


def _mixer_prompt(q, k, vm, os_, qa, ka, va, gc, gr, mlstm_norm, sinks, batch, seq):
    tr = CHUNKS_PER_STEP * CHUNK
    ns = seq // tr
    row = lambda w: pl.BlockSpec((tr, w), lambda b, c: (b * ns + c, 0))
    full = lambda a: pl.BlockSpec(a.shape, lambda b, c: (0,) * a.ndim)
    n = batch * seq
    bias_shape = (A_KV, A_GROUP * CHUNK, 2 * CHUNK)
    return pl.pallas_call(
        _mixer_prompt_kernel,
        grid=(batch, ns),
        in_specs=[row(QK_W // 2), row(QK_W // 2), row(M_WIDTH), row(M_WIDTH), row(A_WIDTH), row(KV_W), row(KV_W),
                  row(N_GATES), pl.BlockSpec((N_GATES, tr), lambda b, c: (0, b * ns + c)),
                  full(mlstm_norm), full(sinks)],
        out_specs=[row(M_WIDTH), row(A_WIDTH),
                   pl.BlockSpec((1, M_HEADS, M_DK, 2 * M_DV), lambda b, c: (b, 0, 0, 0)),
                   pl.BlockSpec((1, 1, M_HEADS), lambda b, c: (b, 0, 0))],
        out_shape=[jax.ShapeDtypeStruct((n, M_WIDTH), F32), jax.ShapeDtypeStruct((n, A_WIDTH), F32),
                   jax.ShapeDtypeStruct((batch, M_HEADS, M_DK, 2 * M_DV), F32),
                   jax.ShapeDtypeStruct((batch, 1, M_HEADS), F32)],
        scratch_shapes=[pltpu.VMEM((M_HEADS, M_DK, 2 * M_DV), F32),
                        pltpu.VMEM((8, 128), F32),
                        pltpu.VMEM((CHUNK, KV_W), F32), pltpu.VMEM((CHUNK, KV_W), F32),
                        pltpu.VMEM(bias_shape, F32), pltpu.VMEM(bias_shape, F32),
                        pltpu.VMEM((A_KV, A_GROUP * CHUNK, KV_W), F32)],
        compiler_params=pltpu.CompilerParams(dimension_semantics=("arbitrary", "arbitrary"),
                                             vmem_limit_bytes=VMEM_LIMIT_BYTES),
        name="mixer_prompt",
    )(q, k, vm, os_, qa, ka, va, gc, gr, mlstm_norm, sinks)


Z_W = QK_W + 2 * M_WIDTH + A_WIDTH + 2 * KV_W
Z_GROUPS = [(0, 512), (512, 1024), (1024, 1536), (1536, 2048), (2048, 2304)]


def _front_kernel(x_ref, g_ref, w_ref, wgt_ref, cw_ref, bgc_ref, mn_ref, sink_ref,
                  ym_ref, att_ref, c_out, m_out, ktail_ref, vtail_ref, ctail_ref,
                  z_ref, g_st, up_ref, c_st, m_st, kp_ref, vp_ref, bias0_ref, bias_ref, sinkrep_ref):
    ci = pl.program_id(1)
    last = pl.num_programs(1) - 1
    L = CHUNK
    J = range(CHUNKS_PER_STEP)
    H = range(M_HEADS)
    KV = range(A_KV)
    TR = CHUNKS_PER_STEP * L
    nxt = ci % 2
    cur = 1 - nxt
    t = lax.broadcasted_iota(jnp.int32, (L, L), 0)
    s = lax.broadcasted_iota(jnp.int32, (L, L), 1)

    @pl.when(ci == 0)
    def _():
        z_ref[cur] = jnp.zeros((TR, Z_W), F32)
        g_st[cur] = jnp.zeros((N_GATES, TR), F32)
        c_st[...] = jnp.zeros_like(c_st)
        m_st[...] = jnp.zeros_like(m_st)
        kp_ref[...] = jnp.zeros_like(kp_ref)
        vp_ref[...] = jnp.zeros_like(vp_ref)
        up_ref[0:8, :] = jnp.zeros((8, QK_W), F32)
        dist = (t - s).astype(F32)
        for kv in KV:
            for g in range(A_GROUP):
                hd = kv * A_GROUP + g
                rows = slice(g * L, (g + 1) * L)
                cur_b = jnp.where(s <= t, -SLOPES[hd] * dist, NEG_INF)
                bias_ref[kv, rows, 0:L] = jnp.where(s >= t, -SLOPES[hd] * (dist + float(WINDOW)), NEG_INF)
                bias_ref[kv, rows, L:2 * L] = cur_b
                bias0_ref[kv, rows, 0:L] = jnp.full((L, L), NEG_INF, F32)
                bias0_ref[kv, rows, L:2 * L] = cur_b
                sinkrep_ref[kv, rows, :] = jnp.broadcast_to(sink_ref[0:1, hd:hd + 1], (L, KV_W))

    @pl.when(ci == 1)
    def _():
        c_st[...] = jnp.zeros_like(c_st)
        m_st[...] = jnp.zeros_like(m_st)
        kp_ref[...] = jnp.zeros_like(kp_ref)
        vp_ref[...] = jnp.zeros_like(vp_ref)
        up_ref[0:8, :] = jnp.zeros((8, QK_W), F32)

    @pl.when(ci == 2)
    def _():
        for kv in KV:
            bias0_ref[kv, :, 0:L] = bias_ref[kv, :, 0:L]

    up_ref[8:8 + TR, :] = z_ref[cur, :, 0:QK_W]
    gcur = g_st[cur] + bgc_ref[...]

    xn = _rms(x_ref[...], g_ref[...]).astype(BF16)
    for lo, hi in Z_GROUPS:
        z_ref[nxt, :, lo:hi] = _dot(xn, w_ref[:, lo:hi])
    g_st[nxt] = _dot_nt(wgt_ref[...], xn)

    qk = _conv_silu(up_ref, 5, cw_ref[...], TR)
    up_ref[5:8, :] = up_ref[5 + TR:8 + TR, :]
    q_all = (qk[:, 0:QK_W // 2] * (M_DK ** -0.5)).astype(BF16)
    k_all_m = qk[:, QK_W // 2:QK_W]
    grow = lax.broadcasted_iota(jnp.int32, (N_GATES, L), 0)
    rows = [slice(j * L, (j + 1) * L) for j in J]
    gr, gc = [], []
    for j in J:
        gch = gcur[:, rows[j]]
        logs = jnp.where(grow >= M_HEADS, jax.nn.log_sigmoid(gch), gch)
        out = jnp.where(grow >= M_HEADS, _chunk_cumsum_lanes(logs), logs)
        gr.append(out)
        gc.append(out.T)

    P = [(j, h) for j in J for h in H]
    ig_c = [gc[j][:, 0:M_HEADS] for j in J]
    b_c = [gc[j][:, M_HEADS:N_GATES] for j in J]
    a_r = [gr[j][0:M_HEADS, :] - gr[j][M_HEADS:N_GATES, :] for j in J]
    causal_bias = jnp.where(s <= t, 0.0, NEG_INF)
    hcol = lax.broadcasted_iota(jnp.int32, (L, M_HEADS), 1)
    m_prev = [m_st[0:1, 0:M_HEADS]]
    w_k, decay = [], []
    for j in J:
        b_last = b_c[j][L - 1:L, :]
        log_w = b_last - b_c[j] + ig_c[j]
        m_new = jnp.maximum(b_last + m_prev[j], jnp.max(log_w, axis=0, keepdims=True))
        w_k.append(jnp.exp(log_w - m_new))
        decay.append(jnp.exp(b_last + m_prev[j] - m_new))
        m_prev.append(m_new)
    m_st[0:1, 0:M_HEADS] = m_prev[-1]

    ones_v = jnp.ones((L, M_DV), BF16)
    qb = {p: q_all[rows[p[0]], p[1] * M_DK:(p[1] + 1) * M_DK] for p in P}
    kf = {p: k_all_m[rows[p[0]], p[1] * M_DK:(p[1] + 1) * M_DK] for p in P}
    kb = {p: kf[p].astype(BF16) for p in P}
    vext = {p: jnp.concatenate([z_ref[cur, rows[p[0]], QK_W + p[1] * M_DV:QK_W + (p[1] + 1) * M_DV].astype(BF16),
                                ones_v], axis=1) for p in P}
    qk_h = {p: _dot_nt(qb[p], kb[p]) for p in P}
    kw = {p: (kf[p] * w_k[p[0]][:, p[1]:p[1] + 1]).astype(BF16) for p in P}
    dc = {p: _dot_tn(kw[p], vext[p]) for p in P}
    log_d = {p: b_c[p[0]][:, p[1]:p[1] + 1] + a_r[p[0]][p[1]:p[1] + 1, :] + causal_bias for p in P}
    m_t, w_inter, emt = [], [], []
    for j in J:
        m_intra = jnp.full((L, M_HEADS), NEG_INF, F32)
        for h in H:
            m_intra = jnp.where(hcol == h, jnp.max(log_d[(j, h)], axis=-1, keepdims=True), m_intra)
        log_inter = b_c[j] + m_prev[j]
        m_t.append(jnp.maximum(log_inter, m_intra))
        w_inter.append(jnp.exp(log_inter - m_t[j]))
        emt.append(jnp.exp(-m_t[j]))
    s_h = {p: (qk_h[p] * jnp.exp(log_d[p] - m_t[p[0]][:, p[1]:p[1] + 1])).astype(BF16) for p in P}
    c_cur = [c_st[h] for h in H]
    qc = {}
    for j in J:
        for h in H:
            qc[(j, h)] = _dot(qb[(j, h)], c_cur[h].astype(BF16))
        c_cur = [decay[j][:, h:h + 1] * c_cur[h] + dc[(j, h)] for h in H]
    for h in H:
        c_st[h] = c_cur[h]
    o_h = {p: _dot(s_h[p], vext[p]) for p in P}
    for j, h in P:
        wi = w_inter[j][:, h:h + 1]
        num = o_h[(j, h)][:, 0:M_DV] + wi * qc[(j, h)][:, 0:M_DV]
        den = o_h[(j, h)][:, M_DV:2 * M_DV] + wi * qc[(j, h)][:, M_DV:2 * M_DV]
        hh = num / jnp.maximum(jnp.abs(den), emt[j][:, h:h + 1])
        cols = slice(h * M_DV, (h + 1) * M_DV)
        o_sig = jax.nn.sigmoid(z_ref[cur, rows[j], QK_W + M_WIDTH + h * M_DV:QK_W + M_WIDTH + (h + 1) * M_DV])
        ym_ref[rows[j], cols] = _mlstm_out(hh, o_sig, mn_ref[:, cols])

    a0 = QK_W + 2 * M_WIDTH
    nk = (CHUNKS_PER_STEP + 1) * L
    low3 = lax.broadcasted_iota(jnp.int32, (nk, KV_W), 1) < A_HD
    low = lax.broadcasted_iota(jnp.int32, (L, KV_W), 1) < A_HD
    ka_cur = z_ref[cur, :, a0 + A_WIDTH:a0 + A_WIDTH + KV_W]
    va_cur = z_ref[cur, :, a0 + A_WIDTH + KV_W:a0 + A_WIDTH + 2 * KV_W]
    k_all = jnp.concatenate([kp_ref[...], ka_cur], axis=0)
    v_all = jnp.concatenate([vp_ref[...], va_cur], axis=0)
    ones_k = jnp.ones((nk, KV_W), BF16)
    kmask = [jnp.where(low3, k_all, 0.0).astype(BF16), jnp.where(low3, 0.0, k_all).astype(BF16)]
    vext_a = [jnp.concatenate([jnp.where(low3, v_all, 1.0).astype(BF16), ones_k], axis=1),
              jnp.concatenate([jnp.where(low3, 1.0, v_all).astype(BF16), ones_k], axis=1)]
    qst = [(jnp.concatenate([z_ref[cur, rows[j], a0 + g * KV_W:a0 + (g + 1) * KV_W] for g in range(A_GROUP)],
                            axis=0) * (A_HD ** -0.5)).astype(BF16) for j in J]
    Q = [(j, kv) for j in J for kv in KV]
    bias_of = lambda j, kv: (bias0_ref if j == 0 else bias_ref)[kv]
    sc = {p: _dot_nt(qst[p[0]], kmask[p[1]][p[0] * L:(p[0] + 2) * L]) + bias_of(*p) for p in Q}
    m_a = {p: jnp.maximum(jnp.max(jnp.maximum(sc[p][:, 0:L], sc[p][:, L:2 * L]), axis=-1, keepdims=True),
                          sinkrep_ref[p[1]]) for p in Q}
    e = {p: jnp.concatenate([jnp.exp(sc[p][:, 0:L] - m_a[p]), jnp.exp(sc[p][:, L:2 * L] - m_a[p])],
                            axis=1).astype(BF16) for p in Q}
    o = {p: _dot(e[p], vext_a[p[1]][p[0] * L:(p[0] + 2) * L]) for p in Q}
    exs = {p: jnp.exp(sinkrep_ref[p[1]] - m_a[p]) for p in Q}
    for j in J:
        for g in range(A_GROUP):
            gr_ = slice(g * L, (g + 1) * L)
            o0, o1 = o[(j, 0)], o[(j, 1)]
            pv = jnp.where(low, o0[gr_, 0:KV_W], o1[gr_, 0:KV_W])
            den = jnp.where(low, o0[gr_, KV_W:2 * KV_W] + exs[(j, 0)][gr_, :],
                            o1[gr_, KV_W:2 * KV_W] + exs[(j, 1)][gr_, :])
            att_ref[rows[j], g * KV_W:(g + 1) * KV_W] = pv / den
    kp_ref[...] = ka_cur[(CHUNKS_PER_STEP - 1) * L:CHUNKS_PER_STEP * L, :]
    vp_ref[...] = va_cur[(CHUNKS_PER_STEP - 1) * L:CHUNKS_PER_STEP * L, :]

    @pl.when(ci == last)
    def _():
        c_out[0] = c_st[...]
        m_out[0] = m_st[0:1, 0:M_HEADS]
        ktail_ref[0] = kp_ref[...]
        vtail_ref[0] = vp_ref[...]
        ctail_ref[0] = up_ref[0:8, :]


def _front(x2d, g_pre, w_main, w_gt, conv_w, bg_col, mlstm_norm, sinks, batch, seq):
    tr = CHUNKS_PER_STEP * CHUNK
    ns = seq // tr
    n = batch * seq
    full = lambda a: pl.BlockSpec(a.shape, lambda b, c: (0,) * a.ndim)
    resident = lambda a: pl.BlockSpec(a.shape, lambda b, c: (0,) * a.ndim, pipeline_mode=pl.Buffered(1))
    out_row = lambda w: pl.BlockSpec((tr, w), lambda b, c: (b * ns + jnp.maximum(c - 1, 0), 0))
    per_seq = lambda shape: pl.BlockSpec((1,) + shape, lambda b, c: (b,) + (0,) * len(shape))
    bias_shape = (A_KV, A_GROUP * CHUNK, 2 * CHUNK)
    return pl.pallas_call(
        _front_kernel,
        grid=(batch, ns + 1),
        in_specs=[pl.BlockSpec((tr, D_MODEL), lambda b, c: (b * ns + jnp.minimum(c, ns - 1), 0)),
                  full(g_pre), resident(w_main), full(w_gt), full(conv_w), full(bg_col),
                  full(mlstm_norm), full(sinks)],
        out_specs=[out_row(M_WIDTH), out_row(A_WIDTH),
                   per_seq((M_HEADS, M_DK, 2 * M_DV)), per_seq((1, M_HEADS)),
                   per_seq((CHUNK, KV_W)), per_seq((CHUNK, KV_W)), per_seq((8, QK_W))],
        out_shape=[jax.ShapeDtypeStruct((n, M_WIDTH), F32), jax.ShapeDtypeStruct((n, A_WIDTH), F32),
                   jax.ShapeDtypeStruct((batch, M_HEADS, M_DK, 2 * M_DV), F32),
                   jax.ShapeDtypeStruct((batch, 1, M_HEADS), F32),
                   jax.ShapeDtypeStruct((batch, CHUNK, KV_W), F32), jax.ShapeDtypeStruct((batch, CHUNK, KV_W), F32),
                   jax.ShapeDtypeStruct((batch, 8, QK_W), F32)],
        scratch_shapes=[pltpu.VMEM((2, tr, Z_W), F32), pltpu.VMEM((2, N_GATES, tr), F32),
                        pltpu.VMEM((8 + tr, QK_W), F32),
                        pltpu.VMEM((M_HEADS, M_DK, 2 * M_DV), F32), pltpu.VMEM((8, 128), F32),
                        pltpu.VMEM((CHUNK, KV_W), F32), pltpu.VMEM((CHUNK, KV_W), F32),
                        pltpu.VMEM(bias_shape, F32), pltpu.VMEM(bias_shape, F32),
                        pltpu.VMEM((A_KV, A_GROUP * CHUNK, KV_W), F32)],
        compiler_params=pltpu.CompilerParams(dimension_semantics=("arbitrary", "arbitrary"),
                                             vmem_limit_bytes=VMEM_LIMIT_BYTES),
        name="front",
    )(x2d, g_pre, w_main, w_gt, conv_w, bg_col, mlstm_norm, sinks)


def _layer_kernel(x_ref, xq_ref, p_ref, g_ref, w_ref, wgt_ref, cw_ref, bgc_ref, mn_ref, sink_ref,
                  an_ref, g_post_ref, g_fpre_ref, g_fpost_ref, wout_ref, wup_ref, wdown_ref, wpg_ref, wpp_ref,
                  y_ref, c_out, m_out, ktail_ref, vtail_ref, ctail_ref,
                  z_ref, g_st, y_st, up_ref, c_st, m_st, kp_ref, vp_ref, bias0_ref, bias_ref, sinkrep_ref):
    ci = pl.program_id(1)
    n_tiles = pl.num_programs(1) - 2
    L = CHUNK
    J = range(CHUNKS_PER_STEP)
    H = range(M_HEADS)
    KV = range(A_KV)
    TR = CHUNKS_PER_STEP * L
    nxt = ci % 2
    cur = 1 - nxt
    t = lax.broadcasted_iota(jnp.int32, (L, L), 0)
    s = lax.broadcasted_iota(jnp.int32, (L, L), 1)

    def reset_state():
        c_st[...] = jnp.zeros_like(c_st)
        m_st[...] = jnp.zeros_like(m_st)
        kp_ref[...] = jnp.zeros_like(kp_ref)
        vp_ref[...] = jnp.zeros_like(vp_ref)
        up_ref[0:8, :] = jnp.zeros((8, QK_W), F32)

    @pl.when(ci == 0)
    def _():
        z_ref[cur] = jnp.zeros((TR, Z_W), F32)
        g_st[cur] = jnp.zeros((N_GATES, TR), F32)
        y_st[cur] = jnp.zeros((TR, D_MODEL), BF16)
        reset_state()
        dist = (t - s).astype(F32)
        for kv in KV:
            for g in range(A_GROUP):
                hd = kv * A_GROUP + g
                rows = slice(g * L, (g + 1) * L)
                cur_b = jnp.where(s <= t, -SLOPES[hd] * dist, NEG_INF)
                bias_ref[kv, rows, 0:L] = jnp.where(s >= t, -SLOPES[hd] * (dist + float(WINDOW)), NEG_INF)
                bias_ref[kv, rows, L:2 * L] = cur_b
                bias0_ref[kv, rows, 0:L] = jnp.full((L, L), NEG_INF, F32)
                bias0_ref[kv, rows, L:2 * L] = cur_b
                sinkrep_ref[kv, rows, :] = jnp.broadcast_to(sink_ref[0:1, hd:hd + 1], (L, KV_W))

    @pl.when(ci == 1)
    def _():
        reset_state()

    @pl.when(ci == 2)
    def _():
        for kv in KV:
            bias0_ref[kv, :, 0:L] = bias_ref[kv, :, 0:L]

    up_ref[8:8 + TR, :] = z_ref[cur, :, 0:QK_W]
    gcur = g_st[cur] + bgc_ref[...]

    xn = _rms(x_ref[...], g_ref[...]).astype(BF16)
    for lo, hi in Z_GROUPS:
        z_ref[nxt, :, lo:hi] = _dot(xn, w_ref[:, lo:hi])
    g_st[nxt] = _dot_nt(wgt_ref[...], xn)

    x1 = xq_ref[...] + _rms(_dot(y_st[cur], wout_ref[...]), g_post_ref[...])
    u = _rms(x1, g_fpre_ref[...]).astype(BF16)

    def ffn_tile(j):
        cols = slice(j * FF_TILE, (j + 1) * FF_TILE)
        hid = jnp.square(jnp.maximum(_dot(u, wup_ref[:, cols]), 0.0)).astype(BF16)
        return _dot(hid, wdown_ref[cols, :])

    qk = _conv_silu(up_ref, 5, cw_ref[...], TR)
    up_ref[5:8, :] = up_ref[5 + TR:8 + TR, :]
    q_all = (qk[:, 0:QK_W // 2] * (M_DK ** -0.5)).astype(BF16)
    k_all_m = qk[:, QK_W // 2:QK_W]
    grow = lax.broadcasted_iota(jnp.int32, (N_GATES, L), 0)
    rows = [slice(j * L, (j + 1) * L) for j in J]
    gr, gc = [], []
    for j in J:
        gch = gcur[:, rows[j]]
        logs = jnp.where(grow >= M_HEADS, jax.nn.log_sigmoid(gch), gch)
        out = jnp.where(grow >= M_HEADS, _chunk_cumsum_lanes(logs), logs)
        gr.append(out)
        gc.append(out.T)
    P = [(j, h) for j in J for h in H]
    ig_c = [gc[j][:, 0:M_HEADS] for j in J]
    b_c = [gc[j][:, M_HEADS:N_GATES] for j in J]
    a_r = [gr[j][0:M_HEADS, :] - gr[j][M_HEADS:N_GATES, :] for j in J]
    causal_bias = jnp.where(s <= t, 0.0, NEG_INF)
    hcol = lax.broadcasted_iota(jnp.int32, (L, M_HEADS), 1)
    m_prev = [m_st[0:1, 0:M_HEADS]]
    w_k, decay = [], []
    for j in J:
        b_last = b_c[j][L - 1:L, :]
        log_w = b_last - b_c[j] + ig_c[j]
        m_new = jnp.maximum(b_last + m_prev[j], jnp.max(log_w, axis=0, keepdims=True))
        w_k.append(jnp.exp(log_w - m_new))
        decay.append(jnp.exp(b_last + m_prev[j] - m_new))
        m_prev.append(m_new)
    m_st[0:1, 0:M_HEADS] = m_prev[-1]

    ones_v = jnp.ones((L, M_DV), BF16)
    qb = {p: q_all[rows[p[0]], p[1] * M_DK:(p[1] + 1) * M_DK] for p in P}
    kf = {p: k_all_m[rows[p[0]], p[1] * M_DK:(p[1] + 1) * M_DK] for p in P}
    kb = {p: kf[p].astype(BF16) for p in P}
    vext = {p: jnp.concatenate([z_ref[cur, rows[p[0]], QK_W + p[1] * M_DV:QK_W + (p[1] + 1) * M_DV].astype(BF16),
                                ones_v], axis=1) for p in P}
    qk_h = {p: _dot_nt(qb[p], kb[p]) for p in P}
    kw = {p: (kf[p] * w_k[p[0]][:, p[1]:p[1] + 1]).astype(BF16) for p in P}
    dc = {p: _dot_tn(kw[p], vext[p]) for p in P}

    f = ffn_tile(0)

    log_d = {p: b_c[p[0]][:, p[1]:p[1] + 1] + a_r[p[0]][p[1]:p[1] + 1, :] + causal_bias for p in P}
    m_t, w_inter, emt = [], [], []
    for j in J:
        m_intra = jnp.full((L, M_HEADS), NEG_INF, F32)
        for h in H:
            m_intra = jnp.where(hcol == h, jnp.max(log_d[(j, h)], axis=-1, keepdims=True), m_intra)
        log_inter = b_c[j] + m_prev[j]
        m_t.append(jnp.maximum(log_inter, m_intra))
        w_inter.append(jnp.exp(log_inter - m_t[j]))
        emt.append(jnp.exp(-m_t[j]))
    s_h = {p: (qk_h[p] * jnp.exp(log_d[p] - m_t[p[0]][:, p[1]:p[1] + 1])).astype(BF16) for p in P}
    c_cur = [c_st[h] for h in H]
    qc = {}
    for j in J:
        for h in H:
            qc[(j, h)] = _dot(qb[(j, h)], c_cur[h].astype(BF16))
        c_cur = [decay[j][:, h:h + 1] * c_cur[h] + dc[(j, h)] for h in H]
    for h in H:
        c_st[h] = c_cur[h]

    f = f + ffn_tile(1)

    o_h = {p: _dot(s_h[p], vext[p]) for p in P}
    for j, h in P:
        wi = w_inter[j][:, h:h + 1]
        num = o_h[(j, h)][:, 0:M_DV] + wi * qc[(j, h)][:, 0:M_DV]
        den = o_h[(j, h)][:, M_DV:2 * M_DV] + wi * qc[(j, h)][:, M_DV:2 * M_DV]
        hh = num / jnp.maximum(jnp.abs(den), emt[j][:, h:h + 1])
        cols = slice(h * M_DV, (h + 1) * M_DV)
        o_sig = jax.nn.sigmoid(z_ref[cur, rows[j], QK_W + M_WIDTH + h * M_DV:QK_W + M_WIDTH + (h + 1) * M_DV])
        y_st[nxt, rows[j], cols] = _mlstm_out(hh, o_sig, mn_ref[:, cols]).astype(BF16)

    a0 = QK_W + 2 * M_WIDTH
    nk = (CHUNKS_PER_STEP + 1) * L
    low3 = lax.broadcasted_iota(jnp.int32, (nk, KV_W), 1) < A_HD
    low = lax.broadcasted_iota(jnp.int32, (L, KV_W), 1) < A_HD
    ka_cur = z_ref[cur, :, a0 + A_WIDTH:a0 + A_WIDTH + KV_W]
    va_cur = z_ref[cur, :, a0 + A_WIDTH + KV_W:a0 + A_WIDTH + 2 * KV_W]
    k_all = jnp.concatenate([kp_ref[...], ka_cur], axis=0)
    v_all = jnp.concatenate([vp_ref[...], va_cur], axis=0)
    ones_k = jnp.ones((nk, KV_W), BF16)
    kmask = [jnp.where(low3, k_all, 0.0).astype(BF16), jnp.where(low3, 0.0, k_all).astype(BF16)]
    vext_a = [jnp.concatenate([jnp.where(low3, v_all, 1.0).astype(BF16), ones_k], axis=1),
              jnp.concatenate([jnp.where(low3, 1.0, v_all).astype(BF16), ones_k], axis=1)]
    qst = [(jnp.concatenate([z_ref[cur, rows[j], a0 + g * KV_W:a0 + (g + 1) * KV_W] for g in range(A_GROUP)],
                            axis=0) * (A_HD ** -0.5)).astype(BF16) for j in J]
    Q = [(j, kv) for j in J for kv in KV]
    bias_of = lambda j, kv: (bias0_ref if j == 0 else bias_ref)[kv]
    sc = {p: _dot_nt(qst[p[0]], kmask[p[1]][p[0] * L:(p[0] + 2) * L]) + bias_of(*p) for p in Q}

    f = f + ffn_tile(2)

    m_a = {p: jnp.maximum(jnp.max(jnp.maximum(sc[p][:, 0:L], sc[p][:, L:2 * L]), axis=-1, keepdims=True),
                          sinkrep_ref[p[1]]) for p in Q}
    e = {p: jnp.concatenate([jnp.exp(sc[p][:, 0:L] - m_a[p]), jnp.exp(sc[p][:, L:2 * L] - m_a[p])],
                            axis=1).astype(BF16) for p in Q}
    o = {p: _dot(e[p], vext_a[p[1]][p[0] * L:(p[0] + 2) * L]) for p in Q}

    f = f + ffn_tile(3)

    exs = {p: jnp.exp(sinkrep_ref[p[1]] - m_a[p]) for p in Q}
    for j in J:
        att = []
        for g in range(A_GROUP):
            gr_ = slice(g * L, (g + 1) * L)
            o0, o1 = o[(j, 0)], o[(j, 1)]
            pv = jnp.where(low, o0[gr_, 0:KV_W], o1[gr_, 0:KV_W])
            den = jnp.where(low, o0[gr_, KV_W:2 * KV_W] + exs[(j, 0)][gr_, :],
                            o1[gr_, KV_W:2 * KV_W] + exs[(j, 1)][gr_, :])
            att.append(pv / den)
        ssq = att[0] * att[0]
        for g in range(1, A_GROUP):
            ssq = ssq + att[g] * att[g]
        scale = lax.rsqrt(jnp.sum(ssq, axis=-1, keepdims=True) * (1.0 / A_WIDTH) + EPS)
        for g in range(A_GROUP):
            cols = slice(g * KV_W, (g + 1) * KV_W)
            y_st[nxt, rows[j], M_WIDTH + g * KV_W:M_WIDTH + (g + 1) * KV_W] = (att[g] * scale * an_ref[:, cols]).astype(BF16)
    kp_ref[...] = ka_cur[(CHUNKS_PER_STEP - 1) * L:CHUNKS_PER_STEP * L, :]
    vp_ref[...] = va_cur[(CHUNKS_PER_STEP - 1) * L:CHUNKS_PER_STEP * L, :]

    x2 = x1 + _rms(f, g_fpost_ref[...])
    gate = jax.nn.sigmoid(_dot(x2.astype(BF16), wpg_ref[...]))
    y_ref[...] = x2 + gate * _dot(p_ref[...].astype(BF16), wpp_ref[...])

    @pl.when(ci == n_tiles)
    def _():
        c_out[0] = c_st[...]
        m_out[0] = m_st[0:1, 0:M_HEADS]
        ktail_ref[0] = kp_ref[...]
        vtail_ref[0] = vp_ref[...]
        ctail_ref[0] = up_ref[0:8, :]


def _layer(x2d, p2d, g_pre, w_main, w_gt, conv_w, bg_col, mlstm_norm, sinks, gains, weights, batch, seq):
    tr = CHUNKS_PER_STEP * CHUNK
    ns = seq // tr
    n = batch * seq
    full = lambda a: pl.BlockSpec(a.shape, lambda b, c: (0,) * a.ndim)
    resident = lambda a: pl.BlockSpec(a.shape, lambda b, c: (0,) * a.ndim, pipeline_mode=pl.Buffered(1))
    lag2 = lambda w: pl.BlockSpec((tr, w), lambda b, c: (b * ns + jnp.clip(c - 2, 0, ns - 1), 0))
    per_seq = lambda shape: pl.BlockSpec((1,) + shape, lambda b, c: (b,) + (0,) * len(shape))
    bias_shape = (A_KV, A_GROUP * CHUNK, 2 * CHUNK)
    return pl.pallas_call(
        _layer_kernel,
        grid=(batch, ns + 2),
        in_specs=[pl.BlockSpec((tr, D_MODEL), lambda b, c: (b * ns + jnp.minimum(c, ns - 1), 0)),
                  lag2(D_MODEL), lag2(P_DIM),
                  full(g_pre), resident(w_main), full(w_gt), full(conv_w), full(bg_col),
                  full(mlstm_norm), full(sinks)] + [full(g) for g in gains] + [resident(w) for w in weights],
        out_specs=[lag2(D_MODEL),
                   per_seq((M_HEADS, M_DK, 2 * M_DV)), per_seq((1, M_HEADS)),
                   per_seq((CHUNK, KV_W)), per_seq((CHUNK, KV_W)), per_seq((8, QK_W))],
        out_shape=[jax.ShapeDtypeStruct((n, D_MODEL), F32),
                   jax.ShapeDtypeStruct((batch, M_HEADS, M_DK, 2 * M_DV), F32),
                   jax.ShapeDtypeStruct((batch, 1, M_HEADS), F32),
                   jax.ShapeDtypeStruct((batch, CHUNK, KV_W), F32), jax.ShapeDtypeStruct((batch, CHUNK, KV_W), F32),
                   jax.ShapeDtypeStruct((batch, 8, QK_W), F32)],
        scratch_shapes=[pltpu.VMEM((2, tr, Z_W), F32), pltpu.VMEM((2, N_GATES, tr), F32),
                        pltpu.VMEM((2, tr, D_MODEL), BF16),
                        pltpu.VMEM((8 + tr, QK_W), F32),
                        pltpu.VMEM((M_HEADS, M_DK, 2 * M_DV), F32), pltpu.VMEM((8, 128), F32),
                        pltpu.VMEM((CHUNK, KV_W), F32), pltpu.VMEM((CHUNK, KV_W), F32),
                        pltpu.VMEM(bias_shape, F32), pltpu.VMEM(bias_shape, F32),
                        pltpu.VMEM((A_KV, A_GROUP * CHUNK, KV_W), F32)],
        compiler_params=pltpu.CompilerParams(dimension_semantics=("arbitrary", "arbitrary"),
                                             vmem_limit_bytes=VMEM_LIMIT_BYTES),
        name="layer",
    )(x2d, x2d, p2d, g_pre, w_main, w_gt, conv_w, bg_col, mlstm_norm, sinks, *gains, *weights)


def _in_proj_sample_kernel(x_ref, g_ref, wn_ref, wt_ref, wgt_ref,
                           qkn_ref, qa_ref, ka_ref, va_ref, qkt_ref, vt_ref, ot_ref, gt_ref):
    xn = _rms(x_ref[...], g_ref[...]).astype(BF16)
    zn = _dot(xn, wn_ref[...])
    qkn_ref[...] = zn[:, 0:QK_W]
    qa_ref[...] = zn[:, QK_W:QK_W + A_WIDTH]
    ka_ref[...] = zn[:, QK_W + A_WIDTH:QK_W + A_WIDTH + KV_W]
    va_ref[...] = zn[:, QK_W + A_WIDTH + KV_W:QK_W + A_WIDTH + 2 * KV_W]
    zt = _dot_nt(wt_ref[...], xn)
    qkt_ref[...] = zt[0:QK_W, :]
    vt_ref[...] = zt[QK_W:QK_W + M_WIDTH, :]
    ot_ref[...] = zt[QK_W + M_WIDTH:QK_W + 2 * M_WIDTH, :]
    gt_ref[...] = _dot_nt(wgt_ref[...], xn)


def _in_proj_sample(x2d, g_pre, w_nat, w_t, w_gt):
    n = x2d.shape[0]
    shapes = [(n, QK_W), (n, A_WIDTH), (n, KV_W), (n, KV_W), (QK_W, n), (M_WIDTH, n), (M_WIDTH, n), (N_GATES, n)]
    return pl.pallas_call(
        _in_proj_sample_kernel,
        out_shape=[jax.ShapeDtypeStruct(s, F32) for s in shapes],
        compiler_params=pltpu.CompilerParams(vmem_limit_bytes=VMEM_LIMIT_BYTES),
        name="in_proj_sample",
    )(x2d, g_pre, w_nat, w_t, w_gt)


def _mlstm_sample_kernel(qt_ref, kt_ref, vt_ref, ot_ref, gt_ref, bg_ref, cbq_ref, cbk_ref, cwq_ref, cwk_ref,
                         gain_ref, c_ref, n_ref, m_ref,
                         ym_ref, c_out, n_out, m_out,
                         ct_ref, q_s, ik_s, num_s):
    h = pl.program_id(0)
    NB = c_ref.shape[0]
    L = qt_ref.shape[1] // NB
    T = range(L)

    def conv(raw_ref, cb_ref, cw_ref):
        ups = [cb_ref[j] for j in range(CONV_W - 1)] + [raw_ref[:, t * NB:(t + 1) * NB] for t in T]
        w = [cw_ref[:, j:j + 1] for j in range(CONV_W)]
        outs = []
        for t in T:
            acc = w[0] * ups[t]
            for j in range(1, CONV_W):
                acc = acc + w[j] * ups[t + j]
            outs.append(jax.nn.silu(acc))
        return outs

    q = [x * (M_DK ** -0.5) for x in conv(qt_ref, cbq_ref, cwq_ref)]
    k = conv(kt_ref, cbk_ref, cwk_ref)

    ig_all = gt_ref[pl.ds(h, 1), :] + bg_ref[pl.ds(h, 1), :]
    lf_all = jax.nn.log_sigmoid(gt_ref[pl.ds(h + M_HEADS, 1), :] + bg_ref[pl.ds(h + M_HEADS, 1), :])
    m = m_ref[pl.ds(h, 1), :]
    f, ms = [], []
    n = n_ref[0]
    den = []
    for t in T:
        ig, lf = ig_all[:, t * NB:(t + 1) * NB], lf_all[:, t * NB:(t + 1) * NB]
        m_new = jnp.maximum(lf + m, ig)
        f_t = jnp.exp(lf + m - m_new)
        ik = jnp.exp(ig - m_new) * k[t]
        m = m_new
        n = f_t * n + ik
        f.append(f_t)
        ms.append(m_new)
        den.append(jnp.sum(q[t] * n, axis=0, keepdims=True))
        ik_s[t] = ik
        q_s[t] = q[t]
    m_out[0] = m
    n_out[0] = n

    for d in range(M_DK):
        ct_ref[d * M_DV:(d + 1) * M_DV, :] = c_ref[:, d * M_DV:(d + 1) * M_DV].T

    VT = SAMPLE_V_TILE
    for vq in range(M_DV // VT):
        vts = [vt_ref[vq * VT:(vq + 1) * VT, t * NB:(t + 1) * NB] for t in T]

        def step(d, accs, vq=vq, vts=vts):
            r0 = pl.multiple_of(d * M_DV + vq * VT, VT)
            c = ct_ref[pl.ds(r0, VT), :]
            new = []
            for t in T:
                c = f[t] * c + ik_s[t, pl.ds(d, 1), :] * vts[t]
                new.append(accs[t] + q_s[t, pl.ds(d, 1), :] * c)
            ct_ref[pl.ds(r0, VT), :] = c
            return tuple(new)

        accs = lax.fori_loop(0, M_DK, step, tuple(jnp.zeros((VT, NB), F32) for _ in T), unroll=8)
        for t in T:
            num_s[t, vq * VT:(vq + 1) * VT, :] = accs[t]

    for t in T:
        hh = num_s[t] / jnp.maximum(jnp.abs(den[t]), jnp.exp(-ms[t]))
        hn = hh * lax.rsqrt(jnp.mean(hh * hh, axis=0, keepdims=True) + EPS) * gain_ref[...]
        y = jax.nn.sigmoid(ot_ref[:, t * NB:(t + 1) * NB]) * hn
        ym_ref[t * NB:(t + 1) * NB, :] = y.T

    for d in range(M_DK):
        c_out[:, d * M_DV:(d + 1) * M_DV] = ct_ref[d * M_DV:(d + 1) * M_DV, :].T


def _mlstm_sample(qkt, vt, ot, gt, bg_col, cbt, cwt, gain_col, c2d, nt, mt):
    nb = c2d.shape[0]
    n = qkt.shape[1]
    L = n // nb
    hblk = lambda rows, off: pl.BlockSpec((rows, n), lambda h, off=off: (h + off, 0))
    full = lambda a: pl.BlockSpec(a.shape, lambda h: (0,) * a.ndim)
    kq = QK_W // 2 // M_DK
    return pl.pallas_call(
        _mlstm_sample_kernel,
        grid=(M_HEADS,),
        in_specs=[hblk(M_DK, 0), hblk(M_DK, kq), hblk(M_DV, 0), hblk(M_DV, 0), full(gt), full(bg_col),
                  pl.BlockSpec((CONV_W - 1, M_DK, nb), lambda h: (0, h, 0)),
                  pl.BlockSpec((CONV_W - 1, M_DK, nb), lambda h: (0, h + kq, 0)),
                  pl.BlockSpec((M_DK, CONV_W), lambda h: (h, 0)),
                  pl.BlockSpec((M_DK, CONV_W), lambda h: (h + kq, 0)),
                  pl.BlockSpec((M_DV, 1), lambda h: (h, 0)),
                  pl.BlockSpec((nb, M_DK * M_DV), lambda h: (0, h)),
                  pl.BlockSpec((1, M_DK, nb), lambda h: (h, 0, 0)),
                  full(mt)],
        out_specs=[pl.BlockSpec((n, M_DV), lambda h: (0, h)),
                   pl.BlockSpec((nb, M_DK * M_DV), lambda h: (0, h)),
                   pl.BlockSpec((1, M_DK, nb), lambda h: (h, 0, 0)),
                   pl.BlockSpec((1, 1, nb), lambda h: (h, 0, 0))],
        out_shape=[jax.ShapeDtypeStruct((n, M_WIDTH), F32), jax.ShapeDtypeStruct(c2d.shape, F32),
                   jax.ShapeDtypeStruct(nt.shape, F32), jax.ShapeDtypeStruct((M_HEADS, 1, nb), F32)],
        scratch_shapes=[pltpu.VMEM((M_DK * M_DV, nb), F32), pltpu.VMEM((L, M_DK, nb), F32),
                        pltpu.VMEM((L, M_DK, nb), F32), pltpu.VMEM((L, M_DV, nb), F32)],
        compiler_params=pltpu.CompilerParams(dimension_semantics=("arbitrary",),
                                             vmem_limit_bytes=VMEM_LIMIT_BYTES),
        name="mlstm_sample",
    )(qkt, qkt, vt, ot, gt, bg_col, cbt, cbt, cwt, cwt, gain_col, c2d, nt, mt)


def _attn_sample_kernel(qm_ref, kn_ref, vn_ref, kct_ref, vct_ref, sinkr_ref, sloper_ref,
                        att_ref, kct_out, vct_out, xk_ref, xv_ref):
    NB, L = kn_ref.shape[0], kn_ref.shape[1]
    R = A_KV * A_GROUP * L

    def bias(nkeys, offset):
        r = lax.broadcasted_iota(jnp.int32, (R, nkeys), 0).astype(F32)
        pos = lax.broadcasted_iota(jnp.int32, (R, nkeys), 1).astype(F32)
        tq = r - L * jnp.floor((r + 0.5) / L)
        dist = tq + offset - pos
        return jnp.where((dist >= 0.0) & (dist <= float(WINDOW)), -sloper_ref[...] * dist, NEG_INF)

    @pl.when(pl.program_id(0) == 0)
    def _():
        xk_ref[...] = jnp.zeros_like(xk_ref)
        xv_ref[...] = jnp.zeros_like(xv_ref)

    bias_c = bias(WINDOW, float(WINDOW))
    bias_n = bias(L, 0.0)
    sink = sinkr_ref[...]
    rq = lax.broadcasted_iota(jnp.int32, (R, KV_W), 0)
    cq = lax.broadcasted_iota(jnp.int32, (R, KV_W), 1)
    same_kv = (rq < A_GROUP * L) == (cq < A_HD)
    newest = lax.broadcasted_iota(jnp.int32, (KV_W, WINDOW), 1) >= WINDOW - L
    SEQ = range(NB)
    qh = [jnp.where(same_kv, qm_ref[b] * (A_HD ** -0.5), 0.0).astype(BF16) for b in SEQ]
    kt = [kct_ref[b] for b in SEQ]
    vt = [vct_ref[b] for b in SEQ]
    knb = [kn_ref[b].astype(BF16) for b in SEQ]
    vnb = [vn_ref[b].astype(BF16) for b in SEQ]
    sc_c = [_dot(qh[b], kt[b].astype(BF16)) + bias_c for b in SEQ]
    sc_n = [_dot_nt(qh[b], knb[b]) + bias_n for b in SEQ]
    m_a = [jnp.maximum(sink, jnp.maximum(jnp.max(sc_c[b], axis=-1, keepdims=True),
                                         jnp.max(sc_n[b], axis=-1, keepdims=True))) for b in SEQ]
    e_c = [jnp.exp(sc_c[b] - m_a[b]) for b in SEQ]
    e_n = [jnp.exp(sc_n[b] - m_a[b]) for b in SEQ]
    den_a = [jnp.exp(sink - m_a[b]) + jnp.sum(e_c[b], axis=-1, keepdims=True)
             + jnp.sum(e_n[b], axis=-1, keepdims=True) for b in SEQ]
    pv = [_dot_nt((e_c[b] / den_a[b]).astype(BF16), vt[b].astype(BF16))
          + _dot((e_n[b] / den_a[b]).astype(BF16), vnb[b]) for b in SEQ]
    for b in SEQ:
        att_ref[b] = pv[b]

    for b in SEQ:
        xk_ref[b, WINDOW - L:WINDOW, :] = kn_ref[b]
        xv_ref[b, WINDOW - L:WINDOW, :] = vn_ref[b]
    for b in SEQ:
        kct_out[b] = jnp.where(newest, xk_ref[b].T, pltpu.roll(kt[b], WINDOW - L, axis=1))
        vct_out[b] = jnp.where(newest, xv_ref[b].T, pltpu.roll(vt[b], WINDOW - L, axis=1))


def _attn_sample(qm, kn, vn, kct, vct, sink_rows, slope_rows):
    nb = qm.shape[0]
    ts = min(SAMPLE_SEQ_TILE, nb)
    per_b = lambda a: pl.BlockSpec((ts,) + a.shape[1:], lambda b: (b,) + (0,) * (a.ndim - 1))
    full = lambda a: pl.BlockSpec(a.shape, lambda b: (0,) * a.ndim)
    outs = [jax.ShapeDtypeStruct(a.shape, F32) for a in (qm, kct, vct)]
    return pl.pallas_call(
        _attn_sample_kernel,
        grid=(nb // ts,),
        in_specs=[per_b(a) for a in (qm, kn, vn, kct, vct)] + [full(sink_rows), full(slope_rows)],
        out_specs=[per_b(o) for o in outs],
        out_shape=outs,
        scratch_shapes=[pltpu.VMEM((ts, WINDOW, KV_W), F32), pltpu.VMEM((ts, WINDOW, KV_W), F32)],
        compiler_params=pltpu.CompilerParams(dimension_semantics=("arbitrary",),
                                             vmem_limit_bytes=VMEM_LIMIT_BYTES),
        name="attn_sample",
    )(qm, kn, vn, kct, vct, sink_rows, slope_rows)


def _post_kernel(x_ref, ym_ref, att_ref, p_ref, an_ref, g_post_ref, g_fpre_ref, g_fpost_ref,
                 wout_ref, wup_ref, wdown_ref, wpg_ref, wpp_ref, o_ref):
    y_a = _rms(att_ref[...], an_ref[...])
    y = jnp.concatenate([ym_ref[...], y_a], axis=-1).astype(BF16)
    x1 = x_ref[...] + _rms(_dot(y, wout_ref[...]), g_post_ref[...])
    u = _rms(x1, g_fpre_ref[...]).astype(BF16)
    f = jnp.zeros_like(x1)
    for j in range(D_FF // FF_TILE):
        cols = slice(j * FF_TILE, (j + 1) * FF_TILE)
        hid = jnp.square(jnp.maximum(_dot(u, wup_ref[:, cols]), 0.0)).astype(BF16)
        f = f + _dot(hid, wdown_ref[cols, :])
    x2 = x1 + _rms(f, g_fpost_ref[...])
    gate = jax.nn.sigmoid(_dot(x2.astype(BF16), wpg_ref[...]))
    o_ref[...] = x2 + gate * _dot(p_ref[...].astype(BF16), wpp_ref[...])


def _post(x2d, ym, att, p2d, gains, weights):
    n = x2d.shape[0]
    tm = min(ROW_TILE, n)
    row = lambda w: pl.BlockSpec((tm, w), lambda i: (i, 0))
    full = lambda a: pl.BlockSpec(a.shape, lambda i: (0,) * a.ndim)
    resident = lambda a: pl.BlockSpec(a.shape, lambda i: (0,) * a.ndim, pipeline_mode=pl.Buffered(1))
    return pl.pallas_call(
        _post_kernel,
        grid=(n // tm,),
        in_specs=[row(D_MODEL), row(M_WIDTH), row(A_WIDTH), row(P_DIM)]
        + [full(g) for g in gains] + [resident(w) for w in weights],
        out_specs=row(D_MODEL),
        out_shape=jax.ShapeDtypeStruct((n, D_MODEL), F32),
        compiler_params=pltpu.CompilerParams(dimension_semantics=("arbitrary",),
                                             vmem_limit_bytes=VMEM_LIMIT_BYTES),
        name="post",
    )(x2d, ym, att, p2d, *gains, *weights)


def _permute_heads(a, axis):
    shape = a.shape
    a = a.reshape(shape[:axis] + (A_HEADS, A_HD) + shape[axis + 1:])
    a = jnp.take(a, np.asarray(HEAD_ORDER), axis=axis)
    return a.reshape(shape)


def kernel(x_prompt, x_sample, p_prompt, p_sample, state_mlstm_c, state_mlstm_n, state_mlstm_m,
           state_mlstm_conv, cache_swa_k, cache_swa_v, norm_mix_pre, w_in, b_gates, conv_w,
           mlstm_norm, attn_sinks, attn_norm, w_out, norm_mix_post, norm_ffn_pre, w_up, w_down,
           norm_ffn_post, w_pgate, w_pproj):
    depth = w_in.shape[0]
    assert depth == 1, "single-layer decoder"
    B, T, _ = x_prompt.shape
    SB, ST, _ = x_sample.shape
    assert T % (CHUNKS_PER_STEP * CHUNK) == 0 and T % ROW_TILE == 0 and SB % SAMPLE_SEQ_TILE == 0
    i = 0

    wi = w_in[i]
    o_qk, o_vm, o_om, o_g, o_qa, o_ka, o_va = np.cumsum(
        [0, QK_W, M_WIDTH, M_WIDTH, N_GATES, A_WIDTH, KV_W])
    w_main = jnp.concatenate([wi[:, o_qk:o_g], _permute_heads(wi[:, o_qa:o_ka], 1), wi[:, o_ka:]],
                             axis=1).astype(BF16)
    w_gt = wi[:, o_g:o_qa].astype(BF16).T
    g_pre = norm_mix_pre[i].reshape(1, D_MODEL)
    bg_col = b_gates[i].reshape(N_GATES, 1)
    cw = conv_w[i]
    mn = mlstm_norm[i].reshape(1, M_WIDTH)
    sinks = attn_sinks[i].reshape(1, A_HEADS)
    gains = [_permute_heads(attn_norm[i], 0).reshape(1, A_WIDTH), norm_mix_post[i].reshape(1, D_MODEL),
             norm_ffn_pre[i].reshape(1, D_MODEL), norm_ffn_post[i].reshape(1, D_MODEL)]
    wo = jnp.concatenate([w_out[i][:M_WIDTH], _permute_heads(w_out[i][M_WIDTH:], 0)], axis=0)
    weights = [wo.astype(BF16), w_up[i].astype(BF16), w_down[i].astype(BF16),
               w_pgate[i].astype(BF16), w_pproj[i].astype(BF16)]

    xp = x_prompt.reshape(B * T, D_MODEL)
    y_p, cn_p, m_p, kt_p, vt_p, tail_p = _layer(xp, p_prompt[i].reshape(B * T, P_DIM), g_pre, w_main, w_gt, cw,
                                                bg_col, mn, sinks, gains, weights, B, T)
    y_prompt = y_p.reshape(B, T, D_MODEL)
    c_p, n_p = cn_p[..., :M_DV], cn_p[..., M_DV]
    conv_p = tail_p[:, 8 - (CONV_W - 1):]
    k_p = kt_p.reshape(B, WINDOW, A_KV, A_HD)
    v_p = vt_p.reshape(B, WINDOW, A_KV, A_HD)

    xs = x_sample.transpose(1, 0, 2).reshape(ST * SB, D_MODEL)
    ps = p_sample[i].transpose(1, 0, 2).reshape(ST * SB, P_DIM)
    w_nat = jnp.concatenate([w_main[:, 0:QK_W], w_main[:, QK_W + 2 * M_WIDTH:]], axis=1)
    w_t = w_main[:, 0:QK_W + 2 * M_WIDTH].T
    qkn_s, qa_s, ka_s, va_s, qkt_s, vt_s, ot_s, gt_s = _in_proj_sample(xs, g_pre, w_nat, w_t, w_gt)
    cbt = state_mlstm_conv[i].transpose(1, 2, 0)
    ym_s, c_s2, nt_s, mt_s = _mlstm_sample(
        qkt_s, vt_s, ot_s, gt_s, bg_col, cbt, cw.T, mn.reshape(M_WIDTH, 1),
        state_mlstm_c[i].reshape(SB, M_HEADS * M_DK * M_DV), state_mlstm_n[i].transpose(1, 2, 0),
        state_mlstm_m[i].T)
    c_s = c_s2.reshape(SB, M_HEADS, M_DK, M_DV)
    n_s = nt_s.transpose(2, 0, 1)
    m_s = mt_s.reshape(M_HEADS, SB).T
    conv_s = qkn_s.reshape(ST, SB, QK_W)[ST - (CONV_W - 1):].transpose(1, 0, 2)

    qs = qa_s.reshape(ST, SB, A_GROUP, A_KV, A_HD).transpose(1, 3, 2, 0, 4).reshape(SB, A_HEADS * ST, A_HD)
    qm = jnp.concatenate([qs, qs], axis=-1)
    kn = ka_s.reshape(ST, SB, KV_W).transpose(1, 0, 2)
    vn = va_s.reshape(ST, SB, KV_W).transpose(1, 0, 2)
    sink_rows = jnp.repeat(attn_sinks[i], ST).reshape(A_HEADS * ST, 1)
    slope_rows = jnp.asarray(np.repeat(np.asarray(SLOPES, np.float32), ST).reshape(A_HEADS * ST, 1))
    kct = cache_swa_k[i].transpose(0, 2, 3, 1).reshape(SB, KV_W, WINDOW)
    vct = cache_swa_v[i].transpose(0, 2, 3, 1).reshape(SB, KV_W, WINDOW)
    att_s, kct_s, vct_s = _attn_sample(qm, kn, vn, kct, vct, sink_rows, slope_rows)
    att_d = att_s.reshape(SB, A_KV, A_GROUP * ST, A_KV, A_HD)
    att_k = jnp.stack([att_d[:, kv, :, kv, :] for kv in range(A_KV)], axis=1)
    att_s2 = att_k.reshape(SB, A_KV, A_GROUP, ST, A_HD).transpose(3, 0, 2, 1, 4).reshape(ST * SB, A_WIDTH)
    y_s = _post(xs, ym_s, att_s2, ps, gains, weights)
    y_sample = y_s.reshape(ST, SB, D_MODEL).transpose(1, 0, 2)
    k_s = kct_s.reshape(SB, A_KV, A_HD, WINDOW).transpose(0, 3, 1, 2)
    v_s = vct_s.reshape(SB, A_KV, A_HD, WINDOW).transpose(0, 3, 1, 2)

    stack = lambda a: a[None]
    return (y_prompt, y_sample,
            stack(c_p), stack(n_p), stack(m_p.reshape(B, M_HEADS)), stack(conv_p), stack(k_p), stack(v_p),
            stack(c_s), stack(n_s), stack(m_s), stack(conv_s), stack(k_s), stack(v_s))
```

```python
import functools

import numpy as np
import jax
import jax.numpy as jnp
from jax import lax
from jax.experimental import pallas as pl
from jax.experimental.pallas import tpu as pltpu

F32 = jnp.float32
BF16 = jnp.bfloat16

D_MODEL = 1024
M_WIDTH = 512
M_HEADS = 4
M_DV = 128
M_DK = 64
QK_W = 512
CONV_W = 4
CHUNK = 128
A_WIDTH = 512
A_HEADS = 8
A_HD = 64
A_KV = 2
A_GROUP = 4
KV_W = 128
WINDOW = 128
D_FF = 4096
P_DIM = 256
EPS = 1e-6
N_GATES = 2 * M_HEADS

VMEM_LIMIT_BYTES = 56 * 1024 * 1024
ROW_TILE = 512
FF_TILE = 1024
CHUNKS_PER_STEP = 2
SAMPLE_SEQ_TILE = 8
SAMPLE_V_TILE = 32

NEG_INF = float("-inf")
SLOPES = [2.0 ** (-8.0 * (h + 1) / A_HEADS) for h in range(A_HEADS)]
HEAD_ORDER = [kv * A_GROUP + g for g in range(A_GROUP) for kv in range(A_KV)]


def _dot(a, b):
    return jnp.dot(a, b, preferred_element_type=F32)


def _dot_nt(a, b):
    return lax.dot_general(a, b, (((1,), (1,)), ((), ())), preferred_element_type=F32)


def _dot_tn(a, b):
    return lax.dot_general(a, b, (((0,), (0,)), ((), ())), preferred_element_type=F32)


def _rms(x, g):
    return x * lax.rsqrt(jnp.mean(x * x, axis=-1, keepdims=True) + EPS) * g


def _conv_silu(up_ref, base, w, L):
    out = w[0:1, :] * up_ref[base:base + L, :]
    for j in range(1, CONV_W):
        out = out + w[j:j + 1, :] * up_ref[base + j:base + j + L, :]
    return jax.nn.silu(out)


def _chunk_cumsum_lanes(x):
    pos = lax.broadcasted_iota(jnp.int32, x.shape, 1)
    shift = 1
    while shift < CHUNK:
        x = x + jnp.where(pos >= shift, pltpu.roll(x, shift, axis=1), 0.0)
        shift *= 2
    return x


def _in_proj_kernel(x_ref, g_ref, w_ref, wgt_ref, cw_ref, bgc_ref,
                    q_ref, k_ref, vm_ref, os_ref, qa_ref, ka_ref, va_ref, gc_ref, gr_ref, tail_ref,
                    up_ref, *, seq_tiles):
    i = pl.program_id(0)
    tm = x_ref.shape[0]
    xn = _rms(x_ref[...], g_ref[...]).astype(BF16)
    z = _dot(xn, w_ref[...])

    @pl.when(i % seq_tiles == 0)
    def _():
        up_ref[0:8, :] = jnp.zeros((8, QK_W), F32)

    up_ref[8:8 + tm, :] = z[:, 0:QK_W]
    qk = _conv_silu(up_ref, 5, cw_ref[...], tm)
    q_ref[...] = (qk[:, 0:QK_W // 2] * (M_DK ** -0.5)).astype(BF16)
    k_ref[...] = qk[:, QK_W // 2:QK_W]
    up_ref[5:8, :] = up_ref[5 + tm:8 + tm, :]

    @pl.when(i % seq_tiles == seq_tiles - 1)
    def _():
        tail_ref[0] = up_ref[tm:8 + tm, :]

    vm_ref[...] = z[:, 512:1024].astype(BF16)
    os_ref[...] = jax.nn.sigmoid(z[:, 1024:1536])
    qa_ref[...] = (z[:, 1536:2048] * (A_HD ** -0.5)).astype(BF16)
    ka_ref[...] = z[:, 2048:2176]
    va_ref[...] = z[:, 2176:2304]

    g = _dot_nt(wgt_ref[...], xn) + bgc_ref[...]
    row = lax.broadcasted_iota(jnp.int32, (N_GATES, CHUNK), 0)
    for c in range(tm // CHUNK):
        gch = g[:, c * CHUNK:(c + 1) * CHUNK]
        logs = jnp.where(row >= M_HEADS, jax.nn.log_sigmoid(gch), gch)
        out = jnp.where(row >= M_HEADS, _chunk_cumsum_lanes(logs), logs)
        gr_ref[:, c * CHUNK:(c + 1) * CHUNK] = out
        gc_ref[c * CHUNK:(c + 1) * CHUNK, :] = out.T


def _in_proj(x2d, g_pre, w_main, w_gt, conv_w, bg_col, batch):
    n = x2d.shape[0]
    tm = ROW_TILE
    seq_tiles = n // batch // tm
    row = lambda w: pl.BlockSpec((tm, w), lambda i: (i, 0))
    full = lambda a: pl.BlockSpec(a.shape, lambda i: (0,) * a.ndim)
    outs = [(QK_W // 2, BF16), (QK_W // 2, F32), (M_WIDTH, BF16), (M_WIDTH, F32), (A_WIDTH, BF16),
            (KV_W, F32), (KV_W, F32), (N_GATES, F32)]
    return pl.pallas_call(
        functools.partial(_in_proj_kernel, seq_tiles=seq_tiles),
        grid=(n // tm,),
        in_specs=[row(D_MODEL), full(g_pre), full(w_main), full(w_gt), full(conv_w), full(bg_col)],
        out_specs=[row(w) for w, _ in outs] + [pl.BlockSpec((N_GATES, tm), lambda i: (0, i)),
                                                pl.BlockSpec((1, 8, QK_W), lambda i: (i // seq_tiles, 0, 0))],
        out_shape=[jax.ShapeDtypeStruct((n, w), dt) for w, dt in outs]
        + [jax.ShapeDtypeStruct((N_GATES, n), F32), jax.ShapeDtypeStruct((batch, 8, QK_W), F32)],
        scratch_shapes=[pltpu.VMEM((8 + tm, QK_W), F32)],
        compiler_params=pltpu.CompilerParams(dimension_semantics=("arbitrary",),
                                             vmem_limit_bytes=VMEM_LIMIT_BYTES),
        name="in_proj",
    )(x2d, g_pre, w_main, w_gt, conv_w, bg_col)


def _mlstm_out(h, o_sig, g):
    return o_sig * (h * lax.rsqrt(jnp.mean(h * h, axis=-1, keepdims=True) + EPS) * g)


def _mixer_prompt_kernel(q_ref, k_ref, vm_ref, os_ref, qa_ref, ka_ref, va_ref, gc_ref, gr_ref, mn_ref, sink_ref,
                         ym_ref, att_ref, c_out, m_out,
                         c_st, m_st, kp_ref, vp_ref, bias0_ref, bias_ref, sinkrep_ref):
    ci = pl.program_id(1)
    L = CHUNK
    J = range(CHUNKS_PER_STEP)
    H = range(M_HEADS)
    KV = range(A_KV)
    t = lax.broadcasted_iota(jnp.int32, (L, L), 0)
    s = lax.broadcasted_iota(jnp.int32, (L, L), 1)

    @pl.when(ci == 0)
    def _():
        c_st[...] = jnp.zeros_like(c_st)
        m_st[...] = jnp.zeros_like(m_st)
        kp_ref[...] = jnp.zeros_like(kp_ref)
        vp_ref[...] = jnp.zeros_like(vp_ref)
        dist = (t - s).astype(F32)
        for kv in KV:
            for g in range(A_GROUP):
                hd = kv * A_GROUP + g
                rows = slice(g * L, (g + 1) * L)
                cur = jnp.where(s <= t, -SLOPES[hd] * dist, NEG_INF)
                bias_ref[kv, rows, 0:L] = jnp.where(s >= t, -SLOPES[hd] * (dist + float(WINDOW)), NEG_INF)
                bias_ref[kv, rows, L:2 * L] = cur
                bias0_ref[kv, rows, 0:L] = jnp.full((L, L), NEG_INF, F32)
                bias0_ref[kv, rows, L:2 * L] = cur
                sinkrep_ref[kv, rows, :] = jnp.broadcast_to(sink_ref[0:1, hd:hd + 1], (L, KV_W))

    @pl.when(ci == 1)
    def _():
        for kv in KV:
            bias0_ref[kv, :, 0:L] = bias_ref[kv, :, 0:L]

    rows = [slice(j * L, (j + 1) * L) for j in J]
    P = [(j, h) for j in J for h in H]

    ig_c = [gc_ref[rows[j], 0:M_HEADS] for j in J]
    b_c = [gc_ref[rows[j], M_HEADS:N_GATES] for j in J]
    a_r = [gr_ref[0:M_HEADS, rows[j]] - gr_ref[M_HEADS:N_GATES, rows[j]] for j in J]
    causal_bias = jnp.where(s <= t, 0.0, NEG_INF)
    hcol = lax.broadcasted_iota(jnp.int32, (L, M_HEADS), 1)
    m_prev = [m_st[0:1, 0:M_HEADS]]
    w_k, decay = [], []
    for j in J:
        b_last = b_c[j][L - 1:L, :]
        log_w = b_last - b_c[j] + ig_c[j]
        m_new = jnp.maximum(b_last + m_prev[j], jnp.max(log_w, axis=0, keepdims=True))
        w_k.append(jnp.exp(log_w - m_new))
        decay.append(jnp.exp(b_last + m_prev[j] - m_new))
        m_prev.append(m_new)
    m_st[0:1, 0:M_HEADS] = m_prev[-1]

    ones_v = jnp.ones((L, M_DV), BF16)
    qb = {p: q_ref[rows[p[0]], p[1] * M_DK:(p[1] + 1) * M_DK] for p in P}
    kf = {p: k_ref[rows[p[0]], p[1] * M_DK:(p[1] + 1) * M_DK] for p in P}
    kb = {p: kf[p].astype(BF16) for p in P}
    vext = {p: jnp.concatenate([vm_ref[rows[p[0]], p[1] * M_DV:(p[1] + 1) * M_DV], ones_v], axis=1) for p in P}
    qk_h = {p: _dot_nt(qb[p], kb[p]) for p in P}
    kw = {p: (kf[p] * w_k[p[0]][:, p[1]:p[1] + 1]).astype(BF16) for p in P}
    dc = {p: _dot_tn(kw[p], vext[p]) for p in P}
    log_d = {p: b_c[p[0]][:, p[1]:p[1] + 1] + a_r[p[0]][p[1]:p[1] + 1, :] + causal_bias for p in P}
    m_t, w_inter, emt = [], [], []
    for j in J:
        m_intra = jnp.full((L, M_HEADS), NEG_INF, F32)
        for h in H:
            m_intra = jnp.where(hcol == h, jnp.max(log_d[(j, h)], axis=-1, keepdims=True), m_intra)
        log_inter = b_c[j] + m_prev[j]
        m_t.append(jnp.maximum(log_inter, m_intra))
        w_inter.append(jnp.exp(log_inter - m_t[j]))
        emt.append(jnp.exp(-m_t[j]))
    s_h = {p: (qk_h[p] * jnp.exp(log_d[p] - m_t[p[0]][:, p[1]:p[1] + 1])).astype(BF16) for p in P}
    o_h = {p: _dot(s_h[p], vext[p]) for p in P}
    c_cur = [c_st[h] for h in H]
    qc = {}
    for j in J:
        for h in H:
            qc[(j, h)] = _dot(qb[(j, h)], c_cur[h].astype(BF16))
        c_cur = [decay[j][:, h:h + 1] * c_cur[h] + dc[(j, h)] for h in H]
    for h in H:
        c_st[h] = c_cur[h]
    for j, h in P:
        wi = w_inter[j][:, h:h + 1]
        num = o_h[(j, h)][:, 0:M_DV] + wi * qc[(j, h)][:, 0:M_DV]
        den = o_h[(j, h)][:, M_DV:2 * M_DV] + wi * qc[(j, h)][:, M_DV:2 * M_DV]
        hh = num / jnp.maximum(jnp.abs(den), emt[j][:, h:h + 1])
        cols = slice(h * M_DV, (h + 1) * M_DV)
        ym_ref[rows[j], cols] = _mlstm_out(hh, os_ref[rows[j], cols], mn_ref[:, cols])

    nk = (CHUNKS_PER_STEP + 1) * L
    low3 = lax.broadcasted_iota(jnp.int32, (nk, KV_W), 1) < A_HD
    low = lax.broadcasted_iota(jnp.int32, (L, KV_W), 1) < A_HD
    k_all = jnp.concatenate([kp_ref[...], ka_ref[...]], axis=0)
    v_all = jnp.concatenate([vp_ref[...], va_ref[...]], axis=0)
    ones_k = jnp.ones((nk, KV_W), BF16)
    kmask = [jnp.where(low3, k_all, 0.0).astype(BF16), jnp.where(low3, 0.0, k_all).astype(BF16)]
    vext_a = [jnp.concatenate([jnp.where(low3, v_all, 1.0).astype(BF16), ones_k], axis=1),
              jnp.concatenate([jnp.where(low3, 1.0, v_all).astype(BF16), ones_k], axis=1)]
    qst = [jnp.concatenate([qa_ref[rows[j], g * KV_W:(g + 1) * KV_W] for g in range(A_GROUP)], axis=0)
           for j in J]
    Q = [(j, kv) for j in J for kv in KV]
    bias_of = lambda j, kv: (bias0_ref if j == 0 else bias_ref)[kv]
    sc = {p: _dot_nt(qst[p[0]], kmask[p[1]][p[0] * L:(p[0] + 2) * L]) + bias_of(*p) for p in Q}
    m_a = {p: jnp.maximum(jnp.max(jnp.maximum(sc[p][:, 0:L], sc[p][:, L:2 * L]), axis=-1, keepdims=True),
                          sinkrep_ref[p[1]]) for p in Q}
    e = {p: jnp.concatenate([jnp.exp(sc[p][:, 0:L] - m_a[p]), jnp.exp(sc[p][:, L:2 * L] - m_a[p])],
                            axis=1).astype(BF16) for p in Q}
    o = {p: _dot(e[p], vext_a[p[1]][p[0] * L:(p[0] + 2) * L]) for p in Q}
    exs = {p: jnp.exp(sinkrep_ref[p[1]] - m_a[p]) for p in Q}
    for j in J:
        for g in range(A_GROUP):
            gr_ = slice(g * L, (g + 1) * L)
            o0, o1 = o[(j, 0)], o[(j, 1)]
            pv = jnp.where(low, o0[gr_, 0:KV_W], o1[gr_, 0:KV_W])
            den = jnp.where(low, o0[gr_, KV_W:2 * KV_W] + exs[(j, 0)][gr_, :],
                            o1[gr_, KV_W:2 * KV_W] + exs[(j, 1)][gr_, :])
            att_ref[rows[j], g * KV_W:(g + 1) * KV_W] = pv / den
    kp_ref[...] = ka_ref[(CHUNKS_PER_STEP - 1) * L:CHUNKS_PER_STEP * L, :]
    vp_ref[...] = va_ref[(CHUNKS_PER_STEP - 1) * L:CHUNKS_PER_STEP * L, :]

    @pl.when(ci == pl.num_programs(1) - 1)
    def _():
        c_out[0] = c_st[...]
        m_out[0] = m_st[0:1, 0:M_HEADS]


def _mixer_prompt(q, k, vm, os_, qa, ka, va, gc, gr, mlstm_norm, sinks, batch, seq):
    tr = CHUNKS_PER_STEP * CHUNK
    ns = seq // tr
    row = lambda w: pl.BlockSpec((tr, w), lambda b, c: (b * ns + c, 0))
    full = lambda a: pl.BlockSpec(a.shape, lambda b, c: (0,) * a.ndim)
    n = batch * seq
    bias_shape = (A_KV, A_GROUP * CHUNK, 2 * CHUNK)
    return pl.pallas_call(
        _mixer_prompt_kernel,
        grid=(batch, ns),
        in_specs=[row(QK_W // 2), row(QK_W // 2), row(M_WIDTH), row(M_WIDTH), row(A_WIDTH), row(KV_W), row(KV_W),
                  row(N_GATES), pl.BlockSpec((N_GATES, tr), lambda b, c: (0, b * ns + c)),
                  full(mlstm_norm), full(sinks)],
        out_specs=[row(M_WIDTH), row(A_WIDTH),
                   pl.BlockSpec((1, M_HEADS, M_DK, 2 * M_DV), lambda b, c: (b, 0, 0, 0)),
                   pl.BlockSpec((1, 1, M_HEADS), lambda b, c: (b, 0, 0))],
        out_shape=[jax.ShapeDtypeStruct((n, M_WIDTH), F32), jax.ShapeDtypeStruct((n, A_WIDTH), F32),
                   jax.ShapeDtypeStruct((batch, M_HEADS, M_DK, 2 * M_DV), F32),
                   jax.ShapeDtypeStruct((batch, 1, M_HEADS), F32)],
        scratch_shapes=[pltpu.VMEM((M_HEADS, M_DK, 2 * M_DV), F32),
                        pltpu.VMEM((8, 128), F32),
                        pltpu.VMEM((CHUNK, KV_W), F32), pltpu.VMEM((CHUNK, KV_W), F32),
                        pltpu.VMEM(bias_shape, F32), pltpu.VMEM(bias_shape, F32),
                        pltpu.VMEM((A_KV, A_GROUP * CHUNK, KV_W), F32)],
        compiler_params=pltpu.CompilerParams(dimension_semantics=("arbitrary", "arbitrary"),
                                             vmem_limit_bytes=VMEM_LIMIT_BYTES),
        name="mixer_prompt",
    )(q, k, vm, os_, qa, ka, va, gc, gr, mlstm_norm, sinks)


Z_W = QK_W + 2 * M_WIDTH + A_WIDTH + 2 * KV_W
Z_GROUPS = [(0, 512), (512, 1024), (1024, 1536), (1536, 2048), (2048, 2304)]
EARLY_GROUPS = 3


def _front_kernel(x_ref, g_ref, w_ref, wgt_ref, cw_ref, bgc_ref, mn_ref, sink_ref,
                  ym_ref, att_ref, c_out, m_out, ktail_ref, vtail_ref, ctail_ref,
                  z_ref, g_st, up_ref, c_st, m_st, kp_ref, vp_ref, bias0_ref, bias_ref, sinkrep_ref):
    ci = pl.program_id(1)
    last = pl.num_programs(1) - 1
    L = CHUNK
    J = range(CHUNKS_PER_STEP)
    H = range(M_HEADS)
    KV = range(A_KV)
    TR = CHUNKS_PER_STEP * L
    nxt = ci % 2
    cur = 1 - nxt
    t = lax.broadcasted_iota(jnp.int32, (L, L), 0)
    s = lax.broadcasted_iota(jnp.int32, (L, L), 1)

    @pl.when(ci == 0)
    def _():
        z_ref[cur] = jnp.zeros((TR, Z_W), F32)
        g_st[cur] = jnp.zeros((N_GATES, TR), F32)
        c_st[...] = jnp.zeros_like(c_st)
        m_st[...] = jnp.zeros_like(m_st)
        kp_ref[...] = jnp.zeros_like(kp_ref)
        vp_ref[...] = jnp.zeros_like(vp_ref)
        up_ref[0:8, :] = jnp.zeros((8, QK_W), F32)
        dist = (t - s).astype(F32)
        for kv in KV:
            for g in range(A_GROUP):
                hd = kv * A_GROUP + g
                rows = slice(g * L, (g + 1) * L)
                cur_b = jnp.where(s <= t, -SLOPES[hd] * dist, NEG_INF)
                bias_ref[kv, rows, 0:L] = jnp.where(s >= t, -SLOPES[hd] * (dist + float(WINDOW)), NEG_INF)
                bias_ref[kv, rows, L:2 * L] = cur_b
                bias0_ref[kv, rows, 0:L] = jnp.full((L, L), NEG_INF, F32)
                bias0_ref[kv, rows, L:2 * L] = cur_b
                sinkrep_ref[kv, rows, :] = jnp.broadcast_to(sink_ref[0:1, hd:hd + 1], (L, KV_W))

    @pl.when(ci == 1)
    def _():
        c_st[...] = jnp.zeros_like(c_st)
        m_st[...] = jnp.zeros_like(m_st)
        kp_ref[...] = jnp.zeros_like(kp_ref)
        vp_ref[...] = jnp.zeros_like(vp_ref)
        up_ref[0:8, :] = jnp.zeros((8, QK_W), F32)

    @pl.when(ci == 2)
    def _():
        for kv in KV:
            bias0_ref[kv, :, 0:L] = bias_ref[kv, :, 0:L]

    up_ref[8:8 + TR, :] = z_ref[cur, :, 0:QK_W]
    gcur = g_st[cur] + bgc_ref[...]

    xn = _rms(x_ref[...], g_ref[...]).astype(BF16)
    for lo, hi in Z_GROUPS:
        z_ref[nxt, :, lo:hi] = _dot(xn, w_ref[:, lo:hi])
    g_st[nxt] = _dot_nt(wgt_ref[...], xn)

    qk = _conv_silu(up_ref, 5, cw_ref[...], TR)
    up_ref[5:8, :] = up_ref[5 + TR:8 + TR, :]
    q_all = (qk[:, 0:QK_W // 2] * (M_DK ** -0.5)).astype(BF16)
    k_all_m = qk[:, QK_W // 2:QK_W]
    grow = lax.broadcasted_iota(jnp.int32, (N_GATES, L), 0)
    rows = [slice(j * L, (j + 1) * L) for j in J]
    gr, gc = [], []
    for j in J:
        gch = gcur[:, rows[j]]
        logs = jnp.where(grow >= M_HEADS, jax.nn.log_sigmoid(gch), gch)
        out = jnp.where(grow >= M_HEADS, _chunk_cumsum_lanes(logs), logs)
        gr.append(out)
        gc.append(out.T)

    P = [(j, h) for j in J for h in H]
    ig_c = [gc[j][:, 0:M_HEADS] for j in J]
    b_c = [gc[j][:, M_HEADS:N_GATES] for j in J]
    a_r = [gr[j][0:M_HEADS, :] - gr[j][M_HEADS:N_GATES, :] for j in J]
    causal_bias = jnp.where(s <= t, 0.0, NEG_INF)
    hcol = lax.broadcasted_iota(jnp.int32, (L, M_HEADS), 1)
    m_prev = [m_st[0:1, 0:M_HEADS]]
    w_k, decay = [], []
    for j in J:
        b_last = b_c[j][L - 1:L, :]
        log_w = b_last - b_c[j] + ig_c[j]
        m_new = jnp.maximum(b_last + m_prev[j], jnp.max(log_w, axis=0, keepdims=True))
        w_k.append(jnp.exp(log_w - m_new))
        decay.append(jnp.exp(b_last + m_prev[j] - m_new))
        m_prev.append(m_new)
    m_st[0:1, 0:M_HEADS] = m_prev[-1]

    ones_v = jnp.ones((L, M_DV), BF16)
    qb = {p: q_all[rows[p[0]], p[1] * M_DK:(p[1] + 1) * M_DK] for p in P}
    kf = {p: k_all_m[rows[p[0]], p[1] * M_DK:(p[1] + 1) * M_DK] for p in P}
    kb = {p: kf[p].astype(BF16) for p in P}
    vext = {p: jnp.concatenate([z_ref[cur, rows[p[0]], QK_W + p[1] * M_DV:QK_W + (p[1] + 1) * M_DV].astype(BF16),
                                ones_v], axis=1) for p in P}
    qk_h = {p: _dot_nt(qb[p], kb[p]) for p in P}
    kw = {p: (kf[p] * w_k[p[0]][:, p[1]:p[1] + 1]).astype(BF16) for p in P}
    dc = {p: _dot_tn(kw[p], vext[p]) for p in P}
    log_d = {p: b_c[p[0]][:, p[1]:p[1] + 1] + a_r[p[0]][p[1]:p[1] + 1, :] + causal_bias for p in P}
    m_t, w_inter, emt = [], [], []
    for j in J:
        m_intra = jnp.full((L, M_HEADS), NEG_INF, F32)
        for h in H:
            m_intra = jnp.where(hcol == h, jnp.max(log_d[(j, h)], axis=-1, keepdims=True), m_intra)
        log_inter = b_c[j] + m_prev[j]
        m_t.append(jnp.maximum(log_inter, m_intra))
        w_inter.append(jnp.exp(log_inter - m_t[j]))
        emt.append(jnp.exp(-m_t[j]))
    s_h = {p: (qk_h[p] * jnp.exp(log_d[p] - m_t[p[0]][:, p[1]:p[1] + 1])).astype(BF16) for p in P}
    c_cur = [c_st[h] for h in H]
    qc = {}
    for j in J:
        for h in H:
            qc[(j, h)] = _dot(qb[(j, h)], c_cur[h].astype(BF16))
        c_cur = [decay[j][:, h:h + 1] * c_cur[h] + dc[(j, h)] for h in H]
    for h in H:
        c_st[h] = c_cur[h]
    o_h = {p: _dot(s_h[p], vext[p]) for p in P}
    for j, h in P:
        wi = w_inter[j][:, h:h + 1]
        num = o_h[(j, h)][:, 0:M_DV] + wi * qc[(j, h)][:, 0:M_DV]
        den = o_h[(j, h)][:, M_DV:2 * M_DV] + wi * qc[(j, h)][:, M_DV:2 * M_DV]
        hh = num / jnp.maximum(jnp.abs(den), emt[j][:, h:h + 1])
        cols = slice(h * M_DV, (h + 1) * M_DV)
        o_sig = jax.nn.sigmoid(z_ref[cur, rows[j], QK_W + M_WIDTH + h * M_DV:QK_W + M_WIDTH + (h + 1) * M_DV])
        ym_ref[rows[j], cols] = _mlstm_out(hh, o_sig, mn_ref[:, cols])

    a0 = QK_W + 2 * M_WIDTH
    nk = (CHUNKS_PER_STEP + 1) * L
    low3 = lax.broadcasted_iota(jnp.int32, (nk, KV_W), 1) < A_HD
    low = lax.broadcasted_iota(jnp.int32, (L, KV_W), 1) < A_HD
    ka_cur = z_ref[cur, :, a0 + A_WIDTH:a0 + A_WIDTH + KV_W]
    va_cur = z_ref[cur, :, a0 + A_WIDTH + KV_W:a0 + A_WIDTH + 2 * KV_W]
    k_all = jnp.concatenate([kp_ref[...], ka_cur], axis=0)
    v_all = jnp.concatenate([vp_ref[...], va_cur], axis=0)
    ones_k = jnp.ones((nk, KV_W), BF16)
    kmask = [jnp.where(low3, k_all, 0.0).astype(BF16), jnp.where(low3, 0.0, k_all).astype(BF16)]
    vext_a = [jnp.concatenate([jnp.where(low3, v_all, 1.0).astype(BF16), ones_k], axis=1),
              jnp.concatenate([jnp.where(low3, 1.0, v_all).astype(BF16), ones_k], axis=1)]
    qst = [(jnp.concatenate([z_ref[cur, rows[j], a0 + g * KV_W:a0 + (g + 1) * KV_W] for g in range(A_GROUP)],
                            axis=0) * (A_HD ** -0.5)).astype(BF16) for j in J]
    Q = [(j, kv) for j in J for kv in KV]
    bias_of = lambda j, kv: (bias0_ref if j == 0 else bias_ref)[kv]
    sc = {p: _dot_nt(qst[p[0]], kmask[p[1]][p[0] * L:(p[0] + 2) * L]) + bias_of(*p) for p in Q}
    m_a = {p: jnp.maximum(jnp.max(jnp.maximum(sc[p][:, 0:L], sc[p][:, L:2 * L]), axis=-1, keepdims=True),
                          sinkrep_ref[p[1]]) for p in Q}
    e = {p: jnp.concatenate([jnp.exp(sc[p][:, 0:L] - m_a[p]), jnp.exp(sc[p][:, L:2 * L] - m_a[p])],
                            axis=1).astype(BF16) for p in Q}
    o = {p: _dot(e[p], vext_a[p[1]][p[0] * L:(p[0] + 2) * L]) for p in Q}
    exs = {p: jnp.exp(sinkrep_ref[p[1]] - m_a[p]) for p in Q}
    for j in J:
        for g in range(A_GROUP):
            gr_ = slice(g * L, (g + 1) * L)
            o0, o1 = o[(j, 0)], o[(j, 1)]
            pv = jnp.where(low, o0[gr_, 0:KV_W], o1[gr_, 0:KV_W])
            den = jnp.where(low, o0[gr_, KV_W:2 * KV_W] + exs[(j, 0)][gr_, :],
                            o1[gr_, KV_W:2 * KV_W] + exs[(j, 1)][gr_, :])
            att_ref[rows[j], g * KV_W:(g + 1) * KV_W] = pv / den
    kp_ref[...] = ka_cur[(CHUNKS_PER_STEP - 1) * L:CHUNKS_PER_STEP * L, :]
    vp_ref[...] = va_cur[(CHUNKS_PER_STEP - 1) * L:CHUNKS_PER_STEP * L, :]

    @pl.when(ci == last)
    def _():
        c_out[0] = c_st[...]
        m_out[0] = m_st[0:1, 0:M_HEADS]
        ktail_ref[0] = kp_ref[...]
        vtail_ref[0] = vp_ref[...]
        ctail_ref[0] = up_ref[0:8, :]


def _front(x2d, g_pre, w_main, w_gt, conv_w, bg_col, mlstm_norm, sinks, batch, seq):
    tr = CHUNKS_PER_STEP * CHUNK
    ns = seq // tr
    n = batch * seq
    full = lambda a: pl.BlockSpec(a.shape, lambda b, c: (0,) * a.ndim)
    resident = lambda a: pl.BlockSpec(a.shape, lambda b, c: (0,) * a.ndim, pipeline_mode=pl.Buffered(1))
    out_row = lambda w: pl.BlockSpec((tr, w), lambda b, c: (b * ns + jnp.maximum(c - 1, 0), 0))
    per_seq = lambda shape: pl.BlockSpec((1,) + shape, lambda b, c: (b,) + (0,) * len(shape))
    bias_shape = (A_KV, A_GROUP * CHUNK, 2 * CHUNK)
    return pl.pallas_call(
        _front_kernel,
        grid=(batch, ns + 1),
        in_specs=[pl.BlockSpec((tr, D_MODEL), lambda b, c: (b * ns + jnp.minimum(c, ns - 1), 0)),
                  full(g_pre), resident(w_main), full(w_gt), full(conv_w), full(bg_col),
                  full(mlstm_norm), full(sinks)],
        out_specs=[out_row(M_WIDTH), out_row(A_WIDTH),
                   per_seq((M_HEADS, M_DK, 2 * M_DV)), per_seq((1, M_HEADS)),
                   per_seq((CHUNK, KV_W)), per_seq((CHUNK, KV_W)), per_seq((8, QK_W))],
        out_shape=[jax.ShapeDtypeStruct((n, M_WIDTH), F32), jax.ShapeDtypeStruct((n, A_WIDTH), F32),
                   jax.ShapeDtypeStruct((batch, M_HEADS, M_DK, 2 * M_DV), F32),
                   jax.ShapeDtypeStruct((batch, 1, M_HEADS), F32),
                   jax.ShapeDtypeStruct((batch, CHUNK, KV_W), F32), jax.ShapeDtypeStruct((batch, CHUNK, KV_W), F32),
                   jax.ShapeDtypeStruct((batch, 8, QK_W), F32)],
        scratch_shapes=[pltpu.VMEM((2, tr, Z_W), F32), pltpu.VMEM((2, N_GATES, tr), F32),
                        pltpu.VMEM((8 + tr, QK_W), F32),
                        pltpu.VMEM((M_HEADS, M_DK, 2 * M_DV), F32), pltpu.VMEM((8, 128), F32),
                        pltpu.VMEM((CHUNK, KV_W), F32), pltpu.VMEM((CHUNK, KV_W), F32),
                        pltpu.VMEM(bias_shape, F32), pltpu.VMEM(bias_shape, F32),
                        pltpu.VMEM((A_KV, A_GROUP * CHUNK, KV_W), F32)],
        compiler_params=pltpu.CompilerParams(dimension_semantics=("arbitrary", "arbitrary"),
                                             vmem_limit_bytes=VMEM_LIMIT_BYTES),
        name="front",
    )(x2d, g_pre, w_main, w_gt, conv_w, bg_col, mlstm_norm, sinks)


def _layer_kernel(x_ref, xq_ref, p_ref, g_ref, w_ref, wgt_ref, cw_ref, bgc_ref, mn_ref, sink_ref,
                  an_ref, g_post_ref, g_fpre_ref, g_fpost_ref, wout_ref, wup_ref, wdown_ref, wpg_ref, wpp_ref,
                  y_ref, c_out, m_out, ktail_ref, vtail_ref, ctail_ref,
                  z_ref, g_st, y_st, up_ref, c_st, m_st, kp_ref, vp_ref, bias0_ref, bias_ref, sinkrep_ref):
    ci = pl.program_id(1)
    n_tiles = pl.num_programs(1) - 2
    L = CHUNK
    J = range(CHUNKS_PER_STEP)
    H = range(M_HEADS)
    KV = range(A_KV)
    TR = CHUNKS_PER_STEP * L
    nxt = ci % 2
    cur = 1 - nxt
    t = lax.broadcasted_iota(jnp.int32, (L, L), 0)
    s = lax.broadcasted_iota(jnp.int32, (L, L), 1)

    def reset_state():
        c_st[...] = jnp.zeros_like(c_st)
        m_st[...] = jnp.zeros_like(m_st)
        kp_ref[...] = jnp.zeros_like(kp_ref)
        vp_ref[...] = jnp.zeros_like(vp_ref)
        up_ref[0:8, :] = jnp.zeros((8, QK_W), F32)

    @pl.when(ci == 0)
    def _():
        z_ref[cur] = jnp.zeros((TR, Z_W), F32)
        g_st[cur] = jnp.zeros((N_GATES, TR), F32)
        y_st[cur] = jnp.zeros((TR, D_MODEL), BF16)
        reset_state()
        dist = (t - s).astype(F32)
        for kv in KV:
            for g in range(A_GROUP):
                hd = kv * A_GROUP + g
                rows = slice(g * L, (g + 1) * L)
                cur_b = jnp.where(s <= t, -SLOPES[hd] * dist, NEG_INF)
                bias_ref[kv, rows, 0:L] = jnp.where(s >= t, -SLOPES[hd] * (dist + float(WINDOW)), NEG_INF)
                bias_ref[kv, rows, L:2 * L] = cur_b
                bias0_ref[kv, rows, 0:L] = jnp.full((L, L), NEG_INF, F32)
                bias0_ref[kv, rows, L:2 * L] = cur_b
                sinkrep_ref[kv, rows, :] = jnp.broadcast_to(sink_ref[0:1, hd:hd + 1], (L, KV_W))

    @pl.when(ci == 1)
    def _():
        reset_state()

    @pl.when(ci == 2)
    def _():
        for kv in KV:
            bias0_ref[kv, :, 0:L] = bias_ref[kv, :, 0:L]

    up_ref[8:8 + TR, :] = z_ref[cur, :, 0:QK_W]
    gcur = g_st[cur] + bgc_ref[...]

    xn = _rms(x_ref[...], g_ref[...]).astype(BF16)
    for lo, hi in Z_GROUPS:
        z_ref[nxt, :, lo:hi] = _dot(xn, w_ref[:, lo:hi])
    g_st[nxt] = _dot_nt(wgt_ref[...], xn)

    x1 = xq_ref[...] + _rms(_dot(y_st[cur], wout_ref[...]), g_post_ref[...])
    u = _rms(x1, g_fpre_ref[...]).astype(BF16)

    def ffn_tile(j):
        cols = slice(j * FF_TILE, (j + 1) * FF_TILE)
        hid = jnp.square(jnp.maximum(_dot(u, wup_ref[:, cols]), 0.0)).astype(BF16)
        return _dot(hid, wdown_ref[cols, :])

    qk = _conv_silu(up_ref, 5, cw_ref[...], TR)
    up_ref[5:8, :] = up_ref[5 + TR:8 + TR, :]
    q_all = (qk[:, 0:QK_W // 2] * (M_DK ** -0.5)).astype(BF16)
    k_all_m = qk[:, QK_W // 2:QK_W]
    grow = lax.broadcasted_iota(jnp.int32, (N_GATES, L), 0)
    rows = [slice(j * L, (j + 1) * L) for j in J]
    gr, gc = [], []
    for j in J:
        gch = gcur[:, rows[j]]
        logs = jnp.where(grow >= M_HEADS, jax.nn.log_sigmoid(gch), gch)
        out = jnp.where(grow >= M_HEADS, _chunk_cumsum_lanes(logs), logs)
        gr.append(out)
        gc.append(out.T)
    P = [(j, h) for j in J for h in H]
    ig_c = [gc[j][:, 0:M_HEADS] for j in J]
    b_c = [gc[j][:, M_HEADS:N_GATES] for j in J]
    a_r = [gr[j][0:M_HEADS, :] - gr[j][M_HEADS:N_GATES, :] for j in J]
    causal_bias = jnp.where(s <= t, 0.0, NEG_INF)
    hcol = lax.broadcasted_iota(jnp.int32, (L, M_HEADS), 1)
    m_prev = [m_st[0:1, 0:M_HEADS]]
    w_k, decay = [], []
    for j in J:
        b_last = b_c[j][L - 1:L, :]
        log_w = b_last - b_c[j] + ig_c[j]
        m_new = jnp.maximum(b_last + m_prev[j], jnp.max(log_w, axis=0, keepdims=True))
        w_k.append(jnp.exp(log_w - m_new))
        decay.append(jnp.exp(b_last + m_prev[j] - m_new))
        m_prev.append(m_new)
    m_st[0:1, 0:M_HEADS] = m_prev[-1]

    ones_v = jnp.ones((L, M_DV), BF16)
    qb = {p: q_all[rows[p[0]], p[1] * M_DK:(p[1] + 1) * M_DK] for p in P}
    kf = {p: k_all_m[rows[p[0]], p[1] * M_DK:(p[1] + 1) * M_DK] for p in P}
    kb = {p: kf[p].astype(BF16) for p in P}
    vext = {p: jnp.concatenate([z_ref[cur, rows[p[0]], QK_W + p[1] * M_DV:QK_W + (p[1] + 1) * M_DV].astype(BF16),
                                ones_v], axis=1) for p in P}
    qk_h = {p: _dot_nt(qb[p], kb[p]) for p in P}
    kw = {p: (kf[p] * w_k[p[0]][:, p[1]:p[1] + 1]).astype(BF16) for p in P}
    dc = {p: _dot_tn(kw[p], vext[p]) for p in P}

    f = ffn_tile(0)

    log_d = {p: b_c[p[0]][:, p[1]:p[1] + 1] + a_r[p[0]][p[1]:p[1] + 1, :] + causal_bias for p in P}
    m_t, w_inter, emt = [], [], []
    for j in J:
        m_intra = jnp.full((L, M_HEADS), NEG_INF, F32)
        for h in H:
            m_intra = jnp.where(hcol == h, jnp.max(log_d[(j, h)], axis=-1, keepdims=True), m_intra)
        log_inter = b_c[j] + m_prev[j]
        m_t.append(jnp.maximum(log_inter, m_intra))
        w_inter.append(jnp.exp(log_inter - m_t[j]))
        emt.append(jnp.exp(-m_t[j]))
    s_h = {p: (qk_h[p] * jnp.exp(log_d[p] - m_t[p[0]][:, p[1]:p[1] + 1])).astype(BF16) for p in P}
    c_cur = [c_st[h] for h in H]
    qc = {}
    for j in J:
        for h in H:
            qc[(j, h)] = _dot(qb[(j, h)], c_cur[h].astype(BF16))
        c_cur = [decay[j][:, h:h + 1] * c_cur[h] + dc[(j, h)] for h in H]
    for h in H:
        c_st[h] = c_cur[h]

    f = f + ffn_tile(1)

    o_h = {p: _dot(s_h[p], vext[p]) for p in P}
    for j, h in P:
        wi = w_inter[j][:, h:h + 1]
        num = o_h[(j, h)][:, 0:M_DV] + wi * qc[(j, h)][:, 0:M_DV]
        den = o_h[(j, h)][:, M_DV:2 * M_DV] + wi * qc[(j, h)][:, M_DV:2 * M_DV]
        hh = num / jnp.maximum(jnp.abs(den), emt[j][:, h:h + 1])
        cols = slice(h * M_DV, (h + 1) * M_DV)
        o_sig = jax.nn.sigmoid(z_ref[cur, rows[j], QK_W + M_WIDTH + h * M_DV:QK_W + M_WIDTH + (h + 1) * M_DV])
        y_st[nxt, rows[j], cols] = _mlstm_out(hh, o_sig, mn_ref[:, cols]).astype(BF16)

    a0 = QK_W + 2 * M_WIDTH
    nk = (CHUNKS_PER_STEP + 1) * L
    low3 = lax.broadcasted_iota(jnp.int32, (nk, KV_W), 1) < A_HD
    low = lax.broadcasted_iota(jnp.int32, (L, KV_W), 1) < A_HD
    ka_cur = z_ref[cur, :, a0 + A_WIDTH:a0 + A_WIDTH + KV_W]
    va_cur = z_ref[cur, :, a0 + A_WIDTH + KV_W:a0 + A_WIDTH + 2 * KV_W]
    k_all = jnp.concatenate([kp_ref[...], ka_cur], axis=0)
    v_all = jnp.concatenate([vp_ref[...], va_cur], axis=0)
    ones_k = jnp.ones((nk, KV_W), BF16)
    kmask = [jnp.where(low3, k_all, 0.0).astype(BF16), jnp.where(low3, 0.0, k_all).astype(BF16)]
    vext_a = [jnp.concatenate([jnp.where(low3, v_all, 1.0).astype(BF16), ones_k], axis=1),
              jnp.concatenate([jnp.where(low3, 1.0, v_all).astype(BF16), ones_k], axis=1)]
    qst = [(jnp.concatenate([z_ref[cur, rows[j], a0 + g * KV_W:a0 + (g + 1) * KV_W] for g in range(A_GROUP)],
                            axis=0) * (A_HD ** -0.5)).astype(BF16) for j in J]
    Q = [(j, kv) for j in J for kv in KV]
    bias_of = lambda j, kv: (bias0_ref if j == 0 else bias_ref)[kv]
    sc = {p: _dot_nt(qst[p[0]], kmask[p[1]][p[0] * L:(p[0] + 2) * L]) + bias_of(*p) for p in Q}

    f = f + ffn_tile(2)

    m_a = {p: jnp.maximum(jnp.max(jnp.maximum(sc[p][:, 0:L], sc[p][:, L:2 * L]), axis=-1, keepdims=True),
                          sinkrep_ref[p[1]]) for p in Q}
    e = {p: jnp.concatenate([jnp.exp(sc[p][:, 0:L] - m_a[p]), jnp.exp(sc[p][:, L:2 * L] - m_a[p])],
                            axis=1).astype(BF16) for p in Q}
    o = {p: _dot(e[p], vext_a[p[1]][p[0] * L:(p[0] + 2) * L]) for p in Q}

    f = f + ffn_tile(3)

    exs = {p: jnp.exp(sinkrep_ref[p[1]] - m_a[p]) for p in Q}
    for j in J:
        att = []
        for g in range(A_GROUP):
            gr_ = slice(g * L, (g + 1) * L)
            o0, o1 = o[(j, 0)], o[(j, 1)]
            pv = jnp.where(low, o0[gr_, 0:KV_W], o1[gr_, 0:KV_W])
            den = jnp.where(low, o0[gr_, KV_W:2 * KV_W] + exs[(j, 0)][gr_, :],
                            o1[gr_, KV_W:2 * KV_W] + exs[(j, 1)][gr_, :])
            att.append(pv / den)
        ssq = att[0] * att[0]
        for g in range(1, A_GROUP):
            ssq = ssq + att[g] * att[g]
        scale = lax.rsqrt(jnp.sum(ssq, axis=-1, keepdims=True) * (1.0 / A_WIDTH) + EPS)
        for g in range(A_GROUP):
            cols = slice(g * KV_W, (g + 1) * KV_W)
            y_st[nxt, rows[j], M_WIDTH + g * KV_W:M_WIDTH + (g + 1) * KV_W] = (att[g] * scale * an_ref[:, cols]).astype(BF16)
    kp_ref[...] = ka_cur[(CHUNKS_PER_STEP - 1) * L:CHUNKS_PER_STEP * L, :]
    vp_ref[...] = va_cur[(CHUNKS_PER_STEP - 1) * L:CHUNKS_PER_STEP * L, :]

    x2 = x1 + _rms(f, g_fpost_ref[...])
    gate = jax.nn.sigmoid(_dot(x2.astype(BF16), wpg_ref[...]))
    y_ref[...] = x2 + gate * _dot(p_ref[...].astype(BF16), wpp_ref[...])

    @pl.when(ci == n_tiles)
    def _():
        c_out[0] = c_st[...]
        m_out[0] = m_st[0:1, 0:M_HEADS]
        ktail_ref[0] = kp_ref[...]
        vtail_ref[0] = vp_ref[...]
        ctail_ref[0] = up_ref[0:8, :]


def _layer(x2d, p2d, g_pre, w_main, w_gt, conv_w, bg_col, mlstm_norm, sinks, gains, weights, batch, seq):
    tr = CHUNKS_PER_STEP * CHUNK
    ns = seq // tr
    n = batch * seq
    full = lambda a: pl.BlockSpec(a.shape, lambda b, c: (0,) * a.ndim)
    resident = lambda a: pl.BlockSpec(a.shape, lambda b, c: (0,) * a.ndim, pipeline_mode=pl.Buffered(1))
    lag2 = lambda w: pl.BlockSpec((tr, w), lambda b, c: (b * ns + jnp.clip(c - 2, 0, ns - 1), 0))
    per_seq = lambda shape: pl.BlockSpec((1,) + shape, lambda b, c: (b,) + (0,) * len(shape))
    bias_shape = (A_KV, A_GROUP * CHUNK, 2 * CHUNK)
    return pl.pallas_call(
        _layer_kernel,
        grid=(batch, ns + 2),
        in_specs=[pl.BlockSpec((tr, D_MODEL), lambda b, c: (b * ns + jnp.minimum(c, ns - 1), 0)),
                  lag2(D_MODEL), lag2(P_DIM),
                  full(g_pre), resident(w_main), full(w_gt), full(conv_w), full(bg_col),
                  full(mlstm_norm), full(sinks)] + [full(g) for g in gains] + [resident(w) for w in weights],
        out_specs=[lag2(D_MODEL),
                   per_seq((M_HEADS, M_DK, 2 * M_DV)), per_seq((1, M_HEADS)),
                   per_seq((CHUNK, KV_W)), per_seq((CHUNK, KV_W)), per_seq((8, QK_W))],
        out_shape=[jax.ShapeDtypeStruct((n, D_MODEL), F32),
                   jax.ShapeDtypeStruct((batch, M_HEADS, M_DK, 2 * M_DV), F32),
                   jax.ShapeDtypeStruct((batch, 1, M_HEADS), F32),
                   jax.ShapeDtypeStruct((batch, CHUNK, KV_W), F32), jax.ShapeDtypeStruct((batch, CHUNK, KV_W), F32),
                   jax.ShapeDtypeStruct((batch, 8, QK_W), F32)],
        scratch_shapes=[pltpu.VMEM((2, tr, Z_W), F32), pltpu.VMEM((2, N_GATES, tr), F32),
                        pltpu.VMEM((2, tr, D_MODEL), BF16),
                        pltpu.VMEM((8 + tr, QK_W), F32),
                        pltpu.VMEM((M_HEADS, M_DK, 2 * M_DV), F32), pltpu.VMEM((8, 128), F32),
                        pltpu.VMEM((CHUNK, KV_W), F32), pltpu.VMEM((CHUNK, KV_W), F32),
                        pltpu.VMEM(bias_shape, F32), pltpu.VMEM(bias_shape, F32),
                        pltpu.VMEM((A_KV, A_GROUP * CHUNK, KV_W), F32)],
        compiler_params=pltpu.CompilerParams(dimension_semantics=("arbitrary", "arbitrary"),
                                             vmem_limit_bytes=VMEM_LIMIT_BYTES),
        name="layer",
    )(x2d, x2d, p2d, g_pre, w_main, w_gt, conv_w, bg_col, mlstm_norm, sinks, *gains, *weights)


def _stream_kernel(x_ref, xq_ref, p_ref, g_ref, w_ref, wgt_ref, cw_ref, bgc_ref, mn_ref, sink_ref,
                   an_ref, g_post_ref, g_fpre_ref, g_fpost_ref, wout_ref, wup_ref, wdown_ref, wpg_ref, wpp_ref,
                   y_ref, c_out, m_out, ktail_ref, vtail_ref, ctail_ref,
                   z_ref, g_st, y_st, up_ref, c_st, m_st, kp_ref, vp_ref, bias_ref, sinkrep_ref, *, seq_tiles):
    ci = pl.program_id(0)
    L = CHUNK
    J = range(CHUNKS_PER_STEP)
    H = range(M_HEADS)
    KV = range(A_KV)
    TR = CHUNKS_PER_STEP * L
    nxt = ci % 2
    cur = 1 - nxt
    mixed = ci - 1
    first = (mixed % seq_tiles) == 0
    t = lax.broadcasted_iota(jnp.int32, (L, L), 0)
    s = lax.broadcasted_iota(jnp.int32, (L, L), 1)

    @pl.when(ci == 0)
    def _():
        z_ref[cur] = jnp.zeros((TR, Z_W), F32)
        g_st[cur] = jnp.zeros((N_GATES, TR), F32)
        y_st[cur] = jnp.zeros((TR, D_MODEL), BF16)
        c_st[...] = jnp.zeros_like(c_st)
        m_st[...] = jnp.zeros_like(m_st)
        kp_ref[...] = jnp.zeros_like(kp_ref)
        vp_ref[...] = jnp.zeros_like(vp_ref)
        up_ref[0:8, :] = jnp.zeros((8, QK_W), F32)
        dist = (t - s).astype(F32)
        for kv in KV:
            for g in range(A_GROUP):
                hd = kv * A_GROUP + g
                rows = slice(g * L, (g + 1) * L)
                bias_ref[kv, rows, 0:L] = jnp.where(s >= t, -SLOPES[hd] * (dist + float(WINDOW)), NEG_INF)
                bias_ref[kv, rows, L:2 * L] = jnp.where(s <= t, -SLOPES[hd] * dist, NEG_INF)
                sinkrep_ref[kv, rows, :] = jnp.broadcast_to(sink_ref[0:1, hd:hd + 1], (L, KV_W))

    up_ref[5:8, :] = jnp.where(first, 0.0, up_ref[5:8, :])
    up_ref[8:8 + TR, :] = z_ref[cur, :, 0:QK_W]
    gcur = g_st[cur] + bgc_ref[...]

    mix = _dot(y_st[cur], wout_ref[...])
    xn = _rms(x_ref[...], g_ref[...]).astype(BF16)
    for lo, hi in Z_GROUPS[:EARLY_GROUPS]:
        z_ref[nxt, :, lo:hi] = _dot(xn, w_ref[:, lo:hi])
    x1 = xq_ref[...] + _rms(mix, g_post_ref[...])
    u = _rms(x1, g_fpre_ref[...]).astype(BF16)

    def ffn_tile(j):
        cols = slice(j * FF_TILE, (j + 1) * FF_TILE)
        hid = jnp.square(jnp.maximum(_dot(u, wup_ref[:, cols]), 0.0)).astype(BF16)
        return _dot(hid, wdown_ref[cols, :])

    qk = _conv_silu(up_ref, 5, cw_ref[...], TR)
    up_ref[5:8, :] = up_ref[5 + TR:8 + TR, :]
    q_all = (qk[:, 0:QK_W // 2] * (M_DK ** -0.5)).astype(BF16)
    k_all_m = qk[:, QK_W // 2:QK_W]
    grow = lax.broadcasted_iota(jnp.int32, (N_GATES, L), 0)
    rows = [slice(j * L, (j + 1) * L) for j in J]
    gr, gc = [], []
    for j in J:
        gch = gcur[:, rows[j]]
        logs = jnp.where(grow >= M_HEADS, jax.nn.log_sigmoid(gch), gch)
        out = jnp.where(grow >= M_HEADS, _chunk_cumsum_lanes(logs), logs)
        gr.append(out)
        gc.append(out.T)
    P = [(j, h) for j in J for h in H]
    ig_c = [gc[j][:, 0:M_HEADS] for j in J]
    b_c = [gc[j][:, M_HEADS:N_GATES] for j in J]
    a_r = [gr[j][0:M_HEADS, :] - gr[j][M_HEADS:N_GATES, :] for j in J]
    causal_bias = jnp.where(s <= t, 0.0, NEG_INF)
    hcol = lax.broadcasted_iota(jnp.int32, (L, M_HEADS), 1)
    m_prev = [jnp.where(first, 0.0, m_st[0:1, 0:M_HEADS])]
    w_k, decay = [], []
    for j in J:
        b_last = b_c[j][L - 1:L, :]
        log_w = b_last - b_c[j] + ig_c[j]
        m_new = jnp.maximum(b_last + m_prev[j], jnp.max(log_w, axis=0, keepdims=True))
        w_k.append(jnp.exp(log_w - m_new))
        decay.append(jnp.exp(b_last + m_prev[j] - m_new))
        m_prev.append(m_new)
    m_st[0:1, 0:M_HEADS] = m_prev[-1]

    f = ffn_tile(0)

    ones_v = jnp.ones((L, M_DV), BF16)
    qb = {p: q_all[rows[p[0]], p[1] * M_DK:(p[1] + 1) * M_DK] for p in P}
    kf = {p: k_all_m[rows[p[0]], p[1] * M_DK:(p[1] + 1) * M_DK] for p in P}
    kb = {p: kf[p].astype(BF16) for p in P}
    vext = {p: jnp.concatenate([z_ref[cur, rows[p[0]], QK_W + p[1] * M_DV:QK_W + (p[1] + 1) * M_DV].astype(BF16),
                                ones_v], axis=1) for p in P}
    qk_h = {p: _dot_nt(qb[p], kb[p]) for p in P}
    kw = {p: (kf[p] * w_k[p[0]][:, p[1]:p[1] + 1]).astype(BF16) for p in P}
    dc = {p: _dot_tn(kw[p], vext[p]) for p in P}

    f = f + ffn_tile(1)

    log_d = {p: b_c[p[0]][:, p[1]:p[1] + 1] + a_r[p[0]][p[1]:p[1] + 1, :] + causal_bias for p in P}
    m_t, w_inter, emt = [], [], []
    for j in J:
        m_intra = jnp.full((L, M_HEADS), NEG_INF, F32)
        for h in H:
            m_intra = jnp.where(hcol == h, jnp.max(log_d[(j, h)], axis=-1, keepdims=True), m_intra)
        log_inter = b_c[j] + m_prev[j]
        m_t.append(jnp.maximum(log_inter, m_intra))
        w_inter.append(jnp.exp(log_inter - m_t[j]))
        emt.append(jnp.exp(-m_t[j]))
    s_h = {p: (qk_h[p] * jnp.exp(log_d[p] - m_t[p[0]][:, p[1]:p[1] + 1])).astype(BF16) for p in P}
    c_cur = [jnp.where(first, 0.0, c_st[h]) for h in H]
    qc = {}
    for j in J:
        for h in H:
            qc[(j, h)] = _dot(qb[(j, h)], c_cur[h].astype(BF16))
        c_cur = [decay[j][:, h:h + 1] * c_cur[h] + dc[(j, h)] for h in H]
    for h in H:
        c_st[h] = c_cur[h]

    f = f + ffn_tile(2)

    o_h = {p: _dot(s_h[p], vext[p]) for p in P}
    for j, h in P:
        wi = w_inter[j][:, h:h + 1]
        num = o_h[(j, h)][:, 0:M_DV] + wi * qc[(j, h)][:, 0:M_DV]
        den = o_h[(j, h)][:, M_DV:2 * M_DV] + wi * qc[(j, h)][:, M_DV:2 * M_DV]
        hh = num / jnp.maximum(jnp.abs(den), emt[j][:, h:h + 1])
        cols = slice(h * M_DV, (h + 1) * M_DV)
        o_sig = jax.nn.sigmoid(z_ref[cur, rows[j], QK_W + M_WIDTH + h * M_DV:QK_W + M_WIDTH + (h + 1) * M_DV])
        y_st[nxt, rows[j], cols] = _mlstm_out(hh, o_sig, mn_ref[:, cols]).astype(BF16)

    a0 = QK_W + 2 * M_WIDTH
    nk = (CHUNKS_PER_STEP + 1) * L
    low3 = lax.broadcasted_iota(jnp.int32, (nk, KV_W), 1) < A_HD
    low = lax.broadcasted_iota(jnp.int32, (L, KV_W), 1) < A_HD
    ka_cur = z_ref[cur, :, a0 + A_WIDTH:a0 + A_WIDTH + KV_W]
    va_cur = z_ref[cur, :, a0 + A_WIDTH + KV_W:a0 + A_WIDTH + 2 * KV_W]
    k_all = jnp.concatenate([kp_ref[...], ka_cur], axis=0)
    v_all = jnp.concatenate([vp_ref[...], va_cur], axis=0)
    ones_k = jnp.ones((nk, KV_W), BF16)
    kmask = [jnp.where(low3, k_all, 0.0).astype(BF16), jnp.where(low3, 0.0, k_all).astype(BF16)]
    vext_a = [jnp.concatenate([jnp.where(low3, v_all, 1.0).astype(BF16), ones_k], axis=1),
              jnp.concatenate([jnp.where(low3, 1.0, v_all).astype(BF16), ones_k], axis=1)]
    qst = [(jnp.concatenate([z_ref[cur, rows[j], a0 + g * KV_W:a0 + (g + 1) * KV_W] for g in range(A_GROUP)],
                            axis=0) * (A_HD ** -0.5)).astype(BF16) for j in J]
    Q = [(j, kv) for j in J for kv in KV]

    def bias_of(j, kv):
        if j > 0:
            return bias_ref[kv]
        return jnp.concatenate([jnp.where(first, NEG_INF, bias_ref[kv, :, 0:L]), bias_ref[kv, :, L:2 * L]], axis=1)

    sc = {p: _dot_nt(qst[p[0]], kmask[p[1]][p[0] * L:(p[0] + 2) * L]) + bias_of(*p) for p in Q}

    f = f + ffn_tile(3)

    m_a = {p: jnp.maximum(jnp.max(jnp.maximum(sc[p][:, 0:L], sc[p][:, L:2 * L]), axis=-1, keepdims=True),
                          sinkrep_ref[p[1]]) for p in Q}
    e = {p: jnp.concatenate([jnp.exp(sc[p][:, 0:L] - m_a[p]), jnp.exp(sc[p][:, L:2 * L] - m_a[p])],
                            axis=1).astype(BF16) for p in Q}
    o = {p: _dot(e[p], vext_a[p[1]][p[0] * L:(p[0] + 2) * L]) for p in Q}

    x2 = x1 + _rms(f, g_fpost_ref[...])
    gate = jax.nn.sigmoid(_dot(x2.astype(BF16), wpg_ref[...]))
    y_ref[...] = x2 + gate * _dot(p_ref[...].astype(BF16), wpp_ref[...])

    for lo, hi in Z_GROUPS[EARLY_GROUPS:]:
        z_ref[nxt, :, lo:hi] = _dot(xn, w_ref[:, lo:hi])
    g_st[nxt] = _dot_nt(wgt_ref[...], xn)

    exs = {p: jnp.exp(sinkrep_ref[p[1]] - m_a[p]) for p in Q}
    for j in J:
        att = []
        for g in range(A_GROUP):
            gr_ = slice(g * L, (g + 1) * L)
            o0, o1 = o[(j, 0)], o[(j, 1)]
            pv = jnp.where(low, o0[gr_, 0:KV_W], o1[gr_, 0:KV_W])
            den = jnp.where(low, o0[gr_, KV_W:2 * KV_W] + exs[(j, 0)][gr_, :],
                            o1[gr_, KV_W:2 * KV_W] + exs[(j, 1)][gr_, :])
            att.append(pv / den)
        ssq = att[0] * att[0]
        for g in range(1, A_GROUP):
            ssq = ssq + att[g] * att[g]
        scale = lax.rsqrt(jnp.sum(ssq, axis=-1, keepdims=True) * (1.0 / A_WIDTH) + EPS)
        for g in range(A_GROUP):
            cols = slice(g * KV_W, (g + 1) * KV_W)
            y_st[nxt, rows[j], M_WIDTH + g * KV_W:M_WIDTH + (g + 1) * KV_W] = (
                att[g] * scale * an_ref[:, cols]).astype(BF16)
    kp_ref[...] = ka_cur[(CHUNKS_PER_STEP - 1) * L:CHUNKS_PER_STEP * L, :]
    vp_ref[...] = va_cur[(CHUNKS_PER_STEP - 1) * L:CHUNKS_PER_STEP * L, :]

    @pl.when((ci > 0) & (ci % seq_tiles == 0))
    def _():
        c_out[0] = c_st[...]
        m_out[0] = m_st[0:1, 0:M_HEADS]
        ktail_ref[0] = kp_ref[...]
        vtail_ref[0] = vp_ref[...]
        ctail_ref[0] = up_ref[0:8, :]


def _stream(x2d, p2d, g_pre, w_main, w_gt, conv_w, bg_col, mlstm_norm, sinks, gains, weights, batch, seq):
    tr = CHUNKS_PER_STEP * CHUNK
    ns = seq // tr
    nt = batch * ns
    n = batch * seq
    full = lambda a: pl.BlockSpec(a.shape, lambda c: (0,) * a.ndim)
    resident = lambda a: pl.BlockSpec(a.shape, lambda c: (0,) * a.ndim, pipeline_mode=pl.Buffered(1))
    lag2 = lambda w: pl.BlockSpec((tr, w), lambda c: (jnp.clip(c - 2, 0, nt - 1), 0))
    per_seq = lambda shape: pl.BlockSpec((1,) + shape,
                                         lambda c: (jnp.clip((c - 1) // ns, 0, batch - 1),) + (0,) * len(shape))
    return pl.pallas_call(
        functools.partial(_stream_kernel, seq_tiles=ns),
        grid=(nt + 2,),
        in_specs=[pl.BlockSpec((tr, D_MODEL), lambda c: (jnp.minimum(c, nt - 1), 0)),
                  lag2(D_MODEL), lag2(P_DIM),
                  full(g_pre), resident(w_main), full(w_gt), full(conv_w), full(bg_col),
                  full(mlstm_norm), full(sinks)] + [full(g) for g in gains] + [resident(w) for w in weights],
        out_specs=[lag2(D_MODEL),
                   per_seq((M_HEADS, M_DK, 2 * M_DV)), per_seq((1, M_HEADS)),
                   per_seq((CHUNK, KV_W)), per_seq((CHUNK, KV_W)), per_seq((8, QK_W))],
        out_shape=[jax.ShapeDtypeStruct((n, D_MODEL), F32),
                   jax.ShapeDtypeStruct((batch, M_HEADS, M_DK, 2 * M_DV), F32),
                   jax.ShapeDtypeStruct((batch, 1, M_HEADS), F32),
                   jax.ShapeDtypeStruct((batch, CHUNK, KV_W), F32), jax.ShapeDtypeStruct((batch, CHUNK, KV_W), F32),
                   jax.ShapeDtypeStruct((batch, 8, QK_W), F32)],
        scratch_shapes=[pltpu.VMEM((2, tr, Z_W), F32), pltpu.VMEM((2, N_GATES, tr), F32),
                        pltpu.VMEM((2, tr, D_MODEL), BF16),
                        pltpu.VMEM((8 + tr, QK_W), F32),
                        pltpu.VMEM((M_HEADS, M_DK, 2 * M_DV), F32), pltpu.VMEM((8, 128), F32),
                        pltpu.VMEM((CHUNK, KV_W), F32), pltpu.VMEM((CHUNK, KV_W), F32),
                        pltpu.VMEM((A_KV, A_GROUP * CHUNK, 2 * CHUNK), F32),
                        pltpu.VMEM((A_KV, A_GROUP * CHUNK, KV_W), F32)],
        compiler_params=pltpu.CompilerParams(dimension_semantics=("arbitrary",),
                                             vmem_limit_bytes=VMEM_LIMIT_BYTES),
        name="prompt_layer",
    )(x2d, x2d, p2d, g_pre, w_main, w_gt, conv_w, bg_col, mlstm_norm, sinks, *gains, *weights)


def _in_proj_sample_kernel(x_ref, g_ref, wn_ref, wt_ref, wgt_ref,
                           qkn_ref, qa_ref, ka_ref, va_ref, qkt_ref, vt_ref, ot_ref, gt_ref):
    xn = _rms(x_ref[...], g_ref[...]).astype(BF16)
    zn = _dot(xn, wn_ref[...])
    qkn_ref[...] = zn[:, 0:QK_W]
    qa_ref[...] = zn[:, QK_W:QK_W + A_WIDTH]
    ka_ref[...] = zn[:, QK_W + A_WIDTH:QK_W + A_WIDTH + KV_W]
    va_ref[...] = zn[:, QK_W + A_WIDTH + KV_W:QK_W + A_WIDTH + 2 * KV_W]
    zt = _dot_nt(wt_ref[...], xn)
    qkt_ref[...] = zt[0:QK_W, :]
    vt_ref[...] = zt[QK_W:QK_W + M_WIDTH, :]
    ot_ref[...] = zt[QK_W + M_WIDTH:QK_W + 2 * M_WIDTH, :]
    gt_ref[...] = _dot_nt(wgt_ref[...], xn)


def _in_proj_sample(x2d, g_pre, w_nat, w_t, w_gt):
    n = x2d.shape[0]
    shapes = [(n, QK_W), (n, A_WIDTH), (n, KV_W), (n, KV_W), (QK_W, n), (M_WIDTH, n), (M_WIDTH, n), (N_GATES, n)]
    return pl.pallas_call(
        _in_proj_sample_kernel,
        out_shape=[jax.ShapeDtypeStruct(s, F32) for s in shapes],
        compiler_params=pltpu.CompilerParams(vmem_limit_bytes=VMEM_LIMIT_BYTES),
        name="in_proj_sample",
    )(x2d, g_pre, w_nat, w_t, w_gt)


def _mlstm_sample_kernel(qt_ref, kt_ref, vt_ref, ot_ref, gt_ref, bg_ref, cbq_ref, cbk_ref, cwq_ref, cwk_ref,
                         gain_ref, c_ref, n_ref, m_ref,
                         ym_ref, c_out, n_out, m_out,
                         ct_ref, q_s, ik_s, num_s):
    h = pl.program_id(0)
    NB = c_ref.shape[0]
    L = qt_ref.shape[1] // NB
    T = range(L)

    def conv(raw_ref, cb_ref, cw_ref):
        ups = [cb_ref[j] for j in range(CONV_W - 1)] + [raw_ref[:, t * NB:(t + 1) * NB] for t in T]
        w = [cw_ref[:, j:j + 1] for j in range(CONV_W)]
        outs = []
        for t in T:
            acc = w[0] * ups[t]
            for j in range(1, CONV_W):
                acc = acc + w[j] * ups[t + j]
            outs.append(jax.nn.silu(acc))
        return outs

    q = [x * (M_DK ** -0.5) for x in conv(qt_ref, cbq_ref, cwq_ref)]
    k = conv(kt_ref, cbk_ref, cwk_ref)

    ig_all = gt_ref[pl.ds(h, 1), :] + bg_ref[pl.ds(h, 1), :]
    lf_all = jax.nn.log_sigmoid(gt_ref[pl.ds(h + M_HEADS, 1), :] + bg_ref[pl.ds(h + M_HEADS, 1), :])
    m = m_ref[pl.ds(h, 1), :]
    f, ms = [], []
    n = n_ref[0]
    den = []
    for t in T:
        ig, lf = ig_all[:, t * NB:(t + 1) * NB], lf_all[:, t * NB:(t + 1) * NB]
        m_new = jnp.maximum(lf + m, ig)
        f_t = jnp.exp(lf + m - m_new)
        ik = jnp.exp(ig - m_new) * k[t]
        m = m_new
        n = f_t * n + ik
        f.append(f_t)
        ms.append(m_new)
        den.append(jnp.sum(q[t] * n, axis=0, keepdims=True))
        ik_s[t] = ik
        q_s[t] = q[t]
    m_out[0] = m
    n_out[0] = n

    for d in range(M_DK):
        ct_ref[d * M_DV:(d + 1) * M_DV, :] = c_ref[:, d * M_DV:(d + 1) * M_DV].T

    VT = SAMPLE_V_TILE
    for vq in range(M_DV // VT):
        vts = [vt_ref[vq * VT:(vq + 1) * VT, t * NB:(t + 1) * NB] for t in T]

        def step(d, accs, vq=vq, vts=vts):
            r0 = pl.multiple_of(d * M_DV + vq * VT, VT)
            c = ct_ref[pl.ds(r0, VT), :]
            new = []
            for t in T:
                c = f[t] * c + ik_s[t, pl.ds(d, 1), :] * vts[t]
                new.append(accs[t] + q_s[t, pl.ds(d, 1), :] * c)
            ct_ref[pl.ds(r0, VT), :] = c
            return tuple(new)

        accs = lax.fori_loop(0, M_DK, step, tuple(jnp.zeros((VT, NB), F32) for _ in T), unroll=8)
        for t in T:
            num_s[t, vq * VT:(vq + 1) * VT, :] = accs[t]

    for t in T:
        hh = num_s[t] / jnp.maximum(jnp.abs(den[t]), jnp.exp(-ms[t]))
        hn = hh * lax.rsqrt(jnp.mean(hh * hh, axis=0, keepdims=True) + EPS) * gain_ref[...]
        y = jax.nn.sigmoid(ot_ref[:, t * NB:(t + 1) * NB]) * hn
        ym_ref[t * NB:(t + 1) * NB, :] = y.T

    for d in range(M_DK):
        c_out[:, d * M_DV:(d + 1) * M_DV] = ct_ref[d * M_DV:(d + 1) * M_DV, :].T


def _mlstm_sample(qkt, vt, ot, gt, bg_col, cbt, cwt, gain_col, c2d, nt, mt):
    nb = c2d.shape[0]
    n = qkt.shape[1]
    L = n // nb
    hblk = lambda rows, off: pl.BlockSpec((rows, n), lambda h, off=off: (h + off, 0))
    full = lambda a: pl.BlockSpec(a.shape, lambda h: (0,) * a.ndim)
    kq = QK_W // 2 // M_DK
    return pl.pallas_call(
        _mlstm_sample_kernel,
        grid=(M_HEADS,),
        in_specs=[hblk(M_DK, 0), hblk(M_DK, kq), hblk(M_DV, 0), hblk(M_DV, 0), full(gt), full(bg_col),
                  pl.BlockSpec((CONV_W - 1, M_DK, nb), lambda h: (0, h, 0)),
                  pl.BlockSpec((CONV_W - 1, M_DK, nb), lambda h: (0, h + kq, 0)),
                  pl.BlockSpec((M_DK, CONV_W), lambda h: (h, 0)),
                  pl.BlockSpec((M_DK, CONV_W), lambda h: (h + kq, 0)),
                  pl.BlockSpec((M_DV, 1), lambda h: (h, 0)),
                  pl.BlockSpec((nb, M_DK * M_DV), lambda h: (0, h)),
                  pl.BlockSpec((1, M_DK, nb), lambda h: (h, 0, 0)),
                  full(mt)],
        out_specs=[pl.BlockSpec((n, M_DV), lambda h: (0, h)),
                   pl.BlockSpec((nb, M_DK * M_DV), lambda h: (0, h)),
                   pl.BlockSpec((1, M_DK, nb), lambda h: (h, 0, 0)),
                   pl.BlockSpec((1, 1, nb), lambda h: (h, 0, 0))],
        out_shape=[jax.ShapeDtypeStruct((n, M_WIDTH), F32), jax.ShapeDtypeStruct(c2d.shape, F32),
                   jax.ShapeDtypeStruct(nt.shape, F32), jax.ShapeDtypeStruct((M_HEADS, 1, nb), F32)],
        scratch_shapes=[pltpu.VMEM((M_DK * M_DV, nb), F32), pltpu.VMEM((L, M_DK, nb), F32),
                        pltpu.VMEM((L, M_DK, nb), F32), pltpu.VMEM((L, M_DV, nb), F32)],
        compiler_params=pltpu.CompilerParams(dimension_semantics=("arbitrary",),
                                             vmem_limit_bytes=VMEM_LIMIT_BYTES),
        name="mlstm_sample",
    )(qkt, qkt, vt, ot, gt, bg_col, cbt, cbt, cwt, cwt, gain_col, c2d, nt, mt)


def _attn_sample_kernel(qm_ref, kn_ref, vn_ref, kct_ref, vct_ref, sinkr_ref, sloper_ref,
                        att_ref, kct_out, vct_out, xk_ref, xv_ref):
    NB, L = kn_ref.shape[0], kn_ref.shape[1]
    R = A_KV * A_GROUP * L

    def bias(nkeys, offset):
        r = lax.broadcasted_iota(jnp.int32, (R, nkeys), 0).astype(F32)
        pos = lax.broadcasted_iota(jnp.int32, (R, nkeys), 1).astype(F32)
        tq = r - L * jnp.floor((r + 0.5) / L)
        dist = tq + offset - pos
        return jnp.where((dist >= 0.0) & (dist <= float(WINDOW)), -sloper_ref[...] * dist, NEG_INF)

    @pl.when(pl.program_id(0) == 0)
    def _():
        xk_ref[...] = jnp.zeros_like(xk_ref)
        xv_ref[...] = jnp.zeros_like(xv_ref)

    bias_c = bias(WINDOW, float(WINDOW))
    bias_n = bias(L, 0.0)
    sink = sinkr_ref[...]
    rq = lax.broadcasted_iota(jnp.int32, (R, KV_W), 0)
    cq = lax.broadcasted_iota(jnp.int32, (R, KV_W), 1)
    same_kv = (rq < A_GROUP * L) == (cq < A_HD)
    newest = lax.broadcasted_iota(jnp.int32, (KV_W, WINDOW), 1) >= WINDOW - L
    SEQ = range(NB)
    qh = [jnp.where(same_kv, qm_ref[b] * (A_HD ** -0.5), 0.0).astype(BF16) for b in SEQ]
    kt = [kct_ref[b] for b in SEQ]
    vt = [vct_ref[b] for b in SEQ]
    knb = [kn_ref[b].astype(BF16) for b in SEQ]
    vnb = [vn_ref[b].astype(BF16) for b in SEQ]
    sc_c = [_dot(qh[b], kt[b].astype(BF16)) + bias_c for b in SEQ]
    sc_n = [_dot_nt(qh[b], knb[b]) + bias_n for b in SEQ]
    m_a = [jnp.maximum(sink, jnp.maximum(jnp.max(sc_c[b], axis=-1, keepdims=True),
                                         jnp.max(sc_n[b], axis=-1, keepdims=True))) for b in SEQ]
    e_c = [jnp.exp(sc_c[b] - m_a[b]) for b in SEQ]
    e_n = [jnp.exp(sc_n[b] - m_a[b]) for b in SEQ]
    den_a = [jnp.exp(sink - m_a[b]) + jnp.sum(e_c[b], axis=-1, keepdims=True)
             + jnp.sum(e_n[b], axis=-1, keepdims=True) for b in SEQ]
    pv = [_dot_nt((e_c[b] / den_a[b]).astype(BF16), vt[b].astype(BF16))
          + _dot((e_n[b] / den_a[b]).astype(BF16), vnb[b]) for b in SEQ]
    for b in SEQ:
        att_ref[b] = pv[b]

    for b in SEQ:
        xk_ref[b, WINDOW - L:WINDOW, :] = kn_ref[b]
        xv_ref[b, WINDOW - L:WINDOW, :] = vn_ref[b]
    for b in SEQ:
        kct_out[b] = jnp.where(newest, xk_ref[b].T, pltpu.roll(kt[b], WINDOW - L, axis=1))
        vct_out[b] = jnp.where(newest, xv_ref[b].T, pltpu.roll(vt[b], WINDOW - L, axis=1))


def _attn_sample(qm, kn, vn, kct, vct, sink_rows, slope_rows):
    nb = qm.shape[0]
    ts = min(SAMPLE_SEQ_TILE, nb)
    per_b = lambda a: pl.BlockSpec((ts,) + a.shape[1:], lambda b: (b,) + (0,) * (a.ndim - 1))
    full = lambda a: pl.BlockSpec(a.shape, lambda b: (0,) * a.ndim)
    outs = [jax.ShapeDtypeStruct(a.shape, F32) for a in (qm, kct, vct)]
    return pl.pallas_call(
        _attn_sample_kernel,
        grid=(nb // ts,),
        in_specs=[per_b(a) for a in (qm, kn, vn, kct, vct)] + [full(sink_rows), full(slope_rows)],
        out_specs=[per_b(o) for o in outs],
        out_shape=outs,
        scratch_shapes=[pltpu.VMEM((ts, WINDOW, KV_W), F32), pltpu.VMEM((ts, WINDOW, KV_W), F32)],
        compiler_params=pltpu.CompilerParams(dimension_semantics=("arbitrary",),
                                             vmem_limit_bytes=VMEM_LIMIT_BYTES),
        name="attn_sample",
    )(qm, kn, vn, kct, vct, sink_rows, slope_rows)


def _post_kernel(x_ref, ym_ref, att_ref, p_ref, an_ref, g_post_ref, g_fpre_ref, g_fpost_ref,
                 wout_ref, wup_ref, wdown_ref, wpg_ref, wpp_ref, o_ref):
    y_a = _rms(att_ref[...], an_ref[...])
    y = jnp.concatenate([ym_ref[...], y_a], axis=-1).astype(BF16)
    x1 = x_ref[...] + _rms(_dot(y, wout_ref[...]), g_post_ref[...])
    u = _rms(x1, g_fpre_ref[...]).astype(BF16)
    f = jnp.zeros_like(x1)
    for j in range(D_FF // FF_TILE):
        cols = slice(j * FF_TILE, (j + 1) * FF_TILE)
        hid = jnp.square(jnp.maximum(_dot(u, wup_ref[:, cols]), 0.0)).astype(BF16)
        f = f + _dot(hid, wdown_ref[cols, :])
    x2 = x1 + _rms(f, g_fpost_ref[...])
    gate = jax.nn.sigmoid(_dot(x2.astype(BF16), wpg_ref[...]))
    o_ref[...] = x2 + gate * _dot(p_ref[...].astype(BF16), wpp_ref[...])


def _post(x2d, ym, att, p2d, gains, weights):
    n = x2d.shape[0]
    tm = min(ROW_TILE, n)
    row = lambda w: pl.BlockSpec((tm, w), lambda i: (i, 0))
    full = lambda a: pl.BlockSpec(a.shape, lambda i: (0,) * a.ndim)
    resident = lambda a: pl.BlockSpec(a.shape, lambda i: (0,) * a.ndim, pipeline_mode=pl.Buffered(1))
    return pl.pallas_call(
        _post_kernel,
        grid=(n // tm,),
        in_specs=[row(D_MODEL), row(M_WIDTH), row(A_WIDTH), row(P_DIM)]
        + [full(g) for g in gains] + [resident(w) for w in weights],
        out_specs=row(D_MODEL),
        out_shape=jax.ShapeDtypeStruct((n, D_MODEL), F32),
        compiler_params=pltpu.CompilerParams(dimension_semantics=("arbitrary",),
                                             vmem_limit_bytes=VMEM_LIMIT_BYTES),
        name="post",
    )(x2d, ym, att, p2d, *gains, *weights)


def _permute_heads(a, axis):
    shape = a.shape
    a = a.reshape(shape[:axis] + (A_HEADS, A_HD) + shape[axis + 1:])
    a = jnp.take(a, np.asarray(HEAD_ORDER), axis=axis)
    return a.reshape(shape)


def kernel(x_prompt, x_sample, p_prompt, p_sample, state_mlstm_c, state_mlstm_n, state_mlstm_m,
           state_mlstm_conv, cache_swa_k, cache_swa_v, norm_mix_pre, w_in, b_gates, conv_w,
           mlstm_norm, attn_sinks, attn_norm, w_out, norm_mix_post, norm_ffn_pre, w_up, w_down,
           norm_ffn_post, w_pgate, w_pproj):
    depth = w_in.shape[0]
    assert depth == 1, "single-layer decoder"
    B, T, _ = x_prompt.shape
    SB, ST, _ = x_sample.shape
    assert T % (CHUNKS_PER_STEP * CHUNK) == 0 and T % ROW_TILE == 0 and SB % SAMPLE_SEQ_TILE == 0
    i = 0

    wi = w_in[i]
    o_qk, o_vm, o_om, o_g, o_qa, o_ka, o_va = np.cumsum(
        [0, QK_W, M_WIDTH, M_WIDTH, N_GATES, A_WIDTH, KV_W])
    w_main = jnp.concatenate([wi[:, o_qk:o_g], _permute_heads(wi[:, o_qa:o_ka], 1), wi[:, o_ka:]],
                             axis=1).astype(BF16)
    w_gt = wi[:, o_g:o_qa].astype(BF16).T
    g_pre = norm_mix_pre[i].reshape(1, D_MODEL)
    bg_col = b_gates[i].reshape(N_GATES, 1)
    cw = conv_w[i]
    mn = mlstm_norm[i].reshape(1, M_WIDTH)
    sinks = attn_sinks[i].reshape(1, A_HEADS)
    gains = [_permute_heads(attn_norm[i], 0).reshape(1, A_WIDTH), norm_mix_post[i].reshape(1, D_MODEL),
             norm_ffn_pre[i].reshape(1, D_MODEL), norm_ffn_post[i].reshape(1, D_MODEL)]
    wo = jnp.concatenate([w_out[i][:M_WIDTH], _permute_heads(w_out[i][M_WIDTH:], 0)], axis=0)
    weights = [wo.astype(BF16), w_up[i].astype(BF16), w_down[i].astype(BF16),
               w_pgate[i].astype(BF16), w_pproj[i].astype(BF16)]

    xp = x_prompt.reshape(B * T, D_MODEL)
    y_p, cn_p, m_p, kt_p, vt_p, tail_p = _stream(xp, p_prompt[i].reshape(B * T, P_DIM), g_pre, w_main, w_gt, cw,
                                                 bg_col, mn, sinks, gains, weights, B, T)
    y_prompt = y_p.reshape(B, T, D_MODEL)
    c_p, n_p = cn_p[..., :M_DV], cn_p[..., M_DV]
    conv_p = tail_p[:, 8 - (CONV_W - 1):]
    k_p = kt_p.reshape(B, WINDOW, A_KV, A_HD)
    v_p = vt_p.reshape(B, WINDOW, A_KV, A_HD)

    xs = x_sample.transpose(1, 0, 2).reshape(ST * SB, D_MODEL)
    ps = p_sample[i].transpose(1, 0, 2).reshape(ST * SB, P_DIM)
    w_nat = jnp.concatenate([w_main[:, 0:QK_W], w_main[:, QK_W + 2 * M_WIDTH:]], axis=1)
    w_t = w_main[:, 0:QK_W + 2 * M_WIDTH].T
    qkn_s, qa_s, ka_s, va_s, qkt_s, vt_s, ot_s, gt_s = _in_proj_sample(xs, g_pre, w_nat, w_t, w_gt)
    cbt = state_mlstm_conv[i].transpose(1, 2, 0)
    ym_s, c_s2, nt_s, mt_s = _mlstm_sample(
        qkt_s, vt_s, ot_s, gt_s, bg_col, cbt, cw.T, mn.reshape(M_WIDTH, 1),
        state_mlstm_c[i].reshape(SB, M_HEADS * M_DK * M_DV), state_mlstm_n[i].transpose(1, 2, 0),
        state_mlstm_m[i].T)
    c_s = c_s2.reshape(SB, M_HEADS, M_DK, M_DV)
    n_s = nt_s.transpose(2, 0, 1)
    m_s = mt_s.reshape(M_HEADS, SB).T
    conv_s = qkn_s.reshape(ST, SB, QK_W)[ST - (CONV_W - 1):].transpose(1, 0, 2)

    qs = qa_s.reshape(ST, SB, A_GROUP, A_KV, A_HD).transpose(1, 3, 2, 0, 4).reshape(SB, A_HEADS * ST, A_HD)
    qm = jnp.concatenate([qs, qs], axis=-1)
    kn = ka_s.reshape(ST, SB, KV_W).transpose(1, 0, 2)
    vn = va_s.reshape(ST, SB, KV_W).transpose(1, 0, 2)
    sink_rows = jnp.repeat(attn_sinks[i], ST).reshape(A_HEADS * ST, 1)
    slope_rows = jnp.asarray(np.repeat(np.asarray(SLOPES, np.float32), ST).reshape(A_HEADS * ST, 1))
    kct = cache_swa_k[i].transpose(0, 2, 3, 1).reshape(SB, KV_W, WINDOW)
    vct = cache_swa_v[i].transpose(0, 2, 3, 1).reshape(SB, KV_W, WINDOW)
    att_s, kct_s, vct_s = _attn_sample(qm, kn, vn, kct, vct, sink_rows, slope_rows)
    att_d = att_s.reshape(SB, A_KV, A_GROUP * ST, A_KV, A_HD)
    att_k = jnp.stack([att_d[:, kv, :, kv, :] for kv in range(A_KV)], axis=1)
    att_s2 = att_k.reshape(SB, A_KV, A_GROUP, ST, A_HD).transpose(3, 0, 2, 1, 4).reshape(ST * SB, A_WIDTH)
    y_s = _post(xs, ym_s, att_s2, ps, gains, weights)
    y_sample = y_s.reshape(ST, SB, D_MODEL).transpose(1, 0, 2)
    k_s = kct_s.reshape(SB, A_KV, A_HD, WINDOW).transpose(0, 3, 1, 2)
    v_s = vct_s.reshape(SB, A_KV, A_HD, WINDOW).transpose(0, 3, 1, 2)

    stack = lambda a: a[None]
    return (y_prompt, y_sample,
            stack(c_p), stack(n_p), stack(m_p.reshape(B, M_HEADS)), stack(conv_p), stack(k_p), stack(v_p),
            stack(c_s), stack(n_s), stack(m_s), stack(conv_s), stack(k_s), stack(v_s))
```

```python
import functools

import numpy as np
import jax
import jax.numpy as jnp
from jax import lax
from jax.experimental import pallas as pl
from jax.experimental.pallas import tpu as pltpu

F32 = jnp.float32
BF16 = jnp.bfloat16

D_MODEL = 1024
M_WIDTH = 512
M_HEADS = 4
M_DV = 128
M_DK = 64
QK_W = 512
CONV_W = 4
CHUNK = 128
A_WIDTH = 512
A_HEADS = 8
A_HD = 64
A_KV = 2
A_GROUP = 4
KV_W = 128
WINDOW = 128
D_FF = 4096
P_DIM = 256
EPS = 1e-6
N_GATES = 2 * M_HEADS

VMEM_LIMIT_BYTES = 56 * 1024 * 1024
ROW_TILE = 512
FF_TILE = 1024
CHUNKS_PER_STEP = 2
SAMPLE_SEQ_TILE = 16
SAMPLE_V_TILE = 32

NEG_INF = float("-inf")
SLOPES = [2.0 ** (-8.0 * (h + 1) / A_HEADS) for h in range(A_HEADS)]
HEAD_ORDER = [kv * A_GROUP + g for g in range(A_GROUP) for kv in range(A_KV)]


def _dot(a, b):
    return jnp.dot(a, b, preferred_element_type=F32)


def _dot_nt(a, b):
    return lax.dot_general(a, b, (((1,), (1,)), ((), ())), preferred_element_type=F32)


def _dot_tn(a, b):
    return lax.dot_general(a, b, (((0,), (0,)), ((), ())), preferred_element_type=F32)


def _rms(x, g):
    return x * lax.rsqrt(jnp.mean(x * x, axis=-1, keepdims=True) + EPS) * g


def _conv_silu(up_ref, base, w, L):
    out = w[0:1, :] * up_ref[base:base + L, :]
    for j in range(1, CONV_W):
        out = out + w[j:j + 1, :] * up_ref[base + j:base + j + L, :]
    return jax.nn.silu(out)


def _chunk_cumsum_lanes(x):
    pos = lax.broadcasted_iota(jnp.int32, x.shape, 1)
    shift = 1
    while shift < CHUNK:
        x = x + jnp.where(pos >= shift, pltpu.roll(x, shift, axis=1), 0.0)
        shift *= 2
    return x


def _in_proj_kernel(x_ref, g_ref, w_ref, wgt_ref, cw_ref, bgc_ref,
                    q_ref, k_ref, vm_ref, os_ref, qa_ref, ka_ref, va_ref, gc_ref, gr_ref, tail_ref,
                    up_ref, *, seq_tiles):
    i = pl.program_id(0)
    tm = x_ref.shape[0]
    xn = _rms(x_ref[...], g_ref[...]).astype(BF16)
    z = _dot(xn, w_ref[...])

    @pl.when(i % seq_tiles == 0)
    def _():
        up_ref[0:8, :] = jnp.zeros((8, QK_W), F32)

    up_ref[8:8 + tm, :] = z[:, 0:QK_W]
    qk = _conv_silu(up_ref, 5, cw_ref[...], tm)
    q_ref[...] = (qk[:, 0:QK_W // 2] * (M_DK ** -0.5)).astype(BF16)
    k_ref[...] = qk[:, QK_W // 2:QK_W]
    up_ref[5:8, :] = up_ref[5 + tm:8 + tm, :]

    @pl.when(i % seq_tiles == seq_tiles - 1)
    def _():
        tail_ref[0] = up_ref[tm:8 + tm, :]

    vm_ref[...] = z[:, 512:1024].astype(BF16)
    os_ref[...] = jax.nn.sigmoid(z[:, 1024:1536])
    qa_ref[...] = (z[:, 1536:2048] * (A_HD ** -0.5)).astype(BF16)
    ka_ref[...] = z[:, 2048:2176]
    va_ref[...] = z[:, 2176:2304]

    g = _dot_nt(wgt_ref[...], xn) + bgc_ref[...]
    row = lax.broadcasted_iota(jnp.int32, (N_GATES, CHUNK), 0)
    for c in range(tm // CHUNK):
        gch = g[:, c * CHUNK:(c + 1) * CHUNK]
        logs = jnp.where(row >= M_HEADS, jax.nn.log_sigmoid(gch), gch)
        out = jnp.where(row >= M_HEADS, _chunk_cumsum_lanes(logs), logs)
        gr_ref[:, c * CHUNK:(c + 1) * CHUNK] = out
        gc_ref[c * CHUNK:(c + 1) * CHUNK, :] = out.T


def _in_proj(x2d, g_pre, w_main, w_gt, conv_w, bg_col, batch):
    n = x2d.shape[0]
    tm = ROW_TILE
    seq_tiles = n // batch // tm
    row = lambda w: pl.BlockSpec((tm, w), lambda i: (i, 0))
    full = lambda a: pl.BlockSpec(a.shape, lambda i: (0,) * a.ndim)
    outs = [(QK_W // 2, BF16), (QK_W // 2, F32), (M_WIDTH, BF16), (M_WIDTH, F32), (A_WIDTH, BF16),
            (KV_W, F32), (KV_W, F32), (N_GATES, F32)]
    return pl.pallas_call(
        functools.partial(_in_proj_kernel, seq_tiles=seq_tiles),
        grid=(n // tm,),
        in_specs=[row(D_MODEL), full(g_pre), full(w_main), full(w_gt), full(conv_w), full(bg_col)],
        out_specs=[row(w) for w, _ in outs] + [pl.BlockSpec((N_GATES, tm), lambda i: (0, i)),
                                                pl.BlockSpec((1, 8, QK_W), lambda i: (i // seq_tiles, 0, 0))],
        out_shape=[jax.ShapeDtypeStruct((n, w), dt) for w, dt in outs]
        + [jax.ShapeDtypeStruct((N_GATES, n), F32), jax.ShapeDtypeStruct((batch, 8, QK_W), F32)],
        scratch_shapes=[pltpu.VMEM((8 + tm, QK_W), F32)],
        compiler_params=pltpu.CompilerParams(dimension_semantics=("arbitrary",),
                                             vmem_limit_bytes=VMEM_LIMIT_BYTES),
        name="in_proj",
    )(x2d, g_pre, w_main, w_gt, conv_w, bg_col)


def _mlstm_out(h, o_sig, g):
    return o_sig * (h * lax.rsqrt(jnp.mean(h * h, axis=-1, keepdims=True) + EPS) * g)


def _mixer_prompt_kernel(q_ref, k_ref, vm_ref, os_ref, qa_ref, ka_ref, va_ref, gc_ref, gr_ref, mn_ref, sink_ref,
                         ym_ref, att_ref, c_out, m_out,
                         c_st, m_st, kp_ref, vp_ref, bias0_ref, bias_ref, sinkrep_ref):
    ci = pl.program_id(1)
    L = CHUNK
    J = range(CHUNKS_PER_STEP)
    H = range(M_HEADS)
    KV = range(A_KV)
    t = lax.broadcasted_iota(jnp.int32, (L, L), 0)
    s = lax.broadcasted_iota(jnp.int32, (L, L), 1)

    @pl.when(ci == 0)
    def _():
        c_st[...] = jnp.zeros_like(c_st)
        m_st[...] = jnp.zeros_like(m_st)
        kp_ref[...] = jnp.zeros_like(kp_ref)
        vp_ref[...] = jnp.zeros_like(vp_ref)
        dist = (t - s).astype(F32)
        for kv in KV:
            for g in range(A_GROUP):
                hd = kv * A_GROUP + g
                rows = slice(g * L, (g + 1) * L)
                cur = jnp.where(s <= t, -SLOPES[hd] * dist, NEG_INF)
                bias_ref[kv, rows, 0:L] = jnp.where(s >= t, -SLOPES[hd] * (dist + float(WINDOW)), NEG_INF)
                bias_ref[kv, rows, L:2 * L] = cur
                bias0_ref[kv, rows, 0:L] = jnp.full((L, L), NEG_INF, F32)
                bias0_ref[kv, rows, L:2 * L] = cur
                sinkrep_ref[kv, rows, :] = jnp.broadcast_to(sink_ref[0:1, hd:hd + 1], (L, KV_W))

    @pl.when(ci == 1)
    def _():
        for kv in KV:
            bias0_ref[kv, :, 0:L] = bias_ref[kv, :, 0:L]

    rows = [slice(j * L, (j + 1) * L) for j in J]
    P = [(j, h) for j in J for h in H]

    ig_c = [gc_ref[rows[j], 0:M_HEADS] for j in J]
    b_c = [gc_ref[rows[j], M_HEADS:N_GATES] for j in J]
    a_r = [gr_ref[0:M_HEADS, rows[j]] - gr_ref[M_HEADS:N_GATES, rows[j]] for j in J]
    causal_bias = jnp.where(s <= t, 0.0, NEG_INF)
    hcol = lax.broadcasted_iota(jnp.int32, (L, M_HEADS), 1)
    m_prev = [m_st[0:1, 0:M_HEADS]]
    w_k, decay = [], []
    for j in J:
        b_last = b_c[j][L - 1:L, :]
        log_w = b_last - b_c[j] + ig_c[j]
        m_new = jnp.maximum(b_last + m_prev[j], jnp.max(log_w, axis=0, keepdims=True))
        w_k.append(jnp.exp(log_w - m_new))
        decay.append(jnp.exp(b_last + m_prev[j] - m_new))
        m_prev.append(m_new)
    m_st[0:1, 0:M_HEADS] = m_prev[-1]

    ones_v = jnp.ones((L, M_DV), BF16)
    qb = {p: q_ref[rows[p[0]], p[1] * M_DK:(p[1] + 1) * M_DK] for p in P}
    kf = {p: k_ref[rows[p[0]], p[1] * M_DK:(p[1] + 1) * M_DK] for p in P}
    kb = {p: kf[p].astype(BF16) for p in P}
    vext = {p: jnp.concatenate([vm_ref[rows[p[0]], p[1] * M_DV:(p[1] + 1) * M_DV], ones_v], axis=1) for p in P}
    qk_h = {p: _dot_nt(qb[p], kb[p]) for p in P}
    kw = {p: (kf[p] * w_k[p[0]][:, p[1]:p[1] + 1]).astype(BF16) for p in P}
    dc = {p: _dot_tn(kw[p], vext[p]) for p in P}
    log_d = {p: b_c[p[0]][:, p[1]:p[1] + 1] + a_r[p[0]][p[1]:p[1] + 1, :] + causal_bias for p in P}
    m_t, w_inter, emt = [], [], []
    for j in J:
        m_intra = jnp.full((L, M_HEADS), NEG_INF, F32)
        for h in H:
            m_intra = jnp.where(hcol == h, jnp.max(log_d[(j, h)], axis=-1, keepdims=True), m_intra)
        log_inter = b_c[j] + m_prev[j]
        m_t.append(jnp.maximum(log_inter, m_intra))
        w_inter.append(jnp.exp(log_inter - m_t[j]))
        emt.append(jnp.exp(-m_t[j]))
    s_h = {p: (qk_h[p] * jnp.exp(log_d[p] - m_t[p[0]][:, p[1]:p[1] + 1])).astype(BF16) for p in P}
    o_h = {p: _dot(s_h[p], vext[p]) for p in P}
    c_cur = [c_st[h] for h in H]
    qc = {}
    for j in J:
        for h in H:
            qc[(j, h)] = _dot(qb[(j, h)], c_cur[h].astype(BF16))
        c_cur = [decay[j][:, h:h + 1] * c_cur[h] + dc[(j, h)] for h in H]
    for h in H:
        c_st[h] = c_cur[h]
    for j, h in P:
        wi = w_inter[j][:, h:h + 1]
        num = o_h[(j, h)][:, 0:M_DV] + wi * qc[(j, h)][:, 0:M_DV]
        den = o_h[(j, h)][:, M_DV:2 * M_DV] + wi * qc[(j, h)][:, M_DV:2 * M_DV]
        hh = num / jnp.maximum(jnp.abs(den), emt[j][:, h:h + 1])
        cols = slice(h * M_DV, (h + 1) * M_DV)
        ym_ref[rows[j], cols] = _mlstm_out(hh, os_ref[rows[j], cols], mn_ref[:, cols])

    nk = (CHUNKS_PER_STEP + 1) * L
    low3 = lax.broadcasted_iota(jnp.int32, (nk, KV_W), 1) < A_HD
    low = lax.broadcasted_iota(jnp.int32, (L, KV_W), 1) < A_HD
    k_all = jnp.concatenate([kp_ref[...], ka_ref[...]], axis=0)
    v_all = jnp.concatenate([vp_ref[...], va_ref[...]], axis=0)
    ones_k = jnp.ones((nk, KV_W), BF16)
    kmask = [jnp.where(low3, k_all, 0.0).astype(BF16), jnp.where(low3, 0.0, k_all).astype(BF16)]
    vext_a = [jnp.concatenate([jnp.where(low3, v_all, 1.0).astype(BF16), ones_k], axis=1),
              jnp.concatenate([jnp.where(low3, 1.0, v_all).astype(BF16), ones_k], axis=1)]
    qst = [jnp.concatenate([qa_ref[rows[j], g * KV_W:(g + 1) * KV_W] for g in range(A_GROUP)], axis=0)
           for j in J]
    Q = [(j, kv) for j in J for kv in KV]
    bias_of = lambda j, kv: (bias0_ref if j == 0 else bias_ref)[kv]
    sc = {p: _dot_nt(qst[p[0]], kmask[p[1]][p[0] * L:(p[0] + 2) * L]) + bias_of(*p) for p in Q}
    m_a = {p: jnp.maximum(jnp.max(jnp.maximum(sc[p][:, 0:L], sc[p][:, L:2 * L]), axis=-1, keepdims=True),
                          sinkrep_ref[p[1]]) for p in Q}
    e = {p: jnp.concatenate([jnp.exp(sc[p][:, 0:L] - m_a[p]), jnp.exp(sc[p][:, L:2 * L] - m_a[p])],
                            axis=1).astype(BF16) for p in Q}
    o = {p: _dot(e[p], vext_a[p[1]][p[0] * L:(p[0] + 2) * L]) for p in Q}
    exs = {p: jnp.exp(sinkrep_ref[p[1]] - m_a[p]) for p in Q}
    for j in J:
        for g in range(A_GROUP):
            gr_ = slice(g * L, (g + 1) * L)
            o0, o1 = o[(j, 0)], o[(j, 1)]
            pv = jnp.where(low, o0[gr_, 0:KV_W], o1[gr_, 0:KV_W])
            den = jnp.where(low, o0[gr_, KV_W:2 * KV_W] + exs[(j, 0)][gr_, :],
                            o1[gr_, KV_W:2 * KV_W] + exs[(j, 1)][gr_, :])
            att_ref[rows[j], g * KV_W:(g + 1) * KV_W] = pv / den
    kp_ref[...] = ka_ref[(CHUNKS_PER_STEP - 1) * L:CHUNKS_PER_STEP * L, :]
    vp_ref[...] = va_ref[(CHUNKS_PER_STEP - 1) * L:CHUNKS_PER_STEP * L, :]

    @pl.when(ci == pl.num_programs(1) - 1)
    def _():
        c_out[0] = c_st[...]
        m_out[0] = m_st[0:1, 0:M_HEADS]


def _mixer_prompt(q, k, vm, os_, qa, ka, va, gc, gr, mlstm_norm, sinks, batch, seq):
    tr = CHUNKS_PER_STEP * CHUNK
    ns = seq // tr
    row = lambda w: pl.BlockSpec((tr, w), lambda b, c: (b * ns + c, 0))
    full = lambda a: pl.BlockSpec(a.shape, lambda b, c: (0,) * a.ndim)
    n = batch * seq
    bias_shape = (A_KV, A_GROUP * CHUNK, 2 * CHUNK)
    return pl.pallas_call(
        _mixer_prompt_kernel,
        grid=(batch, ns),
        in_specs=[row(QK_W // 2), row(QK_W // 2), row(M_WIDTH), row(M_WIDTH), row(A_WIDTH), row(KV_W), row(KV_W),
                  row(N_GATES), pl.BlockSpec((N_GATES, tr), lambda b, c: (0, b * ns + c)),
                  full(mlstm_norm), full(sinks)],
        out_specs=[row(M_WIDTH), row(A_WIDTH),
                   pl.BlockSpec((1, M_HEADS, M_DK, 2 * M_DV), lambda b, c: (b, 0, 0, 0)),
                   pl.BlockSpec((1, 1, M_HEADS), lambda b, c: (b, 0, 0))],
        out_shape=[jax.ShapeDtypeStruct((n, M_WIDTH), F32), jax.ShapeDtypeStruct((n, A_WIDTH), F32),
                   jax.ShapeDtypeStruct((batch, M_HEADS, M_DK, 2 * M_DV), F32),
                   jax.ShapeDtypeStruct((batch, 1, M_HEADS), F32)],
        scratch_shapes=[pltpu.VMEM((M_HEADS, M_DK, 2 * M_DV), F32),
                        pltpu.VMEM((8, 128), F32),
                        pltpu.VMEM((CHUNK, KV_W), F32), pltpu.VMEM((CHUNK, KV_W), F32),
                        pltpu.VMEM(bias_shape, F32), pltpu.VMEM(bias_shape, F32),
                        pltpu.VMEM((A_KV, A_GROUP * CHUNK, KV_W), F32)],
        compiler_params=pltpu.CompilerParams(dimension_semantics=("arbitrary", "arbitrary"),
                                             vmem_limit_bytes=VMEM_LIMIT_BYTES),
        name="mixer_prompt",
    )(q, k, vm, os_, qa, ka, va, gc, gr, mlstm_norm, sinks)


Z_W = QK_W + 2 * M_WIDTH + A_WIDTH + 2 * KV_W
Z_GROUPS = [(0, 512), (512, 1024), (1024, 1536), (1536, 2048), (2048, 2304)]
STREAM_FF_TILE = 1024
EARLY_GROUPS = 3


def _front_kernel(x_ref, g_ref, w_ref, wgt_ref, cw_ref, bgc_ref, mn_ref, sink_ref,
                  ym_ref, att_ref, c_out, m_out, ktail_ref, vtail_ref, ctail_ref,
                  z_ref, g_st, up_ref, c_st, m_st, kp_ref, vp_ref, bias0_ref, bias_ref, sinkrep_ref):
    ci = pl.program_id(1)
    last = pl.num_programs(1) - 1
    L = CHUNK
    J = range(CHUNKS_PER_STEP)
    H = range(M_HEADS)
    KV = range(A_KV)
    TR = CHUNKS_PER_STEP * L
    nxt = ci % 2
    cur = 1 - nxt
    t = lax.broadcasted_iota(jnp.int32, (L, L), 0)
    s = lax.broadcasted_iota(jnp.int32, (L, L), 1)

    @pl.when(ci == 0)
    def _():
        z_ref[cur] = jnp.zeros((TR, Z_W), F32)
        g_st[cur] = jnp.zeros((N_GATES, TR), F32)
        c_st[...] = jnp.zeros_like(c_st)
        m_st[...] = jnp.zeros_like(m_st)
        kp_ref[...] = jnp.zeros_like(kp_ref)
        vp_ref[...] = jnp.zeros_like(vp_ref)
        up_ref[0:8, :] = jnp.zeros((8, QK_W), F32)
        dist = (t - s).astype(F32)
        for kv in KV:
            for g in range(A_GROUP):
                hd = kv * A_GROUP + g
                rows = slice(g * L, (g + 1) * L)
                cur_b = jnp.where(s <= t, -SLOPES[hd] * dist, NEG_INF)
                bias_ref[kv, rows, 0:L] = jnp.where(s >= t, -SLOPES[hd] * (dist + float(WINDOW)), NEG_INF)
                bias_ref[kv, rows, L:2 * L] = cur_b
                bias0_ref[kv, rows, 0:L] = jnp.full((L, L), NEG_INF, F32)
                bias0_ref[kv, rows, L:2 * L] = cur_b
                sinkrep_ref[kv, rows, :] = jnp.broadcast_to(sink_ref[0:1, hd:hd + 1], (L, KV_W))

    @pl.when(ci == 1)
    def _():
        c_st[...] = jnp.zeros_like(c_st)
        m_st[...] = jnp.zeros_like(m_st)
        kp_ref[...] = jnp.zeros_like(kp_ref)
        vp_ref[...] = jnp.zeros_like(vp_ref)
        up_ref[0:8, :] = jnp.zeros((8, QK_W), F32)

    @pl.when(ci == 2)
    def _():
        for kv in KV:
            bias0_ref[kv, :, 0:L] = bias_ref[kv, :, 0:L]

    up_ref[8:8 + TR, :] = z_ref[cur, :, 0:QK_W]
    gcur = g_st[cur] + bgc_ref[...]

    xn = _rms(x_ref[...], g_ref[...]).astype(BF16)
    for lo, hi in Z_GROUPS:
        z_ref[nxt, :, lo:hi] = _dot(xn, w_ref[:, lo:hi])
    g_st[nxt] = _dot_nt(wgt_ref[...], xn)

    qk = _conv_silu(up_ref, 5, cw_ref[...], TR)
    up_ref[5:8, :] = up_ref[5 + TR:8 + TR, :]
    q_all = (qk[:, 0:QK_W // 2] * (M_DK ** -0.5)).astype(BF16)
    k_all_m = qk[:, QK_W // 2:QK_W]
    grow = lax.broadcasted_iota(jnp.int32, (N_GATES, L), 0)
    rows = [slice(j * L, (j + 1) * L) for j in J]
    gr, gc = [], []
    for j in J:
        gch = gcur[:, rows[j]]
        logs = jnp.where(grow >= M_HEADS, jax.nn.log_sigmoid(gch), gch)
        out = jnp.where(grow >= M_HEADS, _chunk_cumsum_lanes(logs), logs)
        gr.append(out)
        gc.append(out.T)

    P = [(j, h) for j in J for h in H]
    ig_c = [gc[j][:, 0:M_HEADS] for j in J]
    b_c = [gc[j][:, M_HEADS:N_GATES] for j in J]
    a_r = [gr[j][0:M_HEADS, :] - gr[j][M_HEADS:N_GATES, :] for j in J]
    causal_bias = jnp.where(s <= t, 0.0, NEG_INF)
    hcol = lax.broadcasted_iota(jnp.int32, (L, M_HEADS), 1)
    m_prev = [m_st[0:1, 0:M_HEADS]]
    w_k, decay = [], []
    for j in J:
        b_last = b_c[j][L - 1:L, :]
        log_w = b_last - b_c[j] + ig_c[j]
        m_new = jnp.maximum(b_last + m_prev[j], jnp.max(log_w, axis=0, keepdims=True))
        w_k.append(jnp.exp(log_w - m_new))
        decay.append(jnp.exp(b_last + m_prev[j] - m_new))
        m_prev.append(m_new)
    m_st[0:1, 0:M_HEADS] = m_prev[-1]

    ones_v = jnp.ones((L, M_DV), BF16)
    qb = {p: q_all[rows[p[0]], p[1] * M_DK:(p[1] + 1) * M_DK] for p in P}
    kf = {p: k_all_m[rows[p[0]], p[1] * M_DK:(p[1] + 1) * M_DK] for p in P}
    kb = {p: kf[p].astype(BF16) for p in P}
    vext = {p: jnp.concatenate([z_ref[cur, rows[p[0]], QK_W + p[1] * M_DV:QK_W + (p[1] + 1) * M_DV].astype(BF16),
                                ones_v], axis=1) for p in P}
    qk_h = {p: _dot_nt(qb[p], kb[p]) for p in P}
    kw = {p: (kf[p] * w_k[p[0]][:, p[1]:p[1] + 1]).astype(BF16) for p in P}
    dc = {p: _dot_tn(kw[p], vext[p]) for p in P}
    log_d = {p: b_c[p[0]][:, p[1]:p[1] + 1] + a_r[p[0]][p[1]:p[1] + 1, :] + causal_bias for p in P}
    m_t, w_inter, emt = [], [], []
    for j in J:
        m_intra = jnp.full((L, M_HEADS), NEG_INF, F32)
        for h in H:
            m_intra = jnp.where(hcol == h, jnp.max(log_d[(j, h)], axis=-1, keepdims=True), m_intra)
        log_inter = b_c[j] + m_prev[j]
        m_t.append(jnp.maximum(log_inter, m_intra))
        w_inter.append(jnp.exp(log_inter - m_t[j]))
        emt.append(jnp.exp(-m_t[j]))
    s_h = {p: (qk_h[p] * jnp.exp(log_d[p] - m_t[p[0]][:, p[1]:p[1] + 1])).astype(BF16) for p in P}
    c_cur = [c_st[h] for h in H]
    qc = {}
    for j in J:
        for h in H:
            qc[(j, h)] = _dot(qb[(j, h)], c_cur[h].astype(BF16))
        c_cur = [decay[j][:, h:h + 1] * c_cur[h] + dc[(j, h)] for h in H]
    for h in H:
        c_st[h] = c_cur[h]
    o_h = {p: _dot(s_h[p], vext[p]) for p in P}
    for j, h in P:
        wi = w_inter[j][:, h:h + 1]
        num = o_h[(j, h)][:, 0:M_DV] + wi * qc[(j, h)][:, 0:M_DV]
        den = o_h[(j, h)][:, M_DV:2 * M_DV] + wi * qc[(j, h)][:, M_DV:2 * M_DV]
        hh = num / jnp.maximum(jnp.abs(den), emt[j][:, h:h + 1])
        cols = slice(h * M_DV, (h + 1) * M_DV)
        o_sig = jax.nn.sigmoid(z_ref[cur, rows[j], QK_W + M_WIDTH + h * M_DV:QK_W + M_WIDTH + (h + 1) * M_DV])
        ym_ref[rows[j], cols] = _mlstm_out(hh, o_sig, mn_ref[:, cols])

    a0 = QK_W + 2 * M_WIDTH
    nk = (CHUNKS_PER_STEP + 1) * L
    low3 = lax.broadcasted_iota(jnp.int32, (nk, KV_W), 1) < A_HD
    low = lax.broadcasted_iota(jnp.int32, (L, KV_W), 1) < A_HD
    ka_cur = z_ref[cur, :, a0 + A_WIDTH:a0 + A_WIDTH + KV_W]
    va_cur = z_ref[cur, :, a0 + A_WIDTH + KV_W:a0 + A_WIDTH + 2 * KV_W]
    k_all = jnp.concatenate([kp_ref[...], ka_cur], axis=0)
    v_all = jnp.concatenate([vp_ref[...], va_cur], axis=0)
    ones_k = jnp.ones((nk, KV_W), BF16)
    kmask = [jnp.where(low3, k_all, 0.0).astype(BF16), jnp.where(low3, 0.0, k_all).astype(BF16)]
    vext_a = [jnp.concatenate([jnp.where(low3, v_all, 1.0).astype(BF16), ones_k], axis=1),
              jnp.concatenate([jnp.where(low3, 1.0, v_all).astype(BF16), ones_k], axis=1)]
    qst = [(jnp.concatenate([z_ref[cur, rows[j], a0 + g * KV_W:a0 + (g + 1) * KV_W] for g in range(A_GROUP)],
                            axis=0) * (A_HD ** -0.5)).astype(BF16) for j in J]
    Q = [(j, kv) for j in J for kv in KV]
    bias_of = lambda j, kv: (bias0_ref if j == 0 else bias_ref)[kv]
    sc = {p: _dot_nt(qst[p[0]], kmask[p[1]][p[0] * L:(p[0] + 2) * L]) + bias_of(*p) for p in Q}
    m_a = {p: jnp.maximum(jnp.max(jnp.maximum(sc[p][:, 0:L], sc[p][:, L:2 * L]), axis=-1, keepdims=True),
                          sinkrep_ref[p[1]]) for p in Q}
    e = {p: jnp.concatenate([jnp.exp(sc[p][:, 0:L] - m_a[p]), jnp.exp(sc[p][:, L:2 * L] - m_a[p])],
                            axis=1).astype(BF16) for p in Q}
    o = {p: _dot(e[p], vext_a[p[1]][p[0] * L:(p[0] + 2) * L]) for p in Q}
    exs = {p: jnp.exp(sinkrep_ref[p[1]] - m_a[p]) for p in Q}
    for j in J:
        for g in range(A_GROUP):
            gr_ = slice(g * L, (g + 1) * L)
            o0, o1 = o[(j, 0)], o[(j, 1)]
            pv = jnp.where(low, o0[gr_, 0:KV_W], o1[gr_, 0:KV_W])
            den = jnp.where(low, o0[gr_, KV_W:2 * KV_W] + exs[(j, 0)][gr_, :],
                            o1[gr_, KV_W:2 * KV_W] + exs[(j, 1)][gr_, :])
            att_ref[rows[j], g * KV_W:(g + 1) * KV_W] = pv / den
    kp_ref[...] = ka_cur[(CHUNKS_PER_STEP - 1) * L:CHUNKS_PER_STEP * L, :]
    vp_ref[...] = va_cur[(CHUNKS_PER_STEP - 1) * L:CHUNKS_PER_STEP * L, :]

    @pl.when(ci == last)
    def _():
        c_out[0] = c_st[...]
        m_out[0] = m_st[0:1, 0:M_HEADS]
        ktail_ref[0] = kp_ref[...]
        vtail_ref[0] = vp_ref[...]
        ctail_ref[0] = up_ref[0:8, :]


def _front(x2d, g_pre, w_main, w_gt, conv_w, bg_col, mlstm_norm, sinks, batch, seq):
    tr = CHUNKS_PER_STEP * CHUNK
    ns = seq // tr
    n = batch * seq
    full = lambda a: pl.BlockSpec(a.shape, lambda b, c: (0,) * a.ndim)
    resident = lambda a: pl.BlockSpec(a.shape, lambda b, c: (0,) * a.ndim, pipeline_mode=pl.Buffered(1))
    out_row = lambda w: pl.BlockSpec((tr, w), lambda b, c: (b * ns + jnp.maximum(c - 1, 0), 0))
    per_seq = lambda shape: pl.BlockSpec((1,) + shape, lambda b, c: (b,) + (0,) * len(shape))
    bias_shape = (A_KV, A_GROUP * CHUNK, 2 * CHUNK)
    return pl.pallas_call(
        _front_kernel,
        grid=(batch, ns + 1),
        in_specs=[pl.BlockSpec((tr, D_MODEL), lambda b, c: (b * ns + jnp.minimum(c, ns - 1), 0)),
                  full(g_pre), resident(w_main), full(w_gt), full(conv_w), full(bg_col),
                  full(mlstm_norm), full(sinks)],
        out_specs=[out_row(M_WIDTH), out_row(A_WIDTH),
                   per_seq((M_HEADS, M_DK, 2 * M_DV)), per_seq((1, M_HEADS)),
                   per_seq((CHUNK, KV_W)), per_seq((CHUNK, KV_W)), per_seq((8, QK_W))],
        out_shape=[jax.ShapeDtypeStruct((n, M_WIDTH), F32), jax.ShapeDtypeStruct((n, A_WIDTH), F32),
                   jax.ShapeDtypeStruct((batch, M_HEADS, M_DK, 2 * M_DV), F32),
                   jax.ShapeDtypeStruct((batch, 1, M_HEADS), F32),
                   jax.ShapeDtypeStruct((batch, CHUNK, KV_W), F32), jax.ShapeDtypeStruct((batch, CHUNK, KV_W), F32),
                   jax.ShapeDtypeStruct((batch, 8, QK_W), F32)],
        scratch_shapes=[pltpu.VMEM((2, tr, Z_W), F32), pltpu.VMEM((2, N_GATES, tr), F32),
                        pltpu.VMEM((8 + tr, QK_W), F32),
                        pltpu.VMEM((M_HEADS, M_DK, 2 * M_DV), F32), pltpu.VMEM((8, 128), F32),
                        pltpu.VMEM((CHUNK, KV_W), F32), pltpu.VMEM((CHUNK, KV_W), F32),
                        pltpu.VMEM(bias_shape, F32), pltpu.VMEM(bias_shape, F32),
                        pltpu.VMEM((A_KV, A_GROUP * CHUNK, KV_W), F32)],
        compiler_params=pltpu.CompilerParams(dimension_semantics=("arbitrary", "arbitrary"),
                                             vmem_limit_bytes=VMEM_LIMIT_BYTES),
        name="front",
    )(x2d, g_pre, w_main, w_gt, conv_w, bg_col, mlstm_norm, sinks)


def _layer_kernel(x_ref, xq_ref, p_ref, g_ref, w_ref, wgt_ref, cw_ref, bgc_ref, mn_ref, sink_ref,
                  an_ref, g_post_ref, g_fpre_ref, g_fpost_ref, wout_ref, wup_ref, wdown_ref, wpg_ref, wpp_ref,
                  y_ref, c_out, m_out, ktail_ref, vtail_ref, ctail_ref,
                  z_ref, g_st, y_st, up_ref, c_st, m_st, kp_ref, vp_ref, bias0_ref, bias_ref, sinkrep_ref):
    ci = pl.program_id(1)
    n_tiles = pl.num_programs(1) - 2
    L = CHUNK
    J = range(CHUNKS_PER_STEP)
    H = range(M_HEADS)
    KV = range(A_KV)
    TR = CHUNKS_PER_STEP * L
    nxt = ci % 2
    cur = 1 - nxt
    t = lax.broadcasted_iota(jnp.int32, (L, L), 0)
    s = lax.broadcasted_iota(jnp.int32, (L, L), 1)

    def reset_state():
        c_st[...] = jnp.zeros_like(c_st)
        m_st[...] = jnp.zeros_like(m_st)
        kp_ref[...] = jnp.zeros_like(kp_ref)
        vp_ref[...] = jnp.zeros_like(vp_ref)
        up_ref[0:8, :] = jnp.zeros((8, QK_W), F32)

    @pl.when(ci == 0)
    def _():
        z_ref[cur] = jnp.zeros((TR, Z_W), F32)
        g_st[cur] = jnp.zeros((N_GATES, TR), F32)
        y_st[cur] = jnp.zeros((TR, D_MODEL), BF16)
        reset_state()
        dist = (t - s).astype(F32)
        for kv in KV:
            for g in range(A_GROUP):
                hd = kv * A_GROUP + g
                rows = slice(g * L, (g + 1) * L)
                cur_b = jnp.where(s <= t, -SLOPES[hd] * dist, NEG_INF)
                bias_ref[kv, rows, 0:L] = jnp.where(s >= t, -SLOPES[hd] * (dist + float(WINDOW)), NEG_INF)
                bias_ref[kv, rows, L:2 * L] = cur_b
                bias0_ref[kv, rows, 0:L] = jnp.full((L, L), NEG_INF, F32)
                bias0_ref[kv, rows, L:2 * L] = cur_b
                sinkrep_ref[kv, rows, :] = jnp.broadcast_to(sink_ref[0:1, hd:hd + 1], (L, KV_W))

    @pl.when(ci == 1)
    def _():
        reset_state()

    @pl.when(ci == 2)
    def _():
        for kv in KV:
            bias0_ref[kv, :, 0:L] = bias_ref[kv, :, 0:L]

    up_ref[8:8 + TR, :] = z_ref[cur, :, 0:QK_W]
    gcur = g_st[cur] + bgc_ref[...]

    xn = _rms(x_ref[...], g_ref[...]).astype(BF16)
    for lo, hi in Z_GROUPS:
        z_ref[nxt, :, lo:hi] = _dot(xn, w_ref[:, lo:hi])
    g_st[nxt] = _dot_nt(wgt_ref[...], xn)

    x1 = xq_ref[...] + _rms(_dot(y_st[cur], wout_ref[...]), g_post_ref[...])
    u = _rms(x1, g_fpre_ref[...]).astype(BF16)

    def ffn_tile(j):
        cols = slice(j * FF_TILE, (j + 1) * FF_TILE)
        hid = jnp.square(jnp.maximum(_dot(u, wup_ref[:, cols]), 0.0)).astype(BF16)
        return _dot(hid, wdown_ref[cols, :])

    qk = _conv_silu(up_ref, 5, cw_ref[...], TR)
    up_ref[5:8, :] = up_ref[5 + TR:8 + TR, :]
    q_all = (qk[:, 0:QK_W // 2] * (M_DK ** -0.5)).astype(BF16)
    k_all_m = qk[:, QK_W // 2:QK_W]
    grow = lax.broadcasted_iota(jnp.int32, (N_GATES, L), 0)
    rows = [slice(j * L, (j + 1) * L) for j in J]
    gr, gc = [], []
    for j in J:
        gch = gcur[:, rows[j]]
        logs = jnp.where(grow >= M_HEADS, jax.nn.log_sigmoid(gch), gch)
        out = jnp.where(grow >= M_HEADS, _chunk_cumsum_lanes(logs), logs)
        gr.append(out)
        gc.append(out.T)
    P = [(j, h) for j in J for h in H]
    ig_c = [gc[j][:, 0:M_HEADS] for j in J]
    b_c = [gc[j][:, M_HEADS:N_GATES] for j in J]
    a_r = [gr[j][0:M_HEADS, :] - gr[j][M_HEADS:N_GATES, :] for j in J]
    causal_bias = jnp.where(s <= t, 0.0, NEG_INF)
    hcol = lax.broadcasted_iota(jnp.int32, (L, M_HEADS), 1)
    m_prev = [m_st[0:1, 0:M_HEADS]]
    w_k, decay = [], []
    for j in J:
        b_last = b_c[j][L - 1:L, :]
        log_w = b_last - b_c[j] + ig_c[j]
        m_new = jnp.maximum(b_last + m_prev[j], jnp.max(log_w, axis=0, keepdims=True))
        w_k.append(jnp.exp(log_w - m_new))
        decay.append(jnp.exp(b_last + m_prev[j] - m_new))
        m_prev.append(m_new)
    m_st[0:1, 0:M_HEADS] = m_prev[-1]

    ones_v = jnp.ones((L, M_DV), BF16)
    qb = {p: q_all[rows[p[0]], p[1] * M_DK:(p[1] + 1) * M_DK] for p in P}
    kf = {p: k_all_m[rows[p[0]], p[1] * M_DK:(p[1] + 1) * M_DK] for p in P}
    kb = {p: kf[p].astype(BF16) for p in P}
    vext = {p: jnp.concatenate([z_ref[cur, rows[p[0]], QK_W + p[1] * M_DV:QK_W + (p[1] + 1) * M_DV].astype(BF16),
                                ones_v], axis=1) for p in P}
    qk_h = {p: _dot_nt(qb[p], kb[p]) for p in P}
    kw = {p: (kf[p] * w_k[p[0]][:, p[1]:p[1] + 1]).astype(BF16) for p in P}
    dc = {p: _dot_tn(kw[p], vext[p]) for p in P}

    f = ffn_tile(0)

    log_d = {p: b_c[p[0]][:, p[1]:p[1] + 1] + a_r[p[0]][p[1]:p[1] + 1, :] + causal_bias for p in P}
    m_t, w_inter, emt = [], [], []
    for j in J:
        m_intra = jnp.full((L, M_HEADS), NEG_INF, F32)
        for h in H:
            m_intra = jnp.where(hcol == h, jnp.max(log_d[(j, h)], axis=-1, keepdims=True), m_intra)
        log_inter = b_c[j] + m_prev[j]
        m_t.append(jnp.maximum(log_inter, m_intra))
        w_inter.append(jnp.exp(log_inter - m_t[j]))
        emt.append(jnp.exp(-m_t[j]))
    s_h = {p: (qk_h[p] * jnp.exp(log_d[p] - m_t[p[0]][:, p[1]:p[1] + 1])).astype(BF16) for p in P}
    c_cur = [c_st[h] for h in H]
    qc = {}
    for j in J:
        for h in H:
            qc[(j, h)] = _dot(qb[(j, h)], c_cur[h].astype(BF16))
        c_cur = [decay[j][:, h:h + 1] * c_cur[h] + dc[(j, h)] for h in H]
    for h in H:
        c_st[h] = c_cur[h]

    f = f + ffn_tile(1)

    o_h = {p: _dot(s_h[p], vext[p]) for p in P}
    for j, h in P:
        wi = w_inter[j][:, h:h + 1]
        num = o_h[(j, h)][:, 0:M_DV] + wi * qc[(j, h)][:, 0:M_DV]
        den = o_h[(j, h)][:, M_DV:2 * M_DV] + wi * qc[(j, h)][:, M_DV:2 * M_DV]
        hh = num / jnp.maximum(jnp.abs(den), emt[j][:, h:h + 1])
        cols = slice(h * M_DV, (h + 1) * M_DV)
        o_sig = jax.nn.sigmoid(z_ref[cur, rows[j], QK_W + M_WIDTH + h * M_DV:QK_W + M_WIDTH + (h + 1) * M_DV])
        y_st[nxt, rows[j], cols] = _mlstm_out(hh, o_sig, mn_ref[:, cols]).astype(BF16)

    a0 = QK_W + 2 * M_WIDTH
    nk = (CHUNKS_PER_STEP + 1) * L
    low3 = lax.broadcasted_iota(jnp.int32, (nk, KV_W), 1) < A_HD
    low = lax.broadcasted_iota(jnp.int32, (L, KV_W), 1) < A_HD
    ka_cur = z_ref[cur, :, a0 + A_WIDTH:a0 + A_WIDTH + KV_W]
    va_cur = z_ref[cur, :, a0 + A_WIDTH + KV_W:a0 + A_WIDTH + 2 * KV_W]
    k_all = jnp.concatenate([kp_ref[...], ka_cur], axis=0)
    v_all = jnp.concatenate([vp_ref[...], va_cur], axis=0)
    ones_k = jnp.ones((nk, KV_W), BF16)
    kmask = [jnp.where(low3, k_all, 0.0).astype(BF16), jnp.where(low3, 0.0, k_all).astype(BF16)]
    vext_a = [jnp.concatenate([jnp.where(low3, v_all, 1.0).astype(BF16), ones_k], axis=1),
              jnp.concatenate([jnp.where(low3, 1.0, v_all).astype(BF16), ones_k], axis=1)]
    qst = [(jnp.concatenate([z_ref[cur, rows[j], a0 + g * KV_W:a0 + (g + 1) * KV_W] for g in range(A_GROUP)],
                            axis=0) * (A_HD ** -0.5)).astype(BF16) for j in J]
    Q = [(j, kv) for j in J for kv in KV]
    bias_of = lambda j, kv: (bias0_ref if j == 0 else bias_ref)[kv]
    sc = {p: _dot_nt(qst[p[0]], kmask[p[1]][p[0] * L:(p[0] + 2) * L]) + bias_of(*p) for p in Q}

    f = f + ffn_tile(2)

    m_a = {p: jnp.maximum(jnp.max(jnp.maximum(sc[p][:, 0:L], sc[p][:, L:2 * L]), axis=-1, keepdims=True),
                          sinkrep_ref[p[1]]) for p in Q}
    e = {p: jnp.concatenate([jnp.exp(sc[p][:, 0:L] - m_a[p]), jnp.exp(sc[p][:, L:2 * L] - m_a[p])],
                            axis=1).astype(BF16) for p in Q}
    o = {p: _dot(e[p], vext_a[p[1]][p[0] * L:(p[0] + 2) * L]) for p in Q}

    f = f + ffn_tile(3)

    exs = {p: jnp.exp(sinkrep_ref[p[1]] - m_a[p]) for p in Q}
    for j in J:
        att = []
        for g in range(A_GROUP):
            gr_ = slice(g * L, (g + 1) * L)
            o0, o1 = o[(j, 0)], o[(j, 1)]
            pv = jnp.where(low, o0[gr_, 0:KV_W], o1[gr_, 0:KV_W])
            den = jnp.where(low, o0[gr_, KV_W:2 * KV_W] + exs[(j, 0)][gr_, :],
                            o1[gr_, KV_W:2 * KV_W] + exs[(j, 1)][gr_, :])
            att.append(pv / den)
        ssq = att[0] * att[0]
        for g in range(1, A_GROUP):
            ssq = ssq + att[g] * att[g]
        scale = lax.rsqrt(jnp.sum(ssq, axis=-1, keepdims=True) * (1.0 / A_WIDTH) + EPS)
        for g in range(A_GROUP):
            cols = slice(g * KV_W, (g + 1) * KV_W)
            y_st[nxt, rows[j], M_WIDTH + g * KV_W:M_WIDTH + (g + 1) * KV_W] = (att[g] * scale * an_ref[:, cols]).astype(BF16)
    kp_ref[...] = ka_cur[(CHUNKS_PER_STEP - 1) * L:CHUNKS_PER_STEP * L, :]
    vp_ref[...] = va_cur[(CHUNKS_PER_STEP - 1) * L:CHUNKS_PER_STEP * L, :]

    x2 = x1 + _rms(f, g_fpost_ref[...])
    gate = jax.nn.sigmoid(_dot(x2.astype(BF16), wpg_ref[...]))
    y_ref[...] = x2 + gate * _dot(p_ref[...].astype(BF16), wpp_ref[...])

    @pl.when(ci == n_tiles)
    def _():
        c_out[0] = c_st[...]
        m_out[0] = m_st[0:1, 0:M_HEADS]
        ktail_ref[0] = kp_ref[...]
        vtail_ref[0] = vp_ref[...]
        ctail_ref[0] = up_ref[0:8, :]


def _layer(x2d, p2d, g_pre, w_main, w_gt, conv_w, bg_col, mlstm_norm, sinks, gains, weights, batch, seq):
    tr = CHUNKS_PER_STEP * CHUNK
    ns = seq // tr
    n = batch * seq
    full = lambda a: pl.BlockSpec(a.shape, lambda b, c: (0,) * a.ndim)
    resident = lambda a: pl.BlockSpec(a.shape, lambda b, c: (0,) * a.ndim, pipeline_mode=pl.Buffered(1))
    lag2 = lambda w: pl.BlockSpec((tr, w), lambda b, c: (b * ns + jnp.clip(c - 2, 0, ns - 1), 0))
    per_seq = lambda shape: pl.BlockSpec((1,) + shape, lambda b, c: (b,) + (0,) * len(shape))
    bias_shape = (A_KV, A_GROUP * CHUNK, 2 * CHUNK)
    return pl.pallas_call(
        _layer_kernel,
        grid=(batch, ns + 2),
        in_specs=[pl.BlockSpec((tr, D_MODEL), lambda b, c: (b * ns + jnp.minimum(c, ns - 1), 0)),
                  lag2(D_MODEL), lag2(P_DIM),
                  full(g_pre), resident(w_main), full(w_gt), full(conv_w), full(bg_col),
                  full(mlstm_norm), full(sinks)] + [full(g) for g in gains] + [resident(w) for w in weights],
        out_specs=[lag2(D_MODEL),
                   per_seq((M_HEADS, M_DK, 2 * M_DV)), per_seq((1, M_HEADS)),
                   per_seq((CHUNK, KV_W)), per_seq((CHUNK, KV_W)), per_seq((8, QK_W))],
        out_shape=[jax.ShapeDtypeStruct((n, D_MODEL), F32),
                   jax.ShapeDtypeStruct((batch, M_HEADS, M_DK, 2 * M_DV), F32),
                   jax.ShapeDtypeStruct((batch, 1, M_HEADS), F32),
                   jax.ShapeDtypeStruct((batch, CHUNK, KV_W), F32), jax.ShapeDtypeStruct((batch, CHUNK, KV_W), F32),
                   jax.ShapeDtypeStruct((batch, 8, QK_W), F32)],
        scratch_shapes=[pltpu.VMEM((2, tr, Z_W), F32), pltpu.VMEM((2, N_GATES, tr), F32),
                        pltpu.VMEM((2, tr, D_MODEL), BF16),
                        pltpu.VMEM((8 + tr, QK_W), F32),
                        pltpu.VMEM((M_HEADS, M_DK, 2 * M_DV), F32), pltpu.VMEM((8, 128), F32),
                        pltpu.VMEM((CHUNK, KV_W), F32), pltpu.VMEM((CHUNK, KV_W), F32),
                        pltpu.VMEM(bias_shape, F32), pltpu.VMEM(bias_shape, F32),
                        pltpu.VMEM((A_KV, A_GROUP * CHUNK, KV_W), F32)],
        compiler_params=pltpu.CompilerParams(dimension_semantics=("arbitrary", "arbitrary"),
                                             vmem_limit_bytes=VMEM_LIMIT_BYTES),
        name="layer",
    )(x2d, x2d, p2d, g_pre, w_main, w_gt, conv_w, bg_col, mlstm_norm, sinks, *gains, *weights)


def _stream_kernel(x_ref, xq_ref, p_ref, g_ref, w_ref, wgt_ref, cw_ref, bgc_ref, mn_ref, sink_ref,
                   an_ref, g_post_ref, g_fpre_ref, g_fpost_ref, wout_ref, wup_ref, wdown_ref, wpg_ref, wpp_ref,
                   y_ref, c_out, m_out, ktail_ref, vtail_ref, ctail_ref,
                   z_ref, g_st, y_st, up_ref, c_st, m_st, kp_ref, vp_ref, bias_ref, sinkrep_ref, *, seq_tiles):
    ci = pl.program_id(0)
    L = CHUNK
    J = range(CHUNKS_PER_STEP)
    H = range(M_HEADS)
    KV = range(A_KV)
    TR = CHUNKS_PER_STEP * L
    nxt = ci % 2
    cur = 1 - nxt
    mixed = ci - 1
    first = (mixed % seq_tiles) == 0
    t = lax.broadcasted_iota(jnp.int32, (L, L), 0)
    s = lax.broadcasted_iota(jnp.int32, (L, L), 1)

    @pl.when(ci == 0)
    def _():
        z_ref[cur] = jnp.zeros((TR, Z_W), F32)
        g_st[cur] = jnp.zeros((N_GATES, TR), F32)
        y_st[cur] = jnp.zeros((TR, D_MODEL), BF16)
        c_st[...] = jnp.zeros_like(c_st)
        m_st[...] = jnp.zeros_like(m_st)
        kp_ref[...] = jnp.zeros_like(kp_ref)
        vp_ref[...] = jnp.zeros_like(vp_ref)
        up_ref[0:8, :] = jnp.zeros((8, QK_W), F32)
        dist = (t - s).astype(F32)
        for kv in KV:
            for g in range(A_GROUP):
                hd = kv * A_GROUP + g
                rows = slice(g * L, (g + 1) * L)
                bias_ref[kv, rows, 0:L] = jnp.where(s >= t, -SLOPES[hd] * (dist + float(WINDOW)), NEG_INF)
                bias_ref[kv, rows, L:2 * L] = jnp.where(s <= t, -SLOPES[hd] * dist, NEG_INF)
                sinkrep_ref[kv, rows, :] = jnp.broadcast_to(sink_ref[0:1, hd:hd + 1], (L, KV_W))

    up_ref[5:8, :] = jnp.where(first, 0.0, up_ref[5:8, :])
    up_ref[8:8 + TR, :] = z_ref[cur, :, 0:QK_W]
    gcur = g_st[cur] + bgc_ref[...]

    mix = _dot(y_st[cur], wout_ref[...])
    xn = _rms(x_ref[...], g_ref[...]).astype(BF16)
    for lo, hi in Z_GROUPS[:EARLY_GROUPS]:
        z_ref[nxt, :, lo:hi] = _dot(xn, w_ref[:, lo:hi])
    x1 = xq_ref[...] + _rms(mix, g_post_ref[...])
    u = _rms(x1, g_fpre_ref[...]).astype(BF16)

    def ffn_tile(j):
        part = D_FF // 4
        acc = None
        for lo in range(j * part, (j + 1) * part, STREAM_FF_TILE):
            cols = slice(lo, lo + STREAM_FF_TILE)
            hid = jnp.square(jnp.maximum(_dot(u, wup_ref[:, cols]), 0.0)).astype(BF16)
            d = _dot(hid, wdown_ref[cols, :])
            acc = d if acc is None else acc + d
        return acc

    qk = _conv_silu(up_ref, 5, cw_ref[...], TR)
    up_ref[5:8, :] = up_ref[5 + TR:8 + TR, :]
    q_all = (qk[:, 0:QK_W // 2] * (M_DK ** -0.5)).astype(BF16)
    k_all_m = qk[:, QK_W // 2:QK_W]
    grow = lax.broadcasted_iota(jnp.int32, (N_GATES, L), 0)
    rows = [slice(j * L, (j + 1) * L) for j in J]
    gr, gc = [], []
    for j in J:
        gch = gcur[:, rows[j]]
        logs = jnp.where(grow >= M_HEADS, jax.nn.log_sigmoid(gch), gch)
        out = jnp.where(grow >= M_HEADS, _chunk_cumsum_lanes(logs), logs)
        gr.append(out)
        gc.append(out.T)
    P = [(j, h) for j in J for h in H]
    ig_c = [gc[j][:, 0:M_HEADS] for j in J]
    b_c = [gc[j][:, M_HEADS:N_GATES] for j in J]
    a_r = [gr[j][0:M_HEADS, :] - gr[j][M_HEADS:N_GATES, :] for j in J]
    causal_bias = jnp.where(s <= t, 0.0, NEG_INF)
    hcol = lax.broadcasted_iota(jnp.int32, (L, M_HEADS), 1)
    m_prev = [jnp.where(first, 0.0, m_st[0:1, 0:M_HEADS])]
    w_k, decay = [], []
    for j in J:
        b_last = b_c[j][L - 1:L, :]
        log_w = b_last - b_c[j] + ig_c[j]
        m_new = jnp.maximum(b_last + m_prev[j], jnp.max(log_w, axis=0, keepdims=True))
        w_k.append(jnp.exp(log_w - m_new))
        decay.append(jnp.exp(b_last + m_prev[j] - m_new))
        m_prev.append(m_new)
    m_st[0:1, 0:M_HEADS] = m_prev[-1]

    f = ffn_tile(0)

    ones_v = jnp.ones((L, M_DV), BF16)
    qb = {p: q_all[rows[p[0]], p[1] * M_DK:(p[1] + 1) * M_DK] for p in P}
    kf = {p: k_all_m[rows[p[0]], p[1] * M_DK:(p[1] + 1) * M_DK] for p in P}
    kb = {p: kf[p].astype(BF16) for p in P}
    vext = {p: jnp.concatenate([z_ref[cur, rows[p[0]], QK_W + p[1] * M_DV:QK_W + (p[1] + 1) * M_DV].astype(BF16),
                                ones_v], axis=1) for p in P}
    qk_h = {p: _dot_nt(qb[p], kb[p]) for p in P}
    kw = {p: (kf[p] * w_k[p[0]][:, p[1]:p[1] + 1]).astype(BF16) for p in P}
    dc = {p: _dot_tn(kw[p], vext[p]) for p in P}

    f = f + ffn_tile(1)

    log_d = {p: b_c[p[0]][:, p[1]:p[1] + 1] + a_r[p[0]][p[1]:p[1] + 1, :] + causal_bias for p in P}
    m_t, w_inter, emt = [], [], []
    for j in J:
        m_intra = jnp.full((L, M_HEADS), NEG_INF, F32)
        for h in H:
            m_intra = jnp.where(hcol == h, jnp.max(log_d[(j, h)], axis=-1, keepdims=True), m_intra)
        log_inter = b_c[j] + m_prev[j]
        m_t.append(jnp.maximum(log_inter, m_intra))
        w_inter.append(jnp.exp(log_inter - m_t[j]))
        emt.append(jnp.exp(-m_t[j]))
    s_h = {p: (qk_h[p] * jnp.exp(log_d[p] - m_t[p[0]][:, p[1]:p[1] + 1])).astype(BF16) for p in P}
    c_cur = [jnp.where(first, 0.0, c_st[h]) for h in H]
    qc = {}
    for j in J:
        for h in H:
            qc[(j, h)] = _dot(qb[(j, h)], c_cur[h].astype(BF16))
        c_cur = [decay[j][:, h:h + 1] * c_cur[h] + dc[(j, h)] for h in H]
    for h in H:
        c_st[h] = c_cur[h]

    f = f + ffn_tile(2)

    o_h = {p: _dot(s_h[p], vext[p]) for p in P}
    for j, h in P:
        wi = w_inter[j][:, h:h + 1]
        num = o_h[(j, h)][:, 0:M_DV] + wi * qc[(j, h)][:, 0:M_DV]
        den = o_h[(j, h)][:, M_DV:2 * M_DV] + wi * qc[(j, h)][:, M_DV:2 * M_DV]
        hh = num / jnp.maximum(jnp.abs(den), emt[j][:, h:h + 1])
        cols = slice(h * M_DV, (h + 1) * M_DV)
        o_sig = jax.nn.sigmoid(z_ref[cur, rows[j], QK_W + M_WIDTH + h * M_DV:QK_W + M_WIDTH + (h + 1) * M_DV])
        y_st[nxt, rows[j], cols] = _mlstm_out(hh, o_sig, mn_ref[:, cols]).astype(BF16)

    a0 = QK_W + 2 * M_WIDTH
    nk = (CHUNKS_PER_STEP + 1) * L
    low3 = lax.broadcasted_iota(jnp.int32, (nk, KV_W), 1) < A_HD
    low = lax.broadcasted_iota(jnp.int32, (L, KV_W), 1) < A_HD
    ka_cur = z_ref[cur, :, a0 + A_WIDTH:a0 + A_WIDTH + KV_W]
    va_cur = z_ref[cur, :, a0 + A_WIDTH + KV_W:a0 + A_WIDTH + 2 * KV_W]
    k_all = jnp.concatenate([kp_ref[...], ka_cur], axis=0)
    v_all = jnp.concatenate([vp_ref[...], va_cur], axis=0)
    ones_k = jnp.ones((nk, KV_W), BF16)
    kmask = [jnp.where(low3, k_all, 0.0).astype(BF16), jnp.where(low3, 0.0, k_all).astype(BF16)]
    vext_a = [jnp.concatenate([jnp.where(low3, v_all, 1.0).astype(BF16), ones_k], axis=1),
              jnp.concatenate([jnp.where(low3, 1.0, v_all).astype(BF16), ones_k], axis=1)]
    qst = [(jnp.concatenate([z_ref[cur, rows[j], a0 + g * KV_W:a0 + (g + 1) * KV_W] for g in range(A_GROUP)],
                            axis=0) * (A_HD ** -0.5)).astype(BF16) for j in J]
    Q = [(j, kv) for j in J for kv in KV]

    def bias_of(j, kv):
        if j > 0:
            return bias_ref[kv]
        return jnp.concatenate([jnp.where(first, NEG_INF, bias_ref[kv, :, 0:L]), bias_ref[kv, :, L:2 * L]], axis=1)

    sc = {p: _dot_nt(qst[p[0]], kmask[p[1]][p[0] * L:(p[0] + 2) * L]) + bias_of(*p) for p in Q}

    f = f + ffn_tile(3)

    m_a = {p: jnp.maximum(jnp.max(jnp.maximum(sc[p][:, 0:L], sc[p][:, L:2 * L]), axis=-1, keepdims=True),
                          sinkrep_ref[p[1]]) for p in Q}
    e = {p: jnp.concatenate([jnp.exp(sc[p][:, 0:L] - m_a[p]), jnp.exp(sc[p][:, L:2 * L] - m_a[p])],
                            axis=1).astype(BF16) for p in Q}
    o = {p: _dot(e[p], vext_a[p[1]][p[0] * L:(p[0] + 2) * L]) for p in Q}

    x2 = x1 + _rms(f, g_fpost_ref[...])
    gate = jax.nn.sigmoid(_dot(x2.astype(BF16), wpg_ref[...]))
    y_ref[...] = x2 + gate * _dot(p_ref[...].astype(BF16), wpp_ref[...])

    for lo, hi in Z_GROUPS[EARLY_GROUPS:]:
        z_ref[nxt, :, lo:hi] = _dot(xn, w_ref[:, lo:hi])
    g_st[nxt] = _dot_nt(wgt_ref[...], xn)

    exs = {p: jnp.exp(sinkrep_ref[p[1]] - m_a[p]) for p in Q}
    for j in J:
        att = []
        for g in range(A_GROUP):
            gr_ = slice(g * L, (g + 1) * L)
            o0, o1 = o[(j, 0)], o[(j, 1)]
            pv = jnp.where(low, o0[gr_, 0:KV_W], o1[gr_, 0:KV_W])
            den = jnp.where(low, o0[gr_, KV_W:2 * KV_W] + exs[(j, 0)][gr_, :],
                            o1[gr_, KV_W:2 * KV_W] + exs[(j, 1)][gr_, :])
            att.append(pv / den)
        ssq = att[0] * att[0]
        for g in range(1, A_GROUP):
            ssq = ssq + att[g] * att[g]
        scale = lax.rsqrt(jnp.sum(ssq, axis=-1, keepdims=True) * (1.0 / A_WIDTH) + EPS)
        for g in range(A_GROUP):
            cols = slice(g * KV_W, (g + 1) * KV_W)
            y_st[nxt, rows[j], M_WIDTH + g * KV_W:M_WIDTH + (g + 1) * KV_W] = (
                att[g] * scale * an_ref[:, cols]).astype(BF16)
    kp_ref[...] = ka_cur[(CHUNKS_PER_STEP - 1) * L:CHUNKS_PER_STEP * L, :]
    vp_ref[...] = va_cur[(CHUNKS_PER_STEP - 1) * L:CHUNKS_PER_STEP * L, :]

    @pl.when((ci > 0) & (ci % seq_tiles == 0))
    def _():
        c_out[0] = c_st[...]
        m_out[0] = m_st[0:1, 0:M_HEADS]
        ktail_ref[0] = kp_ref[...]
        vtail_ref[0] = vp_ref[...]
        ctail_ref[0] = up_ref[0:8, :]


def _stream(x2d, p2d, g_pre, w_main, w_gt, conv_w, bg_col, mlstm_norm, sinks, gains, weights, batch, seq):
    tr = CHUNKS_PER_STEP * CHUNK
    ns = seq // tr
    nt = batch * ns
    n = batch * seq
    full = lambda a: pl.BlockSpec(a.shape, lambda c: (0,) * a.ndim)
    resident = lambda a: pl.BlockSpec(a.shape, lambda c: (0,) * a.ndim, pipeline_mode=pl.Buffered(1))
    lag2 = lambda w: pl.BlockSpec((tr, w), lambda c: (jnp.clip(c - 2, 0, nt - 1), 0))
    per_seq = lambda shape: pl.BlockSpec((1,) + shape,
                                         lambda c: (jnp.clip((c - 1) // ns, 0, batch - 1),) + (0,) * len(shape))
    return pl.pallas_call(
        functools.partial(_stream_kernel, seq_tiles=ns),
        grid=(nt + 2,),
        in_specs=[pl.BlockSpec((tr, D_MODEL), lambda c: (jnp.minimum(c, nt - 1), 0)),
                  lag2(D_MODEL), lag2(P_DIM),
                  full(g_pre), resident(w_main), full(w_gt), full(conv_w), full(bg_col),
                  full(mlstm_norm), full(sinks)] + [full(g) for g in gains] + [resident(w) for w in weights],
        out_specs=[lag2(D_MODEL),
                   per_seq((M_HEADS, M_DK, 2 * M_DV)), per_seq((1, M_HEADS)),
                   per_seq((CHUNK, KV_W)), per_seq((CHUNK, KV_W)), per_seq((8, QK_W))],
        out_shape=[jax.ShapeDtypeStruct((n, D_MODEL), F32),
                   jax.ShapeDtypeStruct((batch, M_HEADS, M_DK, 2 * M_DV), F32),
                   jax.ShapeDtypeStruct((batch, 1, M_HEADS), F32),
                   jax.ShapeDtypeStruct((batch, CHUNK, KV_W), F32), jax.ShapeDtypeStruct((batch, CHUNK, KV_W), F32),
                   jax.ShapeDtypeStruct((batch, 8, QK_W), F32)],
        scratch_shapes=[pltpu.VMEM((2, tr, Z_W), F32), pltpu.VMEM((2, N_GATES, tr), F32),
                        pltpu.VMEM((2, tr, D_MODEL), BF16),
                        pltpu.VMEM((8 + tr, QK_W), F32),
                        pltpu.VMEM((M_HEADS, M_DK, 2 * M_DV), F32), pltpu.VMEM((8, 128), F32),
                        pltpu.VMEM((CHUNK, KV_W), F32), pltpu.VMEM((CHUNK, KV_W), F32),
                        pltpu.VMEM((A_KV, A_GROUP * CHUNK, 2 * CHUNK), F32),
                        pltpu.VMEM((A_KV, A_GROUP * CHUNK, KV_W), F32)],
        compiler_params=pltpu.CompilerParams(dimension_semantics=("arbitrary",),
                                             vmem_limit_bytes=VMEM_LIMIT_BYTES),
        name="prompt_layer",
    )(x2d, x2d, p2d, g_pre, w_main, w_gt, conv_w, bg_col, mlstm_norm, sinks, *gains, *weights)


def _in_proj_sample_kernel(x_ref, g_ref, wn_ref, wt_ref, wgt_ref,
                           qkn_ref, qa_ref, ka_ref, va_ref, qkt_ref, vt_ref, ot_ref, gt_ref):
    xn = _rms(x_ref[...], g_ref[...]).astype(BF16)
    zn = _dot(xn, wn_ref[...])
    qkn_ref[...] = zn[:, 0:QK_W]
    qa_ref[...] = zn[:, QK_W:QK_W + A_WIDTH]
    ka_ref[...] = zn[:, QK_W + A_WIDTH:QK_W + A_WIDTH + KV_W]
    va_ref[...] = zn[:, QK_W + A_WIDTH + KV_W:QK_W + A_WIDTH + 2 * KV_W]
    zt = _dot_nt(wt_ref[...], xn)
    qkt_ref[...] = zt[0:QK_W, :]
    vt_ref[...] = zt[QK_W:QK_W + M_WIDTH, :]
    ot_ref[...] = zt[QK_W + M_WIDTH:QK_W + 2 * M_WIDTH, :]
    gt_ref[...] = _dot_nt(wgt_ref[...], xn)


def _in_proj_sample(x2d, g_pre, w_nat, w_t, w_gt):
    n = x2d.shape[0]
    shapes = [(n, QK_W), (n, A_WIDTH), (n, KV_W), (n, KV_W), (QK_W, n), (M_WIDTH, n), (M_WIDTH, n), (N_GATES, n)]
    return pl.pallas_call(
        _in_proj_sample_kernel,
        out_shape=[jax.ShapeDtypeStruct(s, F32) for s in shapes],
        compiler_params=pltpu.CompilerParams(vmem_limit_bytes=VMEM_LIMIT_BYTES),
        name="in_proj_sample",
    )(x2d, g_pre, w_nat, w_t, w_gt)


def _mlstm_sample_kernel(qt_ref, kt_ref, vt_ref, ot_ref, gt_ref, bg_ref, cbq_ref, cbk_ref, cwq_ref, cwk_ref,
                         gain_ref, c_ref, n_ref, m_ref,
                         ym_ref, c_out, n_out, m_out,
                         ct_ref, q_s, ik_s, num_s):
    h = pl.program_id(0)
    NB = c_ref.shape[0]
    L = qt_ref.shape[1] // NB
    T = range(L)

    def conv(raw_ref, cb_ref, cw_ref):
        ups = [cb_ref[j] for j in range(CONV_W - 1)] + [raw_ref[:, t * NB:(t + 1) * NB] for t in T]
        w = [cw_ref[:, j:j + 1] for j in range(CONV_W)]
        outs = []
        for t in T:
            acc = w[0] * ups[t]
            for j in range(1, CONV_W):
                acc = acc + w[j] * ups[t + j]
            outs.append(jax.nn.silu(acc))
        return outs

    q = [x * (M_DK ** -0.5) for x in conv(qt_ref, cbq_ref, cwq_ref)]
    k = conv(kt_ref, cbk_ref, cwk_ref)

    ig_all = gt_ref[pl.ds(h, 1), :] + bg_ref[pl.ds(h, 1), :]
    lf_all = jax.nn.log_sigmoid(gt_ref[pl.ds(h + M_HEADS, 1), :] + bg_ref[pl.ds(h + M_HEADS, 1), :])
    m = m_ref[pl.ds(h, 1), :]
    f, ms = [], []
    n = n_ref[0]
    den = []
    for t in T:
        ig, lf = ig_all[:, t * NB:(t + 1) * NB], lf_all[:, t * NB:(t + 1) * NB]
        m_new = jnp.maximum(lf + m, ig)
        f_t = jnp.exp(lf + m - m_new)
        ik = jnp.exp(ig - m_new) * k[t]
        m = m_new
        n = f_t * n + ik
        f.append(f_t)
        ms.append(m_new)
        den.append(jnp.sum(q[t] * n, axis=0, keepdims=True))
        ik_s[t] = ik
        q_s[t] = q[t]
    m_out[0] = m
    n_out[0] = n

    c_dbv = pltpu.einshape("bdv->dbv", c_ref[:, 0, :, :])
    for d in range(M_DK):
        ct_ref[d * M_DV:(d + 1) * M_DV, :] = c_dbv[d].T

    VT = SAMPLE_V_TILE
    for vq in range(M_DV // VT):
        vts = [vt_ref[vq * VT:(vq + 1) * VT, t * NB:(t + 1) * NB] for t in T]

        def step(d, accs, vq=vq, vts=vts):
            r0 = pl.multiple_of(d * M_DV + vq * VT, VT)
            c = ct_ref[pl.ds(r0, VT), :]
            new = []
            for t in T:
                c = f[t] * c + ik_s[t, pl.ds(d, 1), :] * vts[t]
                new.append(accs[t] + q_s[t, pl.ds(d, 1), :] * c)
            ct_ref[pl.ds(r0, VT), :] = c
            return tuple(new)

        accs = lax.fori_loop(0, M_DK, step, tuple(jnp.zeros((VT, NB), F32) for _ in T), unroll=8)
        for t in T:
            num_s[t, vq * VT:(vq + 1) * VT, :] = accs[t]

    for t in T:
        hh = num_s[t] / jnp.maximum(jnp.abs(den[t]), jnp.exp(-ms[t]))
        hn = hh * lax.rsqrt(jnp.mean(hh * hh, axis=0, keepdims=True) + EPS) * gain_ref[...]
        y = jax.nn.sigmoid(ot_ref[:, t * NB:(t + 1) * NB]) * hn
        ym_ref[t * NB:(t + 1) * NB, :] = y.T

    for d in range(M_DK):
        c_out[:, d * M_DV:(d + 1) * M_DV] = ct_ref[d * M_DV:(d + 1) * M_DV, :].T


def _mlstm_sample(qkt, vt, ot, gt, bg_col, cbt, cwt, gain_col, c4d, nt, mt):
    nb = c4d.shape[0]
    n = qkt.shape[1]
    L = n // nb
    hblk = lambda rows, off: pl.BlockSpec((rows, n), lambda h, off=off: (h + off, 0))
    full = lambda a: pl.BlockSpec(a.shape, lambda h: (0,) * a.ndim)
    kq = QK_W // 2 // M_DK
    return pl.pallas_call(
        _mlstm_sample_kernel,
        grid=(M_HEADS,),
        in_specs=[hblk(M_DK, 0), hblk(M_DK, kq), hblk(M_DV, 0), hblk(M_DV, 0), full(gt), full(bg_col),
                  pl.BlockSpec((CONV_W - 1, M_DK, nb), lambda h: (0, h, 0)),
                  pl.BlockSpec((CONV_W - 1, M_DK, nb), lambda h: (0, h + kq, 0)),
                  pl.BlockSpec((M_DK, CONV_W), lambda h: (h, 0)),
                  pl.BlockSpec((M_DK, CONV_W), lambda h: (h + kq, 0)),
                  pl.BlockSpec((M_DV, 1), lambda h: (h, 0)),
                  pl.BlockSpec((nb, 1, M_DK, M_DV), lambda h: (0, h, 0, 0)),
                  pl.BlockSpec((1, M_DK, nb), lambda h: (h, 0, 0)),
                  full(mt)],
        out_specs=[pl.BlockSpec((n, M_DV), lambda h: (0, h)),
                   pl.BlockSpec((nb, M_DK * M_DV), lambda h: (0, h)),
                   pl.BlockSpec((1, M_DK, nb), lambda h: (h, 0, 0)),
                   pl.BlockSpec((1, 1, nb), lambda h: (h, 0, 0))],
        out_shape=[jax.ShapeDtypeStruct((n, M_WIDTH), F32),
                   jax.ShapeDtypeStruct((nb, M_HEADS * M_DK * M_DV), F32),
                   jax.ShapeDtypeStruct(nt.shape, F32), jax.ShapeDtypeStruct((M_HEADS, 1, nb), F32)],
        scratch_shapes=[pltpu.VMEM((M_DK * M_DV, nb), F32), pltpu.VMEM((L, M_DK, nb), F32),
                        pltpu.VMEM((L, M_DK, nb), F32), pltpu.VMEM((L, M_DV, nb), F32)],
        compiler_params=pltpu.CompilerParams(dimension_semantics=("arbitrary",),
                                             vmem_limit_bytes=VMEM_LIMIT_BYTES),
        name="mlstm_sample",
    )(qkt, qkt, vt, ot, gt, bg_col, cbt, cbt, cwt, cwt, gain_col, c4d, nt, mt)


def _attn_sample_kernel(qm_ref, kn_ref, vn_ref, kct_ref, vct_ref, sinkr_ref, sloper_ref,
                        att_ref, kct_out, vct_out, xk_ref, xv_ref):
    NB, L = kn_ref.shape[0], kn_ref.shape[1]
    R = A_KV * A_GROUP * L

    def bias(nkeys, offset):
        r = lax.broadcasted_iota(jnp.int32, (R, nkeys), 0).astype(F32)
        pos = lax.broadcasted_iota(jnp.int32, (R, nkeys), 1).astype(F32)
        tq = r - L * jnp.floor((r + 0.5) / L)
        dist = tq + offset - pos
        return jnp.where((dist >= 0.0) & (dist <= float(WINDOW)), -sloper_ref[...] * dist, NEG_INF)

    @pl.when(pl.program_id(0) == 0)
    def _():
        xk_ref[...] = jnp.zeros_like(xk_ref)
        xv_ref[...] = jnp.zeros_like(xv_ref)

    bias_c = bias(WINDOW, float(WINDOW))
    bias_n = bias(L, 0.0)
    sink = sinkr_ref[...]
    rq = lax.broadcasted_iota(jnp.int32, (R, KV_W), 0)
    cq = lax.broadcasted_iota(jnp.int32, (R, KV_W), 1)
    same_kv = (rq < A_GROUP * L) == (cq < A_HD)
    newest = lax.broadcasted_iota(jnp.int32, (KV_W, WINDOW), 1) >= WINDOW - L
    SEQ = range(NB)
    qh = [jnp.where(same_kv, qm_ref[b] * (A_HD ** -0.5), 0.0).astype(BF16) for b in SEQ]
    kt = [kct_ref[b] for b in SEQ]
    vt = [vct_ref[b] for b in SEQ]
    knb = [kn_ref[b].astype(BF16) for b in SEQ]
    vnb = [vn_ref[b].astype(BF16) for b in SEQ]
    sc_c = [_dot(qh[b], kt[b].astype(BF16)) + bias_c for b in SEQ]
    sc_n = [_dot_nt(qh[b], knb[b]) + bias_n for b in SEQ]
    m_a = [jnp.maximum(sink, jnp.maximum(jnp.max(sc_c[b], axis=-1, keepdims=True),
                                         jnp.max(sc_n[b], axis=-1, keepdims=True))) for b in SEQ]
    e_c = [jnp.exp(sc_c[b] - m_a[b]) for b in SEQ]
    e_n = [jnp.exp(sc_n[b] - m_a[b]) for b in SEQ]
    den_a = [jnp.exp(sink - m_a[b]) + jnp.sum(e_c[b], axis=-1, keepdims=True)
             + jnp.sum(e_n[b], axis=-1, keepdims=True) for b in SEQ]
    pv = [_dot_nt((e_c[b] / den_a[b]).astype(BF16), vt[b].astype(BF16))
          + _dot((e_n[b] / den_a[b]).astype(BF16), vnb[b]) for b in SEQ]
    for b in SEQ:
        att_ref[b] = pv[b]

    for b in SEQ:
        xk_ref[b, WINDOW - L:WINDOW, :] = kn_ref[b]
        xv_ref[b, WINDOW - L:WINDOW, :] = vn_ref[b]
    for b in SEQ:
        kct_out[b] = jnp.where(newest, xk_ref[b].T, pltpu.roll(kt[b], WINDOW - L, axis=1))
        vct_out[b] = jnp.where(newest, xv_ref[b].T, pltpu.roll(vt[b], WINDOW - L, axis=1))


def _attn_sample(qm, kn, vn, kct, vct, sink_rows, slope_rows):
    nb = qm.shape[0]
    ts = min(SAMPLE_SEQ_TILE, nb)
    per_b = lambda a: pl.BlockSpec((ts,) + a.shape[1:], lambda b: (b,) + (0,) * (a.ndim - 1))
    full = lambda a: pl.BlockSpec(a.shape, lambda b: (0,) * a.ndim)
    outs = [jax.ShapeDtypeStruct(a.shape, F32) for a in (qm, kct, vct)]
    return pl.pallas_call(
        _attn_sample_kernel,
        grid=(nb // ts,),
        in_specs=[per_b(a) for a in (qm, kn, vn, kct, vct)] + [full(sink_rows), full(slope_rows)],
        out_specs=[per_b(o) for o in outs],
        out_shape=outs,
        scratch_shapes=[pltpu.VMEM((ts, WINDOW, KV_W), F32), pltpu.VMEM((ts, WINDOW, KV_W), F32)],
        compiler_params=pltpu.CompilerParams(dimension_semantics=("arbitrary",),
                                             vmem_limit_bytes=VMEM_LIMIT_BYTES),
        name="attn_sample",
    )(qm, kn, vn, kct, vct, sink_rows, slope_rows)


def _post_kernel(x_ref, ym_ref, att_ref, p_ref, an_ref, g_post_ref, g_fpre_ref, g_fpost_ref,
                 wout_ref, wup_ref, wdown_ref, wpg_ref, wpp_ref, o_ref):
    y_a = _rms(att_ref[...], an_ref[...])
    y = jnp.concatenate([ym_ref[...], y_a], axis=-1).astype(BF16)
    x1 = x_ref[...] + _rms(_dot(y, wout_ref[...]), g_post_ref[...])
    u = _rms(x1, g_fpre_ref[...]).astype(BF16)
    f = jnp.zeros_like(x1)
    for j in range(D_FF // FF_TILE):
        cols = slice(j * FF_TILE, (j + 1) * FF_TILE)
        hid = jnp.square(jnp.maximum(_dot(u, wup_ref[:, cols]), 0.0)).astype(BF16)
        f = f + _dot(hid, wdown_ref[cols, :])
    x2 = x1 + _rms(f, g_fpost_ref[...])
    gate = jax.nn.sigmoid(_dot(x2.astype(BF16), wpg_ref[...]))
    o_ref[...] = x2 + gate * _dot(p_ref[...].astype(BF16), wpp_ref[...])


def _post(x2d, ym, att, p2d, gains, weights):
    n = x2d.shape[0]
    tm = min(ROW_TILE, n)
    row = lambda w: pl.BlockSpec((tm, w), lambda i: (i, 0))
    full = lambda a: pl.BlockSpec(a.shape, lambda i: (0,) * a.ndim)
    resident = lambda a: pl.BlockSpec(a.shape, lambda i: (0,) * a.ndim, pipeline_mode=pl.Buffered(1))
    return pl.pallas_call(
        _post_kernel,
        grid=(n // tm,),
        in_specs=[row(D_MODEL), row(M_WIDTH), row(A_WIDTH), row(P_DIM)]
        + [full(g) for g in gains] + [resident(w) for w in weights],
        out_specs=row(D_MODEL),
        out_shape=jax.ShapeDtypeStruct((n, D_MODEL), F32),
        compiler_params=pltpu.CompilerParams(dimension_semantics=("arbitrary",),
                                             vmem_limit_bytes=VMEM_LIMIT_BYTES),
        name="post",
    )(x2d, ym, att, p2d, *gains, *weights)


def _permute_heads(a, axis):
    shape = a.shape
    a = a.reshape(shape[:axis] + (A_HEADS, A_HD) + shape[axis + 1:])
    a = jnp.take(a, np.asarray(HEAD_ORDER), axis=axis)
    return a.reshape(shape)


def kernel(x_prompt, x_sample, p_prompt, p_sample, state_mlstm_c, state_mlstm_n, state_mlstm_m,
           state_mlstm_conv, cache_swa_k, cache_swa_v, norm_mix_pre, w_in, b_gates, conv_w,
           mlstm_norm, attn_sinks, attn_norm, w_out, norm_mix_post, norm_ffn_pre, w_up, w_down,
           norm_ffn_post, w_pgate, w_pproj):
    depth = w_in.shape[0]
    assert depth == 1, "single-layer decoder"
    B, T, _ = x_prompt.shape
    SB, ST, _ = x_sample.shape
    assert T % (CHUNKS_PER_STEP * CHUNK) == 0 and T % ROW_TILE == 0 and SB % SAMPLE_SEQ_TILE == 0
    i = 0

    wi = w_in[i]
    o_qk, o_vm, o_om, o_g, o_qa, o_ka, o_va = np.cumsum(
        [0, QK_W, M_WIDTH, M_WIDTH, N_GATES, A_WIDTH, KV_W])
    w_main = jnp.concatenate([wi[:, o_qk:o_g], _permute_heads(wi[:, o_qa:o_ka], 1), wi[:, o_ka:]],
                             axis=1).astype(BF16)
    w_gt = wi[:, o_g:o_qa].astype(BF16).T
    g_pre = norm_mix_pre[i].reshape(1, D_MODEL)
    bg_col = b_gates[i].reshape(N_GATES, 1)
    cw = conv_w[i]
    mn = mlstm_norm[i].reshape(1, M_WIDTH)
    sinks = attn_sinks[i].reshape(1, A_HEADS)
    gains = [_permute_heads(attn_norm[i], 0).reshape(1, A_WIDTH), norm_mix_post[i].reshape(1, D_MODEL),
             norm_ffn_pre[i].reshape(1, D_MODEL), norm_ffn_post[i].reshape(1, D_MODEL)]
    wo = jnp.concatenate([w_out[i][:M_WIDTH], _permute_heads(w_out[i][M_WIDTH:], 0)], axis=0)
    weights = [wo.astype(BF16), w_up[i].astype(BF16), w_down[i].astype(BF16),
               w_pgate[i].astype(BF16), w_pproj[i].astype(BF16)]

    xp = x_prompt.reshape(B * T, D_MODEL)
    y_p, cn_p, m_p, kt_p, vt_p, tail_p = _stream(xp, p_prompt[i].reshape(B * T, P_DIM), g_pre, w_main, w_gt, cw,
                                                 bg_col, mn, sinks, gains, weights, B, T)
    y_prompt = y_p.reshape(B, T, D_MODEL)
    c_p, n_p = cn_p[..., :M_DV], cn_p[..., M_DV]
    conv_p = tail_p[:, 8 - (CONV_W - 1):]
    k_p = kt_p.reshape(B, WINDOW, A_KV, A_HD)
    v_p = vt_p.reshape(B, WINDOW, A_KV, A_HD)

    xs = x_sample.transpose(1, 0, 2).reshape(ST * SB, D_MODEL)
    ps = p_sample[i].transpose(1, 0, 2).reshape(ST * SB, P_DIM)
    w_nat = jnp.concatenate([w_main[:, 0:QK_W], w_main[:, QK_W + 2 * M_WIDTH:]], axis=1)
    w_t = w_main[:, 0:QK_W + 2 * M_WIDTH].T
    qkn_s, qa_s, ka_s, va_s, qkt_s, vt_s, ot_s, gt_s = _in_proj_sample(xs, g_pre, w_nat, w_t, w_gt)
    cbt = state_mlstm_conv[i].transpose(1, 2, 0)
    ym_s, c_s2, nt_s, mt_s = _mlstm_sample(
        qkt_s, vt_s, ot_s, gt_s, bg_col, cbt, cw.T, mn.reshape(M_WIDTH, 1),
        state_mlstm_c[i], state_mlstm_n[i].transpose(1, 2, 0),
        state_mlstm_m[i].T)
    c_s = c_s2.reshape(SB, M_HEADS, M_DK, M_DV)
    n_s = nt_s.transpose(2, 0, 1)
    m_s = mt_s.reshape(M_HEADS, SB).T
    conv_s = qkn_s.reshape(ST, SB, QK_W)[ST - (CONV_W - 1):].transpose(1, 0, 2)

    qs = qa_s.reshape(ST, SB, A_GROUP, A_KV, A_HD).transpose(1, 3, 2, 0, 4).reshape(SB, A_HEADS * ST, A_HD)
    qm = jnp.concatenate([qs, qs], axis=-1)
    kn = ka_s.reshape(ST, SB, KV_W).transpose(1, 0, 2)
    vn = va_s.reshape(ST, SB, KV_W).transpose(1, 0, 2)
    sink_rows = jnp.repeat(attn_sinks[i], ST).reshape(A_HEADS * ST, 1)
    slope_rows = jnp.asarray(np.repeat(np.asarray(SLOPES, np.float32), ST).reshape(A_HEADS * ST, 1))
    kct = cache_swa_k[i].transpose(0, 2, 3, 1).reshape(SB, KV_W, WINDOW)
    vct = cache_swa_v[i].transpose(0, 2, 3, 1).reshape(SB, KV_W, WINDOW)
    att_s, kct_s, vct_s = _attn_sample(qm, kn, vn, kct, vct, sink_rows, slope_rows)
    att_d = att_s.reshape(SB, A_KV, A_GROUP * ST, A_KV, A_HD)
    att_k = jnp.stack([att_d[:, kv, :, kv, :] for kv in range(A_KV)], axis=1)
    att_s2 = att_k.reshape(SB, A_KV, A_GROUP, ST, A_HD).transpose(3, 0, 2, 1, 4).reshape(ST * SB, A_WIDTH)
    y_s = _post(xs, ym_s, att_s2, ps, gains, weights)
    y_sample = y_s.reshape(ST, SB, D_MODEL).transpose(1, 0, 2)
    k_s = kct_s.reshape(SB, A_KV, A_HD, WINDOW).transpose(0, 3, 1, 2)
    v_s = vct_s.reshape(SB, A_KV, A_HD, WINDOW).transpose(0, 3, 1, 2)

    stack = lambda a: a[None]
    return (y_prompt, y_sample,
            stack(c_p), stack(n_p), stack(m_p.reshape(B, M_HEADS)), stack(conv_p), stack(k_p), stack(v_p),
            stack(c_s), stack(n_s), stack(m_s), stack(conv_s), stack(k_s), stack(v_s))
```

```python
import functools

import numpy as np
import jax
import jax.numpy as jnp
from jax import lax
from jax.experimental import pallas as pl
from jax.experimental.pallas import tpu as pltpu

F32 = jnp.float32
BF16 = jnp.bfloat16

D_MODEL = 1024
M_WIDTH = 512
M_HEADS = 4
M_DV = 128
M_DK = 64
QK_W = 512
CONV_W = 4
CHUNK = 128
A_WIDTH = 512
A_HEADS = 8
A_HD = 64
A_KV = 2
A_GROUP = 4
KV_W = 128
WINDOW = 128
D_FF = 4096
P_DIM = 256
EPS = 1e-6
N_GATES = 2 * M_HEADS

VMEM_LIMIT_BYTES = 56 * 1024 * 1024
ROW_TILE = 512
FF_TILE = 1024
CHUNKS_PER_STEP = 2
SAMPLE_SEQ_TILE = 16
SAMPLE_V_TILE = 32

NEG_INF = float("-inf")
SLOPES = [2.0 ** (-8.0 * (h + 1) / A_HEADS) for h in range(A_HEADS)]
HEAD_ORDER = [kv * A_GROUP + g for g in range(A_GROUP) for kv in range(A_KV)]


def _dot(a, b):
    return jnp.dot(a, b, preferred_element_type=F32)


def _dot_nt(a, b):
    return lax.dot_general(a, b, (((1,), (1,)), ((), ())), preferred_element_type=F32)


def _dot_tn(a, b):
    return lax.dot_general(a, b, (((0,), (0,)), ((), ())), preferred_element_type=F32)


def _rms(x, g):
    return x * lax.rsqrt(jnp.mean(x * x, axis=-1, keepdims=True) + EPS) * g


def _conv_silu(up_ref, base, w, L):
    out = w[0:1, :] * up_ref[base:base + L, :]
    for j in range(1, CONV_W):
        out = out + w[j:j + 1, :] * up_ref[base + j:base + j + L, :]
    return jax.nn.silu(out)


def _chunk_cumsum_lanes(x):
    pos = lax.broadcasted_iota(jnp.int32, x.shape, 1)
    shift = 1
    while shift < CHUNK:
        x = x + jnp.where(pos >= shift, pltpu.roll(x, shift, axis=1), 0.0)
        shift *= 2
    return x


def _in_proj_kernel(x_ref, g_ref, w_ref, wgt_ref, cw_ref, bgc_ref,
                    q_ref, k_ref, vm_ref, os_ref, qa_ref, ka_ref, va_ref, gc_ref, gr_ref, tail_ref,
                    up_ref, *, seq_tiles):
    i = pl.program_id(0)
    tm = x_ref.shape[0]
    xn = _rms(x_ref[...], g_ref[...]).astype(BF16)
    z = _dot(xn, w_ref[...])

    @pl.when(i % seq_tiles == 0)
    def _():
        up_ref[0:8, :] = jnp.zeros((8, QK_W), F32)

    up_ref[8:8 + tm, :] = z[:, 0:QK_W]
    qk = _conv_silu(up_ref, 5, cw_ref[...], tm)
    q_ref[...] = (qk[:, 0:QK_W // 2] * (M_DK ** -0.5)).astype(BF16)
    k_ref[...] = qk[:, QK_W // 2:QK_W]
    up_ref[5:8, :] = up_ref[5 + tm:8 + tm, :]

    @pl.when(i % seq_tiles == seq_tiles - 1)
    def _():
        tail_ref[0] = up_ref[tm:8 + tm, :]

    vm_ref[...] = z[:, 512:1024].astype(BF16)
    os_ref[...] = jax.nn.sigmoid(z[:, 1024:1536])
    qa_ref[...] = (z[:, 1536:2048] * (A_HD ** -0.5)).astype(BF16)
    ka_ref[...] = z[:, 2048:2176]
    va_ref[...] = z[:, 2176:2304]

    g = _dot_nt(wgt_ref[...], xn) + bgc_ref[...]
    row = lax.broadcasted_iota(jnp.int32, (N_GATES, CHUNK), 0)
    for c in range(tm // CHUNK):
        gch = g[:, c * CHUNK:(c + 1) * CHUNK]
        logs = jnp.where(row >= M_HEADS, jax.nn.log_sigmoid(gch), gch)
        out = jnp.where(row >= M_HEADS, _chunk_cumsum_lanes(logs), logs)
        gr_ref[:, c * CHUNK:(c + 1) * CHUNK] = out
        gc_ref[c * CHUNK:(c + 1) * CHUNK, :] = out.T


def _in_proj(x2d, g_pre, w_main, w_gt, conv_w, bg_col, batch):
    n = x2d.shape[0]
    tm = ROW_TILE
    seq_tiles = n // batch // tm
    row = lambda w: pl.BlockSpec((tm, w), lambda i: (i, 0))
    full = lambda a: pl.BlockSpec(a.shape, lambda i: (0,) * a.ndim)
    outs = [(QK_W // 2, BF16), (QK_W // 2, F32), (M_WIDTH, BF16), (M_WIDTH, F32), (A_WIDTH, BF16),
            (KV_W, F32), (KV_W, F32), (N_GATES, F32)]
    return pl.pallas_call(
        functools.partial(_in_proj_kernel, seq_tiles=seq_tiles),
        grid=(n // tm,),
        in_specs=[row(D_MODEL), full(g_pre), full(w_main), full(w_gt), full(conv_w), full(bg_col)],
        out_specs=[row(w) for w, _ in outs] + [pl.BlockSpec((N_GATES, tm), lambda i: (0, i)),
                                                pl.BlockSpec((1, 8, QK_W), lambda i: (i // seq_tiles, 0, 0))],
        out_shape=[jax.ShapeDtypeStruct((n, w), dt) for w, dt in outs]
        + [jax.ShapeDtypeStruct((N_GATES, n), F32), jax.ShapeDtypeStruct((batch, 8, QK_W), F32)],
        scratch_shapes=[pltpu.VMEM((8 + tm, QK_W), F32)],
        compiler_params=pltpu.CompilerParams(dimension_semantics=("arbitrary",),
                                             vmem_limit_bytes=VMEM_LIMIT_BYTES),
        name="in_proj",
    )(x2d, g_pre, w_main, w_gt, conv_w, bg_col)


def _mlstm_out(h, o_sig, g):
    return o_sig * (h * lax.rsqrt(jnp.mean(h * h, axis=-1, keepdims=True) + EPS) * g)


def _mixer_prompt_kernel(q_ref, k_ref, vm_ref, os_ref, qa_ref, ka_ref, va_ref, gc_ref, gr_ref, mn_ref, sink_ref,
                         ym_ref, att_ref, c_out, m_out,
                         c_st, m_st, kp_ref, vp_ref, bias0_ref, bias_ref, sinkrep_ref):
    ci = pl.program_id(1)
    L = CHUNK
    J = range(CHUNKS_PER_STEP)
    H = range(M_HEADS)
    KV = range(A_KV)
    t = lax.broadcasted_iota(jnp.int32, (L, L), 0)
    s = lax.broadcasted_iota(jnp.int32, (L, L), 1)

    @pl.when(ci == 0)
    def _():
        c_st[...] = jnp.zeros_like(c_st)
        m_st[...] = jnp.zeros_like(m_st)
        kp_ref[...] = jnp.zeros_like(kp_ref)
        vp_ref[...] = jnp.zeros_like(vp_ref)
        dist = (t - s).astype(F32)
        for kv in KV:
            for g in range(A_GROUP):
                hd = kv * A_GROUP + g
                rows = slice(g * L, (g + 1) * L)
                cur = jnp.where(s <= t, -SLOPES[hd] * dist, NEG_INF)
                bias_ref[kv, rows, 0:L] = jnp.where(s >= t, -SLOPES[hd] * (dist + float(WINDOW)), NEG_INF)
                bias_ref[kv, rows, L:2 * L] = cur
                bias0_ref[kv, rows, 0:L] = jnp.full((L, L), NEG_INF, F32)
                bias0_ref[kv, rows, L:2 * L] = cur
                sinkrep_ref[kv, rows, :] = jnp.broadcast_to(sink_ref[0:1, hd:hd + 1], (L, KV_W))

    @pl.when(ci == 1)
    def _():
        for kv in KV:
            bias0_ref[kv, :, 0:L] = bias_ref[kv, :, 0:L]

    rows = [slice(j * L, (j + 1) * L) for j in J]
    P = [(j, h) for j in J for h in H]

    ig_c = [gc_ref[rows[j], 0:M_HEADS] for j in J]
    b_c = [gc_ref[rows[j], M_HEADS:N_GATES] for j in J]
    a_r = [gr_ref[0:M_HEADS, rows[j]] - gr_ref[M_HEADS:N_GATES, rows[j]] for j in J]
    causal_bias = jnp.where(s <= t, 0.0, NEG_INF)
    hcol = lax.broadcasted_iota(jnp.int32, (L, M_HEADS), 1)
    m_prev = [m_st[0:1, 0:M_HEADS]]
    w_k, decay = [], []
    for j in J:
        b_last = b_c[j][L - 1:L, :]
        log_w = b_last - b_c[j] + ig_c[j]
        m_new = jnp.maximum(b_last + m_prev[j], jnp.max(log_w, axis=0, keepdims=True))
        w_k.append(jnp.exp(log_w - m_new))
        decay.append(jnp.exp(b_last + m_prev[j] - m_new))
        m_prev.append(m_new)
    m_st[0:1, 0:M_HEADS] = m_prev[-1]

    ones_v = jnp.ones((L, M_DV), BF16)
    qb = {p: q_ref[rows[p[0]], p[1] * M_DK:(p[1] + 1) * M_DK] for p in P}
    kf = {p: k_ref[rows[p[0]], p[1] * M_DK:(p[1] + 1) * M_DK] for p in P}
    kb = {p: kf[p].astype(BF16) for p in P}
    vext = {p: jnp.concatenate([vm_ref[rows[p[0]], p[1] * M_DV:(p[1] + 1) * M_DV], ones_v], axis=1) for p in P}
    qk_h = {p: _dot_nt(qb[p], kb[p]) for p in P}
    kw = {p: (kf[p] * w_k[p[0]][:, p[1]:p[1] + 1]).astype(BF16) for p in P}
    dc = {p: _dot_tn(kw[p], vext[p]) for p in P}
    log_d = {p: b_c[p[0]][:, p[1]:p[1] + 1] + a_r[p[0]][p[1]:p[1] + 1, :] + causal_bias for p in P}
    m_t, w_inter, emt = [], [], []
    for j in J:
        m_intra = jnp.full((L, M_HEADS), NEG_INF, F32)
        for h in H:
            m_intra = jnp.where(hcol == h, jnp.max(log_d[(j, h)], axis=-1, keepdims=True), m_intra)
        log_inter = b_c[j] + m_prev[j]
        m_t.append(jnp.maximum(log_inter, m_intra))
        w_inter.append(jnp.exp(log_inter - m_t[j]))
        emt.append(jnp.exp(-m_t[j]))
    s_h = {p: (qk_h[p] * jnp.exp(log_d[p] - m_t[p[0]][:, p[1]:p[1] + 1])).astype(BF16) for p in P}
    o_h = {p: _dot(s_h[p], vext[p]) for p in P}
    c_cur = [c_st[h] for h in H]
    qc = {}
    for j in J:
        for h in H:
            qc[(j, h)] = _dot(qb[(j, h)], c_cur[h].astype(BF16))
        c_cur = [decay[j][:, h:h + 1] * c_cur[h] + dc[(j, h)] for h in H]
    for h in H:
        c_st[h] = c_cur[h]
    for j, h in P:
        wi = w_inter[j][:, h:h + 1]
        num = o_h[(j, h)][:, 0:M_DV] + wi * qc[(j, h)][:, 0:M_DV]
        den = o_h[(j, h)][:, M_DV:2 * M_DV] + wi * qc[(j, h)][:, M_DV:2 * M_DV]
        hh = num / jnp.maximum(jnp.abs(den), emt[j][:, h:h + 1])
        cols = slice(h * M_DV, (h + 1) * M_DV)
        ym_ref[rows[j], cols] = _mlstm_out(hh, os_ref[rows[j], cols], mn_ref[:, cols])

    nk = (CHUNKS_PER_STEP + 1) * L
    low3 = lax.broadcasted_iota(jnp.int32, (nk, KV_W), 1) < A_HD
    low = lax.broadcasted_iota(jnp.int32, (L, KV_W), 1) < A_HD
    k_all = jnp.concatenate([kp_ref[...], ka_ref[...]], axis=0)
    v_all = jnp.concatenate([vp_ref[...], va_ref[...]], axis=0)
    ones_k = jnp.ones((nk, KV_W), BF16)
    kmask = [jnp.where(low3, k_all, 0.0).astype(BF16), jnp.where(low3, 0.0, k_all).astype(BF16)]
    vext_a = [jnp.concatenate([jnp.where(low3, v_all, 1.0).astype(BF16), ones_k], axis=1),
              jnp.concatenate([jnp.where(low3, 1.0, v_all).astype(BF16), ones_k], axis=1)]
    qst = [jnp.concatenate([qa_ref[rows[j], g * KV_W:(g + 1) * KV_W] for g in range(A_GROUP)], axis=0)
           for j in J]
    Q = [(j, kv) for j in J for kv in KV]
    bias_of = lambda j, kv: (bias0_ref if j == 0 else bias_ref)[kv]
    sc = {p: _dot_nt(qst[p[0]], kmask[p[1]][p[0] * L:(p[0] + 2) * L]) + bias_of(*p) for p in Q}
    m_a = {p: jnp.maximum(jnp.max(jnp.maximum(sc[p][:, 0:L], sc[p][:, L:2 * L]), axis=-1, keepdims=True),
                          sinkrep_ref[p[1]]) for p in Q}
    e = {p: jnp.concatenate([jnp.exp(sc[p][:, 0:L] - m_a[p]), jnp.exp(sc[p][:, L:2 * L] - m_a[p])],
                            axis=1).astype(BF16) for p in Q}
    o = {p: _dot(e[p], vext_a[p[1]][p[0] * L:(p[0] + 2) * L]) for p in Q}
    exs = {p: jnp.exp(sinkrep_ref[p[1]] - m_a[p]) for p in Q}
    for j in J:
        for g in range(A_GROUP):
            gr_ = slice(g * L, (g + 1) * L)
            o0, o1 = o[(j, 0)], o[(j, 1)]
            pv = jnp.where(low, o0[gr_, 0:KV_W], o1[gr_, 0:KV_W])
            den = jnp.where(low, o0[gr_, KV_W:2 * KV_W] + exs[(j, 0)][gr_, :],
                            o1[gr_, KV_W:2 * KV_W] + exs[(j, 1)][gr_, :])
            att_ref[rows[j], g * KV_W:(g + 1) * KV_W] = pv / den
    kp_ref[...] = ka_ref[(CHUNKS_PER_STEP - 1) * L:CHUNKS_PER_STEP * L, :]
    vp_ref[...] = va_ref[(CHUNKS_PER_STEP - 1) * L:CHUNKS_PER_STEP * L, :]

    @pl.when(ci == pl.num_programs(1) - 1)
    def _():
        c_out[0] = c_st[...]
        m_out[0] = m_st[0:1, 0:M_HEADS]


def _mixer_prompt(q, k, vm, os_, qa, ka, va, gc, gr, mlstm_norm, sinks, batch, seq):
    tr = CHUNKS_PER_STEP * CHUNK
    ns = seq // tr
    row = lambda w: pl.BlockSpec((tr, w), lambda b, c: (b * ns + c, 0))
    full = lambda a: pl.BlockSpec(a.shape, lambda b, c: (0,) * a.ndim)
    n = batch * seq
    bias_shape = (A_KV, A_GROUP * CHUNK, 2 * CHUNK)
    return pl.pallas_call(
        _mixer_prompt_kernel,
        grid=(batch, ns),
        in_specs=[row(QK_W // 2), row(QK_W // 2), row(M_WIDTH), row(M_WIDTH), row(A_WIDTH), row(KV_W), row(KV_W),
                  row(N_GATES), pl.BlockSpec((N_GATES, tr), lambda b, c: (0, b * ns + c)),
                  full(mlstm_norm), full(sinks)],
        out_specs=[row(M_WIDTH), row(A_WIDTH),
                   pl.BlockSpec((1, M_HEADS, M_DK, 2 * M_DV), lambda b, c: (b, 0, 0, 0)),
                   pl.BlockSpec((1, 1, M_HEADS), lambda b, c: (b, 0, 0))],
        out_shape=[jax.ShapeDtypeStruct((n, M_WIDTH), F32), jax.ShapeDtypeStruct((n, A_WIDTH), F32),
                   jax.ShapeDtypeStruct((batch, M_HEADS, M_DK, 2 * M_DV), F32),
                   jax.ShapeDtypeStruct((batch, 1, M_HEADS), F32)],
        scratch_shapes=[pltpu.VMEM((M_HEADS, M_DK, 2 * M_DV), F32),
                        pltpu.VMEM((8, 128), F32),
                        pltpu.VMEM((CHUNK, KV_W), F32), pltpu.VMEM((CHUNK, KV_W), F32),
                        pltpu.VMEM(bias_shape, F32), pltpu.VMEM(bias_shape, F32),
                        pltpu.VMEM((A_KV, A_GROUP * CHUNK, KV_W), F32)],
        compiler_params=pltpu.CompilerParams(dimension_semantics=("arbitrary", "arbitrary"),
                                             vmem_limit_bytes=VMEM_LIMIT_BYTES),
        name="mixer_prompt",
    )(q, k, vm, os_, qa, ka, va, gc, gr, mlstm_norm, sinks)


Z_W = QK_W + 2 * M_WIDTH + A_WIDTH + 2 * KV_W
Z_GROUPS = [(0, 512), (512, 1024), (1024, 1536), (1536, 2048), (2048, 2304)]
STREAM_FF_TILE = 1024
EARLY_GROUPS = 3


def _front_kernel(x_ref, g_ref, w_ref, wgt_ref, cw_ref, bgc_ref, mn_ref, sink_ref,
                  ym_ref, att_ref, c_out, m_out, ktail_ref, vtail_ref, ctail_ref,
                  z_ref, g_st, up_ref, c_st, m_st, kp_ref, vp_ref, bias0_ref, bias_ref, sinkrep_ref):
    ci = pl.program_id(1)
    last = pl.num_programs(1) - 1
    L = CHUNK
    J = range(CHUNKS_PER_STEP)
    H = range(M_HEADS)
    KV = range(A_KV)
    TR = CHUNKS_PER_STEP * L
    nxt = ci % 2
    cur = 1 - nxt
    t = lax.broadcasted_iota(jnp.int32, (L, L), 0)
    s = lax.broadcasted_iota(jnp.int32, (L, L), 1)

    @pl.when(ci == 0)
    def _():
        z_ref[cur] = jnp.zeros((TR, Z_W), F32)
        g_st[cur] = jnp.zeros((N_GATES, TR), F32)
        c_st[...] = jnp.zeros_like(c_st)
        m_st[...] = jnp.zeros_like(m_st)
        kp_ref[...] = jnp.zeros_like(kp_ref)
        vp_ref[...] = jnp.zeros_like(vp_ref)
        up_ref[0:8, :] = jnp.zeros((8, QK_W), F32)
        dist = (t - s).astype(F32)
        for kv in KV:
            for g in range(A_GROUP):
                hd = kv * A_GROUP + g
                rows = slice(g * L, (g + 1) * L)
                cur_b = jnp.where(s <= t, -SLOPES[hd] * dist, NEG_INF)
                bias_ref[kv, rows, 0:L] = jnp.where(s >= t, -SLOPES[hd] * (dist + float(WINDOW)), NEG_INF)
                bias_ref[kv, rows, L:2 * L] = cur_b
                bias0_ref[kv, rows, 0:L] = jnp.full((L, L), NEG_INF, F32)
                bias0_ref[kv, rows, L:2 * L] = cur_b
                sinkrep_ref[kv, rows, :] = jnp.broadcast_to(sink_ref[0:1, hd:hd + 1], (L, KV_W))

    @pl.when(ci == 1)
    def _():
        c_st[...] = jnp.zeros_like(c_st)
        m_st[...] = jnp.zeros_like(m_st)
        kp_ref[...] = jnp.zeros_like(kp_ref)
        vp_ref[...] = jnp.zeros_like(vp_ref)
        up_ref[0:8, :] = jnp.zeros((8, QK_W), F32)

    @pl.when(ci == 2)
    def _():
        for kv in KV:
            bias0_ref[kv, :, 0:L] = bias_ref[kv, :, 0:L]

    up_ref[8:8 + TR, :] = z_ref[cur, :, 0:QK_W]
    gcur = g_st[cur] + bgc_ref[...]

    xn = _rms(x_ref[...], g_ref[...]).astype(BF16)
    for lo, hi in Z_GROUPS:
        z_ref[nxt, :, lo:hi] = _dot(xn, w_ref[:, lo:hi])
    g_st[nxt] = _dot_nt(wgt_ref[...], xn)

    qk = _conv_silu(up_ref, 5, cw_ref[...], TR)
    up_ref[5:8, :] = up_ref[5 + TR:8 + TR, :]
    q_all = (qk[:, 0:QK_W // 2] * (M_DK ** -0.5)).astype(BF16)
    k_all_m = qk[:, QK_W // 2:QK_W]
    grow = lax.broadcasted_iota(jnp.int32, (N_GATES, L), 0)
    rows = [slice(j * L, (j + 1) * L) for j in J]
    gr, gc = [], []
    for j in J:
        gch = gcur[:, rows[j]]
        logs = jnp.where(grow >= M_HEADS, jax.nn.log_sigmoid(gch), gch)
        out = jnp.where(grow >= M_HEADS, _chunk_cumsum_lanes(logs), logs)
        gr.append(out)
        gc.append(out.T)

    P = [(j, h) for j in J for h in H]
    ig_c = [gc[j][:, 0:M_HEADS] for j in J]
    b_c = [gc[j][:, M_HEADS:N_GATES] for j in J]
    a_r = [gr[j][0:M_HEADS, :] - gr[j][M_HEADS:N_GATES, :] for j in J]
    causal_bias = jnp.where(s <= t, 0.0, NEG_INF)
    hcol = lax.broadcasted_iota(jnp.int32, (L, M_HEADS), 1)
    m_prev = [m_st[0:1, 0:M_HEADS]]
    w_k, decay = [], []
    for j in J:
        b_last = b_c[j][L - 1:L, :]
        log_w = b_last - b_c[j] + ig_c[j]
        m_new = jnp.maximum(b_last + m_prev[j], jnp.max(log_w, axis=0, keepdims=True))
        w_k.append(jnp.exp(log_w - m_new))
        decay.append(jnp.exp(b_last + m_prev[j] - m_new))
        m_prev.append(m_new)
    m_st[0:1, 0:M_HEADS] = m_prev[-1]

    ones_v = jnp.ones((L, M_DV), BF16)
    qb = {p: q_all[rows[p[0]], p[1] * M_DK:(p[1] + 1) * M_DK] for p in P}
    kf = {p: k_all_m[rows[p[0]], p[1] * M_DK:(p[1] + 1) * M_DK] for p in P}
    kb = {p: kf[p].astype(BF16) for p in P}
    vext = {p: jnp.concatenate([z_ref[cur, rows[p[0]], QK_W + p[1] * M_DV:QK_W + (p[1] + 1) * M_DV].astype(BF16),
                                ones_v], axis=1) for p in P}
    qk_h = {p: _dot_nt(qb[p], kb[p]) for p in P}
    kw = {p: (kf[p] * w_k[p[0]][:, p[1]:p[1] + 1]).astype(BF16) for p in P}
    dc = {p: _dot_tn(kw[p], vext[p]) for p in P}
    log_d = {p: b_c[p[0]][:, p[1]:p[1] + 1] + a_r[p[0]][p[1]:p[1] + 1, :] + causal_bias for p in P}
    m_t, w_inter, emt = [], [], []
    for j in J:
        m_intra = jnp.full((L, M_HEADS), NEG_INF, F32)
        for h in H:
            m_intra = jnp.where(hcol == h, jnp.max(log_d[(j, h)], axis=-1, keepdims=True), m_intra)
        log_inter = b_c[j] + m_prev[j]
        m_t.append(jnp.maximum(log_inter, m_intra))
        w_inter.append(jnp.exp(log_inter - m_t[j]))
        emt.append(jnp.exp(-m_t[j]))
    s_h = {p: (qk_h[p] * jnp.exp(log_d[p] - m_t[p[0]][:, p[1]:p[1] + 1])).astype(BF16) for p in P}
    c_cur = [c_st[h] for h in H]
    qc = {}
    for j in J:
        for h in H:
            qc[(j, h)] = _dot(qb[(j, h)], c_cur[h].astype(BF16))
        c_cur = [decay[j][:, h:h + 1] * c_cur[h] + dc[(j, h)] for h in H]
    for h in H:
        c_st[h] = c_cur[h]
    o_h = {p: _dot(s_h[p], vext[p]) for p in P}
    for j, h in P:
        wi = w_inter[j][:, h:h + 1]
        num = o_h[(j, h)][:, 0:M_DV] + wi * qc[(j, h)][:, 0:M_DV]
        den = o_h[(j, h)][:, M_DV:2 * M_DV] + wi * qc[(j, h)][:, M_DV:2 * M_DV]
        hh = num / jnp.maximum(jnp.abs(den), emt[j][:, h:h + 1])
        cols = slice(h * M_DV, (h + 1) * M_DV)
        o_sig = jax.nn.sigmoid(z_ref[cur, rows[j], QK_W + M_WIDTH + h * M_DV:QK_W + M_WIDTH + (h + 1) * M_DV])
        ym_ref[rows[j], cols] = _mlstm_out(hh, o_sig, mn_ref[:, cols])

    a0 = QK_W + 2 * M_WIDTH
    nk = (CHUNKS_PER_STEP + 1) * L
    low3 = lax.broadcasted_iota(jnp.int32, (nk, KV_W), 1) < A_HD
    low = lax.broadcasted_iota(jnp.int32, (L, KV_W), 1) < A_HD
    ka_cur = z_ref[cur, :, a0 + A_WIDTH:a0 + A_WIDTH + KV_W]
    va_cur = z_ref[cur, :, a0 + A_WIDTH + KV_W:a0 + A_WIDTH + 2 * KV_W]
    k_all = jnp.concatenate([kp_ref[...], ka_cur], axis=0)
    v_all = jnp.concatenate([vp_ref[...], va_cur], axis=0)
    ones_k = jnp.ones((nk, KV_W), BF16)
    kmask = [jnp.where(low3, k_all, 0.0).astype(BF16), jnp.where(low3, 0.0, k_all).astype(BF16)]
    vext_a = [jnp.concatenate([jnp.where(low3, v_all, 1.0).astype(BF16), ones_k], axis=1),
              jnp.concatenate([jnp.where(low3, 1.0, v_all).astype(BF16), ones_k], axis=1)]
    qst = [(jnp.concatenate([z_ref[cur, rows[j], a0 + g * KV_W:a0 + (g + 1) * KV_W] for g in range(A_GROUP)],
                            axis=0) * (A_HD ** -0.5)).astype(BF16) for j in J]
    Q = [(j, kv) for j in J for kv in KV]
    bias_of = lambda j, kv: (bias0_ref if j == 0 else bias_ref)[kv]
    sc = {p: _dot_nt(qst[p[0]], kmask[p[1]][p[0] * L:(p[0] + 2) * L]) + bias_of(*p) for p in Q}
    m_a = {p: jnp.maximum(jnp.max(jnp.maximum(sc[p][:, 0:L], sc[p][:, L:2 * L]), axis=-1, keepdims=True),
                          sinkrep_ref[p[1]]) for p in Q}
    e = {p: jnp.concatenate([jnp.exp(sc[p][:, 0:L] - m_a[p]), jnp.exp(sc[p][:, L:2 * L] - m_a[p])],
                            axis=1).astype(BF16) for p in Q}
    o = {p: _dot(e[p], vext_a[p[1]][p[0] * L:(p[0] + 2) * L]) for p in Q}
    exs = {p: jnp.exp(sinkrep_ref[p[1]] - m_a[p]) for p in Q}
    for j in J:
        for g in range(A_GROUP):
            gr_ = slice(g * L, (g + 1) * L)
            o0, o1 = o[(j, 0)], o[(j, 1)]
            pv = jnp.where(low, o0[gr_, 0:KV_W], o1[gr_, 0:KV_W])
            den = jnp.where(low, o0[gr_, KV_W:2 * KV_W] + exs[(j, 0)][gr_, :],
                            o1[gr_, KV_W:2 * KV_W] + exs[(j, 1)][gr_, :])
            att_ref[rows[j], g * KV_W:(g + 1) * KV_W] = pv / den
    kp_ref[...] = ka_cur[(CHUNKS_PER_STEP - 1) * L:CHUNKS_PER_STEP * L, :]
    vp_ref[...] = va_cur[(CHUNKS_PER_STEP - 1) * L:CHUNKS_PER_STEP * L, :]

    @pl.when(ci == last)
    def _():
        c_out[0] = c_st[...]
        m_out[0] = m_st[0:1, 0:M_HEADS]
        ktail_ref[0] = kp_ref[...]
        vtail_ref[0] = vp_ref[...]
        ctail_ref[0] = up_ref[0:8, :]


def _front(x2d, g_pre, w_main, w_gt, conv_w, bg_col, mlstm_norm, sinks, batch, seq):
    tr = CHUNKS_PER_STEP * CHUNK
    ns = seq // tr
    n = batch * seq
    full = lambda a: pl.BlockSpec(a.shape, lambda b, c: (0,) * a.ndim)
    resident = lambda a: pl.BlockSpec(a.shape, lambda b, c: (0,) * a.ndim, pipeline_mode=pl.Buffered(1))
    out_row = lambda w: pl.BlockSpec((tr, w), lambda b, c: (b * ns + jnp.maximum(c - 1, 0), 0))
    per_seq = lambda shape: pl.BlockSpec((1,) + shape, lambda b, c: (b,) + (0,) * len(shape))
    bias_shape = (A_KV, A_GROUP * CHUNK, 2 * CHUNK)
    return pl.pallas_call(
        _front_kernel,
        grid=(batch, ns + 1),
        in_specs=[pl.BlockSpec((tr, D_MODEL), lambda b, c: (b * ns + jnp.minimum(c, ns - 1), 0)),
                  full(g_pre), resident(w_main), full(w_gt), full(conv_w), full(bg_col),
                  full(mlstm_norm), full(sinks)],
        out_specs=[out_row(M_WIDTH), out_row(A_WIDTH),
                   per_seq((M_HEADS, M_DK, 2 * M_DV)), per_seq((1, M_HEADS)),
                   per_seq((CHUNK, KV_W)), per_seq((CHUNK, KV_W)), per_seq((8, QK_W))],
        out_shape=[jax.ShapeDtypeStruct((n, M_WIDTH), F32), jax.ShapeDtypeStruct((n, A_WIDTH), F32),
                   jax.ShapeDtypeStruct((batch, M_HEADS, M_DK, 2 * M_DV), F32),
                   jax.ShapeDtypeStruct((batch, 1, M_HEADS), F32),
                   jax.ShapeDtypeStruct((batch, CHUNK, KV_W), F32), jax.ShapeDtypeStruct((batch, CHUNK, KV_W), F32),
                   jax.ShapeDtypeStruct((batch, 8, QK_W), F32)],
        scratch_shapes=[pltpu.VMEM((2, tr, Z_W), F32), pltpu.VMEM((2, N_GATES, tr), F32),
                        pltpu.VMEM((8 + tr, QK_W), F32),
                        pltpu.VMEM((M_HEADS, M_DK, 2 * M_DV), F32), pltpu.VMEM((8, 128), F32),
                        pltpu.VMEM((CHUNK, KV_W), F32), pltpu.VMEM((CHUNK, KV_W), F32),
                        pltpu.VMEM(bias_shape, F32), pltpu.VMEM(bias_shape, F32),
                        pltpu.VMEM((A_KV, A_GROUP * CHUNK, KV_W), F32)],
        compiler_params=pltpu.CompilerParams(dimension_semantics=("arbitrary", "arbitrary"),
                                             vmem_limit_bytes=VMEM_LIMIT_BYTES),
        name="front",
    )(x2d, g_pre, w_main, w_gt, conv_w, bg_col, mlstm_norm, sinks)


def _layer_kernel(x_ref, xq_ref, p_ref, g_ref, w_ref, wgt_ref, cw_ref, bgc_ref, mn_ref, sink_ref,
                  an_ref, g_post_ref, g_fpre_ref, g_fpost_ref, wout_ref, wup_ref, wdown_ref, wpg_ref, wpp_ref,
                  y_ref, c_out, m_out, ktail_ref, vtail_ref, ctail_ref,
                  z_ref, g_st, y_st, up_ref, c_st, m_st, kp_ref, vp_ref, bias0_ref, bias_ref, sinkrep_ref):
    ci = pl.program_id(1)
    n_tiles = pl.num_programs(1) - 2
    L = CHUNK
    J = range(CHUNKS_PER_STEP)
    H = range(M_HEADS)
    KV = range(A_KV)
    TR = CHUNKS_PER_STEP * L
    nxt = ci % 2
    cur = 1 - nxt
    t = lax.broadcasted_iota(jnp.int32, (L, L), 0)
    s = lax.broadcasted_iota(jnp.int32, (L, L), 1)

    def reset_state():
        c_st[...] = jnp.zeros_like(c_st)
        m_st[...] = jnp.zeros_like(m_st)
        kp_ref[...] = jnp.zeros_like(kp_ref)
        vp_ref[...] = jnp.zeros_like(vp_ref)
        up_ref[0:8, :] = jnp.zeros((8, QK_W), F32)

    @pl.when(ci == 0)
    def _():
        z_ref[cur] = jnp.zeros((TR, Z_W), F32)
        g_st[cur] = jnp.zeros((N_GATES, TR), F32)
        y_st[cur] = jnp.zeros((TR, D_MODEL), BF16)
        reset_state()
        dist = (t - s).astype(F32)
        for kv in KV:
            for g in range(A_GROUP):
                hd = kv * A_GROUP + g
                rows = slice(g * L, (g + 1) * L)
                cur_b = jnp.where(s <= t, -SLOPES[hd] * dist, NEG_INF)
                bias_ref[kv, rows, 0:L] = jnp.where(s >= t, -SLOPES[hd] * (dist + float(WINDOW)), NEG_INF)
                bias_ref[kv, rows, L:2 * L] = cur_b
                bias0_ref[kv, rows, 0:L] = jnp.full((L, L), NEG_INF, F32)
                bias0_ref[kv, rows, L:2 * L] = cur_b
                sinkrep_ref[kv, rows, :] = jnp.broadcast_to(sink_ref[0:1, hd:hd + 1], (L, KV_W))

    @pl.when(ci == 1)
    def _():
        reset_state()

    @pl.when(ci == 2)
    def _():
        for kv in KV:
            bias0_ref[kv, :, 0:L] = bias_ref[kv, :, 0:L]

    up_ref[8:8 + TR, :] = z_ref[cur, :, 0:QK_W]
    gcur = g_st[cur] + bgc_ref[...]

    xn = _rms(x_ref[...], g_ref[...]).astype(BF16)
    for lo, hi in Z_GROUPS:
        z_ref[nxt, :, lo:hi] = _dot(xn, w_ref[:, lo:hi])
    g_st[nxt] = _dot_nt(wgt_ref[...], xn)

    x1 = xq_ref[...] + _rms(_dot(y_st[cur], wout_ref[...]), g_post_ref[...])
    u = _rms(x1, g_fpre_ref[...]).astype(BF16)

    def ffn_tile(j):
        cols = slice(j * FF_TILE, (j + 1) * FF_TILE)
        hid = jnp.square(jnp.maximum(_dot(u, wup_ref[:, cols]), 0.0)).astype(BF16)
        return _dot(hid, wdown_ref[cols, :])

    qk = _conv_silu(up_ref, 5, cw_ref[...], TR)
    up_ref[5:8, :] = up_ref[5 + TR:8 + TR, :]
    q_all = (qk[:, 0:QK_W // 2] * (M_DK ** -0.5)).astype(BF16)
    k_all_m = qk[:, QK_W // 2:QK_W]
    grow = lax.broadcasted_iota(jnp.int32, (N_GATES, L), 0)
    rows = [slice(j * L, (j + 1) * L) for j in J]
    gr, gc = [], []
    for j in J:
        gch = gcur[:, rows[j]]
        logs = jnp.where(grow >= M_HEADS, jax.nn.log_sigmoid(gch), gch)
        out = jnp.where(grow >= M_HEADS, _chunk_cumsum_lanes(logs), logs)
        gr.append(out)
        gc.append(out.T)
    P = [(j, h) for j in J for h in H]
    ig_c = [gc[j][:, 0:M_HEADS] for j in J]
    b_c = [gc[j][:, M_HEADS:N_GATES] for j in J]
    a_r = [gr[j][0:M_HEADS, :] - gr[j][M_HEADS:N_GATES, :] for j in J]
    causal_bias = jnp.where(s <= t, 0.0, NEG_INF)
    hcol = lax.broadcasted_iota(jnp.int32, (L, M_HEADS), 1)
    m_prev = [m_st[0:1, 0:M_HEADS]]
    w_k, decay = [], []
    for j in J:
        b_last = b_c[j][L - 1:L, :]
        log_w = b_last - b_c[j] + ig_c[j]
        m_new = jnp.maximum(b_last + m_prev[j], jnp.max(log_w, axis=0, keepdims=True))
        w_k.append(jnp.exp(log_w - m_new))
        decay.append(jnp.exp(b_last + m_prev[j] - m_new))
        m_prev.append(m_new)
    m_st[0:1, 0:M_HEADS] = m_prev[-1]

    ones_v = jnp.ones((L, M_DV), BF16)
    qb = {p: q_all[rows[p[0]], p[1] * M_DK:(p[1] + 1) * M_DK] for p in P}
    kf = {p: k_all_m[rows[p[0]], p[1] * M_DK:(p[1] + 1) * M_DK] for p in P}
    kb = {p: kf[p].astype(BF16) for p in P}
    vext = {p: jnp.concatenate([z_ref[cur, rows[p[0]], QK_W + p[1] * M_DV:QK_W + (p[1] + 1) * M_DV].astype(BF16),
                                ones_v], axis=1) for p in P}
    qk_h = {p: _dot_nt(qb[p], kb[p]) for p in P}
    kw = {p: (kf[p] * w_k[p[0]][:, p[1]:p[1] + 1]).astype(BF16) for p in P}
    dc = {p: _dot_tn(kw[p], vext[p]) for p in P}

    f = ffn_tile(0)

    log_d = {p: b_c[p[0]][:, p[1]:p[1] + 1] + a_r[p[0]][p[1]:p[1] + 1, :] + causal_bias for p in P}
    m_t, w_inter, emt = [], [], []
    for j in J:
        m_intra = jnp.full((L, M_HEADS), NEG_INF, F32)
        for h in H:
            m_intra = jnp.where(hcol == h, jnp.max(log_d[(j, h)], axis=-1, keepdims=True), m_intra)
        log_inter = b_c[j] + m_prev[j]
        m_t.append(jnp.maximum(log_inter, m_intra))
        w_inter.append(jnp.exp(log_inter - m_t[j]))
        emt.append(jnp.exp(-m_t[j]))
    s_h = {p: (qk_h[p] * jnp.exp(log_d[p] - m_t[p[0]][:, p[1]:p[1] + 1])).astype(BF16) for p in P}
    c_cur = [c_st[h] for h in H]
    qc = {}
    for j in J:
        for h in H:
            qc[(j, h)] = _dot(qb[(j, h)], c_cur[h].astype(BF16))
        c_cur = [decay[j][:, h:h + 1] * c_cur[h] + dc[(j, h)] for h in H]
    for h in H:
        c_st[h] = c_cur[h]

    f = f + ffn_tile(1)

    o_h = {p: _dot(s_h[p], vext[p]) for p in P}
    for j, h in P:
        wi = w_inter[j][:, h:h + 1]
        num = o_h[(j, h)][:, 0:M_DV] + wi * qc[(j, h)][:, 0:M_DV]
        den = o_h[(j, h)][:, M_DV:2 * M_DV] + wi * qc[(j, h)][:, M_DV:2 * M_DV]
        hh = num / jnp.maximum(jnp.abs(den), emt[j][:, h:h + 1])
        cols = slice(h * M_DV, (h + 1) * M_DV)
        o_sig = jax.nn.sigmoid(z_ref[cur, rows[j], QK_W + M_WIDTH + h * M_DV:QK_W + M_WIDTH + (h + 1) * M_DV])
        y_st[nxt, rows[j], cols] = _mlstm_out(hh, o_sig, mn_ref[:, cols]).astype(BF16)

    a0 = QK_W + 2 * M_WIDTH
    nk = (CHUNKS_PER_STEP + 1) * L
    low3 = lax.broadcasted_iota(jnp.int32, (nk, KV_W), 1) < A_HD
    low = lax.broadcasted_iota(jnp.int32, (L, KV_W), 1) < A_HD
    ka_cur = z_ref[cur, :, a0 + A_WIDTH:a0 + A_WIDTH + KV_W]
    va_cur = z_ref[cur, :, a0 + A_WIDTH + KV_W:a0 + A_WIDTH + 2 * KV_W]
    k_all = jnp.concatenate([kp_ref[...], ka_cur], axis=0)
    v_all = jnp.concatenate([vp_ref[...], va_cur], axis=0)
    ones_k = jnp.ones((nk, KV_W), BF16)
    kmask = [jnp.where(low3, k_all, 0.0).astype(BF16), jnp.where(low3, 0.0, k_all).astype(BF16)]
    vext_a = [jnp.concatenate([jnp.where(low3, v_all, 1.0).astype(BF16), ones_k], axis=1),
              jnp.concatenate([jnp.where(low3, 1.0, v_all).astype(BF16), ones_k], axis=1)]
    qst = [(jnp.concatenate([z_ref[cur, rows[j], a0 + g * KV_W:a0 + (g + 1) * KV_W] for g in range(A_GROUP)],
                            axis=0) * (A_HD ** -0.5)).astype(BF16) for j in J]
    Q = [(j, kv) for j in J for kv in KV]
    bias_of = lambda j, kv: (bias0_ref if j == 0 else bias_ref)[kv]
    sc = {p: _dot_nt(qst[p[0]], kmask[p[1]][p[0] * L:(p[0] + 2) * L]) + bias_of(*p) for p in Q}

    f = f + ffn_tile(2)

    m_a = {p: jnp.maximum(jnp.max(jnp.maximum(sc[p][:, 0:L], sc[p][:, L:2 * L]), axis=-1, keepdims=True),
                          sinkrep_ref[p[1]]) for p in Q}
    e = {p: jnp.concatenate([jnp.exp(sc[p][:, 0:L] - m_a[p]), jnp.exp(sc[p][:, L:2 * L] - m_a[p])],
                            axis=1).astype(BF16) for p in Q}
    o = {p: _dot(e[p], vext_a[p[1]][p[0] * L:(p[0] + 2) * L]) for p in Q}

    f = f + ffn_tile(3)

    exs = {p: jnp.exp(sinkrep_ref[p[1]] - m_a[p]) for p in Q}
    for j in J:
        att = []
        for g in range(A_GROUP):
            gr_ = slice(g * L, (g + 1) * L)
            o0, o1 = o[(j, 0)], o[(j, 1)]
            pv = jnp.where(low, o0[gr_, 0:KV_W], o1[gr_, 0:KV_W])
            den = jnp.where(low, o0[gr_, KV_W:2 * KV_W] + exs[(j, 0)][gr_, :],
                            o1[gr_, KV_W:2 * KV_W] + exs[(j, 1)][gr_, :])
            att.append(pv / den)
        ssq = att[0] * att[0]
        for g in range(1, A_GROUP):
            ssq = ssq + att[g] * att[g]
        scale = lax.rsqrt(jnp.sum(ssq, axis=-1, keepdims=True) * (1.0 / A_WIDTH) + EPS)
        for g in range(A_GROUP):
            cols = slice(g * KV_W, (g + 1) * KV_W)
            y_st[nxt, rows[j], M_WIDTH + g * KV_W:M_WIDTH + (g + 1) * KV_W] = (att[g] * scale * an_ref[:, cols]).astype(BF16)
    kp_ref[...] = ka_cur[(CHUNKS_PER_STEP - 1) * L:CHUNKS_PER_STEP * L, :]
    vp_ref[...] = va_cur[(CHUNKS_PER_STEP - 1) * L:CHUNKS_PER_STEP * L, :]

    x2 = x1 + _rms(f, g_fpost_ref[...])
    gate = jax.nn.sigmoid(_dot(x2.astype(BF16), wpg_ref[...]))
    y_ref[...] = x2 + gate * _dot(p_ref[...].astype(BF16), wpp_ref[...])

    @pl.when(ci == n_tiles)
    def _():
        c_out[0] = c_st[...]
        m_out[0] = m_st[0:1, 0:M_HEADS]
        ktail_ref[0] = kp_ref[...]
        vtail_ref[0] = vp_ref[...]
        ctail_ref[0] = up_ref[0:8, :]


def _layer(x2d, p2d, g_pre, w_main, w_gt, conv_w, bg_col, mlstm_norm, sinks, gains, weights, batch, seq):
    tr = CHUNKS_PER_STEP * CHUNK
    ns = seq // tr
    n = batch * seq
    full = lambda a: pl.BlockSpec(a.shape, lambda b, c: (0,) * a.ndim)
    resident = lambda a: pl.BlockSpec(a.shape, lambda b, c: (0,) * a.ndim, pipeline_mode=pl.Buffered(1))
    lag2 = lambda w: pl.BlockSpec((tr, w), lambda b, c: (b * ns + jnp.clip(c - 2, 0, ns - 1), 0))
    per_seq = lambda shape: pl.BlockSpec((1,) + shape, lambda b, c: (b,) + (0,) * len(shape))
    bias_shape = (A_KV, A_GROUP * CHUNK, 2 * CHUNK)
    return pl.pallas_call(
        _layer_kernel,
        grid=(batch, ns + 2),
        in_specs=[pl.BlockSpec((tr, D_MODEL), lambda b, c: (b * ns + jnp.minimum(c, ns - 1), 0)),
                  lag2(D_MODEL), lag2(P_DIM),
                  full(g_pre), resident(w_main), full(w_gt), full(conv_w), full(bg_col),
                  full(mlstm_norm), full(sinks)] + [full(g) for g in gains] + [resident(w) for w in weights],
        out_specs=[lag2(D_MODEL),
                   per_seq((M_HEADS, M_DK, 2 * M_DV)), per_seq((1, M_HEADS)),
                   per_seq((CHUNK, KV_W)), per_seq((CHUNK, KV_W)), per_seq((8, QK_W))],
        out_shape=[jax.ShapeDtypeStruct((n, D_MODEL), F32),
                   jax.ShapeDtypeStruct((batch, M_HEADS, M_DK, 2 * M_DV), F32),
                   jax.ShapeDtypeStruct((batch, 1, M_HEADS), F32),
                   jax.ShapeDtypeStruct((batch, CHUNK, KV_W), F32), jax.ShapeDtypeStruct((batch, CHUNK, KV_W), F32),
                   jax.ShapeDtypeStruct((batch, 8, QK_W), F32)],
        scratch_shapes=[pltpu.VMEM((2, tr, Z_W), F32), pltpu.VMEM((2, N_GATES, tr), F32),
                        pltpu.VMEM((2, tr, D_MODEL), BF16),
                        pltpu.VMEM((8 + tr, QK_W), F32),
                        pltpu.VMEM((M_HEADS, M_DK, 2 * M_DV), F32), pltpu.VMEM((8, 128), F32),
                        pltpu.VMEM((CHUNK, KV_W), F32), pltpu.VMEM((CHUNK, KV_W), F32),
                        pltpu.VMEM(bias_shape, F32), pltpu.VMEM(bias_shape, F32),
                        pltpu.VMEM((A_KV, A_GROUP * CHUNK, KV_W), F32)],
        compiler_params=pltpu.CompilerParams(dimension_semantics=("arbitrary", "arbitrary"),
                                             vmem_limit_bytes=VMEM_LIMIT_BYTES),
        name="layer",
    )(x2d, x2d, p2d, g_pre, w_main, w_gt, conv_w, bg_col, mlstm_norm, sinks, *gains, *weights)


def _stream_kernel(x_ref, xq_ref, p_ref, g_ref, w_ref, wgt_ref, cw_ref, bgc_ref, mn_ref, sink_ref,
                   an_ref, g_post_ref, g_fpre_ref, g_fpost_ref, wout_ref, wup_ref, wdown_ref, wpg_ref, wpp_ref,
                   y_ref, c_out, m_out, ktail_ref, vtail_ref, ctail_ref,
                   z_ref, g_st, y_st, up_ref, c_st, m_st, kp_ref, vp_ref, bias_ref, sinkrep_ref, *, seq_tiles):
    ci = pl.program_id(0)
    L = CHUNK
    J = range(CHUNKS_PER_STEP)
    H = range(M_HEADS)
    KV = range(A_KV)
    TR = CHUNKS_PER_STEP * L
    nxt = ci % 2
    cur = 1 - nxt
    mixed = ci - 1
    first = (mixed % seq_tiles) == 0
    t = lax.broadcasted_iota(jnp.int32, (L, L), 0)
    s = lax.broadcasted_iota(jnp.int32, (L, L), 1)

    @pl.when(ci == 0)
    def _():
        z_ref[cur] = jnp.zeros((TR, Z_W), F32)
        g_st[cur] = jnp.zeros((N_GATES, TR), F32)
        y_st[cur] = jnp.zeros((TR, D_MODEL), BF16)
        c_st[...] = jnp.zeros_like(c_st)
        m_st[...] = jnp.zeros_like(m_st)
        kp_ref[...] = jnp.zeros_like(kp_ref)
        vp_ref[...] = jnp.zeros_like(vp_ref)
        up_ref[0:8, :] = jnp.zeros((8, QK_W), F32)
        dist = (t - s).astype(F32)
        for kv in KV:
            for g in range(A_GROUP):
                hd = kv * A_GROUP + g
                rows = slice(g * L, (g + 1) * L)
                bias_ref[kv, rows, 0:L] = jnp.where(s >= t, -SLOPES[hd] * (dist + float(WINDOW)), NEG_INF)
                bias_ref[kv, rows, L:2 * L] = jnp.where(s <= t, -SLOPES[hd] * dist, NEG_INF)
                sinkrep_ref[kv, rows, :] = jnp.broadcast_to(sink_ref[0:1, hd:hd + 1], (L, KV_W))

    up_ref[5:8, :] = jnp.where(first, 0.0, up_ref[5:8, :])
    up_ref[8:8 + TR, :] = z_ref[cur, :, 0:QK_W]
    gcur = g_st[cur] + bgc_ref[...]

    mix = _dot(y_st[cur], wout_ref[...])
    xn = _rms(x_ref[...], g_ref[...]).astype(BF16)
    for lo, hi in Z_GROUPS[:EARLY_GROUPS]:
        z_ref[nxt, :, lo:hi] = _dot(xn, w_ref[:, lo:hi])
    x1 = xq_ref[...] + _rms(mix, g_post_ref[...])
    u = _rms(x1, g_fpre_ref[...]).astype(BF16)

    def ffn_tile(j):
        part = D_FF // 4
        acc = None
        for lo in range(j * part, (j + 1) * part, STREAM_FF_TILE):
            cols = slice(lo, lo + STREAM_FF_TILE)
            hid = jnp.square(jnp.maximum(_dot(u, wup_ref[:, cols]), 0.0)).astype(BF16)
            d = _dot(hid, wdown_ref[cols, :])
            acc = d if acc is None else acc + d
        return acc

    qk = _conv_silu(up_ref, 5, cw_ref[...], TR)
    up_ref[5:8, :] = up_ref[5 + TR:8 + TR, :]
    q_all = (qk[:, 0:QK_W // 2] * (M_DK ** -0.5)).astype(BF16)
    k_all_m = qk[:, QK_W // 2:QK_W]
    grow = lax.broadcasted_iota(jnp.int32, (N_GATES, L), 0)
    rows = [slice(j * L, (j + 1) * L) for j in J]
    gr, gc = [], []
    for j in J:
        gch = gcur[:, rows[j]]
        logs = jnp.where(grow >= M_HEADS, jax.nn.log_sigmoid(gch), gch)
        out = jnp.where(grow >= M_HEADS, _chunk_cumsum_lanes(logs), logs)
        gr.append(out)
        gc.append(out.T)
    P = [(j, h) for j in J for h in H]
    ig_c = [gc[j][:, 0:M_HEADS] for j in J]
    b_c = [gc[j][:, M_HEADS:N_GATES] for j in J]
    a_r = [gr[j][0:M_HEADS, :] - gr[j][M_HEADS:N_GATES, :] for j in J]
    causal_bias = jnp.where(s <= t, 0.0, NEG_INF)
    hcol = lax.broadcasted_iota(jnp.int32, (L, M_HEADS), 1)
    m_prev = [jnp.where(first, 0.0, m_st[0:1, 0:M_HEADS])]
    w_k, decay = [], []
    for j in J:
        b_last = b_c[j][L - 1:L, :]
        log_w = b_last - b_c[j] + ig_c[j]
        m_new = jnp.maximum(b_last + m_prev[j], jnp.max(log_w, axis=0, keepdims=True))
        w_k.append(jnp.exp(log_w - m_new))
        decay.append(jnp.exp(b_last + m_prev[j] - m_new))
        m_prev.append(m_new)
    m_st[0:1, 0:M_HEADS] = m_prev[-1]

    f = ffn_tile(0)

    ones_v = jnp.ones((L, M_DV), BF16)
    qb = {p: q_all[rows[p[0]], p[1] * M_DK:(p[1] + 1) * M_DK] for p in P}
    kf = {p: k_all_m[rows[p[0]], p[1] * M_DK:(p[1] + 1) * M_DK] for p in P}
    kb = {p: kf[p].astype(BF16) for p in P}
    vext = {p: jnp.concatenate([z_ref[cur, rows[p[0]], QK_W + p[1] * M_DV:QK_W + (p[1] + 1) * M_DV].astype(BF16),
                                ones_v], axis=1) for p in P}
    qk_h = {p: _dot_nt(qb[p], kb[p]) for p in P}
    kw = {p: (kf[p] * w_k[p[0]][:, p[1]:p[1] + 1]).astype(BF16) for p in P}
    dc = {p: _dot_tn(kw[p], vext[p]) for p in P}

    f = f + ffn_tile(1)

    log_d = {p: b_c[p[0]][:, p[1]:p[1] + 1] + a_r[p[0]][p[1]:p[1] + 1, :] + causal_bias for p in P}
    m_t, w_inter, emt = [], [], []
    for j in J:
        m_intra = jnp.full((L, M_HEADS), NEG_INF, F32)
        for h in H:
            m_intra = jnp.where(hcol == h, jnp.max(log_d[(j, h)], axis=-1, keepdims=True), m_intra)
        log_inter = b_c[j] + m_prev[j]
        m_t.append(jnp.maximum(log_inter, m_intra))
        w_inter.append(jnp.exp(log_inter - m_t[j]))
        emt.append(jnp.exp(-m_t[j]))
    s_h = {p: (qk_h[p] * jnp.exp(log_d[p] - m_t[p[0]][:, p[1]:p[1] + 1])).astype(BF16) for p in P}
    c_cur = [jnp.where(first, 0.0, c_st[h]) for h in H]
    qc = {}
    for j in J:
        for h in H:
            qc[(j, h)] = _dot(qb[(j, h)], c_cur[h].astype(BF16))
        c_cur = [decay[j][:, h:h + 1] * c_cur[h] + dc[(j, h)] for h in H]
    for h in H:
        c_st[h] = c_cur[h]

    f = f + ffn_tile(2)

    o_h = {p: _dot(s_h[p], vext[p]) for p in P}
    for j, h in P:
        wi = w_inter[j][:, h:h + 1]
        num = o_h[(j, h)][:, 0:M_DV] + wi * qc[(j, h)][:, 0:M_DV]
        den = o_h[(j, h)][:, M_DV:2 * M_DV] + wi * qc[(j, h)][:, M_DV:2 * M_DV]
        hh = num / jnp.maximum(jnp.abs(den), emt[j][:, h:h + 1])
        cols = slice(h * M_DV, (h + 1) * M_DV)
        o_sig = jax.nn.sigmoid(z_ref[cur, rows[j], QK_W + M_WIDTH + h * M_DV:QK_W + M_WIDTH + (h + 1) * M_DV])
        y_st[nxt, rows[j], cols] = _mlstm_out(hh, o_sig, mn_ref[:, cols]).astype(BF16)

    a0 = QK_W + 2 * M_WIDTH
    nk = (CHUNKS_PER_STEP + 1) * L
    low3 = lax.broadcasted_iota(jnp.int32, (nk, KV_W), 1) < A_HD
    low = lax.broadcasted_iota(jnp.int32, (L, KV_W), 1) < A_HD
    ka_cur = z_ref[cur, :, a0 + A_WIDTH:a0 + A_WIDTH + KV_W]
    va_cur = z_ref[cur, :, a0 + A_WIDTH + KV_W:a0 + A_WIDTH + 2 * KV_W]
    k_all = jnp.concatenate([kp_ref[...], ka_cur], axis=0)
    v_all = jnp.concatenate([vp_ref[...], va_cur], axis=0)
    ones_k = jnp.ones((nk, KV_W), BF16)
    kmask = [jnp.where(low3, k_all, 0.0).astype(BF16), jnp.where(low3, 0.0, k_all).astype(BF16)]
    vext_a = [jnp.concatenate([jnp.where(low3, v_all, 1.0).astype(BF16), ones_k], axis=1),
              jnp.concatenate([jnp.where(low3, 1.0, v_all).astype(BF16), ones_k], axis=1)]
    qst = [(jnp.concatenate([z_ref[cur, rows[j], a0 + g * KV_W:a0 + (g + 1) * KV_W] for g in range(A_GROUP)],
                            axis=0) * (A_HD ** -0.5)).astype(BF16) for j in J]
    Q = [(j, kv) for j in J for kv in KV]

    def bias_of(j, kv):
        if j > 0:
            return bias_ref[kv]
        return jnp.concatenate([jnp.where(first, NEG_INF, bias_ref[kv, :, 0:L]), bias_ref[kv, :, L:2 * L]], axis=1)

    sc = {p: _dot_nt(qst[p[0]], kmask[p[1]][p[0] * L:(p[0] + 2) * L]) + bias_of(*p) for p in Q}

    f = f + ffn_tile(3)

    m_a = {p: jnp.maximum(jnp.max(jnp.maximum(sc[p][:, 0:L], sc[p][:, L:2 * L]), axis=-1, keepdims=True),
                          sinkrep_ref[p[1]]) for p in Q}
    e = {p: jnp.concatenate([jnp.exp(sc[p][:, 0:L] - m_a[p]), jnp.exp(sc[p][:, L:2 * L] - m_a[p])],
                            axis=1).astype(BF16) for p in Q}
    o = {p: _dot(e[p], vext_a[p[1]][p[0] * L:(p[0] + 2) * L]) for p in Q}

    x2 = x1 + _rms(f, g_fpost_ref[...])
    gate = jax.nn.sigmoid(_dot(x2.astype(BF16), wpg_ref[...]))
    y_ref[...] = x2 + gate * _dot(p_ref[...].astype(BF16), wpp_ref[...])

    for lo, hi in Z_GROUPS[EARLY_GROUPS:]:
        z_ref[nxt, :, lo:hi] = _dot(xn, w_ref[:, lo:hi])
    g_st[nxt] = _dot_nt(wgt_ref[...], xn)

    exs = {p: jnp.exp(sinkrep_ref[p[1]] - m_a[p]) for p in Q}
    for j in J:
        att = []
        for g in range(A_GROUP):
            gr_ = slice(g * L, (g + 1) * L)
            o0, o1 = o[(j, 0)], o[(j, 1)]
            pv = jnp.where(low, o0[gr_, 0:KV_W], o1[gr_, 0:KV_W])
            den = jnp.where(low, o0[gr_, KV_W:2 * KV_W] + exs[(j, 0)][gr_, :],
                            o1[gr_, KV_W:2 * KV_W] + exs[(j, 1)][gr_, :])
            att.append(pv / den)
        ssq = att[0] * att[0]
        for g in range(1, A_GROUP):
            ssq = ssq + att[g] * att[g]
        scale = lax.rsqrt(jnp.sum(ssq, axis=-1, keepdims=True) * (1.0 / A_WIDTH) + EPS)
        for g in range(A_GROUP):
            cols = slice(g * KV_W, (g + 1) * KV_W)
            y_st[nxt, rows[j], M_WIDTH + g * KV_W:M_WIDTH + (g + 1) * KV_W] = (
                att[g] * scale * an_ref[:, cols]).astype(BF16)
    kp_ref[...] = ka_cur[(CHUNKS_PER_STEP - 1) * L:CHUNKS_PER_STEP * L, :]
    vp_ref[...] = va_cur[(CHUNKS_PER_STEP - 1) * L:CHUNKS_PER_STEP * L, :]

    @pl.when((ci > 0) & (ci % seq_tiles == 0))
    def _():
        c_out[0] = c_st[...]
        m_out[0] = m_st[0:1, 0:M_HEADS]
        ktail_ref[0] = kp_ref[...]
        vtail_ref[0] = vp_ref[...]
        ctail_ref[0] = up_ref[0:8, :]


def _stream(x2d, p2d, g_pre, w_main, w_gt, conv_w, bg_col, mlstm_norm, sinks, gains, weights, batch, seq):
    tr = CHUNKS_PER_STEP * CHUNK
    ns = seq // tr
    nt = batch * ns
    n = batch * seq
    full = lambda a: pl.BlockSpec(a.shape, lambda c: (0,) * a.ndim)
    resident = lambda a: pl.BlockSpec(a.shape, lambda c: (0,) * a.ndim, pipeline_mode=pl.Buffered(1))
    lag2 = lambda w: pl.BlockSpec((tr, w), lambda c: (jnp.clip(c - 2, 0, nt - 1), 0))
    per_seq = lambda shape: pl.BlockSpec((1,) + shape,
                                         lambda c: (jnp.clip((c - 1) // ns, 0, batch - 1),) + (0,) * len(shape))
    return pl.pallas_call(
        functools.partial(_stream_kernel, seq_tiles=ns),
        grid=(nt + 2,),
        in_specs=[pl.BlockSpec((tr, D_MODEL), lambda c: (jnp.minimum(c, nt - 1), 0)),
                  lag2(D_MODEL), lag2(P_DIM),
                  full(g_pre), resident(w_main), full(w_gt), full(conv_w), full(bg_col),
                  full(mlstm_norm), full(sinks)] + [full(g) for g in gains] + [resident(w) for w in weights],
        out_specs=[lag2(D_MODEL),
                   per_seq((M_HEADS, M_DK, 2 * M_DV)), per_seq((1, M_HEADS)),
                   per_seq((CHUNK, KV_W)), per_seq((CHUNK, KV_W)), per_seq((8, QK_W))],
        out_shape=[jax.ShapeDtypeStruct((n, D_MODEL), F32),
                   jax.ShapeDtypeStruct((batch, M_HEADS, M_DK, 2 * M_DV), F32),
                   jax.ShapeDtypeStruct((batch, 1, M_HEADS), F32),
                   jax.ShapeDtypeStruct((batch, CHUNK, KV_W), F32), jax.ShapeDtypeStruct((batch, CHUNK, KV_W), F32),
                   jax.ShapeDtypeStruct((batch, 8, QK_W), F32)],
        scratch_shapes=[pltpu.VMEM((2, tr, Z_W), F32), pltpu.VMEM((2, N_GATES, tr), F32),
                        pltpu.VMEM((2, tr, D_MODEL), BF16),
                        pltpu.VMEM((8 + tr, QK_W), F32),
                        pltpu.VMEM((M_HEADS, M_DK, 2 * M_DV), F32), pltpu.VMEM((8, 128), F32),
                        pltpu.VMEM((CHUNK, KV_W), F32), pltpu.VMEM((CHUNK, KV_W), F32),
                        pltpu.VMEM((A_KV, A_GROUP * CHUNK, 2 * CHUNK), F32),
                        pltpu.VMEM((A_KV, A_GROUP * CHUNK, KV_W), F32)],
        compiler_params=pltpu.CompilerParams(dimension_semantics=("arbitrary",),
                                             vmem_limit_bytes=VMEM_LIMIT_BYTES),
        name="prompt_layer",
    )(x2d, x2d, p2d, g_pre, w_main, w_gt, conv_w, bg_col, mlstm_norm, sinks, *gains, *weights)


def _in_proj_sample_kernel(x_ref, g_ref, wn_ref, wt_ref, wgt_ref,
                           qkn_ref, qa_ref, ka_ref, va_ref, qkt_ref, vt_ref, ot_ref, gt_ref):
    xn = _rms(x_ref[...], g_ref[...]).astype(BF16)
    zn = _dot(xn, wn_ref[...])
    qkn_ref[...] = zn[:, 0:QK_W]
    qa_ref[...] = zn[:, QK_W:QK_W + A_WIDTH]
    ka_ref[...] = zn[:, QK_W + A_WIDTH:QK_W + A_WIDTH + KV_W]
    va_ref[...] = zn[:, QK_W + A_WIDTH + KV_W:QK_W + A_WIDTH + 2 * KV_W]
    zt = _dot_nt(wt_ref[...], xn)
    qkt_ref[...] = zt[0:QK_W, :]
    vt_ref[...] = zt[QK_W:QK_W + M_WIDTH, :]
    ot_ref[...] = zt[QK_W + M_WIDTH:QK_W + 2 * M_WIDTH, :]
    gt_ref[...] = _dot_nt(wgt_ref[...], xn)


def _in_proj_sample(x2d, g_pre, w_nat, w_t, w_gt):
    n = x2d.shape[0]
    shapes = [(n, QK_W), (n, A_WIDTH), (n, KV_W), (n, KV_W), (QK_W, n), (M_WIDTH, n), (M_WIDTH, n), (N_GATES, n)]
    return pl.pallas_call(
        _in_proj_sample_kernel,
        out_shape=[jax.ShapeDtypeStruct(s, F32) for s in shapes],
        compiler_params=pltpu.CompilerParams(vmem_limit_bytes=VMEM_LIMIT_BYTES),
        name="in_proj_sample",
    )(x2d, g_pre, w_nat, w_t, w_gt)


def _mlstm_sample_kernel(qt_ref, kt_ref, vt_ref, ot_ref, gt_ref, bg_ref, cbq_ref, cbk_ref, cwq_ref, cwk_ref,
                         gain_ref, c_ref, n_ref, m_ref,
                         ym_ref, c_out, n_out, m_out,
                         ct_ref, q_s, ik_s, num_s):
    h = pl.program_id(0)
    NB = c_ref.shape[0]
    L = qt_ref.shape[1] // NB
    T = range(L)

    def conv(raw_ref, cb_ref, cw_ref):
        ups = [cb_ref[j] for j in range(CONV_W - 1)] + [raw_ref[:, t * NB:(t + 1) * NB] for t in T]
        w = [cw_ref[:, j:j + 1] for j in range(CONV_W)]
        outs = []
        for t in T:
            acc = w[0] * ups[t]
            for j in range(1, CONV_W):
                acc = acc + w[j] * ups[t + j]
            outs.append(jax.nn.silu(acc))
        return outs

    q = [x * (M_DK ** -0.5) for x in conv(qt_ref, cbq_ref, cwq_ref)]
    k = conv(kt_ref, cbk_ref, cwk_ref)

    ig_all = gt_ref[pl.ds(h, 1), :] + bg_ref[pl.ds(h, 1), :]
    lf_all = jax.nn.log_sigmoid(gt_ref[pl.ds(h + M_HEADS, 1), :] + bg_ref[pl.ds(h + M_HEADS, 1), :])
    m = m_ref[pl.ds(h, 1), :]
    f, ms = [], []
    n = n_ref[0]
    den = []
    for t in T:
        ig, lf = ig_all[:, t * NB:(t + 1) * NB], lf_all[:, t * NB:(t + 1) * NB]
        m_new = jnp.maximum(lf + m, ig)
        f_t = jnp.exp(lf + m - m_new)
        ik = jnp.exp(ig - m_new) * k[t]
        m = m_new
        n = f_t * n + ik
        f.append(f_t)
        ms.append(m_new)
        den.append(jnp.sum(q[t] * n, axis=0, keepdims=True))
        ik_s[t] = ik
        q_s[t] = q[t]
    m_out[0] = m
    n_out[0] = n

    c_dbv = jnp.swapaxes(c_ref[:, 0, :, :], 0, 1)
    for d in range(M_DK):
        ct_ref[d * M_DV:(d + 1) * M_DV, :] = c_dbv[d].T

    VT = SAMPLE_V_TILE
    for vq in range(M_DV // VT):
        vts = [vt_ref[vq * VT:(vq + 1) * VT, t * NB:(t + 1) * NB] for t in T]

        def step(d, accs, vq=vq, vts=vts):
            r0 = pl.multiple_of(d * M_DV + vq * VT, VT)
            c = ct_ref[pl.ds(r0, VT), :]
            new = []
            for t in T:
                c = f[t] * c + ik_s[t, pl.ds(d, 1), :] * vts[t]
                new.append(accs[t] + q_s[t, pl.ds(d, 1), :] * c)
            ct_ref[pl.ds(r0, VT), :] = c
            return tuple(new)

        accs = lax.fori_loop(0, M_DK, step, tuple(jnp.zeros((VT, NB), F32) for _ in T), unroll=8)
        for t in T:
            num_s[t, vq * VT:(vq + 1) * VT, :] = accs[t]

    for t in T:
        hh = num_s[t] / jnp.maximum(jnp.abs(den[t]), jnp.exp(-ms[t]))
        hn = hh * lax.rsqrt(jnp.mean(hh * hh, axis=0, keepdims=True) + EPS) * gain_ref[...]
        y = jax.nn.sigmoid(ot_ref[:, t * NB:(t + 1) * NB]) * hn
        ym_ref[t * NB:(t + 1) * NB, :] = y.T

    for d in range(M_DK):
        c_out[:, d * M_DV:(d + 1) * M_DV] = ct_ref[d * M_DV:(d + 1) * M_DV, :].T


def _mlstm_sample(qkt, vt, ot, gt, bg_col, cbt, cwt, gain_col, c4d, nt, mt):
    nb = c4d.shape[0]
    n = qkt.shape[1]
    L = n // nb
    hblk = lambda rows, off: pl.BlockSpec((rows, n), lambda h, off=off: (h + off, 0))
    full = lambda a: pl.BlockSpec(a.shape, lambda h: (0,) * a.ndim)
    kq = QK_W // 2 // M_DK
    return pl.pallas_call(
        _mlstm_sample_kernel,
        grid=(M_HEADS,),
        in_specs=[hblk(M_DK, 0), hblk(M_DK, kq), hblk(M_DV, 0), hblk(M_DV, 0), full(gt), full(bg_col),
                  pl.BlockSpec((CONV_W - 1, M_DK, nb), lambda h: (0, h, 0)),
                  pl.BlockSpec((CONV_W - 1, M_DK, nb), lambda h: (0, h + kq, 0)),
                  pl.BlockSpec((M_DK, CONV_W), lambda h: (h, 0)),
                  pl.BlockSpec((M_DK, CONV_W), lambda h: (h + kq, 0)),
                  pl.BlockSpec((M_DV, 1), lambda h: (h, 0)),
                  pl.BlockSpec((nb, 1, M_DK, M_DV), lambda h: (0, h, 0, 0)),
                  pl.BlockSpec((1, M_DK, nb), lambda h: (h, 0, 0)),
                  full(mt)],
        out_specs=[pl.BlockSpec((n, M_DV), lambda h: (0, h)),
                   pl.BlockSpec((nb, M_DK * M_DV), lambda h: (0, h)),
                   pl.BlockSpec((1, M_DK, nb), lambda h: (h, 0, 0)),
                   pl.BlockSpec((1, 1, nb), lambda h: (h, 0, 0))],
        out_shape=[jax.ShapeDtypeStruct((n, M_WIDTH), F32),
                   jax.ShapeDtypeStruct((nb, M_HEADS * M_DK * M_DV), F32),
                   jax.ShapeDtypeStruct(nt.shape, F32), jax.ShapeDtypeStruct((M_HEADS, 1, nb), F32)],
        scratch_shapes=[pltpu.VMEM((M_DK * M_DV, nb), F32), pltpu.VMEM((L, M_DK, nb), F32),
                        pltpu.VMEM((L, M_DK, nb), F32), pltpu.VMEM((L, M_DV, nb), F32)],
        compiler_params=pltpu.CompilerParams(dimension_semantics=("arbitrary",),
                                             vmem_limit_bytes=VMEM_LIMIT_BYTES),
        name="mlstm_sample",
    )(qkt, qkt, vt, ot, gt, bg_col, cbt, cbt, cwt, cwt, gain_col, c4d, nt, mt)


def _attn_sample_kernel(qm_ref, kn_ref, vn_ref, kct_ref, vct_ref, sinkr_ref, sloper_ref,
                        att_ref, kct_out, vct_out, xk_ref, xv_ref):
    NB, L = kn_ref.shape[0], kn_ref.shape[1]
    R = A_KV * A_GROUP * L

    def bias(nkeys, offset):
        r = lax.broadcasted_iota(jnp.int32, (R, nkeys), 0).astype(F32)
        pos = lax.broadcasted_iota(jnp.int32, (R, nkeys), 1).astype(F32)
        tq = r - L * jnp.floor((r + 0.5) / L)
        dist = tq + offset - pos
        return jnp.where((dist >= 0.0) & (dist <= float(WINDOW)), -sloper_ref[...] * dist, NEG_INF)

    @pl.when(pl.program_id(0) == 0)
    def _():
        xk_ref[...] = jnp.zeros_like(xk_ref)
        xv_ref[...] = jnp.zeros_like(xv_ref)

    bias_c = bias(WINDOW, float(WINDOW))
    bias_n = bias(L, 0.0)
    sink = sinkr_ref[...]
    rq = lax.broadcasted_iota(jnp.int32, (R, KV_W), 0)
    cq = lax.broadcasted_iota(jnp.int32, (R, KV_W), 1)
    same_kv = (rq < A_GROUP * L) == (cq < A_HD)
    newest = lax.broadcasted_iota(jnp.int32, (KV_W, WINDOW), 1) >= WINDOW - L
    SEQ = range(NB)
    qh = [jnp.where(same_kv, qm_ref[b] * (A_HD ** -0.5), 0.0).astype(BF16) for b in SEQ]
    kt = [kct_ref[b] for b in SEQ]
    vt = [vct_ref[b] for b in SEQ]
    knb = [kn_ref[b].astype(BF16) for b in SEQ]
    vnb = [vn_ref[b].astype(BF16) for b in SEQ]
    sc_c = [_dot(qh[b], kt[b].astype(BF16)) + bias_c for b in SEQ]
    sc_n = [_dot_nt(qh[b], knb[b]) + bias_n for b in SEQ]
    m_a = [jnp.maximum(sink, jnp.maximum(jnp.max(sc_c[b], axis=-1, keepdims=True),
                                         jnp.max(sc_n[b], axis=-1, keepdims=True))) for b in SEQ]
    e_c = [jnp.exp(sc_c[b] - m_a[b]) for b in SEQ]
    e_n = [jnp.exp(sc_n[b] - m_a[b]) for b in SEQ]
    den_a = [jnp.exp(sink - m_a[b]) + jnp.sum(e_c[b], axis=-1, keepdims=True)
             + jnp.sum(e_n[b], axis=-1, keepdims=True) for b in SEQ]
    pv = [_dot_nt((e_c[b] / den_a[b]).astype(BF16), vt[b].astype(BF16))
          + _dot((e_n[b] / den_a[b]).astype(BF16), vnb[b]) for b in SEQ]
    for b in SEQ:
        att_ref[b] = pv[b]

    for b in SEQ:
        xk_ref[b, WINDOW - L:WINDOW, :] = kn_ref[b]
        xv_ref[b, WINDOW - L:WINDOW, :] = vn_ref[b]
    for b in SEQ:
        kct_out[b] = jnp.where(newest, xk_ref[b].T, pltpu.roll(kt[b], WINDOW - L, axis=1))
        vct_out[b] = jnp.where(newest, xv_ref[b].T, pltpu.roll(vt[b], WINDOW - L, axis=1))


def _attn_sample(qm, kn, vn, kct, vct, sink_rows, slope_rows):
    nb = qm.shape[0]
    ts = min(SAMPLE_SEQ_TILE, nb)
    per_b = lambda a: pl.BlockSpec((ts,) + a.shape[1:], lambda b: (b,) + (0,) * (a.ndim - 1))
    full = lambda a: pl.BlockSpec(a.shape, lambda b: (0,) * a.ndim)
    outs = [jax.ShapeDtypeStruct(a.shape, F32) for a in (qm, kct, vct)]
    return pl.pallas_call(
        _attn_sample_kernel,
        grid=(nb // ts,),
        in_specs=[per_b(a) for a in (qm, kn, vn, kct, vct)] + [full(sink_rows), full(slope_rows)],
        out_specs=[per_b(o) for o in outs],
        out_shape=outs,
        scratch_shapes=[pltpu.VMEM((ts, WINDOW, KV_W), F32), pltpu.VMEM((ts, WINDOW, KV_W), F32)],
        compiler_params=pltpu.CompilerParams(dimension_semantics=("arbitrary",),
                                             vmem_limit_bytes=VMEM_LIMIT_BYTES),
        name="attn_sample",
    )(qm, kn, vn, kct, vct, sink_rows, slope_rows)


def _post_kernel(x_ref, ym_ref, att_ref, p_ref, an_ref, g_post_ref, g_fpre_ref, g_fpost_ref,
                 wout_ref, wup_ref, wdown_ref, wpg_ref, wpp_ref, o_ref):
    y_a = _rms(att_ref[...], an_ref[...])
    y = jnp.concatenate([ym_ref[...], y_a], axis=-1).astype(BF16)
    x1 = x_ref[...] + _rms(_dot(y, wout_ref[...]), g_post_ref[...])
    u = _rms(x1, g_fpre_ref[...]).astype(BF16)
    f = jnp.zeros_like(x1)
    for j in range(D_FF // FF_TILE):
        cols = slice(j * FF_TILE, (j + 1) * FF_TILE)
        hid = jnp.square(jnp.maximum(_dot(u, wup_ref[:, cols]), 0.0)).astype(BF16)
        f = f + _dot(hid, wdown_ref[cols, :])
    x2 = x1 + _rms(f, g_fpost_ref[...])
    gate = jax.nn.sigmoid(_dot(x2.astype(BF16), wpg_ref[...]))
    o_ref[...] = x2 + gate * _dot(p_ref[...].astype(BF16), wpp_ref[...])


def _post(x2d, ym, att, p2d, gains, weights):
    n = x2d.shape[0]
    tm = min(ROW_TILE, n)
    row = lambda w: pl.BlockSpec((tm, w), lambda i: (i, 0))
    full = lambda a: pl.BlockSpec(a.shape, lambda i: (0,) * a.ndim)
    resident = lambda a: pl.BlockSpec(a.shape, lambda i: (0,) * a.ndim, pipeline_mode=pl.Buffered(1))
    return pl.pallas_call(
        _post_kernel,
        grid=(n // tm,),
        in_specs=[row(D_MODEL), row(M_WIDTH), row(A_WIDTH), row(P_DIM)]
        + [full(g) for g in gains] + [resident(w) for w in weights],
        out_specs=row(D_MODEL),
        out_shape=jax.ShapeDtypeStruct((n, D_MODEL), F32),
        compiler_params=pltpu.CompilerParams(dimension_semantics=("arbitrary",),
                                             vmem_limit_bytes=VMEM_LIMIT_BYTES),
        name="post",
    )(x2d, ym, att, p2d, *gains, *weights)


def _permute_heads(a, axis):
    shape = a.shape
    a = a.reshape(shape[:axis] + (A_HEADS, A_HD) + shape[axis + 1:])
    a = jnp.take(a, np.asarray(HEAD_ORDER), axis=axis)
    return a.reshape(shape)


def kernel(x_prompt, x_sample, p_prompt, p_sample, state_mlstm_c, state_mlstm_n, state_mlstm_m,
           state_mlstm_conv, cache_swa_k, cache_swa_v, norm_mix_pre, w_in, b_gates, conv_w,
           mlstm_norm, attn_sinks, attn_norm, w_out, norm_mix_post, norm_ffn_pre, w_up, w_down,
           norm_ffn_post, w_pgate, w_pproj):
    depth = w_in.shape[0]
    assert depth == 1, "single-layer decoder"
    B, T, _ = x_prompt.shape
    SB, ST, _ = x_sample.shape
    assert T % (CHUNKS_PER_STEP * CHUNK) == 0 and T % ROW_TILE == 0 and SB % SAMPLE_SEQ_TILE == 0
    i = 0

    wi = w_in[i]
    o_qk, o_vm, o_om, o_g, o_qa, o_ka, o_va = np.cumsum(
        [0, QK_W, M_WIDTH, M_WIDTH, N_GATES, A_WIDTH, KV_W])
    wt = wi.T.astype(BF16)
    wt_main = jnp.concatenate([wt[o_qk:o_g], _permute_heads(wt[o_qa:o_ka], 0), wt[o_ka:]], axis=0)
    w_main = wt_main.T
    w_gt = wt[o_g:o_qa]
    g_pre = norm_mix_pre[i].reshape(1, D_MODEL)
    bg_col = b_gates[i].reshape(N_GATES, 1)
    cw = conv_w[i]
    mn = mlstm_norm[i].reshape(1, M_WIDTH)
    sinks = attn_sinks[i].reshape(1, A_HEADS)
    gains = [_permute_heads(attn_norm[i], 0).reshape(1, A_WIDTH), norm_mix_post[i].reshape(1, D_MODEL),
             norm_ffn_pre[i].reshape(1, D_MODEL), norm_ffn_post[i].reshape(1, D_MODEL)]
    wo = jnp.concatenate([w_out[i][:M_WIDTH], _permute_heads(w_out[i][M_WIDTH:], 0)], axis=0)
    weights = [wo.astype(BF16), w_up[i].astype(BF16), w_down[i].astype(BF16),
               w_pgate[i].astype(BF16), w_pproj[i].astype(BF16)]

    xp = x_prompt.reshape(B * T, D_MODEL)
    y_p, cn_p, m_p, kt_p, vt_p, tail_p = _stream(xp, p_prompt[i].reshape(B * T, P_DIM), g_pre, w_main, w_gt, cw,
                                                 bg_col, mn, sinks, gains, weights, B, T)
    y_prompt = y_p.reshape(B, T, D_MODEL)
    c_p, n_p = cn_p[..., :M_DV], cn_p[..., M_DV]
    conv_p = tail_p[:, 8 - (CONV_W - 1):]
    k_p = kt_p.reshape(B, WINDOW, A_KV, A_HD)
    v_p = vt_p.reshape(B, WINDOW, A_KV, A_HD)

    xs = x_sample.transpose(1, 0, 2).reshape(ST * SB, D_MODEL)
    ps = p_sample[i].transpose(1, 0, 2).reshape(ST * SB, P_DIM)
    w_nat = jnp.concatenate([w_main[:, 0:QK_W], w_main[:, QK_W + 2 * M_WIDTH:]], axis=1)
    w_t = wt_main[0:QK_W + 2 * M_WIDTH]
    qkn_s, qa_s, ka_s, va_s, qkt_s, vt_s, ot_s, gt_s = _in_proj_sample(xs, g_pre, w_nat, w_t, w_gt)
    cbt = state_mlstm_conv[i].transpose(1, 2, 0)
    ym_s, c_s2, nt_s, mt_s = _mlstm_sample(
        qkt_s, vt_s, ot_s, gt_s, bg_col, cbt, cw.T, mn.reshape(M_WIDTH, 1),
        state_mlstm_c[i], state_mlstm_n[i].transpose(1, 2, 0),
        state_mlstm_m[i].T)
    c_s = c_s2.reshape(SB, M_HEADS, M_DK, M_DV)
    n_s = nt_s.transpose(2, 0, 1)
    m_s = mt_s.reshape(M_HEADS, SB).T
    conv_s = qkn_s.reshape(ST, SB, QK_W)[ST - (CONV_W - 1):].transpose(1, 0, 2)

    qs = qa_s.reshape(ST, SB, A_GROUP, A_KV, A_HD).transpose(1, 3, 2, 0, 4).reshape(SB, A_HEADS * ST, A_HD)
    qm = jnp.concatenate([qs, qs], axis=-1)
    kn = ka_s.reshape(ST, SB, KV_W).transpose(1, 0, 2)
    vn = va_s.reshape(ST, SB, KV_W).transpose(1, 0, 2)
    sink_rows = jnp.repeat(attn_sinks[i], ST).reshape(A_HEADS * ST, 1)
    slope_rows = jnp.asarray(np.repeat(np.asarray(SLOPES, np.float32), ST).reshape(A_HEADS * ST, 1))
    kct = cache_swa_k[i].transpose(0, 2, 3, 1).reshape(SB, KV_W, WINDOW)
    vct = cache_swa_v[i].transpose(0, 2, 3, 1).reshape(SB, KV_W, WINDOW)
    att_s, kct_s, vct_s = _attn_sample(qm, kn, vn, kct, vct, sink_rows, slope_rows)
    att_d = att_s.reshape(SB, A_KV, A_GROUP * ST, A_KV, A_HD)
    att_k = jnp.stack([att_d[:, kv, :, kv, :] for kv in range(A_KV)], axis=1)
    att_s2 = att_k.reshape(SB, A_KV, A_GROUP, ST, A_HD).transpose(3, 0, 2, 1, 4).reshape(ST * SB, A_WIDTH)
    y_s = _post(xs, ym_s, att_s2, ps, gains, weights)
    y_sample = y_s.reshape(ST, SB, D_MODEL).transpose(1, 0, 2)
    k_s = kct_s.reshape(SB, A_KV, A_HD, WINDOW).transpose(0, 3, 1, 2)
    v_s = vct_s.reshape(SB, A_KV, A_HD, WINDOW).transpose(0, 3, 1, 2)

    stack = lambda a: a[None]
    return (y_prompt, y_sample,
            stack(c_p), stack(n_p), stack(m_p.reshape(B, M_HEADS)), stack(conv_p), stack(k_p), stack(v_p),
            stack(c_s), stack(n_s), stack(m_s), stack(conv_s), stack(k_s), stack(v_s))
```

```python
import functools

import numpy as np
import jax
import jax.numpy as jnp
from jax import lax
from jax.experimental import pallas as pl
from jax.experimental.pallas import tpu as pltpu

F32 = jnp.float32
BF16 = jnp.bfloat16

D_MODEL = 1024
M_WIDTH = 512
M_HEADS = 4
M_DV = 128
M_DK = 64
QK_W = 512
CONV_W = 4
CHUNK = 128
A_WIDTH = 512
A_HEADS = 8
A_HD = 64
A_KV = 2
A_GROUP = 4
KV_W = 128
WINDOW = 128
D_FF = 4096
P_DIM = 256
EPS = 1e-6
N_GATES = 2 * M_HEADS

VMEM_LIMIT_BYTES = 56 * 1024 * 1024
CHUNKS_PER_STEP = 2
TILE_ROWS = CHUNKS_PER_STEP * CHUNK
PIPE_LAG = 2
FF_TILE = 1024
SAMPLE_SEQ_TILE = 16
SAMPLE_V_TILE = 32

Z_W = QK_W + 2 * M_WIDTH + A_WIDTH + 2 * KV_W
Z_GROUPS = [(0, 512), (512, 1024), (1024, 1536), (1536, 2048), (2048, 2304)]
EARLY_GROUPS = 3

NEG_INF = float("-inf")
SLOPES = [2.0 ** (-8.0 * (h + 1) / A_HEADS) for h in range(A_HEADS)]
HEAD_ORDER = [kv * A_GROUP + g for g in range(A_GROUP) for kv in range(A_KV)]


def _dot(a, b):
    return jnp.dot(a, b, preferred_element_type=F32)


def _dot_nt(a, b):
    return lax.dot_general(a, b, (((1,), (1,)), ((), ())), preferred_element_type=F32)


def _dot_tn(a, b):
    return lax.dot_general(a, b, (((0,), (0,)), ((), ())), preferred_element_type=F32)


def _rms(x, g):
    return x * lax.rsqrt(jnp.mean(x * x, axis=-1, keepdims=True) + EPS) * g


def _conv_silu(up_ref, base, w, L):
    out = w[0:1, :] * up_ref[base:base + L, :]
    for j in range(1, CONV_W):
        out = out + w[j:j + 1, :] * up_ref[base + j:base + j + L, :]
    return jax.nn.silu(out)


def _chunk_cumsum_lanes(x):
    pos = lax.broadcasted_iota(jnp.int32, x.shape, 1)
    shift = 1
    while shift < CHUNK:
        x = x + jnp.where(pos >= shift, pltpu.roll(x, shift, axis=1), 0.0)
        shift *= 2
    return x


def _mlstm_out(h, o_sig, g):
    return o_sig * (h * lax.rsqrt(jnp.mean(h * h, axis=-1, keepdims=True) + EPS) * g)


def _stream_kernel(x_ref, xq_ref, p_ref, xs_ref, ps_ref, yms_ref, atts_ref,
                   g_ref, w_ref, wgt_ref, cw_ref, bgc_ref, mn_ref, sink_ref,
                   an_ref, g_post_ref, g_fpre_ref, g_fpost_ref, wout_ref, wup_ref, wdown_ref, wpg_ref, wpp_ref,
                   y_ref, ys_ref, c_out, m_out, ktail_ref, vtail_ref, ctail_ref,
                   z_ref, g_st, y_st, up_ref, c_st, m_st, kp_ref, vp_ref, bias_ref, sinkrep_ref, *, seq_tiles):
    ci = pl.program_id(0)
    L = CHUNK
    J = range(CHUNKS_PER_STEP)
    H = range(M_HEADS)
    KV = range(A_KV)
    TR = TILE_ROWS
    nxt = ci % 2
    cur = 1 - nxt
    first = ((ci - 1) % seq_tiles) == 0
    sample_step = ci < PIPE_LAG
    t = lax.broadcasted_iota(jnp.int32, (L, L), 0)
    s = lax.broadcasted_iota(jnp.int32, (L, L), 1)

    @pl.when(ci == 0)
    def _():
        z_ref[cur] = jnp.zeros((TR, Z_W), F32)
        g_st[cur] = jnp.zeros((N_GATES, TR), F32)
        y_st[cur] = jnp.zeros((TR, D_MODEL), BF16)
        c_st[...] = jnp.zeros_like(c_st)
        m_st[...] = jnp.zeros_like(m_st)
        kp_ref[...] = jnp.zeros_like(kp_ref)
        vp_ref[...] = jnp.zeros_like(vp_ref)
        up_ref[0:8, :] = jnp.zeros((8, QK_W), F32)
        dist = (t - s).astype(F32)
        for kv in KV:
            for g in range(A_GROUP):
                hd = kv * A_GROUP + g
                rows = slice(g * L, (g + 1) * L)
                bias_ref[kv, rows, 0:L] = jnp.where(s >= t, -SLOPES[hd] * (dist + float(WINDOW)), NEG_INF)
                bias_ref[kv, rows, L:2 * L] = jnp.where(s <= t, -SLOPES[hd] * dist, NEG_INF)
                sinkrep_ref[kv, rows, :] = jnp.broadcast_to(sink_ref[0:1, hd:hd + 1], (L, KV_W))

    up_ref[5:8, :] = jnp.where(first, 0.0, up_ref[5:8, :])
    up_ref[8:8 + TR, :] = z_ref[cur, :, 0:QK_W]
    gcur = g_st[cur] + bgc_ref[...]

    y_smp = jnp.concatenate([yms_ref[...], _rms(atts_ref[...], an_ref[...])], axis=-1).astype(BF16)
    y_in = jnp.where(sample_step, y_smp, y_st[cur])
    x_in = jnp.where(sample_step, xs_ref[...], xq_ref[...])
    p_in = jnp.where(sample_step, ps_ref[...], p_ref[...]).astype(BF16)

    mix = _dot(y_in, wout_ref[...])
    xn = _rms(x_ref[...], g_ref[...]).astype(BF16)
    for lo, hi in Z_GROUPS[:EARLY_GROUPS]:
        z_ref[nxt, :, lo:hi] = _dot(xn, w_ref[:, lo:hi])
    x1 = x_in + _rms(mix, g_post_ref[...])
    u = _rms(x1, g_fpre_ref[...]).astype(BF16)

    def ffn_tile(j):
        cols = slice(j * FF_TILE, (j + 1) * FF_TILE)
        hid = jnp.square(jnp.maximum(_dot(u, wup_ref[:, cols]), 0.0)).astype(BF16)
        return _dot(hid, wdown_ref[cols, :])

    qk = _conv_silu(up_ref, 5, cw_ref[...], TR)
    up_ref[5:8, :] = up_ref[5 + TR:8 + TR, :]
    q_all = (qk[:, 0:QK_W // 2] * (M_DK ** -0.5)).astype(BF16)
    k_all_m = qk[:, QK_W // 2:QK_W]
    grow = lax.broadcasted_iota(jnp.int32, (N_GATES, L), 0)
    rows = [slice(j * L, (j + 1) * L) for j in J]
    gr, gc = [], []
    for j in J:
        gch = gcur[:, rows[j]]
        logs = jnp.where(grow >= M_HEADS, jax.nn.log_sigmoid(gch), gch)
        out = jnp.where(grow >= M_HEADS, _chunk_cumsum_lanes(logs), logs)
        gr.append(out)
        gc.append(out.T)
    P = [(j, h) for j in J for h in H]
    ig_c = [gc[j][:, 0:M_HEADS] for j in J]
    b_c = [gc[j][:, M_HEADS:N_GATES] for j in J]
    a_r = [gr[j][0:M_HEADS, :] - gr[j][M_HEADS:N_GATES, :] for j in J]
    causal_bias = jnp.where(s <= t, 0.0, NEG_INF)
    hcol = lax.broadcasted_iota(jnp.int32, (L, M_HEADS), 1)
    m_prev = [jnp.where(first, 0.0, m_st[0:1, 0:M_HEADS])]
    w_k, decay = [], []
    for j in J:
        b_last = b_c[j][L - 1:L, :]
        log_w = b_last - b_c[j] + ig_c[j]
        m_new = jnp.maximum(b_last + m_prev[j], jnp.max(log_w, axis=0, keepdims=True))
        w_k.append(jnp.exp(log_w - m_new))
        decay.append(jnp.exp(b_last + m_prev[j] - m_new))
        m_prev.append(m_new)
    m_st[0:1, 0:M_HEADS] = m_prev[-1]

    f = ffn_tile(0)

    ones_v = jnp.ones((L, M_DV), BF16)
    qb = {p: q_all[rows[p[0]], p[1] * M_DK:(p[1] + 1) * M_DK] for p in P}
    kf = {p: k_all_m[rows[p[0]], p[1] * M_DK:(p[1] + 1) * M_DK] for p in P}
    kb = {p: kf[p].astype(BF16) for p in P}
    vext = {p: jnp.concatenate([z_ref[cur, rows[p[0]], QK_W + p[1] * M_DV:QK_W + (p[1] + 1) * M_DV].astype(BF16),
                                ones_v], axis=1) for p in P}
    qk_h = {p: _dot_nt(qb[p], kb[p]) for p in P}
    kw = {p: (kf[p] * w_k[p[0]][:, p[1]:p[1] + 1]).astype(BF16) for p in P}
    dc = {p: _dot_tn(kw[p], vext[p]) for p in P}

    f = f + ffn_tile(1)

    log_d = {p: b_c[p[0]][:, p[1]:p[1] + 1] + a_r[p[0]][p[1]:p[1] + 1, :] + causal_bias for p in P}
    m_t, w_inter, emt = [], [], []
    for j in J:
        m_intra = jnp.full((L, M_HEADS), NEG_INF, F32)
        for h in H:
            m_intra = jnp.where(hcol == h, jnp.max(log_d[(j, h)], axis=-1, keepdims=True), m_intra)
        log_inter = b_c[j] + m_prev[j]
        m_t.append(jnp.maximum(log_inter, m_intra))
        w_inter.append(jnp.exp(log_inter - m_t[j]))
        emt.append(jnp.exp(-m_t[j]))
    s_h = {p: (qk_h[p] * jnp.exp(log_d[p] - m_t[p[0]][:, p[1]:p[1] + 1])).astype(BF16) for p in P}
    c_cur = [jnp.where(first, 0.0, c_st[h]) for h in H]
    qc = {}
    for j in J:
        for h in H:
            qc[(j, h)] = _dot(qb[(j, h)], c_cur[h].astype(BF16))
        c_cur = [decay[j][:, h:h + 1] * c_cur[h] + dc[(j, h)] for h in H]
    for h in H:
        c_st[h] = c_cur[h]

    f = f + ffn_tile(2)

    o_h = {p: _dot(s_h[p], vext[p]) for p in P}
    for j, h in P:
        wi = w_inter[j][:, h:h + 1]
        num = o_h[(j, h)][:, 0:M_DV] + wi * qc[(j, h)][:, 0:M_DV]
        den = o_h[(j, h)][:, M_DV:2 * M_DV] + wi * qc[(j, h)][:, M_DV:2 * M_DV]
        hh = num / jnp.maximum(jnp.abs(den), emt[j][:, h:h + 1])
        cols = slice(h * M_DV, (h + 1) * M_DV)
        o_sig = jax.nn.sigmoid(z_ref[cur, rows[j], QK_W + M_WIDTH + h * M_DV:QK_W + M_WIDTH + (h + 1) * M_DV])
        y_st[nxt, rows[j], cols] = _mlstm_out(hh, o_sig, mn_ref[:, cols]).astype(BF16)

    a0 = QK_W + 2 * M_WIDTH
    nk = (CHUNKS_PER_STEP + 1) * L
    low3 = lax.broadcasted_iota(jnp.int32, (nk, KV_W), 1) < A_HD
    low = lax.broadcasted_iota(jnp.int32, (L, KV_W), 1) < A_HD
    ka_cur = z_ref[cur, :, a0 + A_WIDTH:a0 + A_WIDTH + KV_W]
    va_cur = z_ref[cur, :, a0 + A_WIDTH + KV_W:a0 + A_WIDTH + 2 * KV_W]
    k_all = jnp.concatenate([kp_ref[...], ka_cur], axis=0)
    v_all = jnp.concatenate([vp_ref[...], va_cur], axis=0)
    ones_k = jnp.ones((nk, KV_W), BF16)
    kmask = [jnp.where(low3, k_all, 0.0).astype(BF16), jnp.where(low3, 0.0, k_all).astype(BF16)]
    vext_a = [jnp.concatenate([jnp.where(low3, v_all, 1.0).astype(BF16), ones_k], axis=1),
              jnp.concatenate([jnp.where(low3, 1.0, v_all).astype(BF16), ones_k], axis=1)]
    qst = [(jnp.concatenate([z_ref[cur, rows[j], a0 + g * KV_W:a0 + (g + 1) * KV_W] for g in range(A_GROUP)],
                            axis=0) * (A_HD ** -0.5)).astype(BF16) for j in J]
    Q = [(j, kv) for j in J for kv in KV]

    def bias_of(j, kv):
        if j > 0:
            return bias_ref[kv]
        return jnp.concatenate([jnp.where(first, NEG_INF, bias_ref[kv, :, 0:L]), bias_ref[kv, :, L:2 * L]], axis=1)

    sc = {p: _dot_nt(qst[p[0]], kmask[p[1]][p[0] * L:(p[0] + 2) * L]) + bias_of(*p) for p in Q}

    f = f + ffn_tile(3)

    m_a = {p: jnp.maximum(jnp.max(jnp.maximum(sc[p][:, 0:L], sc[p][:, L:2 * L]), axis=-1, keepdims=True),
                          sinkrep_ref[p[1]]) for p in Q}
    e = {p: jnp.concatenate([jnp.exp(sc[p][:, 0:L] - m_a[p]), jnp.exp(sc[p][:, L:2 * L] - m_a[p])],
                            axis=1).astype(BF16) for p in Q}
    o = {p: _dot(e[p], vext_a[p[1]][p[0] * L:(p[0] + 2) * L]) for p in Q}

    x2 = x1 + _rms(f, g_fpost_ref[...])
    gate = jax.nn.sigmoid(_dot(x2.astype(BF16), wpg_ref[...]))
    y_val = x2 + gate * _dot(p_in, wpp_ref[...])
    y_ref[...] = y_val
    ys_ref[...] = y_val

    for lo, hi in Z_GROUPS[EARLY_GROUPS:]:
        z_ref[nxt, :, lo:hi] = _dot(xn, w_ref[:, lo:hi])
    g_st[nxt] = _dot_nt(wgt_ref[...], xn)

    exs = {p: jnp.exp(sinkrep_ref[p[1]] - m_a[p]) for p in Q}
    for j in J:
        att = []
        for g in range(A_GROUP):
            gr_ = slice(g * L, (g + 1) * L)
            o0, o1 = o[(j, 0)], o[(j, 1)]
            pv = jnp.where(low, o0[gr_, 0:KV_W], o1[gr_, 0:KV_W])
            den = jnp.where(low, o0[gr_, KV_W:2 * KV_W] + exs[(j, 0)][gr_, :],
                            o1[gr_, KV_W:2 * KV_W] + exs[(j, 1)][gr_, :])
            att.append(pv / den)
        ssq = att[0] * att[0]
        for g in range(1, A_GROUP):
            ssq = ssq + att[g] * att[g]
        scale = lax.rsqrt(jnp.sum(ssq, axis=-1, keepdims=True) * (1.0 / A_WIDTH) + EPS)
        for g in range(A_GROUP):
            cols = slice(g * KV_W, (g + 1) * KV_W)
            y_st[nxt, rows[j], M_WIDTH + g * KV_W:M_WIDTH + (g + 1) * KV_W] = (
                att[g] * scale * an_ref[:, cols]).astype(BF16)
    kp_ref[...] = ka_cur[(CHUNKS_PER_STEP - 1) * L:CHUNKS_PER_STEP * L, :]
    vp_ref[...] = va_cur[(CHUNKS_PER_STEP - 1) * L:CHUNKS_PER_STEP * L, :]

    @pl.when((ci > 0) & (ci % seq_tiles == 0))
    def _():
        c_out[0] = c_st[...]
        m_out[0] = m_st[0:1, 0:M_HEADS]
        ktail_ref[0] = kp_ref[...]
        vtail_ref[0] = vp_ref[...]
        ctail_ref[0] = up_ref[0:8, :]


def _stream(x2d, p2d, xs, ps, ym_s, att_s, g_pre, w_main, w_gt, conv_w, bg_col, mlstm_norm, sinks, gains, weights,
            batch, seq):
    tr = TILE_ROWS
    ns = seq // tr
    nt = batch * ns
    n = batch * seq
    assert xs.shape[0] == PIPE_LAG * tr, "the sample rows fill exactly the pipeline's warm-up steps"
    full = lambda a: pl.BlockSpec(a.shape, lambda c: (0,) * a.ndim)
    resident = lambda a: pl.BlockSpec(a.shape, lambda c: (0,) * a.ndim, pipeline_mode=pl.Buffered(1))
    lagged = lambda w: pl.BlockSpec((tr, w), lambda c: (jnp.clip(c - PIPE_LAG, 0, nt - 1), 0))
    warmup = lambda w: pl.BlockSpec((tr, w), lambda c: (jnp.minimum(c, PIPE_LAG - 1), 0))
    per_seq = lambda shape: pl.BlockSpec((1,) + shape,
                                         lambda c: (jnp.clip((c - 1) // ns, 0, batch - 1),) + (0,) * len(shape))
    return pl.pallas_call(
        functools.partial(_stream_kernel, seq_tiles=ns),
        grid=(nt + PIPE_LAG,),
        in_specs=[pl.BlockSpec((tr, D_MODEL), lambda c: (jnp.minimum(c, nt - 1), 0)),
                  lagged(D_MODEL), lagged(P_DIM),
                  warmup(D_MODEL), warmup(P_DIM), warmup(M_WIDTH), warmup(A_WIDTH),
                  full(g_pre), resident(w_main), full(w_gt), full(conv_w), full(bg_col),
                  full(mlstm_norm), full(sinks)] + [full(g) for g in gains] + [resident(w) for w in weights],
        out_specs=[lagged(D_MODEL), pl.BlockSpec((tr, D_MODEL), lambda c: (jnp.minimum(c, PIPE_LAG), 0)),
                   per_seq((M_HEADS, M_DK, 2 * M_DV)), per_seq((1, M_HEADS)),
                   per_seq((CHUNK, KV_W)), per_seq((CHUNK, KV_W)), per_seq((8, QK_W))],
        out_shape=[jax.ShapeDtypeStruct((n, D_MODEL), F32),
                   jax.ShapeDtypeStruct(((PIPE_LAG + 1) * tr, D_MODEL), F32),
                   jax.ShapeDtypeStruct((batch, M_HEADS, M_DK, 2 * M_DV), F32),
                   jax.ShapeDtypeStruct((batch, 1, M_HEADS), F32),
                   jax.ShapeDtypeStruct((batch, CHUNK, KV_W), F32), jax.ShapeDtypeStruct((batch, CHUNK, KV_W), F32),
                   jax.ShapeDtypeStruct((batch, 8, QK_W), F32)],
        scratch_shapes=[pltpu.VMEM((2, tr, Z_W), F32), pltpu.VMEM((2, N_GATES, tr), F32),
                        pltpu.VMEM((2, tr, D_MODEL), BF16),
                        pltpu.VMEM((8 + tr, QK_W), F32),
                        pltpu.VMEM((M_HEADS, M_DK, 2 * M_DV), F32), pltpu.VMEM((8, 128), F32),
                        pltpu.VMEM((CHUNK, KV_W), F32), pltpu.VMEM((CHUNK, KV_W), F32),
                        pltpu.VMEM((A_KV, A_GROUP * CHUNK, 2 * CHUNK), F32),
                        pltpu.VMEM((A_KV, A_GROUP * CHUNK, KV_W), F32)],
        compiler_params=pltpu.CompilerParams(dimension_semantics=("arbitrary",),
                                             vmem_limit_bytes=VMEM_LIMIT_BYTES),
        name="prompt_layer",
    )(x2d, x2d, p2d, xs, ps, ym_s, att_s, g_pre, w_main, w_gt, conv_w, bg_col, mlstm_norm, sinks, *gains, *weights)


def _in_proj_sample_kernel(x_ref, g_ref, wn_ref, wt_ref, wgt_ref,
                           qkn_ref, qa_ref, ka_ref, va_ref, qkt_ref, vt_ref, ot_ref, gt_ref):
    xn = _rms(x_ref[...], g_ref[...]).astype(BF16)
    zn = _dot(xn, wn_ref[...])
    qkn_ref[...] = zn[:, 0:QK_W]
    qa_ref[...] = zn[:, QK_W:QK_W + A_WIDTH]
    ka_ref[...] = zn[:, QK_W + A_WIDTH:QK_W + A_WIDTH + KV_W]
    va_ref[...] = zn[:, QK_W + A_WIDTH + KV_W:QK_W + A_WIDTH + 2 * KV_W]
    zt = _dot_nt(wt_ref[...], xn)
    qkt_ref[...] = zt[0:QK_W, :]
    vt_ref[...] = zt[QK_W:QK_W + M_WIDTH, :]
    ot_ref[...] = zt[QK_W + M_WIDTH:QK_W + 2 * M_WIDTH, :]
    gt_ref[...] = _dot_nt(wgt_ref[...], xn)


def _in_proj_sample(x2d, g_pre, w_nat, w_t, w_gt):
    n = x2d.shape[0]
    shapes = [(n, QK_W), (n, A_WIDTH), (n, KV_W), (n, KV_W), (QK_W, n), (M_WIDTH, n), (M_WIDTH, n), (N_GATES, n)]
    return pl.pallas_call(
        _in_proj_sample_kernel,
        out_shape=[jax.ShapeDtypeStruct(s, F32) for s in shapes],
        compiler_params=pltpu.CompilerParams(vmem_limit_bytes=VMEM_LIMIT_BYTES),
        name="in_proj_sample",
    )(x2d, g_pre, w_nat, w_t, w_gt)


def _mlstm_sample_kernel(qt_ref, kt_ref, vt_ref, ot_ref, gt_ref, bg_ref, cbq_ref, cbk_ref, cwq_ref, cwk_ref,
                         gain_ref, c_ref, n_ref, m_ref,
                         ym_ref, c_out, n_out, m_out,
                         ct_ref, q_s, ik_s, num_s):
    h = pl.program_id(0)
    NB = c_ref.shape[0]
    L = qt_ref.shape[1] // NB
    T = range(L)

    def conv(raw_ref, cb_ref, cw_ref):
        ups = [cb_ref[j] for j in range(CONV_W - 1)] + [raw_ref[:, t * NB:(t + 1) * NB] for t in T]
        w = [cw_ref[:, j:j + 1] for j in range(CONV_W)]
        outs = []
        for t in T:
            acc = w[0] * ups[t]
            for j in range(1, CONV_W):
                acc = acc + w[j] * ups[t + j]
            outs.append(jax.nn.silu(acc))
        return outs

    q = [x * (M_DK ** -0.5) for x in conv(qt_ref, cbq_ref, cwq_ref)]
    k = conv(kt_ref, cbk_ref, cwk_ref)

    ig_all = gt_ref[pl.ds(h, 1), :] + bg_ref[pl.ds(h, 1), :]
    lf_all = jax.nn.log_sigmoid(gt_ref[pl.ds(h + M_HEADS, 1), :] + bg_ref[pl.ds(h + M_HEADS, 1), :])
    m = m_ref[pl.ds(h, 1), :]
    f, ms = [], []
    n = n_ref[0]
    den = []
    for t in T:
        ig, lf = ig_all[:, t * NB:(t + 1) * NB], lf_all[:, t * NB:(t + 1) * NB]
        m_new = jnp.maximum(lf + m, ig)
        f_t = jnp.exp(lf + m - m_new)
        ik = jnp.exp(ig - m_new) * k[t]
        m = m_new
        n = f_t * n + ik
        f.append(f_t)
        ms.append(m_new)
        den.append(jnp.sum(q[t] * n, axis=0, keepdims=True))
        ik_s[t] = ik
        q_s[t] = q[t]
    m_out[0] = m
    n_out[0] = n

    c_dbv = jnp.swapaxes(c_ref[:, 0, :, :], 0, 1)
    for d in range(M_DK):
        ct_ref[d * M_DV:(d + 1) * M_DV, :] = c_dbv[d].T

    VT = SAMPLE_V_TILE
    for vq in range(M_DV // VT):
        vts = [vt_ref[vq * VT:(vq + 1) * VT, t * NB:(t + 1) * NB] for t in T]

        def step(d, accs, vq=vq, vts=vts):
            r0 = pl.multiple_of(d * M_DV + vq * VT, VT)
            c = ct_ref[pl.ds(r0, VT), :]
            new = []
            for t in T:
                c = f[t] * c + ik_s[t, pl.ds(d, 1), :] * vts[t]
                new.append(accs[t] + q_s[t, pl.ds(d, 1), :] * c)
            ct_ref[pl.ds(r0, VT), :] = c
            return tuple(new)

        accs = lax.fori_loop(0, M_DK, step, tuple(jnp.zeros((VT, NB), F32) for _ in T), unroll=8)
        for t in T:
            num_s[t, vq * VT:(vq + 1) * VT, :] = accs[t]

    for t in T:
        hh = num_s[t] / jnp.maximum(jnp.abs(den[t]), jnp.exp(-ms[t]))
        hn = hh * lax.rsqrt(jnp.mean(hh * hh, axis=0, keepdims=True) + EPS) * gain_ref[...]
        y = jax.nn.sigmoid(ot_ref[:, t * NB:(t + 1) * NB]) * hn
        ym_ref[t * NB:(t + 1) * NB, :] = y.T

    for d in range(M_DK):
        c_out[:, d * M_DV:(d + 1) * M_DV] = ct_ref[d * M_DV:(d + 1) * M_DV, :].T


def _mlstm_sample(qkt, vt, ot, gt, bg_col, cbt, cwt, gain_col, c4d, nt, mt):
    nb = c4d.shape[0]
    n = qkt.shape[1]
    L = n // nb
    hblk = lambda rows, off: pl.BlockSpec((rows, n), lambda h, off=off: (h + off, 0))
    full = lambda a: pl.BlockSpec(a.shape, lambda h: (0,) * a.ndim)
    kq = QK_W // 2 // M_DK
    return pl.pallas_call(
        _mlstm_sample_kernel,
        grid=(M_HEADS,),
        in_specs=[hblk(M_DK, 0), hblk(M_DK, kq), hblk(M_DV, 0), hblk(M_DV, 0), full(gt), full(bg_col),
                  pl.BlockSpec((CONV_W - 1, M_DK, nb), lambda h: (0, h, 0)),
                  pl.BlockSpec((CONV_W - 1, M_DK, nb), lambda h: (0, h + kq, 0)),
                  pl.BlockSpec((M_DK, CONV_W), lambda h: (h, 0)),
                  pl.BlockSpec((M_DK, CONV_W), lambda h: (h + kq, 0)),
                  pl.BlockSpec((M_DV, 1), lambda h: (h, 0)),
                  pl.BlockSpec((nb, 1, M_DK, M_DV), lambda h: (0, h, 0, 0)),
                  pl.BlockSpec((1, M_DK, nb), lambda h: (h, 0, 0)),
                  full(mt)],
        out_specs=[pl.BlockSpec((n, M_DV), lambda h: (0, h)),
                   pl.BlockSpec((nb, M_DK * M_DV), lambda h: (0, h)),
                   pl.BlockSpec((1, M_DK, nb), lambda h: (h, 0, 0)),
                   pl.BlockSpec((1, 1, nb), lambda h: (h, 0, 0))],
        out_shape=[jax.ShapeDtypeStruct((n, M_WIDTH), F32),
                   jax.ShapeDtypeStruct((nb, M_HEADS * M_DK * M_DV), F32),
                   jax.ShapeDtypeStruct(nt.shape, F32), jax.ShapeDtypeStruct((M_HEADS, 1, nb), F32)],
        scratch_shapes=[pltpu.VMEM((M_DK * M_DV, nb), F32), pltpu.VMEM((L, M_DK, nb), F32),
                        pltpu.VMEM((L, M_DK, nb), F32), pltpu.VMEM((L, M_DV, nb), F32)],
        compiler_params=pltpu.CompilerParams(dimension_semantics=("arbitrary",),
                                             vmem_limit_bytes=VMEM_LIMIT_BYTES),
        name="mlstm_sample",
    )(qkt, qkt, vt, ot, gt, bg_col, cbt, cbt, cwt, cwt, gain_col, c4d, nt, mt)


def _attn_sample_kernel(qm_ref, kn_ref, vn_ref, kct_ref, vct_ref, sinkr_ref, sloper_ref,
                        att_ref, kct_out, vct_out, xk_ref, xv_ref):
    NB, L = kn_ref.shape[0], kn_ref.shape[1]
    R = A_KV * A_GROUP * L

    def bias(nkeys, offset):
        r = lax.broadcasted_iota(jnp.int32, (R, nkeys), 0).astype(F32)
        pos = lax.broadcasted_iota(jnp.int32, (R, nkeys), 1).astype(F32)
        tq = r - L * jnp.floor((r + 0.5) / L)
        dist = tq + offset - pos
        return jnp.where((dist >= 0.0) & (dist <= float(WINDOW)), -sloper_ref[...] * dist, NEG_INF)

    @pl.when(pl.program_id(0) == 0)
    def _():
        xk_ref[...] = jnp.zeros_like(xk_ref)
        xv_ref[...] = jnp.zeros_like(xv_ref)

    bias_c = bias(WINDOW, float(WINDOW))
    bias_n = bias(L, 0.0)
    sink = sinkr_ref[...]
    rq = lax.broadcasted_iota(jnp.int32, (R, KV_W), 0)
    cq = lax.broadcasted_iota(jnp.int32, (R, KV_W), 1)
    same_kv = (rq < A_GROUP * L) == (cq < A_HD)
    newest = lax.broadcasted_iota(jnp.int32, (KV_W, WINDOW), 1) >= WINDOW - L
    SEQ = range(NB)
    qh = [jnp.where(same_kv, qm_ref[b] * (A_HD ** -0.5), 0.0).astype(BF16) for b in SEQ]
    kt = [kct_ref[b] for b in SEQ]
    vt = [vct_ref[b] for b in SEQ]
    knb = [kn_ref[b].astype(BF16) for b in SEQ]
    vnb = [vn_ref[b].astype(BF16) for b in SEQ]
    sc_c = [_dot(qh[b], kt[b].astype(BF16)) + bias_c for b in SEQ]
    sc_n = [_dot_nt(qh[b], knb[b]) + bias_n for b in SEQ]
    m_a = [jnp.maximum(sink, jnp.maximum(jnp.max(sc_c[b], axis=-1, keepdims=True),
                                         jnp.max(sc_n[b], axis=-1, keepdims=True))) for b in SEQ]
    e_c = [jnp.exp(sc_c[b] - m_a[b]) for b in SEQ]
    e_n = [jnp.exp(sc_n[b] - m_a[b]) for b in SEQ]
    den_a = [jnp.exp(sink - m_a[b]) + jnp.sum(e_c[b], axis=-1, keepdims=True)
             + jnp.sum(e_n[b], axis=-1, keepdims=True) for b in SEQ]
    pv = [_dot_nt((e_c[b] / den_a[b]).astype(BF16), vt[b].astype(BF16))
          + _dot((e_n[b] / den_a[b]).astype(BF16), vnb[b]) for b in SEQ]
    for b in SEQ:
        att_ref[b] = pv[b]

    for b in SEQ:
        xk_ref[b, WINDOW - L:WINDOW, :] = kn_ref[b]
        xv_ref[b, WINDOW - L:WINDOW, :] = vn_ref[b]
    for b in SEQ:
        kct_out[b] = jnp.where(newest, xk_ref[b].T, pltpu.roll(kt[b], WINDOW - L, axis=1))
        vct_out[b] = jnp.where(newest, xv_ref[b].T, pltpu.roll(vt[b], WINDOW - L, axis=1))


def _attn_sample(qm, kn, vn, kct, vct, sink_rows, slope_rows):
    nb = qm.shape[0]
    ts = min(SAMPLE_SEQ_TILE, nb)
    per_b = lambda a: pl.BlockSpec((ts,) + a.shape[1:], lambda b: (b,) + (0,) * (a.ndim - 1))
    full = lambda a: pl.BlockSpec(a.shape, lambda b: (0,) * a.ndim)
    outs = [jax.ShapeDtypeStruct(a.shape, F32) for a in (qm, kct, vct)]
    return pl.pallas_call(
        _attn_sample_kernel,
        grid=(nb // ts,),
        in_specs=[per_b(a) for a in (qm, kn, vn, kct, vct)] + [full(sink_rows), full(slope_rows)],
        out_specs=[per_b(o) for o in outs],
        out_shape=outs,
        scratch_shapes=[pltpu.VMEM((ts, WINDOW, KV_W), F32), pltpu.VMEM((ts, WINDOW, KV_W), F32)],
        compiler_params=pltpu.CompilerParams(dimension_semantics=("arbitrary",),
                                             vmem_limit_bytes=VMEM_LIMIT_BYTES),
        name="attn_sample",
    )(qm, kn, vn, kct, vct, sink_rows, slope_rows)


def _permute_heads(a, axis):
    shape = a.shape
    a = a.reshape(shape[:axis] + (A_HEADS, A_HD) + shape[axis + 1:])
    a = jnp.take(a, np.asarray(HEAD_ORDER), axis=axis)
    return a.reshape(shape)


def kernel(x_prompt, x_sample, p_prompt, p_sample, state_mlstm_c, state_mlstm_n, state_mlstm_m,
           state_mlstm_conv, cache_swa_k, cache_swa_v, norm_mix_pre, w_in, b_gates, conv_w,
           mlstm_norm, attn_sinks, attn_norm, w_out, norm_mix_post, norm_ffn_pre, w_up, w_down,
           norm_ffn_post, w_pgate, w_pproj):
    depth = w_in.shape[0]
    assert depth == 1, "single-layer decoder"
    B, T, _ = x_prompt.shape
    SB, ST, _ = x_sample.shape
    assert T % TILE_ROWS == 0 and SB % SAMPLE_SEQ_TILE == 0
    i = 0

    wi = w_in[i]
    o_qk, o_vm, o_om, o_g, o_qa, o_ka, o_va = np.cumsum(
        [0, QK_W, M_WIDTH, M_WIDTH, N_GATES, A_WIDTH, KV_W])
    wt = wi.T.astype(BF16)
    wt_main = jnp.concatenate([wt[o_qk:o_g], _permute_heads(wt[o_qa:o_ka], 0), wt[o_ka:]], axis=0)
    w_main = wt_main.T
    w_gt = wt[o_g:o_qa]
    g_pre = norm_mix_pre[i].reshape(1, D_MODEL)
    bg_col = b_gates[i].reshape(N_GATES, 1)
    cw = conv_w[i]
    mn = mlstm_norm[i].reshape(1, M_WIDTH)
    sinks = attn_sinks[i].reshape(1, A_HEADS)
    gains = [_permute_heads(attn_norm[i], 0).reshape(1, A_WIDTH), norm_mix_post[i].reshape(1, D_MODEL),
             norm_ffn_pre[i].reshape(1, D_MODEL), norm_ffn_post[i].reshape(1, D_MODEL)]
    wo = jnp.concatenate([w_out[i][:M_WIDTH], _permute_heads(w_out[i][M_WIDTH:], 0)], axis=0)
    weights = [wo.astype(BF16), w_up[i].astype(BF16), w_down[i].astype(BF16),
               w_pgate[i].astype(BF16), w_pproj[i].astype(BF16)]

    xs = x_sample.transpose(1, 0, 2).reshape(ST * SB, D_MODEL)
    ps = p_sample[i].transpose(1, 0, 2).reshape(ST * SB, P_DIM)
    w_nat = jnp.concatenate([w_main[:, 0:QK_W], w_main[:, QK_W + 2 * M_WIDTH:]], axis=1)
    w_t = wt_main[0:QK_W + 2 * M_WIDTH]
    qkn_s, qa_s, ka_s, va_s, qkt_s, vt_s, ot_s, gt_s = _in_proj_sample(xs, g_pre, w_nat, w_t, w_gt)
    cbt = state_mlstm_conv[i].transpose(1, 2, 0)
    ym_s, c_s2, nt_s, mt_s = _mlstm_sample(
        qkt_s, vt_s, ot_s, gt_s, bg_col, cbt, cw.T, mn.reshape(M_WIDTH, 1),
        state_mlstm_c[i], state_mlstm_n[i].transpose(1, 2, 0), state_mlstm_m[i].T)
    c_s = c_s2.reshape(SB, M_HEADS, M_DK, M_DV)
    n_s = nt_s.transpose(2, 0, 1)
    m_s = mt_s.reshape(M_HEADS, SB).T
    conv_s = qkn_s.reshape(ST, SB, QK_W)[ST - (CONV_W - 1):].transpose(1, 0, 2)

    qs = qa_s.reshape(ST, SB, A_GROUP, A_KV, A_HD).transpose(1, 3, 2, 0, 4).reshape(SB, A_HEADS * ST, A_HD)
    qm = jnp.concatenate([qs, qs], axis=-1)
    kn = ka_s.reshape(ST, SB, KV_W).transpose(1, 0, 2)
    vn = va_s.reshape(ST, SB, KV_W).transpose(1, 0, 2)
    sink_rows = jnp.repeat(attn_sinks[i], ST).reshape(A_HEADS * ST, 1)
    slope_rows = jnp.asarray(np.repeat(np.asarray(SLOPES, np.float32), ST).reshape(A_HEADS * ST, 1))
    kct = cache_swa_k[i].transpose(0, 2, 3, 1).reshape(SB, KV_W, WINDOW)
    vct = cache_swa_v[i].transpose(0, 2, 3, 1).reshape(SB, KV_W, WINDOW)
    att_s, kct_s, vct_s = _attn_sample(qm, kn, vn, kct, vct, sink_rows, slope_rows)
    att_d = att_s.reshape(SB, A_KV, A_GROUP * ST, A_KV, A_HD)
    att_k = jnp.stack([att_d[:, kv, :, kv, :] for kv in range(A_KV)], axis=1)
    att_s2 = att_k.reshape(SB, A_KV, A_GROUP, ST, A_HD).transpose(3, 0, 2, 1, 4).reshape(ST * SB, A_WIDTH)
    k_s = kct_s.reshape(SB, A_KV, A_HD, WINDOW).transpose(0, 3, 1, 2)
    v_s = vct_s.reshape(SB, A_KV, A_HD, WINDOW).transpose(0, 3, 1, 2)

    xp = x_prompt.reshape(B * T, D_MODEL)
    y_p, y_s, cn_p, m_p, kt_p, vt_p, tail_p = _stream(
        xp, p_prompt[i].reshape(B * T, P_DIM), xs, ps, ym_s, att_s2,
        g_pre, w_main, w_gt, cw, bg_col, mn, sinks, gains, weights, B, T)
    y_prompt = y_p.reshape(B, T, D_MODEL)
    y_sample = y_s[:ST * SB].reshape(ST, SB, D_MODEL).transpose(1, 0, 2)
    c_p, n_p = cn_p[..., :M_DV], cn_p[..., M_DV]
    conv_p = tail_p[:, 8 - (CONV_W - 1):]
    k_p = kt_p.reshape(B, WINDOW, A_KV, A_HD)
    v_p = vt_p.reshape(B, WINDOW, A_KV, A_HD)

    stack = lambda a: a[None]
    return (y_prompt, y_sample,
            stack(c_p), stack(n_p), stack(m_p.reshape(B, M_HEADS)), stack(conv_p), stack(k_p), stack(v_p),
            stack(c_s), stack(n_s), stack(m_s), stack(conv_s), stack(k_s), stack(v_s))
```
